```python
import math
import jax, jax.numpy as jnp
from jax import lax
import numpy as np

D_MODEL = 1024
BATCH = 1
SEQ = 16384
DEPTH = 4
DEC_BATCH = 8
DEC_SEQ = 16
PAST_LEN = 1024

CHUNK = 64
N_META = 16
N_MIXERS = 2
N_FOX = (DEPTH + 1) // 2
N_RET = DEPTH // 2
EPS = 1e-6

FOX_HEADS = 16
FOX_HEAD_DIM = 64
FOX_WIDTH = FOX_HEADS * FOX_HEAD_DIM
FOX_QBLOCK = 128
FOX_IN = 4 * FOX_WIDTH + FOX_HEADS

RET_HEADS = 4
RET_QK_DIM = 256
RET_V_DIM = 512
RET_QK_WIDTH = RET_HEADS * RET_QK_DIM
RET_V_WIDTH = RET_HEADS * RET_V_DIM
RET_IN = 2 * RET_QK_WIDTH + 2 * RET_V_WIDTH
ROPE_BASE = 10000.0

PEER_HEADS = 8
PEER_NKEYS = 128
PEER_EXPERTS = PEER_NKEYS * PEER_NKEYS
PEER_QDIM = 256
PEER_HALF = PEER_QDIM // 2
PEER_TOPK = 16
PEER_BLOCK = 128

kernel_name = 'fox_retention_peer_stream'


def _rmsnorm(x, g):
    xf = x.astype(jnp.float32)
    y = xf * lax.rsqrt(jnp.mean(xf * xf, axis=-1, keepdims=True) + EPS)
    return (y * g.astype(jnp.float32)).astype(x.dtype)


def _rotary(x, pos):
    half = x.shape[-1] // 2
    inv = ROPE_BASE ** (-jnp.arange(half, dtype=jnp.float32) / half)
    ang = pos.astype(jnp.float32)[:, None] * inv[None, :]
    cos = jnp.cos(ang)[:, None, :]
    sin = jnp.sin(ang)[:, None, :]
    xf = x.astype(jnp.float32)
    x1, x2 = xf[..., :half], xf[..., half:]
    return jnp.concatenate([x1 * cos - x2 * sin, x1 * sin + x2 * cos], axis=-1).astype(x.dtype)


def _ret_log_decay():
    return jnp.log(1.0 - 2.0 ** (-5.0 - jnp.arange(RET_HEADS, dtype=jnp.float32)))


def _fox_project(h, w_in, b_f, g_q, g_k):
    b, l, _ = h.shape
    z = h @ w_in
    shp = (b, l, FOX_HEADS, FOX_HEAD_DIM)
    q = _rmsnorm(z[..., :FOX_WIDTH].reshape(shp), g_q)
    k = _rmsnorm(z[..., FOX_WIDTH:2 * FOX_WIDTH].reshape(shp), g_k)
    v = z[..., 2 * FOX_WIDTH:3 * FOX_WIDTH].reshape(shp)
    og = z[..., 3 * FOX_WIDTH:4 * FOX_WIDTH]
    lf = jax.nn.log_sigmoid((z[..., 4 * FOX_WIDTH:] + b_f).astype(jnp.float32))
    return q, k, v, og, lf


def _fox_attend_prompt(q, k, v, lf):
    b, l, h, dh = q.shape
    c = jnp.cumsum(lf, axis=1)
    c_t = c.transpose(0, 2, 1)
    nb = -(-l // FOX_QBLOCK)
    pad = nb * FOX_QBLOCK - l
    qb = jnp.pad(q, ((0, 0), (0, pad), (0, 0), (0, 0))).reshape(b, nb, FOX_QBLOCK, h, dh).transpose(1, 0, 2, 3, 4)
    cb = jnp.pad(c, ((0, 0), (0, pad), (0, 0))).reshape(b, nb, FOX_QBLOCK, h).transpose(1, 0, 2, 3)
    kpos = jnp.arange(l)
    scale = dh ** -0.5

    def block(args):
        q_blk, c_blk, bi = args
        qpos = bi * FOX_QBLOCK + jnp.arange(FOX_QBLOCK)
        s = jnp.einsum('bqhd,bkhd->bhqk', q_blk, k).astype(jnp.float32) * scale
        s = s + (c_blk.transpose(0, 2, 1)[..., :, None] - c_t[:, :, None, :])
        s = jnp.where(kpos[None, :] <= qpos[:, None], s, -jnp.inf)
        p = jax.nn.softmax(s, axis=-1)
        return jnp.einsum('bhqk,bkhd->bqhd', p.astype(v.dtype), v)

    o = lax.map(block, (qb, cb, jnp.arange(nb)))
    return o.transpose(1, 0, 2, 3, 4).reshape(b, nb * FOX_QBLOCK, h, dh)[:, :l]


def _fox_attend_sample(q, k_new, v_new, lf_new, k_cache, v_cache, lf_cache):
    p_len = k_cache.shape[1]
    n = q.shape[1]
    k = jnp.concatenate([k_cache, k_new], axis=1)
    v = jnp.concatenate([v_cache, v_new], axis=1)
    c = jnp.cumsum(jnp.concatenate([lf_cache.astype(jnp.float32), lf_new], axis=1), axis=1)
    c_t = c.transpose(0, 2, 1)
    s = jnp.einsum('bqhd,bkhd->bhqk', q, k).astype(jnp.float32) * (q.shape[-1] ** -0.5)
    s = s + (c_t[:, :, p_len:, None] - c_t[:, :, None, :])
    mask = jnp.arange(p_len + n)[None, :] <= (p_len + jnp.arange(n))[:, None]
    s = jnp.where(mask, s, -jnp.inf)
    p = jax.nn.softmax(s, axis=-1)
    return jnp.einsum('bhqk,bkhd->bqhd', p.astype(v.dtype), v)


def _fox_output(o, og, w_out):
    b, l = o.shape[:2]
    y = o.reshape(b, l, FOX_WIDTH) * jax.nn.sigmoid(og.astype(jnp.float32)).astype(o.dtype)
    return y @ w_out


def _ret_project(h, w_in, pos):
    b, l, _ = h.shape
    z = h @ w_in
    q = _rotary(z[..., :RET_QK_WIDTH].reshape(b, l, RET_HEADS, RET_QK_DIM), pos) * (RET_QK_DIM ** -0.5)
    k = _rotary(z[..., RET_QK_WIDTH:2 * RET_QK_WIDTH].reshape(b, l, RET_HEADS, RET_QK_DIM), pos)
    v = z[..., 2 * RET_QK_WIDTH:2 * RET_QK_WIDTH + RET_V_WIDTH].reshape(b, l, RET_HEADS, RET_V_DIM)
    g = z[..., 2 * RET_QK_WIDTH + RET_V_WIDTH:]
    return q, k, v, g


def _retention_prompt(q, k, v):
    b, l, h, _ = q.shape
    lead = (-l) % CHUNK
    nc = (l + lead) // CHUNK
    padf = lambda a: jnp.pad(a.astype(jnp.float32), ((0, 0), (lead, 0), (0, 0), (0, 0)))
    qc = padf(q).reshape(b, nc, CHUNK, h, RET_QK_DIM)
    kc = padf(k).reshape(b, nc, CHUNK, h, RET_QK_DIM)
    vc = padf(v).reshape(b, nc, CHUNK, h, RET_V_DIM)
    logg = _ret_log_decay()
    idx = jnp.arange(CHUNK, dtype=jnp.float32)
    dmat = jnp.exp(logg[:, None, None] * jnp.abs(idx[:, None] - idx[None, :]))
    s = jnp.einsum('bnihd,bnjhd->bnhij', qc, kc) * dmat
    o_intra = jnp.einsum('bnhij,bnjhe->bnihe', s, vc)
    q_dec = jnp.exp(logg[None, :] * (idx[:, None] + 1.0))
    k_dec = jnp.exp(logg[None, :] * (CHUNK - 1.0 - idx[:, None]))
    c_dec = jnp.exp(logg * CHUNK)

    def step(state, inp):
        qn, kn, vn = inp
        o = jnp.einsum('bihd,bhde->bihe', qn, state) * q_dec[None, :, :, None]
        state = state * c_dec[None, :, None, None] + jnp.einsum('bjhd,jh,bjhe->bhde', kn, k_dec, vn)
        return state, o

    s0 = jnp.zeros((b, h, RET_QK_DIM, RET_V_DIM), jnp.float32)
    s_fin, o_inter = lax.scan(step, s0, (qc.transpose(1, 0, 2, 3, 4), kc.transpose(1, 0, 2, 3, 4), vc.transpose(1, 0, 2, 3, 4)))
    o = o_intra + o_inter.transpose(1, 0, 2, 3, 4)
    return o.reshape(b, nc * CHUNK, h, RET_V_DIM)[:, lead:], s_fin


def _retention_sample(q, k, v, state):
    n = q.shape[1]
    qf, kf, vf = q.astype(jnp.float32), k.astype(jnp.float32), v.astype(jnp.float32)
    st = state.astype(jnp.float32)
    logg = _ret_log_decay()
    idx = jnp.arange(n, dtype=jnp.float32)
    dmat = jnp.exp(logg[:, None, None] * jnp.abs(idx[:, None] - idx[None, :]))
    s = jnp.einsum('bihd,bjhd->bhij', qf, kf) * dmat
    o = jnp.einsum('bhij,bjhe->bihe', s, vf)
    o = o + jnp.einsum('bihd,bhde->bihe', qf, st) * jnp.exp(logg[None, :] * (idx[:, None] + 1.0))[None, :, :, None]
    k_dec = jnp.exp(logg[None, :] * (n - 1.0 - idx[:, None]))
    new_state = st * jnp.exp(logg * n)[None, :, None, None] + jnp.einsum('bjhd,jh,bjhe->bhde', kf, k_dec, vf)
    return o, new_state


def _ret_output(o, g, gn, w_out):
    b, l = o.shape[:2]
    of = o.astype(jnp.float32)
    mu = jnp.mean(of, axis=-1, keepdims=True)
    var = jnp.mean(jnp.square(of - mu), axis=-1, keepdims=True)
    y = ((of - mu) * lax.rsqrt(var + EPS)).reshape(b, l, RET_V_WIDTH) * gn.astype(jnp.float32)
    y = jax.nn.silu(g.astype(jnp.float32)) * y
    return y.astype(g.dtype) @ w_out


def _peer(h, w_q, subkeys, u_tab, v_tab):
    t, d = h.shape
    nb = -(-t // PEER_BLOCK)
    hb = jnp.pad(h, ((0, nb * PEER_BLOCK - t), (0, 0))).reshape(nb, PEER_BLOCK, d)

    def block(xb):
        qh = (xb @ w_q).reshape(PEER_BLOCK, PEER_HEADS, 2, PEER_HALF)
        s = jnp.einsum('thpc,hpkc->thpk', qh, subkeys).astype(jnp.float32)
        sv, si = lax.top_k(s, PEER_TOPK)
        cand = (sv[:, :, 0, :, None] + sv[:, :, 1, None, :]).reshape(PEER_BLOCK, PEER_HEADS, PEER_TOPK * PEER_TOPK)
        cid = (si[:, :, 0, :, None] * PEER_NKEYS + si[:, :, 1, None, :]).reshape(PEER_BLOCK, PEER_HEADS, PEER_TOPK * PEER_TOPK)
        fv, fi = lax.top_k(cand, PEER_TOPK)
        eid = jnp.take_along_axis(cid, fi, axis=-1)
        gate = jax.nn.softmax(fv, axis=-1)
        a = jax.nn.gelu(jnp.einsum('thkd,td->thk', u_tab[eid], xb).astype(jnp.float32), approximate=False)
        w = (gate * a).astype(xb.dtype)
        return jnp.einsum('thk,thkd->td', w, v_tab[eid])

    return lax.map(block, hb).reshape(nb * PEER_BLOCK, d)[:t]


def setup_inputs(seed: int = 0) -> dict:
    key = jax.random.key(seed)
    ks = jax.random.split(key, 24)
    f32 = jnp.float32
    nrm = lambda k, shape, scale: scale * jax.random.normal(k, shape, f32)
    return {
        'x_prompt': nrm(ks[0], (BATCH, SEQ, D_MODEL), 1.0),
        'x_sample': nrm(ks[1], (DEC_BATCH, DEC_SEQ, D_MODEL), 1.0),
        'cache_fox_k': nrm(ks[2], (N_FOX, DEC_BATCH, PAST_LEN, FOX_HEADS, FOX_HEAD_DIM), 1.0),
        'cache_fox_v': nrm(ks[3], (N_FOX, DEC_BATCH, PAST_LEN, FOX_HEADS, FOX_HEAD_DIM), 1.0),
        'cache_fox_lf': jax.nn.log_sigmoid(3.0 + nrm(ks[4], (N_FOX, DEC_BATCH, PAST_LEN, FOX_HEADS), 1.5)),
        'state_ret': nrm(ks[5], (N_RET, DEC_BATCH, RET_HEADS, RET_QK_DIM, RET_V_DIM), 0.5),
        'meta_tokens': nrm(ks[6], (N_META, D_MODEL), 1.0),
        'norm_mix': 1.0 + nrm(ks[7], (DEPTH, D_MODEL), 0.02),
        'norm_ffn': 1.0 + nrm(ks[8], (DEPTH, D_MODEL), 0.02),
        'fox_w_in': nrm(ks[9], (N_FOX, D_MODEL, FOX_IN), D_MODEL ** -0.5),
        'fox_b_f': jax.random.uniform(ks[10], (N_FOX, FOX_HEADS), f32, 1.0, 6.0),
        'fox_q_norm': 1.0 + nrm(ks[11], (N_FOX, FOX_HEAD_DIM), 0.02),
        'fox_k_norm': 1.0 + nrm(ks[12], (N_FOX, FOX_HEAD_DIM), 0.02),
        'fox_w_out': nrm(ks[13], (N_FOX, FOX_WIDTH, D_MODEL), 0.5 * FOX_WIDTH ** -0.5),
        'ret_w_in': nrm(ks[14], (N_RET, D_MODEL, RET_IN), D_MODEL ** -0.5),
        'ret_gn': 1.0 + nrm(ks[15], (N_RET, RET_V_WIDTH), 0.02),
        'ret_w_out': nrm(ks[16], (N_RET, RET_V_WIDTH, D_MODEL), 0.5 * RET_V_WIDTH ** -0.5),
        'peer_w_q': nrm(ks[17], (DEPTH, D_MODEL, PEER_HEADS * PEER_QDIM), D_MODEL ** -0.5),
        'peer_subkeys': nrm(ks[18], (DEPTH, PEER_HEADS, 2, PEER_NKEYS, PEER_HALF), PEER_HALF ** -0.5),
        'peer_u': nrm(ks[19], (DEPTH, PEER_EXPERTS, D_MODEL), D_MODEL ** -0.5),
        'peer_v': nrm(ks[20], (DEPTH, PEER_EXPERTS, D_MODEL), 0.1),
    }


def reference(x_prompt, x_sample, cache_fox_k, cache_fox_v, cache_fox_lf, state_ret,
              meta_tokens, norm_mix, norm_ffn, fox_w_in, fox_b_f, fox_q_norm, fox_k_norm, fox_w_out,
              ret_w_in, ret_gn, ret_w_out, peer_w_q, peer_subkeys, peer_u, peer_v):
    b, s_len, d = x_prompt.shape
    bd, n, _ = x_sample.shape
    p_len = cache_fox_k.shape[2]
    l = N_META + s_len
    xp = jnp.concatenate([jnp.broadcast_to(meta_tokens[None].astype(x_prompt.dtype), (b, N_META, d)), x_prompt], axis=1)
    xs = x_sample
    pos_p = jnp.arange(l) - N_META
    pos_s = p_len + jnp.arange(n)
    kp_l, vp_l, lfp_l, ks_l, vs_l, lfs_l, srp_l, srs_l = [], [], [], [], [], [], [], []
    for layer in range(DEPTH):
        j = layer // N_MIXERS
        hp = _rmsnorm(xp, norm_mix[layer])
        hs = _rmsnorm(xs, norm_mix[layer])
        if layer % N_MIXERS == 0:
            qp, kp, vp, gp, lfp = _fox_project(hp, fox_w_in[j], fox_b_f[j], fox_q_norm[j], fox_k_norm[j])
            qs, kss, vss, gs, lfs = _fox_project(hs, fox_w_in[j], fox_b_f[j], fox_q_norm[j], fox_k_norm[j])
            op = _fox_attend_prompt(qp, kp, vp, lfp)
            osm = _fox_attend_sample(qs, kss, vss, lfs, cache_fox_k[j], cache_fox_v[j], cache_fox_lf[j])
            xp = xp + _fox_output(op, gp, fox_w_out[j])
            xs = xs + _fox_output(osm, gs, fox_w_out[j])
            kp_l.append(kp)
            vp_l.append(vp)
            lfp_l.append(lfp.astype(x_prompt.dtype))
            ks_l.append(kss)
            vs_l.append(vss)
            lfs_l.append(lfs.astype(cache_fox_lf.dtype))
        else:
            qp, kp, vp, gp = _ret_project(hp, ret_w_in[j], pos_p)
            qs, kss, vss, gs = _ret_project(hs, ret_w_in[j], pos_s)
            op, st_p = _retention_prompt(qp, kp, vp)
            osm, st_s = _retention_sample(qs, kss, vss, state_ret[j])
            xp = xp + _ret_output(op, gp, ret_gn[j], ret_w_out[j])
            xs = xs + _ret_output(osm, gs, ret_gn[j], ret_w_out[j])
            srp_l.append(st_p.astype(x_prompt.dtype))
            srs_l.append(st_s.astype(state_ret.dtype))
        xp = xp + _peer(_rmsnorm(xp, norm_ffn[layer]).reshape(b * l, d), peer_w_q[layer], peer_subkeys[layer], peer_u[layer], peer_v[layer]).reshape(b, l, d)
        xs = xs + _peer(_rmsnorm(xs, norm_ffn[layer]).reshape(bd * n, d), peer_w_q[layer], peer_subkeys[layer], peer_u[layer], peer_v[layer]).reshape(bd, n, d)
    y_prompt = xp[:, N_META:]
    y_sample = xs
    new_fox_k_prompt = jnp.stack(kp_l)
    new_fox_v_prompt = jnp.stack(vp_l)
    new_fox_lf_prompt = jnp.stack(lfp_l)
    new_state_ret_prompt = jnp.stack(srp_l)
    new_fox_k_sample = jnp.stack(ks_l)
    new_fox_v_sample = jnp.stack(vs_l)
    new_fox_lf_sample = jnp.stack(lfs_l)
    new_state_ret_sample = jnp.stack(srs_l)
    return (y_prompt, y_sample, new_fox_k_prompt, new_fox_v_prompt, new_fox_lf_prompt, new_state_ret_prompt,
            new_fox_k_sample, new_fox_v_sample, new_fox_lf_sample, new_state_ret_sample)
```

```python
import functools
import math

import jax
import jax.numpy as jnp
from jax import lax
from jax.experimental import pallas as pl
from jax.experimental.pallas import tpu as pltpu

F32 = jnp.float32
BF16 = jnp.bfloat16

EPS = 1e-6
N_META = 16
CHUNK = 64
ROPE_BASE = 10000.0
PEER_TOPK = 16

LANES = 128
MXU_DIM = 256
VMEM_LIMIT = 56 * 1024 * 1024

ROW_TILE = 512
PROJ_TILE = 256
RET_BLOCK = 256
PEER_EXPERT_CHUNK = 1024
MASKED_KEY = 1e30
NO_RANK = 99.0

_NT = (((1,), (1,)), ((), ()))


def _const_spec(shape):
    return pl.BlockSpec(shape, lambda *_: (0,) * len(shape), pipeline_mode=pl.Buffered(1))


def _params(semantics):
    return pltpu.CompilerParams(dimension_semantics=semantics, vmem_limit_bytes=VMEM_LIMIT)


def _split3(x):
    a = x.astype(BF16)
    r = x - a.astype(F32)
    b = r.astype(BF16)
    c = (r - b.astype(F32)).astype(BF16)
    return a, b, c


def _dot(a, b):
    return jnp.dot(a, b, preferred_element_type=F32)


def _dot_nt(a, b):
    return lax.dot_general(a, b, _NT, preferred_element_type=F32)


def _rmsnorm_rows(x, g):
    ms = jnp.mean(x * x, axis=-1, keepdims=True)
    return x * lax.rsqrt(ms + EPS) * g


def _log_sigmoid(x):
    return jnp.minimum(x, 0.0) - jnp.log1p(jnp.exp(-jnp.abs(x)))


def _gelu(x):
    return 0.5 * x * (1.0 + lax.erf(x * (2.0 ** -0.5)))


def _fox_proj_kernel(x_ref, gn_ref, w_ref, wf_ref, bf_ref, gq_ref, gk_ref, bd_ref, tril_ref,
                     q_ref, kf_ref, kb_ref, vf_ref, vb_ref, og_ref, lf_ref, c_ref, carry_ref,
                     *, tm, front, head_dim, width):
    i = pl.program_id(0)

    @pl.when(i == 0)
    def _():
        carry_ref[...] = jnp.zeros_like(carry_ref)

    h = _rmsnorm_rows(x_ref[...], gn_ref[...]).astype(BF16)

    def head_norm(z, g):
        zz = (z * z).astype(BF16)
        ss = jnp.concatenate(
            [_dot(zz[:, c * MXU_DIM:(c + 1) * MXU_DIM], bd_ref[...]) for c in range(width // MXU_DIM)], axis=1)
        return z * lax.rsqrt(ss * (1.0 / head_dim) + EPS) * g

    zq = _dot(h, w_ref[:, 0:width])
    q_ref[...] = (head_norm(zq, gq_ref[...]) * (head_dim ** -0.5)).astype(BF16)
    zk = _dot(h, w_ref[:, width:2 * width])
    kn = head_norm(zk, gk_ref[...])
    kf_ref[...] = kn
    kb_ref[...] = kn.astype(BF16)
    zv = _dot(h, w_ref[:, 2 * width:3 * width])
    vf_ref[...] = zv
    vb_ref[...] = zv.astype(BF16)
    og_ref[...] = _dot(h, w_ref[:, 3 * width:4 * width])

    lf = _log_sigmoid(_dot(h, wf_ref[...]) + bf_ref[...])
    lf_ref[...] = lf
    row = i * tm + lax.broadcasted_iota(jnp.int32, (tm, 1), 0)
    real = row >= front
    l1, l2, l3 = _split3(jnp.where(real, lf, 0.0))
    tril = tril_ref[...]
    c = _dot(tril, l1) + _dot(tril, l2) + _dot(tril, l3) + carry_ref[...]
    carry_ref[...] = c[tm - 1:tm, :]
    c_ref[...] = jnp.where(real, c, MASKED_KEY)


def _fox_proj(x, gn, w_main, w_f, b_f, gq, gk, *, front, head_dim):
    t_all, d = x.shape
    width = w_main.shape[1] // 4
    tm = PROJ_TILE
    blk = MXU_DIM // head_dim
    bd = jnp.kron(jnp.eye(blk, dtype=F32), jnp.ones((head_dim, head_dim), F32)).astype(BF16)
    tril = (lax.broadcasted_iota(jnp.int32, (tm, tm), 0) >= lax.broadcasted_iota(jnp.int32, (tm, tm), 1)).astype(BF16)
    row = lambda n: pl.BlockSpec((tm, n), lambda i: (i, 0))
    sds = jax.ShapeDtypeStruct
    return pl.pallas_call(
        functools.partial(_fox_proj_kernel, tm=tm, front=front, head_dim=head_dim, width=width),
        grid=(t_all // tm,),
        in_specs=[row(d), _const_spec((1, d)), _const_spec(w_main.shape), _const_spec(w_f.shape),
                  _const_spec((1, LANES)), _const_spec((1, width)), _const_spec((1, width)),
                  _const_spec(bd.shape), _const_spec(tril.shape)],
        out_specs=[row(width), row(width), row(width), row(width), row(width), row(width), row(LANES), row(LANES)],
        out_shape=[sds((t_all, width), BF16), sds((t_all, width), F32), sds((t_all, width), BF16),
                   sds((t_all, width), F32), sds((t_all, width), BF16), sds((t_all, width), F32),
                   sds((t_all, LANES), F32), sds((t_all, LANES), F32)],
        scratch_shapes=[pltpu.VMEM((1, LANES), F32)],
        compiler_params=_params(("arbitrary",)),
        name="fox_proj",
    )(x, gn, w_main, w_f, b_f, gq, gk, bd, tril)


def _fox_attn_kernel(q_ref, k_ref, v_ref, c_ref, og_ref, o_ref, m_ref, acc_ref, *, tile, head_dim):
    qi = pl.program_id(1)
    lane = lax.broadcasted_iota(jnp.int32, (1, 2 * head_dim), 1)
    lo = lane < head_dim
    head_lanes = (lo, jnp.logical_not(lo))
    q = q_ref[...]
    zero = jnp.zeros_like(q)
    qm = tuple(jnp.where(hl, q, zero) for hl in head_lanes)
    c_last = tuple(c_ref[hh, pl.ds(qi, 1), :][:, tile - 1:tile] for hh in range(2))
    m_ref[...] = jnp.full(m_ref.shape, -jnp.inf, F32)
    acc_ref[...] = jnp.zeros(acc_ref.shape, F32)
    one = jnp.ones((), BF16)

    def block(j, diagonal):
        start = pl.multiple_of(j * tile, tile)
        k = k_ref[pl.ds(start, tile), :]
        v = v_ref[pl.ds(start, tile), :]
        for hh in range(2):
            s = _dot_nt(qm[hh], k)
            u = s - (c_ref[hh, pl.ds(j, 1), :] - c_last[hh])
            if diagonal:
                r = lax.broadcasted_iota(jnp.int32, (tile, tile), 0)
                cidx = lax.broadcasted_iota(jnp.int32, (tile, tile), 1)
                u = jnp.where(cidx <= r, u, -jnp.inf)
            m_prev = m_ref[hh]
            m_new = jnp.maximum(m_prev, jnp.max(u, axis=1, keepdims=True))
            p = jnp.exp(u - m_new).astype(BF16)
            v_aug = jnp.where(head_lanes[hh], v, one)
            acc_ref[hh] = jnp.exp(m_prev - m_new) * acc_ref[hh] + _dot(p, v_aug)
            m_ref[hh] = m_new

    def body(j, carry):
        block(j, False)
        return carry

    lax.fori_loop(0, qi, body, 0)
    block(qi, True)

    outs = []
    for hh in range(2):
        a = acc_ref[hh]
        outs.append(a / pltpu.roll(a, head_dim, 1))
    o = jnp.where(lo, outs[0], outs[1])
    o_ref[...] = (o * jax.nn.sigmoid(og_ref[...])).astype(BF16)


def _fox_attn_prompt(q_b, k_b, v_b, c3, og, *, t_prompt, head_dim):
    t_all, width = q_b.shape
    tile = ROW_TILE
    pair = 2 * head_dim
    n_blk = t_prompt // tile
    return pl.pallas_call(
        functools.partial(_fox_attn_kernel, tile=tile, head_dim=head_dim),
        grid=(width // pair, n_blk),
        in_specs=[pl.BlockSpec((tile, pair), lambda p, i: (i, p)),
                  pl.BlockSpec((t_prompt, pair), lambda p, i: (0, p)),
                  pl.BlockSpec((t_prompt, pair), lambda p, i: (0, p)),
                  pl.BlockSpec((2, n_blk, tile), lambda p, i: (p, 0, 0)),
                  pl.BlockSpec((tile, pair), lambda p, i: (i, p))],
        out_specs=pl.BlockSpec((tile, pair), lambda p, i: (i, p)),
        out_shape=jax.ShapeDtypeStruct((t_all, width), BF16),
        scratch_shapes=[pltpu.VMEM((2, tile, 1), F32), pltpu.VMEM((2, tile, pair), F32)],
        compiler_params=_params(("arbitrary", "arbitrary")),
        name="fox_attn_prompt",
    )(q_b, k_b, v_b, c3, og)


def _fox_sample_kernel(prev_ref, q_ref, kn_ref, vn_ref, og_ref, lfn_ref, kc_ref, vc_ref, lfc_ref, su_ref, bt_ref,
                       o_ref, *, n_new, head_dim, n_heads):
    del prev_ref
    b = pl.program_id(0)
    pair = 2 * head_dim
    n_rows = kn_ref.shape[0]

    @pl.when(b == 0)
    def _():
        o_ref[...] = jnp.zeros_like(o_ref)

    su = su_ref[...]
    suf = sum(_dot(t, su) for t in _split3(lfc_ref[0]))
    bt = bt_ref[...]
    pre = sum(_dot(t, bt) for t in _split3(lfn_ref[...]))

    r0 = pl.multiple_of(b * n_new, n_new)
    qrow = lax.broadcasted_iota(jnp.int32, (n_new, n_rows), 0)
    col = lax.broadcasted_iota(jnp.int32, (n_new, n_rows), 1)
    visible = jnp.logical_and(col >= b * n_new, col <= b * n_new + qrow)
    lane = lax.broadcasted_iota(jnp.int32, (1, pair), 1)
    lo = lane < head_dim
    head_lanes = (lo, jnp.logical_not(lo))

    for p in range(n_heads // 2):
        cols = slice(p * pair, (p + 1) * pair)
        q = q_ref[pl.ds(r0, n_new), cols]
        k_c = kc_ref[0, :, cols].astype(BF16)
        v_c = vc_ref[0, :, cols].astype(BF16)
        k_n = kn_ref[:, cols]
        v_n = vn_ref[:, cols]
        outs = []
        for hh in range(2):
            head = 2 * p + hh
            qh = jnp.where(head_lanes[hh], q, jnp.zeros_like(q))
            u_c = _dot_nt(qh, k_c) + suf[head:head + 1, :]
            u_n = jnp.where(visible, _dot_nt(qh, k_n) - pre[head:head + 1, :], -jnp.inf)
            m = jnp.maximum(jnp.max(u_c, axis=1, keepdims=True), jnp.max(u_n, axis=1, keepdims=True))
            p_c = jnp.exp(u_c - m)
            p_n = jnp.exp(u_n - m)
            den = jnp.sum(p_c, axis=1, keepdims=True) + jnp.sum(p_n, axis=1, keepdims=True)
            outs.append((_dot(p_c.astype(BF16), v_c) + _dot(p_n.astype(BF16), v_n)) / den)
        o = jnp.where(lo, outs[0], outs[1])
        gate = jax.nn.sigmoid(og_ref[pl.ds(r0, n_new), cols])
        o_ref[pl.ds(r0, n_new), cols] = (o * gate).astype(BF16)


def _fox_attn_sample(o_prev, q_b, k_b, v_b, og, lf_new_t, cache_k, cache_v, cache_lf_t, *, t_prompt, n_new, head_dim):
    t_all, width = q_b.shape
    n_streams, past, _ = cache_k.shape
    n_heads = width // head_dim
    n_rows = n_streams * n_new
    assert t_prompt % n_rows == 0 and n_rows % LANES == 0
    su = (lax.broadcasted_iota(jnp.int32, (past, past), 0) > lax.broadcasted_iota(jnp.int32, (past, past), 1)).astype(BF16)
    rr = lax.broadcasted_iota(jnp.int32, (n_rows, n_rows), 0)
    cc = lax.broadcasted_iota(jnp.int32, (n_rows, n_rows), 1)
    bt = jnp.logical_and(rr // n_new == cc // n_new, rr <= cc).astype(BF16)
    rows = pl.BlockSpec((n_rows, width), lambda b: (t_prompt // n_rows, 0), pipeline_mode=pl.Buffered(1))
    tail = t_all - t_prompt
    return pl.pallas_call(
        functools.partial(_fox_sample_kernel, n_new=n_new, head_dim=head_dim, n_heads=n_heads),
        grid=(n_streams,),
        in_specs=[pl.BlockSpec(memory_space=pl.ANY), rows, rows, rows, rows,
                  _const_spec(lf_new_t.shape),
                  pl.BlockSpec((1, past, width), lambda b: (b, 0, 0)),
                  pl.BlockSpec((1, past, width), lambda b: (b, 0, 0)),
                  pl.BlockSpec((1, n_heads, past), lambda b: (b, 0, 0)),
                  _const_spec(su.shape), _const_spec(bt.shape)],
        out_specs=pl.BlockSpec((tail, width), lambda b: (t_prompt // tail, 0)),
        out_shape=jax.ShapeDtypeStruct((t_all, width), BF16),
        input_output_aliases={0: 0},
        compiler_params=_params(("arbitrary",)),
        name="fox_attn_sample",
    )(o_prev, q_b, k_b, v_b, og, lf_new_t, cache_k, cache_v, cache_lf_t, su, bt)


def _out_proj_kernel(y_ref, x_ref, w_ref, o_ref):
    o_ref[...] = x_ref[...] + _dot(y_ref[...], w_ref[...])


def _out_proj(y_b, x, w):
    t_all, d = x.shape
    k = y_b.shape[1]
    tm = ROW_TILE
    return pl.pallas_call(
        _out_proj_kernel,
        grid=(t_all // tm,),
        in_specs=[pl.BlockSpec((tm, k), lambda i: (i, 0)), pl.BlockSpec((tm, d), lambda i: (i, 0)),
                  _const_spec(w.shape)],
        out_specs=pl.BlockSpec((tm, d), lambda i: (i, 0)),
        out_shape=jax.ShapeDtypeStruct((t_all, d), F32),
        compiler_params=_params(("arbitrary",)),
        name="out_proj",
    )(y_b, x, w)


def _ret_proj_kernel(x_ref, gn_ref, w_ref, inv_ref, q_ref, kt_ref, v_ref, g_ref,
                     *, tm, t_prompt, pos0, past, n_new, n_heads, qk_dim, v_width):
    i = pl.program_id(0)
    h = _rmsnorm_rows(x_ref[...], gn_ref[...]).astype(BF16)
    row = i * tm + lax.broadcasted_iota(jnp.int32, (tm, 1), 0)
    pos = jnp.where(row < t_prompt, row - pos0, past + lax.rem(row - t_prompt, n_new))
    ang = pos.astype(F32) * inv_ref[...]
    cos = jnp.cos(ang)
    sin = jnp.sin(ang)
    half = qk_dim // 2
    qk_width = n_heads * qk_dim

    def rotary(z):
        parts = []
        for hh in range(n_heads):
            z1 = z[:, hh * qk_dim:hh * qk_dim + half]
            z2 = z[:, hh * qk_dim + half:(hh + 1) * qk_dim]
            parts += [z1 * cos - z2 * sin, z1 * sin + z2 * cos]
        return jnp.concatenate(parts, axis=1)

    q = rotary(_dot(h, w_ref[:, 0:qk_width])) * (qk_dim ** -0.5)
    q_ref[...] = q.astype(BF16)
    kt_ref[...] = rotary(_dot(h, w_ref[:, qk_width:2 * qk_width])).T
    v_ref[...] = _dot(h, w_ref[:, 2 * qk_width:2 * qk_width + v_width]).astype(BF16)
    g_ref[...] = _dot(h, w_ref[:, 2 * qk_width + v_width:])


def _ret_proj(x, gn, w_in, inv_freq, *, t_prompt, pos0, past, n_new, n_heads, qk_dim, v_width):
    t_all, d = x.shape
    tm = PROJ_TILE
    qk_width = n_heads * qk_dim
    row = lambda n: pl.BlockSpec((tm, n), lambda i: (i, 0))
    sds = jax.ShapeDtypeStruct
    return pl.pallas_call(
        functools.partial(_ret_proj_kernel, tm=tm, t_prompt=t_prompt, pos0=pos0, past=past, n_new=n_new,
                          n_heads=n_heads, qk_dim=qk_dim, v_width=v_width),
        grid=(t_all // tm,),
        in_specs=[row(d), _const_spec((1, d)), _const_spec(w_in.shape), _const_spec(inv_freq.shape)],
        out_specs=[row(qk_width), pl.BlockSpec((qk_width, tm), lambda i: (0, i)), row(v_width), row(v_width)],
        out_shape=[sds((t_all, qk_width), BF16), sds((qk_width, t_all), F32),
                   sds((t_all, v_width), BF16), sds((t_all, v_width), F32)],
        compiler_params=_params(("arbitrary",)),
        name="ret_proj",
    )(x, gn, w_in, inv_freq)


def _group_norm_gate(o, g, gn):
    mu = jnp.mean(o, axis=-1, keepdims=True)
    var = jnp.mean(jnp.square(o - mu), axis=-1, keepdims=True)
    y = (o - mu) * lax.rsqrt(var + EPS) * gn
    return (g * jax.nn.sigmoid(g) * y).astype(BF16)


def _ret_prompt_kernel(logg_ref, q_ref, kt_ref, v_ref, g_ref, gn_ref, y_ref, st_ref, s_ref, *, tb):
    hh = pl.program_id(0)
    blk = pl.program_id(1)
    lg = jnp.full((1, 1), logg_ref[hh], F32)

    @pl.when(blk == 0)
    def _():
        s_ref[...] = jnp.zeros_like(s_ref)

    r = lax.broadcasted_iota(jnp.int32, (tb, tb), 0)
    c = lax.broadcasted_iota(jnp.int32, (tb, tb), 1)
    decay = jnp.exp(lg * jnp.abs(r - c).astype(F32))
    dmat = jnp.where(c // CHUNK <= r // CHUNK, decay, 0.0)

    q = q_ref[...]
    kt = kt_ref[...]
    v = v_ref[...]
    s = _dot(q, kt.astype(BF16)) * dmat
    state = s_ref[...]
    q_dec = jnp.exp(lg * (lax.broadcasted_iota(jnp.int32, (tb, 1), 0) + 1).astype(F32))
    o = _dot(s.astype(BF16), v) + _dot(q, state.astype(BF16)) * q_dec
    k_dec = jnp.exp(lg * (tb - 1 - lax.broadcasted_iota(jnp.int32, (1, tb), 1)).astype(F32))
    new_state = state * jnp.exp(lg * tb) + _dot((kt * k_dec).astype(BF16), v)
    s_ref[...] = new_state
    y_ref[...] = _group_norm_gate(o, g_ref[...], gn_ref[...])

    @pl.when(blk == pl.num_programs(1) - 1)
    def _():
        st_ref[0] = new_state


def _ret_prompt(logg, q_b, k_t, v_b, g, gn, *, t_prompt, n_heads):
    t_all, qk_width = q_b.shape
    v_width = v_b.shape[1]
    qk_dim, v_dim = qk_width // n_heads, v_width // n_heads
    tb = RET_BLOCK
    return pl.pallas_call(
        functools.partial(_ret_prompt_kernel, tb=tb),
        grid=(n_heads, t_prompt // tb),
        in_specs=[pl.BlockSpec(memory_space=pltpu.SMEM),
                  pl.BlockSpec((tb, qk_dim), lambda h, i: (i, h)),
                  pl.BlockSpec((qk_dim, tb), lambda h, i: (h, i)),
                  pl.BlockSpec((tb, v_dim), lambda h, i: (i, h)),
                  pl.BlockSpec((tb, v_dim), lambda h, i: (i, h)),
                  pl.BlockSpec((1, v_dim), lambda h, i: (0, h))],
        out_specs=[pl.BlockSpec((tb, v_dim), lambda h, i: (i, h)),
                   pl.BlockSpec((1, qk_dim, v_dim), lambda h, i: (h, 0, 0))],
        out_shape=[jax.ShapeDtypeStruct((t_all, v_width), BF16),
                   jax.ShapeDtypeStruct((n_heads, qk_dim, v_dim), F32)],
        scratch_shapes=[pltpu.VMEM((qk_dim, v_dim), F32)],
        compiler_params=_params(("arbitrary", "arbitrary")),
        name="ret_prompt",
    )(logg, q_b, k_t, v_b, g, gn)


def _ret_sample_kernel(logg_ref, prev_ref, q_ref, kt_ref, v_ref, g_ref, gn_ref, st_ref, y_ref, nst_ref, *, n_new):
    del prev_ref
    hh = pl.program_id(0)
    b = pl.program_id(1)
    lg = jnp.full((1, 1), logg_ref[hh], F32)
    n_rows = kt_ref.shape[1]

    @pl.when(b == 0)
    def _():
        y_ref[...] = jnp.zeros_like(y_ref)

    r0 = pl.multiple_of(b * n_new, n_new)
    q = q_ref[pl.ds(r0, n_new), :]
    kt = kt_ref[...]
    v = v_ref[...]
    qrow = lax.broadcasted_iota(jnp.int32, (n_new, n_rows), 0)
    col = lax.broadcasted_iota(jnp.int32, (n_new, n_rows), 1) - b * n_new
    own = jnp.logical_and(col >= 0, col < n_new)
    dmat = jnp.where(own, jnp.exp(lg * jnp.abs(qrow - col).astype(F32)), 0.0)
    s = _dot(q, kt.astype(BF16)) * dmat
    state = st_ref[0, 0]
    q_dec = jnp.exp(lg * (lax.broadcasted_iota(jnp.int32, (n_new, 1), 0) + 1).astype(F32))
    o = _dot(s.astype(BF16), v) + _dot(q, state.astype(BF16)) * q_dec
    col1 = lax.broadcasted_iota(jnp.int32, (1, n_rows), 1) - b * n_new
    own1 = jnp.logical_and(col1 >= 0, col1 < n_new)
    k_dec = jnp.where(own1, jnp.exp(lg * (n_new - 1 - col1).astype(F32)), 0.0)
    nst_ref[0, 0] = state * jnp.exp(lg * n_new) + _dot((kt * k_dec).astype(BF16), v)
    y_ref[pl.ds(r0, n_new), :] = _group_norm_gate(o, g_ref[pl.ds(r0, n_new), :], gn_ref[...])


def _ret_sample(logg, y_prev, q_b, k_t, v_b, g, gn, state, *, t_prompt, n_new):
    t_all, qk_width = q_b.shape
    v_width = v_b.shape[1]
    n_streams, n_heads, qk_dim, v_dim = state.shape
    n_rows = n_streams * n_new
    tail = t_all - t_prompt
    blk = t_prompt // n_rows
    return pl.pallas_call(
        functools.partial(_ret_sample_kernel, n_new=n_new),
        grid=(n_heads, n_streams),
        in_specs=[pl.BlockSpec(memory_space=pltpu.SMEM),
                  pl.BlockSpec(memory_space=pl.ANY),
                  pl.BlockSpec((n_rows, qk_dim), lambda h, b: (blk, h)),
                  pl.BlockSpec((qk_dim, n_rows), lambda h, b: (h, blk)),
                  pl.BlockSpec((n_rows, v_dim), lambda h, b: (blk, h)),
                  pl.BlockSpec((n_rows, v_dim), lambda h, b: (blk, h)),
                  pl.BlockSpec((1, v_dim), lambda h, b: (0, h)),
                  pl.BlockSpec((1, 1, qk_dim, v_dim), lambda h, b: (b, h, 0, 0))],
        out_specs=[pl.BlockSpec((tail, v_dim), lambda h, b: (t_prompt // tail, h)),
                   pl.BlockSpec((1, 1, qk_dim, v_dim), lambda h, b: (b, h, 0, 0))],
        out_shape=[jax.ShapeDtypeStruct((t_all, v_width), BF16),
                   jax.ShapeDtypeStruct(state.shape, F32)],
        input_output_aliases={1: 0},
        compiler_params=_params(("arbitrary", "arbitrary")),
        name="ret_sample",
    )(logg, y_prev, q_b, k_t, v_b, g, gn, state)


def _top_values(s, k, with_rank):
    work = s
    vals = []
    rank = jnp.full(s.shape, NO_RANK, F32) if with_rank else None
    for r in range(k):
        m = jnp.max(work, axis=0, keepdims=True)
        vals.append(m)
        hit = work == m
        if with_rank:
            rank = jnp.where(hit, float(r), rank)
        work = jnp.where(hit, -jnp.inf, work)
    return vals, rank


def _peer_kernel(x_ref, gn_ref, wq_ref, sk_ref, u_ref, vt_ref, o_ref,
                 hb_ref, e0_ref, cnt_ref, e1_ref, rank_ref, w_ref, y_ref, *, n_heads, n_keys, topk):
    c = pl.program_id(1)
    ce = u_ref.shape[0]
    rows_per_chunk = ce // n_keys

    @pl.when(c == 0)
    def _():
        hb = _rmsnorm_rows(x_ref[...], gn_ref[...]).astype(BF16)
        hb_ref[...] = hb
        y_ref[...] = jnp.zeros_like(y_ref)
        for h in range(n_heads):
            qh = _dot_nt(wq_ref[h * 2 * n_keys:(h + 1) * 2 * n_keys, :], hb)
            s0 = _dot(sk_ref[2 * h], qh[0:n_keys].astype(BF16))
            s1 = _dot(sk_ref[2 * h + 1], qh[n_keys:2 * n_keys].astype(BF16))
            a, _ = _top_values(s0, topk, False)
            b, rank1 = _top_values(s1, topk, True)
            half = topk // 2
            b_all = jnp.concatenate(b, axis=0)
            b_half = jnp.concatenate(b[:half], axis=0)
            cand = jnp.concatenate(
                [a[0] + b_all] + [a[i] + b_half for i in range(1, half)] + [jnp.concatenate(a[half:], axis=0) + b[0]],
                axis=0)
            f, _ = _top_values(cand, topk, False)
            tau = f[topk - 1]
            z = sum(jnp.exp(fk - f[0]) for fk in f)
            cnt = sum(jnp.where(s0 + bj >= tau, 1.0, 0.0) for bj in b)
            cnt_ref[h] = jnp.where(s0 >= a[topk - 1], cnt, 0.0)
            e0_ref[h] = jnp.exp(s0 - a[0])
            e1_ref[h] = jnp.exp(s1 - b[0]) / z
            rank_ref[h] = rank1

    act = _dot_nt(u_ref[...], hb_ref[...])
    for il in range(rows_per_chunk):
        i = c * rows_per_chunk + il
        gate = None
        for h in range(n_heads):
            n_sel = cnt_ref[h, pl.ds(i, 1), :]
            term = jnp.where(rank_ref[h] < n_sel, e1_ref[h], 0.0) * e0_ref[h, pl.ds(i, 1), :]
            gate = term if gate is None else gate + term
        rows = slice(il * n_keys, (il + 1) * n_keys)
        w_ref[rows, :] = (gate * _gelu(act[rows, :])).astype(BF16)
    y_ref[...] += _dot(vt_ref[...], w_ref[...])

    @pl.when(c == pl.num_programs(1) - 1)
    def _():
        o_ref[...] = x_ref[...] + y_ref[...].T


def _peer(x, gn, wq_t, subkeys, u, v_t, *, n_heads, n_keys):
    t_all, d = x.shape
    n_experts = u.shape[0]
    tm = ROW_TILE
    ce = PEER_EXPERT_CHUNK
    head_scratch = pltpu.VMEM((n_heads, n_keys, tm), F32)
    return pl.pallas_call(
        functools.partial(_peer_kernel, n_heads=n_heads, n_keys=n_keys, topk=PEER_TOPK),
        grid=(t_all // tm, n_experts // ce),
        in_specs=[pl.BlockSpec((tm, d), lambda i, c: (i, 0)), _const_spec((1, d)), _const_spec(wq_t.shape),
                  _const_spec(subkeys.shape),
                  pl.BlockSpec((ce, d), lambda i, c: (c, 0)),
                  pl.BlockSpec((d, ce), lambda i, c: (0, c))],
        out_specs=pl.BlockSpec((tm, d), lambda i, c: (i, 0)),
        out_shape=jax.ShapeDtypeStruct((t_all, d), F32),
        scratch_shapes=[pltpu.VMEM((tm, d), BF16), head_scratch, head_scratch, head_scratch, head_scratch,
                        pltpu.VMEM((ce, tm), BF16), pltpu.VMEM((d, tm), F32)],
        compiler_params=_params(("arbitrary", "arbitrary")),
        name="peer",
    )(x, gn, wq_t, subkeys, u, v_t)


def kernel(x_prompt, x_sample, cache_fox_k, cache_fox_v, cache_fox_lf, state_ret, meta_tokens, norm_mix, norm_ffn,
           fox_w_in, fox_b_f, fox_q_norm, fox_k_norm, fox_w_out, ret_w_in, ret_gn, ret_w_out,
           peer_w_q, peer_subkeys, peer_u, peer_v):
    batch, seq, d = x_prompt.shape
    n_streams, n_new, _ = x_sample.shape
    n_meta = meta_tokens.shape[0]
    assert batch == 1 and n_meta == N_META
    depth = norm_mix.shape[0]
    n_fox, _, past, fox_heads, fox_hd = cache_fox_k.shape
    fox_width = fox_heads * fox_hd
    n_ret, _, ret_heads, ret_qk, ret_v = state_ret.shape
    ret_v_width = ret_heads * ret_v
    peer_heads, _, n_keys, peer_half = peer_subkeys.shape[1:]
    assert n_keys == LANES and peer_half == LANES and fox_width == d

    length = n_meta + seq
    front = (-length) % ROW_TILE
    t_prompt = front + length
    n_rows = n_streams * n_new
    tail = ROW_TILE
    assert n_rows <= tail
    t_all = t_prompt + tail

    x = jnp.concatenate([jnp.zeros((front, d), F32), meta_tokens.astype(F32), x_prompt[0],
                         x_sample.reshape(n_rows, d), jnp.zeros((tail - n_rows, d), F32)], axis=0)

    half = ret_qk // 2
    inv_freq = (ROPE_BASE ** (-jnp.arange(half, dtype=F32) / half)).reshape(1, half)
    logg = jnp.log(1.0 - 2.0 ** (-5.0 - jnp.arange(ret_heads, dtype=F32)))

    kp, vp, lfp, ks, vs, lfs, srp, srs = [], [], [], [], [], [], [], []
    for layer in range(depth):
        j = layer // 2
        gn_mix = norm_mix[layer].reshape(1, d)
        if layer % 2 == 0:
            w_in = fox_w_in[j]
            w_main = w_in[:, :4 * fox_width].astype(BF16)
            w_f = jnp.pad(w_in[:, 4 * fox_width:], ((0, 0), (0, LANES - fox_heads))).astype(BF16)
            b_f = jnp.pad(fox_b_f[j], (0, LANES - fox_heads)).reshape(1, LANES)
            gq = jnp.tile(fox_q_norm[j], fox_heads).reshape(1, fox_width)
            gk = jnp.tile(fox_k_norm[j], fox_heads).reshape(1, fox_width)
            q_b, k_f, k_b, v_f, v_b, og, lf, csum = _fox_proj(x, gn_mix, w_main, w_f, b_f, gq, gk,
                                                              front=front, head_dim=fox_hd)
            c3 = csum[:t_prompt, :fox_heads].T.reshape(fox_heads, t_prompt // ROW_TILE, ROW_TILE)
            o_b = _fox_attn_prompt(q_b, k_b, v_b, c3, og, t_prompt=t_prompt, head_dim=fox_hd)
            lf_new_t = lf[t_prompt:t_prompt + n_rows, :fox_heads].T
            o_b = _fox_attn_sample(o_b, q_b, k_b, v_b, og, lf_new_t,
                                   cache_fox_k[j].reshape(n_streams, past, fox_width),
                                   cache_fox_v[j].reshape(n_streams, past, fox_width),
                                   cache_fox_lf[j].transpose(0, 2, 1),
                                   t_prompt=t_prompt, n_new=n_new, head_dim=fox_hd)
            x = _out_proj(o_b, x, fox_w_out[j].astype(BF16))
            kp.append(k_f[front:t_prompt].reshape(1, length, fox_heads, fox_hd))
            vp.append(v_f[front:t_prompt].reshape(1, length, fox_heads, fox_hd))
            lfp.append(lf[front:t_prompt, :fox_heads].reshape(1, length, fox_heads))
            ks.append(k_f[t_prompt:t_prompt + n_rows].reshape(n_streams, n_new, fox_heads, fox_hd))
            vs.append(v_f[t_prompt:t_prompt + n_rows].reshape(n_streams, n_new, fox_heads, fox_hd))
            lfs.append(lf[t_prompt:t_prompt + n_rows, :fox_heads].reshape(n_streams, n_new, fox_heads))
        else:
            q_b, k_t, v_b, g = _ret_proj(x, gn_mix, ret_w_in[j].astype(BF16), inv_freq,
                                         t_prompt=t_prompt, pos0=front + n_meta, past=past, n_new=n_new,
                                         n_heads=ret_heads, qk_dim=ret_qk, v_width=ret_v_width)
            gn_ret = ret_gn[j].reshape(1, ret_v_width)
            y_b, st_p = _ret_prompt(logg, q_b, k_t, v_b, g, gn_ret, t_prompt=t_prompt, n_heads=ret_heads)
            y_b, st_s = _ret_sample(logg, y_b, q_b, k_t, v_b, g, gn_ret, state_ret[j],
                                    t_prompt=t_prompt, n_new=n_new)
            x = _out_proj(y_b, x, ret_w_out[j].astype(BF16))
            srp.append(st_p[None])
            srs.append(st_s)
        x = _peer(x, norm_ffn[layer].reshape(1, d), peer_w_q[layer].T.astype(BF16),
                  peer_subkeys[layer].reshape(2 * peer_heads, n_keys, peer_half).astype(BF16),
                  peer_u[layer].astype(BF16), peer_v[layer].T.astype(BF16),
                  n_heads=peer_heads, n_keys=n_keys)

    y_prompt = x[front + n_meta:t_prompt][None]
    y_sample = x[t_prompt:t_prompt + n_rows].reshape(n_streams, n_new, d)
    return (y_prompt, y_sample, jnp.stack(kp), jnp.stack(vp), jnp.stack(lfp), jnp.stack(srp),
            jnp.stack(ks), jnp.stack(vs), jnp.stack(lfs), jnp.stack(srs))
```

```python
import functools
import math

import jax
import jax.numpy as jnp
from jax import lax
from jax.experimental import pallas as pl
from jax.experimental.pallas import tpu as pltpu

F32 = jnp.float32
BF16 = jnp.bfloat16

EPS = 1e-6
N_META = 16
CHUNK = 64
ROPE_BASE = 10000.0
PEER_TOPK = 16

LANES = 128
MXU_DIM = 256
VMEM_LIMIT = 56 * 1024 * 1024

ROW_TILE = 512
PROJ_TILE = 256
RET_BLOCK = 256
PEER_EXPERT_CHUNK = 1024
MASKED_KEY = 1e30
NO_RANK = 99.0
LOG2E = math.log2(math.e)

_NT = (((1,), (1,)), ((), ()))


def _const_spec(shape):
    return pl.BlockSpec(shape, lambda *_: (0,) * len(shape), pipeline_mode=pl.Buffered(1))


def _params(semantics):
    return pltpu.CompilerParams(dimension_semantics=semantics, vmem_limit_bytes=VMEM_LIMIT)


def _split3(x):
    a = x.astype(BF16)
    r = x - a.astype(F32)
    b = r.astype(BF16)
    c = (r - b.astype(F32)).astype(BF16)
    return a, b, c


def _dot(a, b):
    return jnp.dot(a, b, preferred_element_type=F32)


def _dot_nt(a, b):
    return lax.dot_general(a, b, _NT, preferred_element_type=F32)


def _rmsnorm_rows(x, g):
    ms = jnp.mean(x * x, axis=-1, keepdims=True)
    return x * lax.rsqrt(ms + EPS) * g


def _log_sigmoid(x):
    return jnp.minimum(x, 0.0) - jnp.log1p(jnp.exp(-jnp.abs(x)))


def _gelu(x):
    return 0.5 * x * (1.0 + lax.erf(x * (2.0 ** -0.5)))


def _fox_proj_kernel(x_ref, gn_ref, w_ref, wf_ref, bf_ref, gq_ref, gk_ref, bd_ref, tril_ref,
                     q_ref, kf_ref, kb_ref, vf_ref, vb_ref, og_ref, lf_ref, c_ref, carry_ref,
                     *, tm, front, head_dim, width):
    i = pl.program_id(0)

    @pl.when(i == 0)
    def _():
        carry_ref[...] = jnp.zeros_like(carry_ref)

    h = _rmsnorm_rows(x_ref[...], gn_ref[...]).astype(BF16)

    def head_norm(z, g):
        zz = (z * z).astype(BF16)
        ss = jnp.concatenate(
            [_dot(zz[:, c * MXU_DIM:(c + 1) * MXU_DIM], bd_ref[...]) for c in range(width // MXU_DIM)], axis=1)
        return z * lax.rsqrt(ss * (1.0 / head_dim) + EPS) * g

    zq = _dot(h, w_ref[:, 0:width])
    q_ref[...] = (head_norm(zq, gq_ref[...]) * (head_dim ** -0.5 * LOG2E)).astype(BF16)
    zk = _dot(h, w_ref[:, width:2 * width])
    kn = head_norm(zk, gk_ref[...])
    kf_ref[...] = kn
    kb_ref[...] = kn.astype(BF16)
    zv = _dot(h, w_ref[:, 2 * width:3 * width])
    vf_ref[...] = zv
    vb_ref[...] = zv.astype(BF16)
    og_ref[...] = _dot(h, w_ref[:, 3 * width:4 * width])

    lf = _log_sigmoid(_dot(h, wf_ref[...]) + bf_ref[...])
    lf_ref[...] = lf
    row = i * tm + lax.broadcasted_iota(jnp.int32, (tm, 1), 0)
    real = row >= front
    l1, l2, l3 = _split3(jnp.where(real, lf, 0.0))
    tril = tril_ref[...]
    c = _dot(tril, l1) + _dot(tril, l2) + _dot(tril, l3) + carry_ref[...]
    carry_ref[...] = c[tm - 1:tm, :]
    c_ref[...] = jnp.where(real, c, MASKED_KEY)


def _fox_proj(x, gn, w_main, w_f, b_f, gq, gk, *, front, head_dim):
    t_all, d = x.shape
    width = w_main.shape[1] // 4
    tm = PROJ_TILE
    blk = MXU_DIM // head_dim
    bd = jnp.kron(jnp.eye(blk, dtype=F32), jnp.ones((head_dim, head_dim), F32)).astype(BF16)
    tril = (lax.broadcasted_iota(jnp.int32, (tm, tm), 0) >= lax.broadcasted_iota(jnp.int32, (tm, tm), 1)).astype(BF16)
    row = lambda n: pl.BlockSpec((tm, n), lambda i: (i, 0))
    sds = jax.ShapeDtypeStruct
    return pl.pallas_call(
        functools.partial(_fox_proj_kernel, tm=tm, front=front, head_dim=head_dim, width=width),
        grid=(t_all // tm,),
        in_specs=[row(d), _const_spec((1, d)), _const_spec(w_main.shape), _const_spec(w_f.shape),
                  _const_spec((1, LANES)), _const_spec((1, width)), _const_spec((1, width)),
                  _const_spec(bd.shape), _const_spec(tril.shape)],
        out_specs=[row(width), row(width), row(width), row(width), row(width), row(width), row(LANES), row(LANES)],
        out_shape=[sds((t_all, width), BF16), sds((t_all, width), F32), sds((t_all, width), BF16),
                   sds((t_all, width), F32), sds((t_all, width), BF16), sds((t_all, width), F32),
                   sds((t_all, LANES), F32), sds((t_all, LANES), F32)],
        scratch_shapes=[pltpu.VMEM((1, LANES), F32)],
        compiler_params=_params(("arbitrary",)),
        name="fox_proj",
    )(x, gn, w_main, w_f, b_f, gq, gk, bd, tril)


def _fox_attn_kernel(q_ref, k_ref, v_ref, c_ref, og_ref, o_ref, qm_ref, m_ref, acc_ref, *, tile, head_dim):
    qi = pl.program_id(1)
    lane = lax.broadcasted_iota(jnp.int32, (1, 2 * head_dim), 1)
    lo = lane < head_dim
    head_lanes = (lo, jnp.logical_not(lo))
    q = q_ref[...]
    for hh in range(2):
        qm_ref[hh] = jnp.where(head_lanes[hh], q, jnp.zeros_like(q))
    m_ref[...] = jnp.full(m_ref.shape, -jnp.inf, F32)
    acc_ref[...] = jnp.zeros(acc_ref.shape, F32)
    one = jnp.ones((), BF16)

    def block(j, diagonal):
        start = pl.multiple_of(j * tile, tile)
        k = k_ref[pl.ds(start, tile), :]
        v = v_ref[pl.ds(start, tile), :]
        for hh in range(2):
            c_last = c_ref[hh, pl.ds(qi, 1), :][:, tile - 1:tile]
            bias = (c_ref[hh, pl.ds(j, 1), :] - c_last) * LOG2E
            u = _dot_nt(qm_ref[hh], k) - bias
            if diagonal:
                r = lax.broadcasted_iota(jnp.int32, (tile, tile), 0)
                cidx = lax.broadcasted_iota(jnp.int32, (tile, tile), 1)
                u = jnp.where(cidx <= r, u, -jnp.inf)
            m_prev = m_ref[hh]
            m_new = jnp.maximum(m_prev, jnp.max(u, axis=1, keepdims=True))
            p = jnp.exp2(u - pltpu.repeat(m_new, tile // LANES, 1)).astype(BF16)
            v_aug = jnp.where(head_lanes[hh], v, one)
            acc_ref[hh] = jnp.exp2(m_prev - m_new) * acc_ref[hh] + _dot(p, v_aug)
            m_ref[hh] = m_new

    def body(j, carry):
        block(j, False)
        return carry

    lax.fori_loop(0, qi, body, 0)
    block(qi, True)

    outs = []
    for hh in range(2):
        a = acc_ref[hh]
        outs.append(a / pltpu.roll(a, head_dim, 1))
    o = jnp.where(lo, outs[0], outs[1])
    o_ref[...] = (o * jax.nn.sigmoid(og_ref[...])).astype(BF16)


def _fox_attn_prompt(q_b, k_b, v_b, c3, og, *, t_prompt, head_dim):
    t_all, width = q_b.shape
    tile = ROW_TILE
    pair = 2 * head_dim
    n_blk = t_prompt // tile
    return pl.pallas_call(
        functools.partial(_fox_attn_kernel, tile=tile, head_dim=head_dim),
        grid=(width // pair, n_blk),
        in_specs=[pl.BlockSpec((tile, pair), lambda p, i: (i, p)),
                  pl.BlockSpec((t_prompt, pair), lambda p, i: (0, p)),
                  pl.BlockSpec((t_prompt, pair), lambda p, i: (0, p)),
                  pl.BlockSpec((2, n_blk, tile), lambda p, i: (p, 0, 0)),
                  pl.BlockSpec((tile, pair), lambda p, i: (i, p))],
        out_specs=pl.BlockSpec((tile, pair), lambda p, i: (i, p)),
        out_shape=jax.ShapeDtypeStruct((t_all, width), BF16),
        scratch_shapes=[pltpu.VMEM((2, tile, pair), BF16), pltpu.VMEM((2, tile, LANES), F32),
                        pltpu.VMEM((2, tile, pair), F32)],
        compiler_params=_params(("arbitrary", "arbitrary")),
        name="fox_attn_prompt",
    )(q_b, k_b, v_b, c3, og)


def _fox_sample_kernel(prev_ref, q_ref, kn_ref, vn_ref, og_ref, lfn_ref, kc_ref, vc_ref, lfc_ref, su_ref, bt_ref,
                       o_ref, *, n_new, head_dim, n_heads):
    del prev_ref
    b = pl.program_id(0)
    pair = 2 * head_dim
    n_rows = kn_ref.shape[0]

    @pl.when(b == 0)
    def _():
        o_ref[...] = jnp.zeros_like(o_ref)

    su = su_ref[...]
    suf = sum(_dot(t, su) for t in _split3(lfc_ref[0])) * LOG2E
    bt = bt_ref[...]
    pre = sum(_dot(t, bt) for t in _split3(lfn_ref[...])) * LOG2E

    r0 = pl.multiple_of(b * n_new, n_new)
    qrow = lax.broadcasted_iota(jnp.int32, (n_new, n_rows), 0)
    col = lax.broadcasted_iota(jnp.int32, (n_new, n_rows), 1)
    visible = jnp.logical_and(col >= b * n_new, col <= b * n_new + qrow)
    lane = lax.broadcasted_iota(jnp.int32, (1, pair), 1)
    lo = lane < head_dim
    head_lanes = (lo, jnp.logical_not(lo))

    for p in range(n_heads // 2):
        cols = slice(p * pair, (p + 1) * pair)
        q = q_ref[pl.ds(r0, n_new), cols]
        k_c = kc_ref[0, :, cols].astype(BF16)
        v_c = vc_ref[0, :, cols].astype(BF16)
        k_n = kn_ref[:, cols]
        v_n = vn_ref[:, cols]
        outs = []
        for hh in range(2):
            head = 2 * p + hh
            qh = jnp.where(head_lanes[hh], q, jnp.zeros_like(q))
            u_c = _dot_nt(qh, k_c) + suf[head:head + 1, :]
            u_n = jnp.where(visible, _dot_nt(qh, k_n) - pre[head:head + 1, :], -jnp.inf)
            m = jnp.maximum(jnp.max(u_c, axis=1, keepdims=True), jnp.max(u_n, axis=1, keepdims=True))
            p_c = jnp.exp2(u_c - m)
            p_n = jnp.exp2(u_n - m)
            den = jnp.sum(p_c, axis=1, keepdims=True) + jnp.sum(p_n, axis=1, keepdims=True)
            outs.append((_dot(p_c.astype(BF16), v_c) + _dot(p_n.astype(BF16), v_n)) / den)
        o = jnp.where(lo, outs[0], outs[1])
        gate = jax.nn.sigmoid(og_ref[pl.ds(r0, n_new), cols])
        o_ref[pl.ds(r0, n_new), cols] = (o * gate).astype(BF16)


def _fox_attn_sample(o_prev, q_b, k_b, v_b, og, lf_new_t, cache_k, cache_v, cache_lf_t, *, t_prompt, n_new, head_dim):
    t_all, width = q_b.shape
    n_streams, past, _ = cache_k.shape
    n_heads = width // head_dim
    n_rows = n_streams * n_new
    assert t_prompt % n_rows == 0 and n_rows % LANES == 0
    su = (lax.broadcasted_iota(jnp.int32, (past, past), 0) > lax.broadcasted_iota(jnp.int32, (past, past), 1)).astype(BF16)
    rr = lax.broadcasted_iota(jnp.int32, (n_rows, n_rows), 0)
    cc = lax.broadcasted_iota(jnp.int32, (n_rows, n_rows), 1)
    bt = jnp.logical_and(rr // n_new == cc // n_new, rr <= cc).astype(BF16)
    rows = pl.BlockSpec((n_rows, width), lambda b: (t_prompt // n_rows, 0), pipeline_mode=pl.Buffered(1))
    tail = t_all - t_prompt
    return pl.pallas_call(
        functools.partial(_fox_sample_kernel, n_new=n_new, head_dim=head_dim, n_heads=n_heads),
        grid=(n_streams,),
        in_specs=[pl.BlockSpec(memory_space=pl.ANY), rows, rows, rows, rows,
                  _const_spec(lf_new_t.shape),
                  pl.BlockSpec((1, past, width), lambda b: (b, 0, 0)),
                  pl.BlockSpec((1, past, width), lambda b: (b, 0, 0)),
                  pl.BlockSpec((1, n_heads, past), lambda b: (b, 0, 0)),
                  _const_spec(su.shape), _const_spec(bt.shape)],
        out_specs=pl.BlockSpec((tail, width), lambda b: (t_prompt // tail, 0)),
        out_shape=jax.ShapeDtypeStruct((t_all, width), BF16),
        input_output_aliases={0: 0},
        compiler_params=_params(("arbitrary",)),
        name="fox_attn_sample",
    )(o_prev, q_b, k_b, v_b, og, lf_new_t, cache_k, cache_v, cache_lf_t, su, bt)


def _out_proj_kernel(y_ref, x_ref, w_ref, o_ref):
    o_ref[...] = x_ref[...] + _dot(y_ref[...], w_ref[...])


def _out_proj(y_b, x, w):
    t_all, d = x.shape
    k = y_b.shape[1]
    tm = ROW_TILE
    return pl.pallas_call(
        _out_proj_kernel,
        grid=(t_all // tm,),
        in_specs=[pl.BlockSpec((tm, k), lambda i: (i, 0)), pl.BlockSpec((tm, d), lambda i: (i, 0)),
                  _const_spec(w.shape)],
        out_specs=pl.BlockSpec((tm, d), lambda i: (i, 0)),
        out_shape=jax.ShapeDtypeStruct((t_all, d), F32),
        compiler_params=_params(("arbitrary",)),
        name="out_proj",
    )(y_b, x, w)


def _ret_proj_kernel(x_ref, gn_ref, w_ref, inv_ref, q_ref, kt_ref, v_ref, g_ref,
                     *, tm, t_prompt, pos0, past, n_new, n_heads, qk_dim, v_width):
    i = pl.program_id(0)
    h = _rmsnorm_rows(x_ref[...], gn_ref[...]).astype(BF16)
    row = i * tm + lax.broadcasted_iota(jnp.int32, (tm, 1), 0)
    pos = jnp.where(row < t_prompt, row - pos0, past + lax.rem(row - t_prompt, n_new))
    ang = pos.astype(F32) * inv_ref[...]
    cos = jnp.cos(ang)
    sin = jnp.sin(ang)
    half = qk_dim // 2
    qk_width = n_heads * qk_dim

    def rotary(z):
        parts = []
        for hh in range(n_heads):
            z1 = z[:, hh * qk_dim:hh * qk_dim + half]
            z2 = z[:, hh * qk_dim + half:(hh + 1) * qk_dim]
            parts += [z1 * cos - z2 * sin, z1 * sin + z2 * cos]
        return jnp.concatenate(parts, axis=1)

    q = rotary(_dot(h, w_ref[:, 0:qk_width])) * (qk_dim ** -0.5)
    q_ref[...] = q.astype(BF16)
    kt_ref[...] = rotary(_dot(h, w_ref[:, qk_width:2 * qk_width])).T
    v_ref[...] = _dot(h, w_ref[:, 2 * qk_width:2 * qk_width + v_width]).astype(BF16)
    g_ref[...] = _dot(h, w_ref[:, 2 * qk_width + v_width:])


def _ret_proj(x, gn, w_in, inv_freq, *, t_prompt, pos0, past, n_new, n_heads, qk_dim, v_width):
    t_all, d = x.shape
    tm = PROJ_TILE
    qk_width = n_heads * qk_dim
    row = lambda n: pl.BlockSpec((tm, n), lambda i: (i, 0))
    sds = jax.ShapeDtypeStruct
    return pl.pallas_call(
        functools.partial(_ret_proj_kernel, tm=tm, t_prompt=t_prompt, pos0=pos0, past=past, n_new=n_new,
                          n_heads=n_heads, qk_dim=qk_dim, v_width=v_width),
        grid=(t_all // tm,),
        in_specs=[row(d), _const_spec((1, d)), _const_spec(w_in.shape), _const_spec(inv_freq.shape)],
        out_specs=[row(qk_width), pl.BlockSpec((qk_width, tm), lambda i: (0, i)), row(v_width), row(v_width)],
        out_shape=[sds((t_all, qk_width), BF16), sds((qk_width, t_all), F32),
                   sds((t_all, v_width), BF16), sds((t_all, v_width), F32)],
        compiler_params=_params(("arbitrary",)),
        name="ret_proj",
    )(x, gn, w_in, inv_freq)


def _group_norm_gate(o, g, gn):
    mu = jnp.mean(o, axis=-1, keepdims=True)
    var = jnp.mean(jnp.square(o - mu), axis=-1, keepdims=True)
    y = (o - mu) * lax.rsqrt(var + EPS) * gn
    return (g * jax.nn.sigmoid(g) * y).astype(BF16)


def _ret_prompt_kernel(logg_ref, q_ref, kt_ref, v_ref, g_ref, gn_ref, y_ref, st_ref, s_ref, *, tb):
    hh = pl.program_id(0)
    blk = pl.program_id(1)
    lg = jnp.full((1, 1), logg_ref[hh], F32)

    @pl.when(blk == 0)
    def _():
        s_ref[...] = jnp.zeros_like(s_ref)

    r = lax.broadcasted_iota(jnp.int32, (tb, tb), 0)
    c = lax.broadcasted_iota(jnp.int32, (tb, tb), 1)
    decay = jnp.exp(lg * jnp.abs(r - c).astype(F32))
    dmat = jnp.where(c // CHUNK <= r // CHUNK, decay, 0.0)

    q = q_ref[...]
    kt = kt_ref[...]
    v = v_ref[...]
    s = _dot(q, kt.astype(BF16)) * dmat
    state = s_ref[...]
    q_dec = jnp.exp(lg * (lax.broadcasted_iota(jnp.int32, (tb, 1), 0) + 1).astype(F32))
    o = _dot(s.astype(BF16), v) + _dot(q, state.astype(BF16)) * q_dec
    k_dec = jnp.exp(lg * (tb - 1 - lax.broadcasted_iota(jnp.int32, (1, tb), 1)).astype(F32))
    new_state = state * jnp.exp(lg * tb) + _dot((kt * k_dec).astype(BF16), v)
    s_ref[...] = new_state
    y_ref[...] = _group_norm_gate(o, g_ref[...], gn_ref[...])

    @pl.when(blk == pl.num_programs(1) - 1)
    def _():
        st_ref[0] = new_state


def _ret_prompt(logg, q_b, k_t, v_b, g, gn, *, t_prompt, n_heads):
    t_all, qk_width = q_b.shape
    v_width = v_b.shape[1]
    qk_dim, v_dim = qk_width // n_heads, v_width // n_heads
    tb = RET_BLOCK
    return pl.pallas_call(
        functools.partial(_ret_prompt_kernel, tb=tb),
        grid=(n_heads, t_prompt // tb),
        in_specs=[pl.BlockSpec(memory_space=pltpu.SMEM),
                  pl.BlockSpec((tb, qk_dim), lambda h, i: (i, h)),
                  pl.BlockSpec((qk_dim, tb), lambda h, i: (h, i)),
                  pl.BlockSpec((tb, v_dim), lambda h, i: (i, h)),
                  pl.BlockSpec((tb, v_dim), lambda h, i: (i, h)),
                  pl.BlockSpec((1, v_dim), lambda h, i: (0, h))],
        out_specs=[pl.BlockSpec((tb, v_dim), lambda h, i: (i, h)),
                   pl.BlockSpec((1, qk_dim, v_dim), lambda h, i: (h, 0, 0))],
        out_shape=[jax.ShapeDtypeStruct((t_all, v_width), BF16),
                   jax.ShapeDtypeStruct((n_heads, qk_dim, v_dim), F32)],
        scratch_shapes=[pltpu.VMEM((qk_dim, v_dim), F32)],
        compiler_params=_params(("arbitrary", "arbitrary")),
        name="ret_prompt",
    )(logg, q_b, k_t, v_b, g, gn)


def _ret_sample_kernel(logg_ref, prev_ref, q_ref, kt_ref, v_ref, g_ref, gn_ref, st_ref, y_ref, nst_ref, *, n_new):
    del prev_ref
    hh = pl.program_id(0)
    b = pl.program_id(1)
    lg = jnp.full((1, 1), logg_ref[hh], F32)
    n_rows = kt_ref.shape[1]

    @pl.when(b == 0)
    def _():
        y_ref[...] = jnp.zeros_like(y_ref)

    r0 = pl.multiple_of(b * n_new, n_new)
    q = q_ref[pl.ds(r0, n_new), :]
    kt = kt_ref[...]
    v = v_ref[...]
    qrow = lax.broadcasted_iota(jnp.int32, (n_new, n_rows), 0)
    col = lax.broadcasted_iota(jnp.int32, (n_new, n_rows), 1) - b * n_new
    own = jnp.logical_and(col >= 0, col < n_new)
    dmat = jnp.where(own, jnp.exp(lg * jnp.abs(qrow - col).astype(F32)), 0.0)
    s = _dot(q, kt.astype(BF16)) * dmat
    state = st_ref[0, 0]
    q_dec = jnp.exp(lg * (lax.broadcasted_iota(jnp.int32, (n_new, 1), 0) + 1).astype(F32))
    o = _dot(s.astype(BF16), v) + _dot(q, state.astype(BF16)) * q_dec
    col1 = lax.broadcasted_iota(jnp.int32, (1, n_rows), 1) - b * n_new
    own1 = jnp.logical_and(col1 >= 0, col1 < n_new)
    k_dec = jnp.where(own1, jnp.exp(lg * (n_new - 1 - col1).astype(F32)), 0.0)
    nst_ref[0, 0] = state * jnp.exp(lg * n_new) + _dot((kt * k_dec).astype(BF16), v)
    y_ref[pl.ds(r0, n_new), :] = _group_norm_gate(o, g_ref[pl.ds(r0, n_new), :], gn_ref[...])


def _ret_sample(logg, y_prev, q_b, k_t, v_b, g, gn, state, *, t_prompt, n_new):
    t_all, qk_width = q_b.shape
    v_width = v_b.shape[1]
    n_streams, n_heads, qk_dim, v_dim = state.shape
    n_rows = n_streams * n_new
    tail = t_all - t_prompt
    blk = t_prompt // n_rows
    return pl.pallas_call(
        functools.partial(_ret_sample_kernel, n_new=n_new),
        grid=(n_heads, n_streams),
        in_specs=[pl.BlockSpec(memory_space=pltpu.SMEM),
                  pl.BlockSpec(memory_space=pl.ANY),
                  pl.BlockSpec((n_rows, qk_dim), lambda h, b: (blk, h)),
                  pl.BlockSpec((qk_dim, n_rows), lambda h, b: (h, blk)),
                  pl.BlockSpec((n_rows, v_dim), lambda h, b: (blk, h)),
                  pl.BlockSpec((n_rows, v_dim), lambda h, b: (blk, h)),
                  pl.BlockSpec((1, v_dim), lambda h, b: (0, h)),
                  pl.BlockSpec((1, 1, qk_dim, v_dim), lambda h, b: (b, h, 0, 0))],
        out_specs=[pl.BlockSpec((tail, v_dim), lambda h, b: (t_prompt // tail, h)),
                   pl.BlockSpec((1, 1, qk_dim, v_dim), lambda h, b: (b, h, 0, 0))],
        out_shape=[jax.ShapeDtypeStruct((t_all, v_width), BF16),
                   jax.ShapeDtypeStruct(state.shape, F32)],
        input_output_aliases={1: 0},
        compiler_params=_params(("arbitrary", "arbitrary")),
        name="ret_sample",
    )(logg, y_prev, q_b, k_t, v_b, g, gn, state)


def _top_values(s, k, with_rank):
    work = s
    vals = []
    rank = jnp.full(s.shape, NO_RANK, F32) if with_rank else None
    for r in range(k):
        m = jnp.max(work, axis=0, keepdims=True)
        vals.append(m)
        hit = work == m
        if with_rank:
            rank = jnp.where(hit, float(r), rank)
        work = jnp.where(hit, -jnp.inf, work)
    return vals, rank


def _peer_kernel(x_ref, gn_ref, wq_ref, sk_ref, u_ref, vt_ref, o_ref,
                 hb_ref, e0_ref, cnt_ref, e1_ref, rank_ref, w_ref, y_ref, *, n_heads, n_keys, topk):
    c = pl.program_id(1)
    ce = u_ref.shape[0]
    rows_per_chunk = ce // n_keys

    @pl.when(c == 0)
    def _():
        hb = _rmsnorm_rows(x_ref[...], gn_ref[...]).astype(BF16)
        hb_ref[...] = hb
        y_ref[...] = jnp.zeros_like(y_ref)
        for h in range(n_heads):
            qh = _dot_nt(wq_ref[h * 2 * n_keys:(h + 1) * 2 * n_keys, :], hb)
            s0 = _dot(sk_ref[2 * h], qh[0:n_keys].astype(BF16))
            s1 = _dot(sk_ref[2 * h + 1], qh[n_keys:2 * n_keys].astype(BF16))
            a, _ = _top_values(s0, topk, False)
            b, rank1 = _top_values(s1, topk, True)
            half = topk // 2
            b_all = jnp.concatenate(b, axis=0)
            b_half = jnp.concatenate(b[:half], axis=0)
            cand = jnp.concatenate(
                [a[0] + b_all] + [a[i] + b_half for i in range(1, half)] + [jnp.concatenate(a[half:], axis=0) + b[0]],
                axis=0)
            f, _ = _top_values(cand, topk, False)
            tau = f[topk - 1]
            z = sum(jnp.exp(fk - f[0]) for fk in f)
            cnt = sum(jnp.where(s0 + bj >= tau, 1.0, 0.0) for bj in b)
            cnt_ref[h] = jnp.where(s0 >= a[topk - 1], cnt, 0.0)
            e0_ref[h] = jnp.exp(s0 - a[0])
            e1_ref[h] = jnp.exp(s1 - b[0]) / z
            rank_ref[h] = rank1

    act = _dot_nt(u_ref[...], hb_ref[...])
    for il in range(rows_per_chunk):
        i = c * rows_per_chunk + il
        gate = None
        for h in range(n_heads):
            n_sel = cnt_ref[h, pl.ds(i, 1), :]
            term = jnp.where(rank_ref[h] < n_sel, e1_ref[h], 0.0) * e0_ref[h, pl.ds(i, 1), :]
            gate = term if gate is None else gate + term
        rows = slice(il * n_keys, (il + 1) * n_keys)
        w_ref[rows, :] = (gate * _gelu(act[rows, :])).astype(BF16)
    y_ref[...] += _dot(vt_ref[...], w_ref[...])

    @pl.when(c == pl.num_programs(1) - 1)
    def _():
        o_ref[...] = x_ref[...] + y_ref[...].T


def _peer(x, gn, wq_t, subkeys, u, v_t, *, n_heads, n_keys):
    t_all, d = x.shape
    n_experts = u.shape[0]
    tm = ROW_TILE
    ce = PEER_EXPERT_CHUNK
    head_scratch = pltpu.VMEM((n_heads, n_keys, tm), F32)
    return pl.pallas_call(
        functools.partial(_peer_kernel, n_heads=n_heads, n_keys=n_keys, topk=PEER_TOPK),
        grid=(t_all // tm, n_experts // ce),
        in_specs=[pl.BlockSpec((tm, d), lambda i, c: (i, 0)), _const_spec((1, d)), _const_spec(wq_t.shape),
                  _const_spec(subkeys.shape),
                  pl.BlockSpec((ce, d), lambda i, c: (c, 0)),
                  pl.BlockSpec((d, ce), lambda i, c: (0, c))],
        out_specs=pl.BlockSpec((tm, d), lambda i, c: (i, 0)),
        out_shape=jax.ShapeDtypeStruct((t_all, d), F32),
        scratch_shapes=[pltpu.VMEM((tm, d), BF16), head_scratch, head_scratch, head_scratch, head_scratch,
                        pltpu.VMEM((ce, tm), BF16), pltpu.VMEM((d, tm), F32)],
        compiler_params=_params(("arbitrary", "arbitrary")),
        name="peer",
    )(x, gn, wq_t, subkeys, u, v_t)


def kernel(x_prompt, x_sample, cache_fox_k, cache_fox_v, cache_fox_lf, state_ret, meta_tokens, norm_mix, norm_ffn,
           fox_w_in, fox_b_f, fox_q_norm, fox_k_norm, fox_w_out, ret_w_in, ret_gn, ret_w_out,
           peer_w_q, peer_subkeys, peer_u, peer_v):
    batch, seq, d = x_prompt.shape
    n_streams, n_new, _ = x_sample.shape
    n_meta = meta_tokens.shape[0]
    assert batch == 1 and n_meta == N_META
    depth = norm_mix.shape[0]
    n_fox, _, past, fox_heads, fox_hd = cache_fox_k.shape
    fox_width = fox_heads * fox_hd
    n_ret, _, ret_heads, ret_qk, ret_v = state_ret.shape
    ret_v_width = ret_heads * ret_v
    peer_heads, _, n_keys, peer_half = peer_subkeys.shape[1:]
    assert n_keys == LANES and peer_half == LANES and fox_width == d

    length = n_meta + seq
    front = (-length) % ROW_TILE
    t_prompt = front + length
    n_rows = n_streams * n_new
    tail = ROW_TILE
    assert n_rows <= tail
    t_all = t_prompt + tail

    x = jnp.concatenate([jnp.zeros((front, d), F32), meta_tokens.astype(F32), x_prompt[0],
                         x_sample.reshape(n_rows, d), jnp.zeros((tail - n_rows, d), F32)], axis=0)

    half = ret_qk // 2
    inv_freq = (ROPE_BASE ** (-jnp.arange(half, dtype=F32) / half)).reshape(1, half)
    logg = jnp.log(1.0 - 2.0 ** (-5.0 - jnp.arange(ret_heads, dtype=F32)))

    kp, vp, lfp, ks, vs, lfs, srp, srs = [], [], [], [], [], [], [], []
    for layer in range(depth):
        j = layer // 2
        gn_mix = norm_mix[layer].reshape(1, d)
        if layer % 2 == 0:
            w_in = fox_w_in[j]
            w_main = w_in[:, :4 * fox_width].astype(BF16)
            w_f = jnp.pad(w_in[:, 4 * fox_width:], ((0, 0), (0, LANES - fox_heads))).astype(BF16)
            b_f = jnp.pad(fox_b_f[j], (0, LANES - fox_heads)).reshape(1, LANES)
            gq = jnp.tile(fox_q_norm[j], fox_heads).reshape(1, fox_width)
            gk = jnp.tile(fox_k_norm[j], fox_heads).reshape(1, fox_width)
            q_b, k_f, k_b, v_f, v_b, og, lf, csum = _fox_proj(x, gn_mix, w_main, w_f, b_f, gq, gk,
                                                              front=front, head_dim=fox_hd)
            c3 = csum[:t_prompt, :fox_heads].T.reshape(fox_heads, t_prompt // ROW_TILE, ROW_TILE)
            o_b = _fox_attn_prompt(q_b, k_b, v_b, c3, og, t_prompt=t_prompt, head_dim=fox_hd)
            lf_new_t = lf[t_prompt:t_prompt + n_rows, :fox_heads].T
            o_b = _fox_attn_sample(o_b, q_b, k_b, v_b, og, lf_new_t,
                                   cache_fox_k[j].reshape(n_streams, past, fox_width),
                                   cache_fox_v[j].reshape(n_streams, past, fox_width),
                                   cache_fox_lf[j].transpose(0, 2, 1),
                                   t_prompt=t_prompt, n_new=n_new, head_dim=fox_hd)
            x = _out_proj(o_b, x, fox_w_out[j].astype(BF16))
            kp.append(k_f[front:t_prompt].reshape(1, length, fox_heads, fox_hd))
            vp.append(v_f[front:t_prompt].reshape(1, length, fox_heads, fox_hd))
            lfp.append(lf[front:t_prompt, :fox_heads].reshape(1, length, fox_heads))
            ks.append(k_f[t_prompt:t_prompt + n_rows].reshape(n_streams, n_new, fox_heads, fox_hd))
            vs.append(v_f[t_prompt:t_prompt + n_rows].reshape(n_streams, n_new, fox_heads, fox_hd))
            lfs.append(lf[t_prompt:t_prompt + n_rows, :fox_heads].reshape(n_streams, n_new, fox_heads))
        else:
            q_b, k_t, v_b, g = _ret_proj(x, gn_mix, ret_w_in[j].astype(BF16), inv_freq,
                                         t_prompt=t_prompt, pos0=front + n_meta, past=past, n_new=n_new,
                                         n_heads=ret_heads, qk_dim=ret_qk, v_width=ret_v_width)
            gn_ret = ret_gn[j].reshape(1, ret_v_width)
            y_b, st_p = _ret_prompt(logg, q_b, k_t, v_b, g, gn_ret, t_prompt=t_prompt, n_heads=ret_heads)
            y_b, st_s = _ret_sample(logg, y_b, q_b, k_t, v_b, g, gn_ret, state_ret[j],
                                    t_prompt=t_prompt, n_new=n_new)
            x = _out_proj(y_b, x, ret_w_out[j].astype(BF16))
            srp.append(st_p[None])
            srs.append(st_s)
        x = _peer(x, norm_ffn[layer].reshape(1, d), peer_w_q[layer].T.astype(BF16),
                  peer_subkeys[layer].reshape(2 * peer_heads, n_keys, peer_half).astype(BF16),
                  peer_u[layer].astype(BF16), peer_v[layer].T.astype(BF16),
                  n_heads=peer_heads, n_keys=n_keys)

    y_prompt = x[front + n_meta:t_prompt][None]
    y_sample = x[t_prompt:t_prompt + n_rows].reshape(n_streams, n_new, d)
    return (y_prompt, y_sample, jnp.stack(kp), jnp.stack(vp), jnp.stack(lfp), jnp.stack(srp),
            jnp.stack(ks), jnp.stack(vs), jnp.stack(lfs), jnp.stack(srs))
```

```python
import functools
import math

import jax
import jax.numpy as jnp
from jax import lax
from jax.experimental import pallas as pl
from jax.experimental.pallas import tpu as pltpu

F32 = jnp.float32
BF16 = jnp.bfloat16

EPS = 1e-6
N_META = 16
CHUNK = 64
ROPE_BASE = 10000.0
PEER_TOPK = 16

LANES = 128
MXU_DIM = 256
VMEM_LIMIT = 56 * 1024 * 1024

ROW_TILE = 512
PROJ_TILE = 256
RET_BLOCK = 256
PEER_EXPERT_CHUNK = 1024
MASKED_KEY = 1e30
NO_RANK = 99.0
LOG2E = math.log2(math.e)

_NT = (((1,), (1,)), ((), ()))


def _const_spec(shape):
    return pl.BlockSpec(shape, lambda *_: (0,) * len(shape), pipeline_mode=pl.Buffered(1))


def _params(semantics):
    return pltpu.CompilerParams(dimension_semantics=semantics, vmem_limit_bytes=VMEM_LIMIT)


def _split3(x):
    a = x.astype(BF16)
    r = x - a.astype(F32)
    b = r.astype(BF16)
    c = (r - b.astype(F32)).astype(BF16)
    return a, b, c


def _dot(a, b):
    return jnp.dot(a, b, preferred_element_type=F32)


def _dot_nt(a, b):
    return lax.dot_general(a, b, _NT, preferred_element_type=F32)


def _rmsnorm_rows(x, g):
    ms = jnp.mean(x * x, axis=-1, keepdims=True)
    return x * lax.rsqrt(ms + EPS) * g


def _log_sigmoid(x):
    return jnp.minimum(x, 0.0) - jnp.log1p(jnp.exp(-jnp.abs(x)))


def _gelu(x):
    return 0.5 * x * (1.0 + lax.erf(x * (2.0 ** -0.5)))


def _fox_proj_kernel(x_ref, gn_ref, w_ref, wf_ref, bf_ref, gq_ref, gk_ref, bd_ref, tril_ref,
                     q_ref, kf_ref, kb_ref, vf_ref, vb_ref, og_ref, lf_ref, c_ref, carry_ref,
                     *, tm, front, head_dim, width):
    i = pl.program_id(0)

    @pl.when(i == 0)
    def _():
        carry_ref[...] = jnp.zeros_like(carry_ref)

    h = _rmsnorm_rows(x_ref[...], gn_ref[...]).astype(BF16)

    def head_norm(z, g):
        zz = (z * z).astype(BF16)
        ss = jnp.concatenate(
            [_dot(zz[:, c * MXU_DIM:(c + 1) * MXU_DIM], bd_ref[...]) for c in range(width // MXU_DIM)], axis=1)
        return z * lax.rsqrt(ss * (1.0 / head_dim) + EPS) * g

    zq = _dot(h, w_ref[:, 0:width])
    q_ref[...] = (head_norm(zq, gq_ref[...]) * (head_dim ** -0.5 * LOG2E)).astype(BF16)
    zk = _dot(h, w_ref[:, width:2 * width])
    kn = head_norm(zk, gk_ref[...])
    kf_ref[...] = kn
    kb_ref[...] = kn.astype(BF16)
    zv = _dot(h, w_ref[:, 2 * width:3 * width])
    vf_ref[...] = zv
    vb_ref[...] = zv.astype(BF16)
    og_ref[...] = _dot(h, w_ref[:, 3 * width:4 * width])

    lf = _log_sigmoid(_dot(h, wf_ref[...]) + bf_ref[...])
    lf_ref[...] = lf
    row = i * tm + lax.broadcasted_iota(jnp.int32, (tm, 1), 0)
    real = row >= front
    l1, l2, l3 = _split3(jnp.where(real, lf, 0.0))
    tril = tril_ref[...]
    c = _dot(tril, l1) + _dot(tril, l2) + _dot(tril, l3) + carry_ref[...]
    carry_ref[...] = c[tm - 1:tm, :]
    c_ref[...] = jnp.where(real, c, MASKED_KEY)


def _fox_proj(x, gn, w_main, w_f, b_f, gq, gk, *, front, head_dim):
    t_all, d = x.shape
    width = w_main.shape[1] // 4
    tm = PROJ_TILE
    blk = MXU_DIM // head_dim
    bd = jnp.kron(jnp.eye(blk, dtype=F32), jnp.ones((head_dim, head_dim), F32)).astype(BF16)
    tril = (lax.broadcasted_iota(jnp.int32, (tm, tm), 0) >= lax.broadcasted_iota(jnp.int32, (tm, tm), 1)).astype(BF16)
    row = lambda n: pl.BlockSpec((tm, n), lambda i: (i, 0))
    sds = jax.ShapeDtypeStruct
    return pl.pallas_call(
        functools.partial(_fox_proj_kernel, tm=tm, front=front, head_dim=head_dim, width=width),
        grid=(t_all // tm,),
        in_specs=[row(d), _const_spec((1, d)), _const_spec(w_main.shape), _const_spec(w_f.shape),
                  _const_spec((1, LANES)), _const_spec((1, width)), _const_spec((1, width)),
                  _const_spec(bd.shape), _const_spec(tril.shape)],
        out_specs=[row(width), row(width), row(width), row(width), row(width), row(width), row(LANES), row(LANES)],
        out_shape=[sds((t_all, width), BF16), sds((t_all, width), F32), sds((t_all, width), BF16),
                   sds((t_all, width), F32), sds((t_all, width), BF16), sds((t_all, width), F32),
                   sds((t_all, LANES), F32), sds((t_all, LANES), F32)],
        scratch_shapes=[pltpu.VMEM((1, LANES), F32)],
        compiler_params=_params(("arbitrary",)),
        name="fox_proj",
    )(x, gn, w_main, w_f, b_f, gq, gk, bd, tril)


def _fox_attn_kernel(q_ref, k_ref, v_ref, c_ref, og_ref, o_ref, qm_ref, m_ref, alpha_ref, p_ref, acc_ref,
                     *, tile, head_dim):
    qi = pl.program_id(1)
    lane = lax.broadcasted_iota(jnp.int32, (1, 2 * head_dim), 1)
    lo = lane < head_dim
    head_lanes = (lo, jnp.logical_not(lo))
    q = q_ref[...]
    for hh in range(2):
        qm_ref[hh] = jnp.where(head_lanes[hh], q, jnp.zeros_like(q))
    m_ref[...] = jnp.full(m_ref.shape, -jnp.inf, F32)
    acc_ref[...] = jnp.zeros(acc_ref.shape, F32)
    alpha_ref[1] = jnp.ones(alpha_ref.shape[1:], F32)
    p_ref[1] = jnp.zeros(p_ref.shape[1:], BF16)
    one = jnp.ones((), BF16)

    def scores(j, slot, diagonal):
        k = k_ref[pl.ds(pl.multiple_of(j * tile, tile), tile), :]
        for hh in range(2):
            c_last = c_ref[hh, pl.ds(qi, 1), :][:, tile - 1:tile]
            bias = (c_ref[hh, pl.ds(j, 1), :] - c_last) * LOG2E
            u = _dot_nt(qm_ref[hh], k) - bias
            if diagonal:
                r = lax.broadcasted_iota(jnp.int32, (tile, tile), 0)
                cidx = lax.broadcasted_iota(jnp.int32, (tile, tile), 1)
                u = jnp.where(cidx <= r, u, -jnp.inf)
            m_prev = m_ref[hh]
            m_new = jnp.maximum(m_prev, jnp.max(u, axis=1, keepdims=True))
            p_ref[slot, hh] = jnp.exp2(u - pltpu.repeat(m_new, tile // LANES, 1)).astype(BF16)
            alpha_ref[slot, hh] = jnp.exp2(m_prev - m_new)
            m_ref[hh] = m_new

    def accumulate(j, slot):
        v = v_ref[pl.ds(pl.multiple_of(j * tile, tile), tile), :]
        for hh in range(2):
            v_aug = jnp.where(head_lanes[hh], v, one)
            acc_ref[hh] = alpha_ref[slot, hh] * acc_ref[hh] + _dot(p_ref[slot, hh], v_aug)

    odd = lax.rem(qi, 2)

    @pl.when(odd == 1)
    def _():
        scores(0, 0, False)
        accumulate(0, 0)

    def body(jj, pending):
        j0 = odd + 2 * jj
        scores(j0, 0, False)
        accumulate(pending, 1)
        scores(j0 + 1, 1, False)
        accumulate(j0, 0)
        return j0 + 1

    pending = lax.fori_loop(0, qi // 2, body, 0)
    scores(qi, 0, True)
    accumulate(pending, 1)
    accumulate(qi, 0)

    outs = []
    for hh in range(2):
        a = acc_ref[hh]
        outs.append(a / pltpu.roll(a, head_dim, 1))
    o = jnp.where(lo, outs[0], outs[1])
    o_ref[...] = (o * jax.nn.sigmoid(og_ref[...])).astype(BF16)


def _fox_attn_prompt(q_b, k_b, v_b, c3, og, *, t_prompt, head_dim):
    t_all, width = q_b.shape
    tile = ROW_TILE
    pair = 2 * head_dim
    n_blk = t_prompt // tile
    return pl.pallas_call(
        functools.partial(_fox_attn_kernel, tile=tile, head_dim=head_dim),
        grid=(width // pair, n_blk),
        in_specs=[pl.BlockSpec((tile, pair), lambda p, i: (i, p)),
                  pl.BlockSpec((t_prompt, pair), lambda p, i: (0, p)),
                  pl.BlockSpec((t_prompt, pair), lambda p, i: (0, p)),
                  pl.BlockSpec((2, n_blk, tile), lambda p, i: (p, 0, 0)),
                  pl.BlockSpec((tile, pair), lambda p, i: (i, p))],
        out_specs=pl.BlockSpec((tile, pair), lambda p, i: (i, p)),
        out_shape=jax.ShapeDtypeStruct((t_all, width), BF16),
        scratch_shapes=[pltpu.VMEM((2, tile, pair), BF16), pltpu.VMEM((2, tile, LANES), F32),
                        pltpu.VMEM((2, 2, tile, LANES), F32), pltpu.VMEM((2, 2, tile, tile), BF16),
                        pltpu.VMEM((2, tile, pair), F32)],
        compiler_params=_params(("arbitrary", "arbitrary")),
        name="fox_attn_prompt",
    )(q_b, k_b, v_b, c3, og)


def _fox_sample_kernel(prev_ref, q_ref, kn_ref, vn_ref, og_ref, lfn_ref, kc_ref, vc_ref, lfc_ref, su_ref, bt_ref,
                       o_ref, *, n_new, head_dim, n_heads):
    del prev_ref
    b = pl.program_id(0)
    pair = 2 * head_dim
    n_rows = kn_ref.shape[0]

    @pl.when(b == 0)
    def _():
        o_ref[...] = jnp.zeros_like(o_ref)

    su = su_ref[...]
    suf = sum(_dot(t, su) for t in _split3(lfc_ref[0])) * LOG2E
    bt = bt_ref[...]
    pre = sum(_dot(t, bt) for t in _split3(lfn_ref[...])) * LOG2E

    r0 = pl.multiple_of(b * n_new, n_new)
    qrow = lax.broadcasted_iota(jnp.int32, (n_new, n_rows), 0)
    col = lax.broadcasted_iota(jnp.int32, (n_new, n_rows), 1)
    visible = jnp.logical_and(col >= b * n_new, col <= b * n_new + qrow)
    lane = lax.broadcasted_iota(jnp.int32, (1, pair), 1)
    lo = lane < head_dim
    head_lanes = (lo, jnp.logical_not(lo))

    for p in range(n_heads // 2):
        cols = slice(p * pair, (p + 1) * pair)
        q = q_ref[pl.ds(r0, n_new), cols]
        k_c = kc_ref[0, :, cols].astype(BF16)
        v_c = vc_ref[0, :, cols].astype(BF16)
        k_n = kn_ref[:, cols]
        v_n = vn_ref[:, cols]
        outs = []
        for hh in range(2):
            head = 2 * p + hh
            qh = jnp.where(head_lanes[hh], q, jnp.zeros_like(q))
            u_c = _dot_nt(qh, k_c) + suf[head:head + 1, :]
            u_n = jnp.where(visible, _dot_nt(qh, k_n) - pre[head:head + 1, :], -jnp.inf)
            m = jnp.maximum(jnp.max(u_c, axis=1, keepdims=True), jnp.max(u_n, axis=1, keepdims=True))
            p_c = jnp.exp2(u_c - m)
            p_n = jnp.exp2(u_n - m)
            den = jnp.sum(p_c, axis=1, keepdims=True) + jnp.sum(p_n, axis=1, keepdims=True)
            outs.append((_dot(p_c.astype(BF16), v_c) + _dot(p_n.astype(BF16), v_n)) / den)
        o = jnp.where(lo, outs[0], outs[1])
        gate = jax.nn.sigmoid(og_ref[pl.ds(r0, n_new), cols])
        o_ref[pl.ds(r0, n_new), cols] = (o * gate).astype(BF16)


def _fox_attn_sample(o_prev, q_b, k_b, v_b, og, lf_new_t, cache_k, cache_v, cache_lf_t, *, t_prompt, n_new, head_dim):
    t_all, width = q_b.shape
    n_streams, past, _ = cache_k.shape
    n_heads = width // head_dim
    n_rows = n_streams * n_new
    assert t_prompt % n_rows == 0 and n_rows % LANES == 0
    su = (lax.broadcasted_iota(jnp.int32, (past, past), 0) > lax.broadcasted_iota(jnp.int32, (past, past), 1)).astype(BF16)
    rr = lax.broadcasted_iota(jnp.int32, (n_rows, n_rows), 0)
    cc = lax.broadcasted_iota(jnp.int32, (n_rows, n_rows), 1)
    bt = jnp.logical_and(rr // n_new == cc // n_new, rr <= cc).astype(BF16)
    rows = pl.BlockSpec((n_rows, width), lambda b: (t_prompt // n_rows, 0), pipeline_mode=pl.Buffered(1))
    tail = t_all - t_prompt
    return pl.pallas_call(
        functools.partial(_fox_sample_kernel, n_new=n_new, head_dim=head_dim, n_heads=n_heads),
        grid=(n_streams,),
        in_specs=[pl.BlockSpec(memory_space=pl.ANY), rows, rows, rows, rows,
                  _const_spec(lf_new_t.shape),
                  pl.BlockSpec((1, past, width), lambda b: (b, 0, 0)),
                  pl.BlockSpec((1, past, width), lambda b: (b, 0, 0)),
                  pl.BlockSpec((1, n_heads, past), lambda b: (b, 0, 0)),
                  _const_spec(su.shape), _const_spec(bt.shape)],
        out_specs=pl.BlockSpec((tail, width), lambda b: (t_prompt // tail, 0)),
        out_shape=jax.ShapeDtypeStruct((t_all, width), BF16),
        input_output_aliases={0: 0},
        compiler_params=_params(("arbitrary",)),
        name="fox_attn_sample",
    )(o_prev, q_b, k_b, v_b, og, lf_new_t, cache_k, cache_v, cache_lf_t, su, bt)


def _out_proj_kernel(y_ref, x_ref, w_ref, o_ref):
    o_ref[...] = x_ref[...] + _dot(y_ref[...], w_ref[...])


def _out_proj(y_b, x, w):
    t_all, d = x.shape
    k = y_b.shape[1]
    tm = ROW_TILE
    return pl.pallas_call(
        _out_proj_kernel,
        grid=(t_all // tm,),
        in_specs=[pl.BlockSpec((tm, k), lambda i: (i, 0)), pl.BlockSpec((tm, d), lambda i: (i, 0)),
                  _const_spec(w.shape)],
        out_specs=pl.BlockSpec((tm, d), lambda i: (i, 0)),
        out_shape=jax.ShapeDtypeStruct((t_all, d), F32),
        compiler_params=_params(("arbitrary",)),
        name="out_proj",
    )(y_b, x, w)


def _ret_proj_kernel(x_ref, gn_ref, w_ref, inv_ref, q_ref, kt_ref, v_ref, g_ref,
                     *, tm, t_prompt, pos0, past, n_new, n_heads, qk_dim, v_width):
    i = pl.program_id(0)
    h = _rmsnorm_rows(x_ref[...], gn_ref[...]).astype(BF16)
    row = i * tm + lax.broadcasted_iota(jnp.int32, (tm, 1), 0)
    pos = jnp.where(row < t_prompt, row - pos0, past + lax.rem(row - t_prompt, n_new))
    ang = pos.astype(F32) * inv_ref[...]
    cos = jnp.cos(ang)
    sin = jnp.sin(ang)
    half = qk_dim // 2
    qk_width = n_heads * qk_dim

    def rotary(z):
        parts = []
        for hh in range(n_heads):
            z1 = z[:, hh * qk_dim:hh * qk_dim + half]
            z2 = z[:, hh * qk_dim + half:(hh + 1) * qk_dim]
            parts += [z1 * cos - z2 * sin, z1 * sin + z2 * cos]
        return jnp.concatenate(parts, axis=1)

    q = rotary(_dot(h, w_ref[:, 0:qk_width])) * (qk_dim ** -0.5)
    q_ref[...] = q.astype(BF16)
    kt_ref[...] = rotary(_dot(h, w_ref[:, qk_width:2 * qk_width])).T
    v_ref[...] = _dot(h, w_ref[:, 2 * qk_width:2 * qk_width + v_width]).astype(BF16)
    g_ref[...] = _dot(h, w_ref[:, 2 * qk_width + v_width:])


def _ret_proj(x, gn, w_in, inv_freq, *, t_prompt, pos0, past, n_new, n_heads, qk_dim, v_width):
    t_all, d = x.shape
    tm = PROJ_TILE
    qk_width = n_heads * qk_dim
    row = lambda n: pl.BlockSpec((tm, n), lambda i: (i, 0))
    sds = jax.ShapeDtypeStruct
    return pl.pallas_call(
        functools.partial(_ret_proj_kernel, tm=tm, t_prompt=t_prompt, pos0=pos0, past=past, n_new=n_new,
                          n_heads=n_heads, qk_dim=qk_dim, v_width=v_width),
        grid=(t_all // tm,),
        in_specs=[row(d), _const_spec((1, d)), _const_spec(w_in.shape), _const_spec(inv_freq.shape)],
        out_specs=[row(qk_width), pl.BlockSpec((qk_width, tm), lambda i: (0, i)), row(v_width), row(v_width)],
        out_shape=[sds((t_all, qk_width), BF16), sds((qk_width, t_all), F32),
                   sds((t_all, v_width), BF16), sds((t_all, v_width), F32)],
        compiler_params=_params(("arbitrary",)),
        name="ret_proj",
    )(x, gn, w_in, inv_freq)


def _group_norm_gate(o, g, gn):
    mu = jnp.mean(o, axis=-1, keepdims=True)
    var = jnp.mean(jnp.square(o - mu), axis=-1, keepdims=True)
    y = (o - mu) * lax.rsqrt(var + EPS) * gn
    return (g * jax.nn.sigmoid(g) * y).astype(BF16)


def _ret_prompt_kernel(logg_ref, q_ref, kt_ref, v_ref, g_ref, gn_ref, y_ref, st_ref, s_ref, *, tb):
    hh = pl.program_id(0)
    blk = pl.program_id(1)
    lg = jnp.full((1, 1), logg_ref[hh], F32)

    @pl.when(blk == 0)
    def _():
        s_ref[...] = jnp.zeros_like(s_ref)

    r = lax.broadcasted_iota(jnp.int32, (tb, tb), 0)
    c = lax.broadcasted_iota(jnp.int32, (tb, tb), 1)
    decay = jnp.exp(lg * jnp.abs(r - c).astype(F32))
    dmat = jnp.where(c // CHUNK <= r // CHUNK, decay, 0.0)

    q = q_ref[...]
    kt = kt_ref[...]
    v = v_ref[...]
    s = _dot(q, kt.astype(BF16)) * dmat
    state = s_ref[...]
    q_dec = jnp.exp(lg * (lax.broadcasted_iota(jnp.int32, (tb, 1), 0) + 1).astype(F32))
    o = _dot(s.astype(BF16), v) + _dot(q, state.astype(BF16)) * q_dec
    k_dec = jnp.exp(lg * (tb - 1 - lax.broadcasted_iota(jnp.int32, (1, tb), 1)).astype(F32))
    new_state = state * jnp.exp(lg * tb) + _dot((kt * k_dec).astype(BF16), v)
    s_ref[...] = new_state
    y_ref[...] = _group_norm_gate(o, g_ref[...], gn_ref[...])

    @pl.when(blk == pl.num_programs(1) - 1)
    def _():
        st_ref[0] = new_state


def _ret_prompt(logg, q_b, k_t, v_b, g, gn, *, t_prompt, n_heads):
    t_all, qk_width = q_b.shape
    v_width = v_b.shape[1]
    qk_dim, v_dim = qk_width // n_heads, v_width // n_heads
    tb = RET_BLOCK
    return pl.pallas_call(
        functools.partial(_ret_prompt_kernel, tb=tb),
        grid=(n_heads, t_prompt // tb),
        in_specs=[pl.BlockSpec(memory_space=pltpu.SMEM),
                  pl.BlockSpec((tb, qk_dim), lambda h, i: (i, h)),
                  pl.BlockSpec((qk_dim, tb), lambda h, i: (h, i)),
                  pl.BlockSpec((tb, v_dim), lambda h, i: (i, h)),
                  pl.BlockSpec((tb, v_dim), lambda h, i: (i, h)),
                  pl.BlockSpec((1, v_dim), lambda h, i: (0, h))],
        out_specs=[pl.BlockSpec((tb, v_dim), lambda h, i: (i, h)),
                   pl.BlockSpec((1, qk_dim, v_dim), lambda h, i: (h, 0, 0))],
        out_shape=[jax.ShapeDtypeStruct((t_all, v_width), BF16),
                   jax.ShapeDtypeStruct((n_heads, qk_dim, v_dim), F32)],
        scratch_shapes=[pltpu.VMEM((qk_dim, v_dim), F32)],
        compiler_params=_params(("arbitrary", "arbitrary")),
        name="ret_prompt",
    )(logg, q_b, k_t, v_b, g, gn)


def _ret_sample_kernel(logg_ref, prev_ref, q_ref, kt_ref, v_ref, g_ref, gn_ref, st_ref, y_ref, nst_ref, *, n_new):
    del prev_ref
    hh = pl.program_id(0)
    b = pl.program_id(1)
    lg = jnp.full((1, 1), logg_ref[hh], F32)
    n_rows = kt_ref.shape[1]

    @pl.when(b == 0)
    def _():
        y_ref[...] = jnp.zeros_like(y_ref)

    r0 = pl.multiple_of(b * n_new, n_new)
    q = q_ref[pl.ds(r0, n_new), :]
    kt = kt_ref[...]
    v = v_ref[...]
    qrow = lax.broadcasted_iota(jnp.int32, (n_new, n_rows), 0)
    col = lax.broadcasted_iota(jnp.int32, (n_new, n_rows), 1) - b * n_new
    own = jnp.logical_and(col >= 0, col < n_new)
    dmat = jnp.where(own, jnp.exp(lg * jnp.abs(qrow - col).astype(F32)), 0.0)
    s = _dot(q, kt.astype(BF16)) * dmat
    state = st_ref[0, 0]
    q_dec = jnp.exp(lg * (lax.broadcasted_iota(jnp.int32, (n_new, 1), 0) + 1).astype(F32))
    o = _dot(s.astype(BF16), v) + _dot(q, state.astype(BF16)) * q_dec
    col1 = lax.broadcasted_iota(jnp.int32, (1, n_rows), 1) - b * n_new
    own1 = jnp.logical_and(col1 >= 0, col1 < n_new)
    k_dec = jnp.where(own1, jnp.exp(lg * (n_new - 1 - col1).astype(F32)), 0.0)
    nst_ref[0, 0] = state * jnp.exp(lg * n_new) + _dot((kt * k_dec).astype(BF16), v)
    y_ref[pl.ds(r0, n_new), :] = _group_norm_gate(o, g_ref[pl.ds(r0, n_new), :], gn_ref[...])


def _ret_sample(logg, y_prev, q_b, k_t, v_b, g, gn, state, *, t_prompt, n_new):
    t_all, qk_width = q_b.shape
    v_width = v_b.shape[1]
    n_streams, n_heads, qk_dim, v_dim = state.shape
    n_rows = n_streams * n_new
    tail = t_all - t_prompt
    blk = t_prompt // n_rows
    return pl.pallas_call(
        functools.partial(_ret_sample_kernel, n_new=n_new),
        grid=(n_heads, n_streams),
        in_specs=[pl.BlockSpec(memory_space=pltpu.SMEM),
                  pl.BlockSpec(memory_space=pl.ANY),
                  pl.BlockSpec((n_rows, qk_dim), lambda h, b: (blk, h)),
                  pl.BlockSpec((qk_dim, n_rows), lambda h, b: (h, blk)),
                  pl.BlockSpec((n_rows, v_dim), lambda h, b: (blk, h)),
                  pl.BlockSpec((n_rows, v_dim), lambda h, b: (blk, h)),
                  pl.BlockSpec((1, v_dim), lambda h, b: (0, h)),
                  pl.BlockSpec((1, 1, qk_dim, v_dim), lambda h, b: (b, h, 0, 0))],
        out_specs=[pl.BlockSpec((tail, v_dim), lambda h, b: (t_prompt // tail, h)),
                   pl.BlockSpec((1, 1, qk_dim, v_dim), lambda h, b: (b, h, 0, 0))],
        out_shape=[jax.ShapeDtypeStruct((t_all, v_width), BF16),
                   jax.ShapeDtypeStruct(state.shape, F32)],
        input_output_aliases={1: 0},
        compiler_params=_params(("arbitrary", "arbitrary")),
        name="ret_sample",
    )(logg, y_prev, q_b, k_t, v_b, g, gn, state)


def _top_values(s, k, with_rank):
    work = s
    vals = []
    rank = jnp.full(s.shape, NO_RANK, F32) if with_rank else None
    for r in range(k):
        m = jnp.max(work, axis=0, keepdims=True)
        vals.append(m)
        hit = work == m
        if with_rank:
            rank = jnp.where(hit, float(r), rank)
        work = jnp.where(hit, -jnp.inf, work)
    return vals, rank


def _peer_kernel(x_ref, gn_ref, wq_ref, sk_ref, u_ref, vt_ref, o_ref,
                 hb_ref, e0_ref, cnt_ref, e1_ref, rank_ref, w_ref, y_ref, *, n_heads, n_keys, topk):
    c = pl.program_id(1)
    ce = u_ref.shape[0]
    rows_per_chunk = ce // n_keys

    @pl.when(c == 0)
    def _():
        hb = _rmsnorm_rows(x_ref[...], gn_ref[...]).astype(BF16)
        hb_ref[...] = hb
        y_ref[...] = jnp.zeros_like(y_ref)
        for h in range(n_heads):
            qh = _dot_nt(wq_ref[h * 2 * n_keys:(h + 1) * 2 * n_keys, :], hb)
            s0 = _dot(sk_ref[2 * h], qh[0:n_keys].astype(BF16))
            s1 = _dot(sk_ref[2 * h + 1], qh[n_keys:2 * n_keys].astype(BF16))
            a, rank0 = _top_values(s0, topk, True)
            b, rank1 = _top_values(s1, topk, True)
            half = topk // 2
            b_all = jnp.concatenate(b, axis=0)
            b_half = jnp.concatenate(b[:half], axis=0)
            cand = jnp.concatenate(
                [a[0] + b_all] + [a[i] + b_half for i in range(1, half)] + [jnp.concatenate(a[half:], axis=0) + b[0]],
                axis=0)
            f, _ = _top_values(cand, topk, False)
            tau = f[topk - 1]
            z = sum(jnp.exp(fk - f[0]) for fk in f)
            a_all = jnp.concatenate(a, axis=0)
            cnt_sorted = sum(jnp.where(a_all + bj >= tau, 1.0, 0.0) for bj in b)
            cnt = jnp.zeros_like(s0)
            for r in range(topk):
                cnt = jnp.where(rank0 == float(r), cnt_sorted[r:r + 1, :], cnt)
            cnt_ref[h] = cnt
            e0_ref[h] = jnp.exp(s0 - a[0])
            e1_ref[h] = (jnp.exp(s1 - b[0]) / z).astype(BF16)
            rank_ref[h] = rank1.astype(BF16)

    act = _dot_nt(u_ref[...], hb_ref[...])
    for il in range(rows_per_chunk):
        i = c * rows_per_chunk + il
        gate = None
        for h in range(n_heads):
            n_sel = cnt_ref[h, pl.ds(i, 1), :].astype(BF16)
            e0 = e0_ref[h, pl.ds(i, 1), :].astype(BF16)
            term = jnp.where(rank_ref[h] < n_sel, e1_ref[h], jnp.zeros((), BF16)) * e0
            gate = term if gate is None else gate + term
        rows = slice(il * n_keys, (il + 1) * n_keys)
        w_ref[rows, :] = gate * _gelu(act[rows, :]).astype(BF16)
    y_ref[...] += _dot(vt_ref[...], w_ref[...])

    @pl.when(c == pl.num_programs(1) - 1)
    def _():
        o_ref[...] = x_ref[...] + y_ref[...].T


def _peer(x, gn, wq_t, subkeys, u, v_t, *, n_heads, n_keys):
    t_all, d = x.shape
    n_experts = u.shape[0]
    tm = ROW_TILE
    ce = PEER_EXPERT_CHUNK
    head_scratch = pltpu.VMEM((n_heads, n_keys, tm), F32)
    head_scratch_b = pltpu.VMEM((n_heads, n_keys, tm), BF16)
    return pl.pallas_call(
        functools.partial(_peer_kernel, n_heads=n_heads, n_keys=n_keys, topk=PEER_TOPK),
        grid=(t_all // tm, n_experts // ce),
        in_specs=[pl.BlockSpec((tm, d), lambda i, c: (i, 0)), _const_spec((1, d)), _const_spec(wq_t.shape),
                  _const_spec(subkeys.shape),
                  pl.BlockSpec((ce, d), lambda i, c: (c, 0)),
                  pl.BlockSpec((d, ce), lambda i, c: (0, c))],
        out_specs=pl.BlockSpec((tm, d), lambda i, c: (i, 0)),
        out_shape=jax.ShapeDtypeStruct((t_all, d), F32),
        scratch_shapes=[pltpu.VMEM((tm, d), BF16), head_scratch, head_scratch, head_scratch_b, head_scratch_b,
                        pltpu.VMEM((ce, tm), BF16), pltpu.VMEM((d, tm), F32)],
        compiler_params=_params(("arbitrary", "arbitrary")),
        name="peer",
    )(x, gn, wq_t, subkeys, u, v_t)


def kernel(x_prompt, x_sample, cache_fox_k, cache_fox_v, cache_fox_lf, state_ret, meta_tokens, norm_mix, norm_ffn,
           fox_w_in, fox_b_f, fox_q_norm, fox_k_norm, fox_w_out, ret_w_in, ret_gn, ret_w_out,
           peer_w_q, peer_subkeys, peer_u, peer_v):
    batch, seq, d = x_prompt.shape
    n_streams, n_new, _ = x_sample.shape
    n_meta = meta_tokens.shape[0]
    assert batch == 1 and n_meta == N_META
    depth = norm_mix.shape[0]
    n_fox, _, past, fox_heads, fox_hd = cache_fox_k.shape
    fox_width = fox_heads * fox_hd
    n_ret, _, ret_heads, ret_qk, ret_v = state_ret.shape
    ret_v_width = ret_heads * ret_v
    peer_heads, _, n_keys, peer_half = peer_subkeys.shape[1:]
    assert n_keys == LANES and peer_half == LANES and fox_width == d

    length = n_meta + seq
    front = (-length) % ROW_TILE
    t_prompt = front + length
    n_rows = n_streams * n_new
    tail = ROW_TILE
    assert n_rows <= tail
    t_all = t_prompt + tail

    x = jnp.concatenate([jnp.zeros((front, d), F32), meta_tokens.astype(F32), x_prompt[0],
                         x_sample.reshape(n_rows, d), jnp.zeros((tail - n_rows, d), F32)], axis=0)

    half = ret_qk // 2
    inv_freq = (ROPE_BASE ** (-jnp.arange(half, dtype=F32) / half)).reshape(1, half)
    logg = jnp.log(1.0 - 2.0 ** (-5.0 - jnp.arange(ret_heads, dtype=F32)))

    kp, vp, lfp, ks, vs, lfs, srp, srs = [], [], [], [], [], [], [], []
    for layer in range(depth):
        j = layer // 2
        gn_mix = norm_mix[layer].reshape(1, d)
        if layer % 2 == 0:
            w_in = fox_w_in[j]
            w_main = w_in[:, :4 * fox_width].astype(BF16)
            w_f = jnp.pad(w_in[:, 4 * fox_width:], ((0, 0), (0, LANES - fox_heads))).astype(BF16)
            b_f = jnp.pad(fox_b_f[j], (0, LANES - fox_heads)).reshape(1, LANES)
            gq = jnp.tile(fox_q_norm[j], fox_heads).reshape(1, fox_width)
            gk = jnp.tile(fox_k_norm[j], fox_heads).reshape(1, fox_width)
            q_b, k_f, k_b, v_f, v_b, og, lf, csum = _fox_proj(x, gn_mix, w_main, w_f, b_f, gq, gk,
                                                              front=front, head_dim=fox_hd)
            c3 = csum[:t_prompt, :fox_heads].T.reshape(fox_heads, t_prompt // ROW_TILE, ROW_TILE)
            o_b = _fox_attn_prompt(q_b, k_b, v_b, c3, og, t_prompt=t_prompt, head_dim=fox_hd)
            lf_new_t = lf[t_prompt:t_prompt + n_rows, :fox_heads].T
            o_b = _fox_attn_sample(o_b, q_b, k_b, v_b, og, lf_new_t,
                                   cache_fox_k[j].reshape(n_streams, past, fox_width),
                                   cache_fox_v[j].reshape(n_streams, past, fox_width),
                                   cache_fox_lf[j].transpose(0, 2, 1),
                                   t_prompt=t_prompt, n_new=n_new, head_dim=fox_hd)
            x = _out_proj(o_b, x, fox_w_out[j].astype(BF16))
            kp.append(k_f[front:t_prompt].reshape(1, length, fox_heads, fox_hd))
            vp.append(v_f[front:t_prompt].reshape(1, length, fox_heads, fox_hd))
            lfp.append(lf[front:t_prompt, :fox_heads].reshape(1, length, fox_heads))
            ks.append(k_f[t_prompt:t_prompt + n_rows].reshape(n_streams, n_new, fox_heads, fox_hd))
            vs.append(v_f[t_prompt:t_prompt + n_rows].reshape(n_streams, n_new, fox_heads, fox_hd))
            lfs.append(lf[t_prompt:t_prompt + n_rows, :fox_heads].reshape(n_streams, n_new, fox_heads))
        else:
            q_b, k_t, v_b, g = _ret_proj(x, gn_mix, ret_w_in[j].astype(BF16), inv_freq,
                                         t_prompt=t_prompt, pos0=front + n_meta, past=past, n_new=n_new,
                                         n_heads=ret_heads, qk_dim=ret_qk, v_width=ret_v_width)
            gn_ret = ret_gn[j].reshape(1, ret_v_width)
            y_b, st_p = _ret_prompt(logg, q_b, k_t, v_b, g, gn_ret, t_prompt=t_prompt, n_heads=ret_heads)
            y_b, st_s = _ret_sample(logg, y_b, q_b, k_t, v_b, g, gn_ret, state_ret[j],
                                    t_prompt=t_prompt, n_new=n_new)
            x = _out_proj(y_b, x, ret_w_out[j].astype(BF16))
            srp.append(st_p[None])
            srs.append(st_s)
        x = _peer(x, norm_ffn[layer].reshape(1, d), peer_w_q[layer].T.astype(BF16),
                  peer_subkeys[layer].reshape(2 * peer_heads, n_keys, peer_half).astype(BF16),
                  peer_u[layer].astype(BF16), peer_v[layer].T.astype(BF16),
                  n_heads=peer_heads, n_keys=n_keys)

    y_prompt = x[front + n_meta:t_prompt][None]
    y_sample = x[t_prompt:t_prompt + n_rows].reshape(n_streams, n_new, d)
    return (y_prompt, y_sample, jnp.stack(kp), jnp.stack(vp), jnp.stack(lfp), jnp.stack(srp),
            jnp.stack(ks), jnp.stack(vs), jnp.stack(lfs), jnp.stack(srs))
```

```python
import functools
import math

import jax
import jax.numpy as jnp
from jax import lax
from jax.experimental import pallas as pl
from jax.experimental.pallas import tpu as pltpu

F32 = jnp.float32
BF16 = jnp.bfloat16

EPS = 1e-6
N_META = 16
CHUNK = 64
ROPE_BASE = 10000.0
PEER_TOPK = 16

LANES = 128
MXU_DIM = 256
BF16_TILE_ROWS = 16
VMEM_LIMIT = 56 * 1024 * 1024

ROW_TILE = 512
PROJ_TILE = 256
RET_BLOCK = 256
PEER_EXPERT_CHUNK = 2048
MASKED_KEY = 1e30
NO_RANK = 99.0
LOG2E = math.log2(math.e)

_NT = (((1,), (1,)), ((), ()))


def _const_spec(shape):
    return pl.BlockSpec(shape, lambda *_: (0,) * len(shape), pipeline_mode=pl.Buffered(1))


def _params(semantics):
    return pltpu.CompilerParams(dimension_semantics=semantics, vmem_limit_bytes=VMEM_LIMIT)


def _split3(x):
    a = x.astype(BF16)
    r = x - a.astype(F32)
    b = r.astype(BF16)
    c = (r - b.astype(F32)).astype(BF16)
    return a, b, c


def _dot(a, b):
    return jnp.dot(a, b, preferred_element_type=F32)


def _dot_nt(a, b):
    return lax.dot_general(a, b, _NT, preferred_element_type=F32)


def _rmsnorm_rows(x, g):
    ms = jnp.mean(x * x, axis=-1, keepdims=True)
    return x * lax.rsqrt(ms + EPS) * g


def _log_sigmoid(x):
    return jnp.minimum(x, 0.0) - jnp.log1p(jnp.exp(-jnp.abs(x)))


def _fox_proj_kernel(x_ref, gn_ref, w_ref, wf_ref, bf_ref, gq_ref, gk_ref, bd_ref, tril_ref,
                     q_ref, kf_ref, kb_ref, vf_ref, vb_ref, og_ref, lf_ref, c_ref, carry_ref,
                     *, tm, front, head_dim, width):
    i = pl.program_id(0)

    @pl.when(i == 0)
    def _():
        carry_ref[...] = jnp.zeros_like(carry_ref)

    h = _rmsnorm_rows(x_ref[...], gn_ref[...]).astype(BF16)

    def head_norm(z, g):
        zz = (z * z).astype(BF16)
        ss = jnp.concatenate(
            [_dot(zz[:, c * MXU_DIM:(c + 1) * MXU_DIM], bd_ref[...]) for c in range(width // MXU_DIM)], axis=1)
        return z * lax.rsqrt(ss * (1.0 / head_dim) + EPS) * g

    zq = _dot(h, w_ref[:, 0:width])
    q_ref[...] = (head_norm(zq, gq_ref[...]) * (head_dim ** -0.5 * LOG2E)).astype(BF16)
    zk = _dot(h, w_ref[:, width:2 * width])
    kn = head_norm(zk, gk_ref[...])
    kf_ref[...] = kn
    kb_ref[...] = kn.astype(BF16)
    zv = _dot(h, w_ref[:, 2 * width:3 * width])
    vf_ref[...] = zv
    vb_ref[...] = zv.astype(BF16)
    og_ref[...] = _dot(h, w_ref[:, 3 * width:4 * width])

    lf = _log_sigmoid(_dot(h, wf_ref[...]) + bf_ref[...])
    lf_ref[...] = lf
    row = i * tm + lax.broadcasted_iota(jnp.int32, (tm, 1), 0)
    real = row >= front
    l1, l2, l3 = _split3(jnp.where(real, lf, 0.0))
    tril = tril_ref[...]
    c = _dot(tril, l1) + _dot(tril, l2) + _dot(tril, l3) + carry_ref[...]
    carry_ref[...] = c[tm - 1:tm, :]
    c_ref[...] = jnp.where(real, c, MASKED_KEY)


def _fox_proj(x, gn, w_main, w_f, b_f, gq, gk, *, front, head_dim):
    t_all, d = x.shape
    width = w_main.shape[1] // 4
    tm = PROJ_TILE
    blk = MXU_DIM // head_dim
    bd = jnp.kron(jnp.eye(blk, dtype=F32), jnp.ones((head_dim, head_dim), F32)).astype(BF16)
    tril = (lax.broadcasted_iota(jnp.int32, (tm, tm), 0) >= lax.broadcasted_iota(jnp.int32, (tm, tm), 1)).astype(BF16)
    row = lambda n: pl.BlockSpec((tm, n), lambda i: (i, 0))
    sds = jax.ShapeDtypeStruct
    return pl.pallas_call(
        functools.partial(_fox_proj_kernel, tm=tm, front=front, head_dim=head_dim, width=width),
        grid=(t_all // tm,),
        in_specs=[row(d), _const_spec((1, d)), _const_spec(w_main.shape), _const_spec(w_f.shape),
                  _const_spec((1, LANES)), _const_spec((1, width)), _const_spec((1, width)),
                  _const_spec(bd.shape), _const_spec(tril.shape)],
        out_specs=[row(width), row(width), row(width), row(width), row(width), row(width), row(LANES), row(LANES)],
        out_shape=[sds((t_all, width), BF16), sds((t_all, width), F32), sds((t_all, width), BF16),
                   sds((t_all, width), F32), sds((t_all, width), BF16), sds((t_all, width), F32),
                   sds((t_all, LANES), F32), sds((t_all, LANES), F32)],
        scratch_shapes=[pltpu.VMEM((1, LANES), F32)],
        compiler_params=_params(("arbitrary",)),
        name="fox_proj",
    )(x, gn, w_main, w_f, b_f, gq, gk, bd, tril)


def _fox_attn_kernel(q_ref, k_ref, v_ref, c_ref, og_ref, o_ref, qm_ref, m_ref, alpha_ref, p_ref, acc_ref,
                     *, tile, head_dim):
    qi = pl.program_id(1)
    lane = lax.broadcasted_iota(jnp.int32, (1, 2 * head_dim), 1)
    lo = lane < head_dim
    head_lanes = (lo, jnp.logical_not(lo))
    q = q_ref[...]
    for hh in range(2):
        qm_ref[hh] = jnp.where(head_lanes[hh], q, jnp.zeros_like(q))
    m_ref[...] = jnp.full(m_ref.shape, -jnp.inf, F32)
    acc_ref[...] = jnp.zeros(acc_ref.shape, F32)
    alpha_ref[1] = jnp.ones(alpha_ref.shape[1:], F32)
    p_ref[1] = jnp.zeros(p_ref.shape[1:], BF16)
    one = jnp.ones((), BF16)

    def scores(j, slot, diagonal):
        k = k_ref[pl.ds(pl.multiple_of(j * tile, tile), tile), :]
        for hh in range(2):
            c_last = c_ref[hh, pl.ds(qi, 1), :][:, tile - 1:tile]
            bias = (c_ref[hh, pl.ds(j, 1), :] - c_last) * LOG2E
            u = _dot_nt(qm_ref[hh], k) - bias
            if diagonal:
                r = lax.broadcasted_iota(jnp.int32, (tile, tile), 0)
                cidx = lax.broadcasted_iota(jnp.int32, (tile, tile), 1)
                u = jnp.where(cidx <= r, u, -jnp.inf)
            m_prev = m_ref[hh]
            m_new = jnp.maximum(m_prev, jnp.max(u, axis=1, keepdims=True))
            m_wide = jnp.concatenate([m_new] * (tile // LANES), axis=1)
            p_ref[slot, hh] = jnp.exp2(u - m_wide).astype(BF16)
            alpha_ref[slot, hh] = jnp.exp2(m_prev - m_new)
            m_ref[hh] = m_new

    def accumulate(j, slot):
        v = v_ref[pl.ds(pl.multiple_of(j * tile, tile), tile), :]
        for hh in range(2):
            v_aug = jnp.where(head_lanes[hh], v, one)
            acc_ref[hh] = alpha_ref[slot, hh] * acc_ref[hh] + _dot(p_ref[slot, hh], v_aug)

    odd = lax.rem(qi, 2)

    @pl.when(odd == 1)
    def _():
        scores(0, 0, False)
        accumulate(0, 0)

    def body(jj, pending):
        j0 = odd + 2 * jj
        scores(j0, 0, False)
        accumulate(pending, 1)
        scores(j0 + 1, 1, False)
        accumulate(j0, 0)
        return j0 + 1

    pending = lax.fori_loop(0, qi // 2, body, 0)
    scores(qi, 0, True)
    accumulate(pending, 1)
    accumulate(qi, 0)

    outs = []
    for hh in range(2):
        a = acc_ref[hh]
        outs.append(a / pltpu.roll(a, head_dim, 1))
    o = jnp.where(lo, outs[0], outs[1])
    o_ref[...] = (o * jax.nn.sigmoid(og_ref[...])).astype(BF16)


def _fox_attn_prompt(q_b, k_b, v_b, c3, og, *, t_prompt, head_dim):
    t_all, width = q_b.shape
    tile = ROW_TILE
    pair = 2 * head_dim
    n_blk = t_prompt // tile
    return pl.pallas_call(
        functools.partial(_fox_attn_kernel, tile=tile, head_dim=head_dim),
        grid=(width // pair, n_blk),
        in_specs=[pl.BlockSpec((tile, pair), lambda p, i: (i, p)),
                  pl.BlockSpec((t_prompt, pair), lambda p, i: (0, p)),
                  pl.BlockSpec((t_prompt, pair), lambda p, i: (0, p)),
                  pl.BlockSpec((2, n_blk, tile), lambda p, i: (p, 0, 0)),
                  pl.BlockSpec((tile, pair), lambda p, i: (i, p))],
        out_specs=pl.BlockSpec((tile, pair), lambda p, i: (i, p)),
        out_shape=jax.ShapeDtypeStruct((t_all, width), BF16),
        scratch_shapes=[pltpu.VMEM((2, tile, pair), BF16), pltpu.VMEM((2, tile, LANES), F32),
                        pltpu.VMEM((2, 2, tile, LANES), F32), pltpu.VMEM((2, 2, tile, tile), BF16),
                        pltpu.VMEM((2, tile, pair), F32)],
        compiler_params=_params(("arbitrary", "arbitrary")),
        name="fox_attn_prompt",
    )(q_b, k_b, v_b, c3, og)


def _fox_sample_kernel(prev_ref, q_ref, kn_ref, vn_ref, og_ref, lfn_ref, kc_ref, vc_ref, lfc_ref, su_ref, bt_ref,
                       o_ref, *, n_new, head_dim, n_heads):
    del prev_ref
    b = pl.program_id(0)
    pair = 2 * head_dim
    n_rows = kn_ref.shape[0]

    @pl.when(b == 0)
    def _():
        o_ref[...] = jnp.zeros_like(o_ref)

    su = su_ref[...]
    suf = sum(_dot(t, su) for t in _split3(lfc_ref[0])) * LOG2E
    bt = bt_ref[...]
    pre = sum(_dot(t, bt) for t in _split3(lfn_ref[...])) * LOG2E

    r0 = pl.multiple_of(b * n_new, n_new)
    qrow = lax.broadcasted_iota(jnp.int32, (n_new, n_rows), 0)
    col = lax.broadcasted_iota(jnp.int32, (n_new, n_rows), 1)
    visible = jnp.logical_and(col >= b * n_new, col <= b * n_new + qrow)
    lane = lax.broadcasted_iota(jnp.int32, (1, pair), 1)
    lo = lane < head_dim
    head_lanes = (lo, jnp.logical_not(lo))

    for p in range(n_heads // 2):
        cols = slice(p * pair, (p + 1) * pair)
        q = q_ref[pl.ds(r0, n_new), cols]
        k_c = kc_ref[0, :, cols].astype(BF16)
        v_c = vc_ref[0, :, cols].astype(BF16)
        k_n = kn_ref[:, cols]
        v_n = vn_ref[:, cols]
        outs = []
        for hh in range(2):
            head = 2 * p + hh
            qh = jnp.where(head_lanes[hh], q, jnp.zeros_like(q))
            u_c = _dot_nt(qh, k_c) + suf[head:head + 1, :]
            u_n = jnp.where(visible, _dot_nt(qh, k_n) - pre[head:head + 1, :], -jnp.inf)
            m = jnp.maximum(jnp.max(u_c, axis=1, keepdims=True), jnp.max(u_n, axis=1, keepdims=True))
            p_c = jnp.exp2(u_c - m)
            p_n = jnp.exp2(u_n - m)
            den = jnp.sum(p_c, axis=1, keepdims=True) + jnp.sum(p_n, axis=1, keepdims=True)
            outs.append((_dot(p_c.astype(BF16), v_c) + _dot(p_n.astype(BF16), v_n)) / den)
        o = jnp.where(lo, outs[0], outs[1])
        gate = jax.nn.sigmoid(og_ref[pl.ds(r0, n_new), cols])
        o_ref[pl.ds(r0, n_new), cols] = (o * gate).astype(BF16)


def _fox_attn_sample(o_prev, q_b, k_b, v_b, og, lf_new_t, cache_k, cache_v, cache_lf_t, *, t_prompt, n_new, head_dim):
    t_all, width = q_b.shape
    n_streams, past, _ = cache_k.shape
    n_heads = width // head_dim
    n_rows = n_streams * n_new
    assert t_prompt % n_rows == 0 and n_rows % LANES == 0
    su = (lax.broadcasted_iota(jnp.int32, (past, past), 0) > lax.broadcasted_iota(jnp.int32, (past, past), 1)).astype(BF16)
    rr = lax.broadcasted_iota(jnp.int32, (n_rows, n_rows), 0)
    cc = lax.broadcasted_iota(jnp.int32, (n_rows, n_rows), 1)
    bt = jnp.logical_and(rr // n_new == cc // n_new, rr <= cc).astype(BF16)
    rows = pl.BlockSpec((n_rows, width), lambda b: (t_prompt // n_rows, 0), pipeline_mode=pl.Buffered(1))
    tail = t_all - t_prompt
    return pl.pallas_call(
        functools.partial(_fox_sample_kernel, n_new=n_new, head_dim=head_dim, n_heads=n_heads),
        grid=(n_streams,),
        in_specs=[pl.BlockSpec(memory_space=pl.ANY), rows, rows, rows, rows,
                  _const_spec(lf_new_t.shape),
                  pl.BlockSpec((1, past, width), lambda b: (b, 0, 0)),
                  pl.BlockSpec((1, past, width), lambda b: (b, 0, 0)),
                  pl.BlockSpec((1, n_heads, past), lambda b: (b, 0, 0)),
                  _const_spec(su.shape), _const_spec(bt.shape)],
        out_specs=pl.BlockSpec((tail, width), lambda b: (t_prompt // tail, 0)),
        out_shape=jax.ShapeDtypeStruct((t_all, width), BF16),
        input_output_aliases={0: 0},
        compiler_params=_params(("arbitrary",)),
        name="fox_attn_sample",
    )(o_prev, q_b, k_b, v_b, og, lf_new_t, cache_k, cache_v, cache_lf_t, su, bt)


def _out_proj_kernel(y_ref, x_ref, w_ref, o_ref):
    o_ref[...] = x_ref[...] + _dot(y_ref[...], w_ref[...])


def _out_proj(y_b, x, w):
    t_all, d = x.shape
    k = y_b.shape[1]
    tm = ROW_TILE
    return pl.pallas_call(
        _out_proj_kernel,
        grid=(t_all // tm,),
        in_specs=[pl.BlockSpec((tm, k), lambda i: (i, 0)), pl.BlockSpec((tm, d), lambda i: (i, 0)),
                  _const_spec(w.shape)],
        out_specs=pl.BlockSpec((tm, d), lambda i: (i, 0)),
        out_shape=jax.ShapeDtypeStruct((t_all, d), F32),
        compiler_params=_params(("arbitrary",)),
        name="out_proj",
    )(y_b, x, w)


def _ret_proj_kernel(x_ref, gn_ref, w_ref, inv_ref, q_ref, kt_ref, v_ref, g_ref,
                     *, tm, t_prompt, pos0, past, n_new, n_heads, qk_dim, v_width):
    i = pl.program_id(0)
    h = _rmsnorm_rows(x_ref[...], gn_ref[...]).astype(BF16)
    row = i * tm + lax.broadcasted_iota(jnp.int32, (tm, 1), 0)
    pos = jnp.where(row < t_prompt, row - pos0, past + lax.rem(row - t_prompt, n_new))
    ang = pos.astype(F32) * inv_ref[...]
    cos = jnp.cos(ang)
    sin = jnp.sin(ang)
    half = qk_dim // 2
    qk_width = n_heads * qk_dim

    def rotary(z):
        parts = []
        for hh in range(n_heads):
            z1 = z[:, hh * qk_dim:hh * qk_dim + half]
            z2 = z[:, hh * qk_dim + half:(hh + 1) * qk_dim]
            parts += [z1 * cos - z2 * sin, z1 * sin + z2 * cos]
        return jnp.concatenate(parts, axis=1)

    q = rotary(_dot(h, w_ref[:, 0:qk_width])) * (qk_dim ** -0.5)
    q_ref[...] = q.astype(BF16)
    kt_ref[...] = rotary(_dot(h, w_ref[:, qk_width:2 * qk_width])).T
    v_ref[...] = _dot(h, w_ref[:, 2 * qk_width:2 * qk_width + v_width]).astype(BF16)
    g_ref[...] = _dot(h, w_ref[:, 2 * qk_width + v_width:])


def _ret_proj(x, gn, w_in, inv_freq, *, t_prompt, pos0, past, n_new, n_heads, qk_dim, v_width):
    t_all, d = x.shape
    tm = PROJ_TILE
    qk_width = n_heads * qk_dim
    row = lambda n: pl.BlockSpec((tm, n), lambda i: (i, 0))
    sds = jax.ShapeDtypeStruct
    return pl.pallas_call(
        functools.partial(_ret_proj_kernel, tm=tm, t_prompt=t_prompt, pos0=pos0, past=past, n_new=n_new,
                          n_heads=n_heads, qk_dim=qk_dim, v_width=v_width),
        grid=(t_all // tm,),
        in_specs=[row(d), _const_spec((1, d)), _const_spec(w_in.shape), _const_spec(inv_freq.shape)],
        out_specs=[row(qk_width), pl.BlockSpec((qk_width, tm), lambda i: (0, i)), row(v_width), row(v_width)],
        out_shape=[sds((t_all, qk_width), BF16), sds((qk_width, t_all), F32),
                   sds((t_all, v_width), BF16), sds((t_all, v_width), F32)],
        compiler_params=_params(("arbitrary",)),
        name="ret_proj",
    )(x, gn, w_in, inv_freq)


def _group_norm_gate(o, g, gn):
    mu = jnp.mean(o, axis=-1, keepdims=True)
    var = jnp.mean(jnp.square(o - mu), axis=-1, keepdims=True)
    y = (o - mu) * lax.rsqrt(var + EPS) * gn
    return (g * jax.nn.sigmoid(g) * y).astype(BF16)


def _ret_prompt_kernel(logg_ref, q_ref, kt_ref, v_ref, g_ref, gn_ref, y_ref, st_ref, s_ref, *, tb):
    hh = pl.program_id(0)
    blk = pl.program_id(1)
    lg = jnp.full((1, 1), logg_ref[hh], F32)

    @pl.when(blk == 0)
    def _():
        s_ref[...] = jnp.zeros_like(s_ref)

    r = lax.broadcasted_iota(jnp.int32, (tb, tb), 0)
    c = lax.broadcasted_iota(jnp.int32, (tb, tb), 1)
    decay = jnp.exp(lg * jnp.abs(r - c).astype(F32))
    dmat = jnp.where(c // CHUNK <= r // CHUNK, decay, 0.0)

    q = q_ref[...]
    kt = kt_ref[...]
    v = v_ref[...]
    s = _dot(q, kt.astype(BF16)) * dmat
    state = s_ref[...]
    q_dec = jnp.exp(lg * (lax.broadcasted_iota(jnp.int32, (tb, 1), 0) + 1).astype(F32))
    o = _dot(s.astype(BF16), v) + _dot(q, state.astype(BF16)) * q_dec
    k_dec = jnp.exp(lg * (tb - 1 - lax.broadcasted_iota(jnp.int32, (1, tb), 1)).astype(F32))
    new_state = state * jnp.exp(lg * tb) + _dot((kt * k_dec).astype(BF16), v)
    s_ref[...] = new_state
    y_ref[...] = _group_norm_gate(o, g_ref[...], gn_ref[...])

    @pl.when(blk == pl.num_programs(1) - 1)
    def _():
        st_ref[0] = new_state


def _ret_prompt(logg, q_b, k_t, v_b, g, gn, *, t_prompt, n_heads):
    t_all, qk_width = q_b.shape
    v_width = v_b.shape[1]
    qk_dim, v_dim = qk_width // n_heads, v_width // n_heads
    tb = RET_BLOCK
    return pl.pallas_call(
        functools.partial(_ret_prompt_kernel, tb=tb),
        grid=(n_heads, t_prompt // tb),
        in_specs=[pl.BlockSpec(memory_space=pltpu.SMEM),
                  pl.BlockSpec((tb, qk_dim), lambda h, i: (i, h)),
                  pl.BlockSpec((qk_dim, tb), lambda h, i: (h, i)),
                  pl.BlockSpec((tb, v_dim), lambda h, i: (i, h)),
                  pl.BlockSpec((tb, v_dim), lambda h, i: (i, h)),
                  pl.BlockSpec((1, v_dim), lambda h, i: (0, h))],
        out_specs=[pl.BlockSpec((tb, v_dim), lambda h, i: (i, h)),
                   pl.BlockSpec((1, qk_dim, v_dim), lambda h, i: (h, 0, 0))],
        out_shape=[jax.ShapeDtypeStruct((t_all, v_width), BF16),
                   jax.ShapeDtypeStruct((n_heads, qk_dim, v_dim), F32)],
        scratch_shapes=[pltpu.VMEM((qk_dim, v_dim), F32)],
        compiler_params=_params(("arbitrary", "arbitrary")),
        name="ret_prompt",
    )(logg, q_b, k_t, v_b, g, gn)


def _ret_sample_kernel(logg_ref, prev_ref, q_ref, kt_ref, v_ref, g_ref, gn_ref, st_ref, y_ref, nst_ref, *, n_new):
    del prev_ref
    hh = pl.program_id(0)
    b = pl.program_id(1)
    lg = jnp.full((1, 1), logg_ref[hh], F32)
    n_rows = kt_ref.shape[1]

    @pl.when(b == 0)
    def _():
        y_ref[...] = jnp.zeros_like(y_ref)

    r0 = pl.multiple_of(b * n_new, n_new)
    q = q_ref[pl.ds(r0, n_new), :]
    kt = kt_ref[...]
    v = v_ref[...]
    qrow = lax.broadcasted_iota(jnp.int32, (n_new, n_rows), 0)
    col = lax.broadcasted_iota(jnp.int32, (n_new, n_rows), 1) - b * n_new
    own = jnp.logical_and(col >= 0, col < n_new)
    dmat = jnp.where(own, jnp.exp(lg * jnp.abs(qrow - col).astype(F32)), 0.0)
    s = _dot(q, kt.astype(BF16)) * dmat
    state = st_ref[0, 0]
    q_dec = jnp.exp(lg * (lax.broadcasted_iota(jnp.int32, (n_new, 1), 0) + 1).astype(F32))
    o = _dot(s.astype(BF16), v) + _dot(q, state.astype(BF16)) * q_dec
    col1 = lax.broadcasted_iota(jnp.int32, (1, n_rows), 1) - b * n_new
    own1 = jnp.logical_and(col1 >= 0, col1 < n_new)
    k_dec = jnp.where(own1, jnp.exp(lg * (n_new - 1 - col1).astype(F32)), 0.0)
    nst_ref[0, 0] = state * jnp.exp(lg * n_new) + _dot((kt * k_dec).astype(BF16), v)
    y_ref[pl.ds(r0, n_new), :] = _group_norm_gate(o, g_ref[pl.ds(r0, n_new), :], gn_ref[...])


def _ret_sample(logg, y_prev, q_b, k_t, v_b, g, gn, state, *, t_prompt, n_new):
    t_all, qk_width = q_b.shape
    v_width = v_b.shape[1]
    n_streams, n_heads, qk_dim, v_dim = state.shape
    n_rows = n_streams * n_new
    tail = t_all - t_prompt
    blk = t_prompt // n_rows
    return pl.pallas_call(
        functools.partial(_ret_sample_kernel, n_new=n_new),
        grid=(n_heads, n_streams),
        in_specs=[pl.BlockSpec(memory_space=pltpu.SMEM),
                  pl.BlockSpec(memory_space=pl.ANY),
                  pl.BlockSpec((n_rows, qk_dim), lambda h, b: (blk, h)),
                  pl.BlockSpec((qk_dim, n_rows), lambda h, b: (h, blk)),
                  pl.BlockSpec((n_rows, v_dim), lambda h, b: (blk, h)),
                  pl.BlockSpec((n_rows, v_dim), lambda h, b: (blk, h)),
                  pl.BlockSpec((1, v_dim), lambda h, b: (0, h)),
                  pl.BlockSpec((1, 1, qk_dim, v_dim), lambda h, b: (b, h, 0, 0))],
        out_specs=[pl.BlockSpec((tail, v_dim), lambda h, b: (t_prompt // tail, h)),
                   pl.BlockSpec((1, 1, qk_dim, v_dim), lambda h, b: (b, h, 0, 0))],
        out_shape=[jax.ShapeDtypeStruct((t_all, v_width), BF16),
                   jax.ShapeDtypeStruct(state.shape, F32)],
        input_output_aliases={1: 0},
        compiler_params=_params(("arbitrary", "arbitrary")),
        name="ret_sample",
    )(logg, y_prev, q_b, k_t, v_b, g, gn, state)


def _top_values(s, k, with_rank):
    work = s
    vals = []
    rank = jnp.full(s.shape, NO_RANK, F32) if with_rank else None
    for r in range(k):
        m = jnp.max(work, axis=0, keepdims=True)
        vals.append(m)
        hit = work == m
        if with_rank:
            rank = jnp.where(hit, float(r), rank)
        work = jnp.where(hit, -jnp.inf, work)
    return vals, rank


def _peer_kernel(x_ref, gn_ref, wq_ref, sk_ref, u_ref, vt_ref, o_ref,
                 hb_ref, e0_ref, cnt_ref, e1_ref, rank_ref, w_ref, y_ref, *, n_heads, n_keys, topk):
    c = pl.program_id(1)
    ce, tm = u_ref.shape[0], x_ref.shape[0]
    rows_per_chunk = ce // n_keys
    sub = rank_ref.shape[2]

    @pl.when(c == 0)
    def _():
        hb = _rmsnorm_rows(x_ref[...], gn_ref[...]).astype(BF16)
        hb_ref[...] = hb
        y_ref[...] = jnp.zeros_like(y_ref)
        for h in range(n_heads):
            qh = _dot_nt(wq_ref[h * 2 * n_keys:(h + 1) * 2 * n_keys, :], hb)
            s0 = _dot(sk_ref[2 * h], qh[0:n_keys].astype(BF16))
            s1 = _dot(sk_ref[2 * h + 1], qh[n_keys:2 * n_keys].astype(BF16))
            a, rank0 = _top_values(s0, topk, True)
            b, rank1 = _top_values(s1, topk, True)
            half = topk // 2
            b_all = jnp.concatenate(b, axis=0)
            b_half = jnp.concatenate(b[:half], axis=0)
            cand = jnp.concatenate(
                [a[0] + b_all] + [a[i] + b_half for i in range(1, half)] + [jnp.concatenate(a[half:], axis=0) + b[0]],
                axis=0)
            f, _ = _top_values(cand, topk, False)
            tau = f[topk - 1]
            z = sum(jnp.exp(fk - f[0]) for fk in f)
            a_all = jnp.concatenate(a, axis=0)
            cnt_sorted = sum(jnp.where(a_all + bj >= tau, 1.0, 0.0) for bj in b)
            cnt = jnp.zeros_like(s0)
            for r in range(topk):
                cnt = jnp.where(rank0 == float(r), cnt_sorted[r:r + 1, :], cnt)
            cnt_ref[h] = cnt
            e0_ref[h] = jnp.exp(s0 - a[0])
            e1_ref[h] = (jnp.exp(s1 - b[0]) * (0.5 / z)).astype(BF16).reshape(n_keys // sub, sub, tm)
            rank_ref[h] = rank1.astype(BF16).reshape(n_keys // sub, sub, tm)

    hb = hb_ref[...]
    half = ce // 2
    acts = (_dot_nt(u_ref[0:half, :], hb), _dot_nt(u_ref[half:ce, :], hb))
    rows_per_slab = MXU_DIM // n_keys
    y_add = None
    for slab in range(ce // MXU_DIM):
        for il in range(slab * rows_per_slab, (slab + 1) * rows_per_slab):
            i = c * rows_per_chunk + il
            gate = None
            for h in range(n_heads):
                n_sel = jnp.broadcast_to(cnt_ref[h, pl.ds(i, 1), :], (sub, tm)).astype(BF16)
                e0 = jnp.broadcast_to(e0_ref[h, pl.ds(i, 1), :], (sub, tm)).astype(BF16)
                term = jnp.where(rank_ref[h] < n_sel[None], e1_ref[h], jnp.zeros((), BF16)) * e0[None]
                gate = term if gate is None else gate + term
            r0 = il * n_keys
            x_act = acts[r0 // half][r0 % half:r0 % half + n_keys, :]
            gelu2 = x_act * (1.0 + lax.erf(x_act * (2.0 ** -0.5)))
            w_ref[r0:r0 + n_keys, :] = gate.reshape(n_keys, tm) * gelu2.astype(BF16)
        rows = slice(slab * MXU_DIM, (slab + 1) * MXU_DIM)
        part = _dot(vt_ref[:, rows], w_ref[rows, :])
        y_add = part if y_add is None else y_add + part
    y_ref[...] += y_add

    @pl.when(c == pl.num_programs(1) - 1)
    def _():
        o_ref[...] = x_ref[...] + y_ref[...].T


def _peer(x, gn, wq_t, subkeys, u, v_t, *, n_heads, n_keys):
    t_all, d = x.shape
    n_experts = u.shape[0]
    tm = ROW_TILE
    ce = PEER_EXPERT_CHUNK
    head_scratch = pltpu.VMEM((n_heads, n_keys, tm), F32)
    head_scratch_b = pltpu.VMEM((n_heads, n_keys // BF16_TILE_ROWS, BF16_TILE_ROWS, tm), BF16)
    return pl.pallas_call(
        functools.partial(_peer_kernel, n_heads=n_heads, n_keys=n_keys, topk=PEER_TOPK),
        grid=(t_all // tm, n_experts // ce),
        in_specs=[pl.BlockSpec((tm, d), lambda i, c: (i, 0)), _const_spec((1, d)), _const_spec(wq_t.shape),
                  _const_spec(subkeys.shape),
                  pl.BlockSpec((ce, d), lambda i, c: (c, 0)),
                  pl.BlockSpec((d, ce), lambda i, c: (0, c))],
        out_specs=pl.BlockSpec((tm, d), lambda i, c: (i, 0)),
        out_shape=jax.ShapeDtypeStruct((t_all, d), F32),
        scratch_shapes=[pltpu.VMEM((tm, d), BF16), head_scratch, head_scratch, head_scratch_b, head_scratch_b,
                        pltpu.VMEM((ce, tm), BF16), pltpu.VMEM((d, tm), F32)],
        compiler_params=_params(("arbitrary", "arbitrary")),
        name="peer",
    )(x, gn, wq_t, subkeys, u, v_t)


def kernel(x_prompt, x_sample, cache_fox_k, cache_fox_v, cache_fox_lf, state_ret, meta_tokens, norm_mix, norm_ffn,
           fox_w_in, fox_b_f, fox_q_norm, fox_k_norm, fox_w_out, ret_w_in, ret_gn, ret_w_out,
           peer_w_q, peer_subkeys, peer_u, peer_v):
    batch, seq, d = x_prompt.shape
    n_streams, n_new, _ = x_sample.shape
    n_meta = meta_tokens.shape[0]
    assert batch == 1 and n_meta == N_META
    depth = norm_mix.shape[0]
    n_fox, _, past, fox_heads, fox_hd = cache_fox_k.shape
    fox_width = fox_heads * fox_hd
    n_ret, _, ret_heads, ret_qk, ret_v = state_ret.shape
    ret_v_width = ret_heads * ret_v
    peer_heads, _, n_keys, peer_half = peer_subkeys.shape[1:]
    assert n_keys == LANES and peer_half == LANES and fox_width == d

    length = n_meta + seq
    front = (-length) % ROW_TILE
    t_prompt = front + length
    n_rows = n_streams * n_new
    tail = ROW_TILE
    assert n_rows <= tail
    t_all = t_prompt + tail

    x = jnp.concatenate([jnp.zeros((front, d), F32), meta_tokens.astype(F32), x_prompt[0],
                         x_sample.reshape(n_rows, d), jnp.zeros((tail - n_rows, d), F32)], axis=0)

    half = ret_qk // 2
    inv_freq = (ROPE_BASE ** (-jnp.arange(half, dtype=F32) / half)).reshape(1, half)
    logg = jnp.log(1.0 - 2.0 ** (-5.0 - jnp.arange(ret_heads, dtype=F32)))

    kp, vp, lfp, ks, vs, lfs, srp, srs = [], [], [], [], [], [], [], []
    for layer in range(depth):
        j = layer // 2
        gn_mix = norm_mix[layer].reshape(1, d)
        if layer % 2 == 0:
            w_in = fox_w_in[j]
            w_main = w_in[:, :4 * fox_width].astype(BF16)
            w_f = jnp.pad(w_in[:, 4 * fox_width:], ((0, 0), (0, LANES - fox_heads))).astype(BF16)
            b_f = jnp.pad(fox_b_f[j], (0, LANES - fox_heads)).reshape(1, LANES)
            gq = jnp.tile(fox_q_norm[j], fox_heads).reshape(1, fox_width)
            gk = jnp.tile(fox_k_norm[j], fox_heads).reshape(1, fox_width)
            q_b, k_f, k_b, v_f, v_b, og, lf, csum = _fox_proj(x, gn_mix, w_main, w_f, b_f, gq, gk,
                                                              front=front, head_dim=fox_hd)
            c3 = csum[:t_prompt, :fox_heads].T.reshape(fox_heads, t_prompt // ROW_TILE, ROW_TILE)
            o_b = _fox_attn_prompt(q_b, k_b, v_b, c3, og, t_prompt=t_prompt, head_dim=fox_hd)
            lf_new_t = lf[t_prompt:t_prompt + n_rows, :fox_heads].T
            o_b = _fox_attn_sample(o_b, q_b, k_b, v_b, og, lf_new_t,
                                   cache_fox_k[j].reshape(n_streams, past, fox_width),
                                   cache_fox_v[j].reshape(n_streams, past, fox_width),
                                   cache_fox_lf[j].transpose(0, 2, 1),
                                   t_prompt=t_prompt, n_new=n_new, head_dim=fox_hd)
            x = _out_proj(o_b, x, fox_w_out[j].astype(BF16))
            kp.append(k_f[front:t_prompt].reshape(1, length, fox_heads, fox_hd))
            vp.append(v_f[front:t_prompt].reshape(1, length, fox_heads, fox_hd))
            lfp.append(lf[front:t_prompt, :fox_heads].reshape(1, length, fox_heads))
            ks.append(k_f[t_prompt:t_prompt + n_rows].reshape(n_streams, n_new, fox_heads, fox_hd))
            vs.append(v_f[t_prompt:t_prompt + n_rows].reshape(n_streams, n_new, fox_heads, fox_hd))
            lfs.append(lf[t_prompt:t_prompt + n_rows, :fox_heads].reshape(n_streams, n_new, fox_heads))
        else:
            q_b, k_t, v_b, g = _ret_proj(x, gn_mix, ret_w_in[j].astype(BF16), inv_freq,
                                         t_prompt=t_prompt, pos0=front + n_meta, past=past, n_new=n_new,
                                         n_heads=ret_heads, qk_dim=ret_qk, v_width=ret_v_width)
            gn_ret = ret_gn[j].reshape(1, ret_v_width)
            y_b, st_p = _ret_prompt(logg, q_b, k_t, v_b, g, gn_ret, t_prompt=t_prompt, n_heads=ret_heads)
            y_b, st_s = _ret_sample(logg, y_b, q_b, k_t, v_b, g, gn_ret, state_ret[j],
                                    t_prompt=t_prompt, n_new=n_new)
            x = _out_proj(y_b, x, ret_w_out[j].astype(BF16))
            srp.append(st_p[None])
            srs.append(st_s)
        x = _peer(x, norm_ffn[layer].reshape(1, d), peer_w_q[layer].T.astype(BF16),
                  peer_subkeys[layer].reshape(2 * peer_heads, n_keys, peer_half).astype(BF16),
                  peer_u[layer].astype(BF16), peer_v[layer].T.astype(BF16),
                  n_heads=peer_heads, n_keys=n_keys)

    y_prompt = x[front + n_meta:t_prompt][None]
    y_sample = x[t_prompt:t_prompt + n_rows].reshape(n_streams, n_new, d)
    return (y_prompt, y_sample, jnp.stack(kp), jnp.stack(vp), jnp.stack(lfp), jnp.stack(srp),
            jnp.stack(ks), jnp.stack(vs), jnp.stack(lfs), jnp.stack(srs))
```

```python
import functools
import math

import jax
import jax.numpy as jnp
from jax import lax
from jax.experimental import pallas as pl
from jax.experimental.pallas import tpu as pltpu

F32 = jnp.float32
BF16 = jnp.bfloat16

EPS = 1e-6
N_META = 16
CHUNK = 64
ROPE_BASE = 10000.0
PEER_TOPK = 16

LANES = 128
MXU_DIM = 256
BF16_TILE_ROWS = 16
VMEM_LIMIT = 56 * 1024 * 1024

ROW_TILE = 512
PROJ_TILE = 256
RET_BLOCK = 256
PEER_EXPERT_CHUNK = 2048
MASKED_KEY = 1e30
NO_RANK = 99.0
LOG2E = math.log2(math.e)
ZERO_WEIGHT_LOG2 = 160.0

_NT = (((1,), (1,)), ((), ()))


def _const_spec(shape):
    return pl.BlockSpec(shape, lambda *_: (0,) * len(shape), pipeline_mode=pl.Buffered(1))


def _params(semantics):
    return pltpu.CompilerParams(dimension_semantics=semantics, vmem_limit_bytes=VMEM_LIMIT)


def _split3(x):
    a = x.astype(BF16)
    r = x - a.astype(F32)
    b = r.astype(BF16)
    c = (r - b.astype(F32)).astype(BF16)
    return a, b, c


def _dot(a, b):
    return jnp.dot(a, b, preferred_element_type=F32)


def _dot_nt(a, b):
    return lax.dot_general(a, b, _NT, preferred_element_type=F32)


def _rmsnorm_rows(x, g):
    ms = jnp.mean(x * x, axis=-1, keepdims=True)
    return x * lax.rsqrt(ms + EPS) * g


def _log_sigmoid(x):
    return jnp.minimum(x, 0.0) - jnp.log1p(jnp.exp(-jnp.abs(x)))


def _fox_proj_kernel(x_ref, gn_ref, w_ref, wf_ref, bf_ref, gq_ref, gk_ref, bd_ref, tril_ref,
                     q_ref, kf_ref, kb_ref, vf_ref, vb_ref, og_ref, lf_ref, c_ref, carry_ref,
                     *, tm, front, head_dim, width):
    i = pl.program_id(0)

    @pl.when(i == 0)
    def _():
        carry_ref[...] = jnp.zeros_like(carry_ref)

    h = _rmsnorm_rows(x_ref[...], gn_ref[...]).astype(BF16)

    def head_norm(z, g):
        zz = (z * z).astype(BF16)
        ss = jnp.concatenate(
            [_dot(zz[:, c * MXU_DIM:(c + 1) * MXU_DIM], bd_ref[...]) for c in range(width // MXU_DIM)], axis=1)
        return z * lax.rsqrt(ss * (1.0 / head_dim) + EPS) * g

    zq = _dot(h, w_ref[:, 0:width])
    q_ref[...] = (head_norm(zq, gq_ref[...]) * (head_dim ** -0.5 * LOG2E)).astype(BF16)
    zk = _dot(h, w_ref[:, width:2 * width])
    kn = head_norm(zk, gk_ref[...])
    kf_ref[...] = kn
    kb_ref[...] = kn.astype(BF16)
    zv = _dot(h, w_ref[:, 2 * width:3 * width])
    vf_ref[...] = zv
    vb_ref[...] = zv.astype(BF16)
    og_ref[...] = _dot(h, w_ref[:, 3 * width:4 * width])

    lf = _log_sigmoid(_dot(h, wf_ref[...]) + bf_ref[...])
    lf_ref[...] = lf
    row = i * tm + lax.broadcasted_iota(jnp.int32, (tm, 1), 0)
    real = row >= front
    l1, l2, l3 = _split3(jnp.where(real, lf, 0.0))
    tril = tril_ref[...]
    c = _dot(tril, l1) + _dot(tril, l2) + _dot(tril, l3) + carry_ref[...]
    carry_ref[...] = c[tm - 1:tm, :]
    c_ref[...] = jnp.where(real, c, MASKED_KEY)


def _fox_proj(x, gn, w_main, w_f, b_f, gq, gk, *, front, head_dim):
    t_all, d = x.shape
    width = w_main.shape[1] // 4
    tm = PROJ_TILE
    blk = MXU_DIM // head_dim
    bd = jnp.kron(jnp.eye(blk, dtype=F32), jnp.ones((head_dim, head_dim), F32)).astype(BF16)
    tril = (lax.broadcasted_iota(jnp.int32, (tm, tm), 0) >= lax.broadcasted_iota(jnp.int32, (tm, tm), 1)).astype(BF16)
    row = lambda n: pl.BlockSpec((tm, n), lambda i: (i, 0))
    sds = jax.ShapeDtypeStruct
    return pl.pallas_call(
        functools.partial(_fox_proj_kernel, tm=tm, front=front, head_dim=head_dim, width=width),
        grid=(t_all // tm,),
        in_specs=[row(d), _const_spec((1, d)), _const_spec(w_main.shape), _const_spec(w_f.shape),
                  _const_spec((1, LANES)), _const_spec((1, width)), _const_spec((1, width)),
                  _const_spec(bd.shape), _const_spec(tril.shape)],
        out_specs=[row(width), row(width), row(width), row(width), row(width), row(width), row(LANES), row(LANES)],
        out_shape=[sds((t_all, width), BF16), sds((t_all, width), F32), sds((t_all, width), BF16),
                   sds((t_all, width), F32), sds((t_all, width), BF16), sds((t_all, width), F32),
                   sds((t_all, LANES), F32), sds((t_all, LANES), F32)],
        scratch_shapes=[pltpu.VMEM((1, LANES), F32)],
        compiler_params=_params(("arbitrary",)),
        name="fox_proj",
    )(x, gn, w_main, w_f, b_f, gq, gk, bd, tril)


def _fox_attn_kernel(first_ref, q_ref, k_ref, v_ref, c_ref, og_ref, o_ref, qm_ref, m_ref, alpha_ref, p_ref, acc_ref,
                     *, tile, head_dim):
    qi = pl.program_id(1)
    first = first_ref[pl.program_id(0), qi]
    lane = lax.broadcasted_iota(jnp.int32, (1, 2 * head_dim), 1)
    lo = lane < head_dim
    head_lanes = (lo, jnp.logical_not(lo))
    q = q_ref[...]
    for hh in range(2):
        qm_ref[hh] = jnp.where(head_lanes[hh], q, jnp.zeros_like(q))
    m_ref[...] = jnp.full(m_ref.shape, -jnp.inf, F32)
    acc_ref[...] = jnp.zeros(acc_ref.shape, F32)
    alpha_ref[1] = jnp.ones(alpha_ref.shape[1:], F32)
    p_ref[1] = jnp.zeros(p_ref.shape[1:], BF16)
    one = jnp.ones((), BF16)

    def scores(j, slot, diagonal):
        k = k_ref[pl.ds(pl.multiple_of(j * tile, tile), tile), :]
        for hh in range(2):
            c_last = c_ref[hh, pl.ds(qi, 1), :][:, tile - 1:tile]
            bias = (c_ref[hh, pl.ds(j, 1), :] - c_last) * LOG2E
            u = _dot_nt(qm_ref[hh], k) - bias
            if diagonal:
                r = lax.broadcasted_iota(jnp.int32, (tile, tile), 0)
                cidx = lax.broadcasted_iota(jnp.int32, (tile, tile), 1)
                u = jnp.where(cidx <= r, u, -jnp.inf)
            m_prev = m_ref[hh]
            m_new = jnp.maximum(m_prev, jnp.max(u, axis=1, keepdims=True))
            m_wide = jnp.concatenate([m_new] * (tile // LANES), axis=1)
            p_ref[slot, hh] = jnp.exp2(u - m_wide).astype(BF16)
            alpha_ref[slot, hh] = jnp.exp2(m_prev - m_new)
            m_ref[hh] = m_new

    def accumulate(j, slot):
        v = v_ref[pl.ds(pl.multiple_of(j * tile, tile), tile), :]
        for hh in range(2):
            v_aug = jnp.where(head_lanes[hh], v, one)
            acc_ref[hh] = alpha_ref[slot, hh] * acc_ref[hh] + _dot(p_ref[slot, hh], v_aug)

    n_off = qi - first
    odd = lax.rem(n_off, 2)

    @pl.when(odd == 1)
    def _():
        scores(first, 0, False)
        accumulate(first, 0)

    def body(jj, pending):
        j0 = first + odd + 2 * jj
        scores(j0, 0, False)
        accumulate(pending, 1)
        scores(j0 + 1, 1, False)
        accumulate(j0, 0)
        return j0 + 1

    pending = lax.fori_loop(0, n_off // 2, body, first)
    scores(qi, 0, True)
    accumulate(pending, 1)
    accumulate(qi, 0)

    outs = []
    for hh in range(2):
        a = acc_ref[hh]
        outs.append(a / pltpu.roll(a, head_dim, 1))
    o = jnp.where(lo, outs[0], outs[1])
    o_ref[...] = (o * jax.nn.sigmoid(og_ref[...])).astype(BF16)


def _first_needed_block(q_b, k_b, csum, *, t_prompt, head_dim, tile):
    n_heads = q_b.shape[1] // head_dim

    def max_row_norm(x):
        sq = jnp.square(x[:t_prompt].astype(F32)).reshape(t_prompt, n_heads, head_dim).sum(-1)
        return jnp.sqrt(jnp.max(sq, axis=0))

    c = csum[:t_prompt, :n_heads]
    c_first, c_last = c[0::tile], c[tile - 1::tile]
    bound = 2.0 * max_row_norm(q_b) * max_row_norm(k_b) + LOG2E * (c_first[:, None, :] - c_last[None, :, :])
    blk = jnp.arange(c_first.shape[0])
    earlier = blk[None, :, None] < blk[:, None, None]
    skipped = jnp.sum(jnp.logical_and(bound < -ZERO_WEIGHT_LOG2, earlier), axis=1)
    return skipped.reshape(-1, n_heads // 2, 2).min(-1).T.astype(jnp.int32)


def _fox_attn_prompt(q_b, k_b, v_b, csum, og, *, t_prompt, head_dim):
    t_all, width = q_b.shape
    tile = ROW_TILE
    pair = 2 * head_dim
    n_blk = t_prompt // tile
    n_heads = width // head_dim
    c3 = csum[:t_prompt, :n_heads].T.reshape(n_heads, n_blk, tile)
    first = _first_needed_block(q_b, k_b, csum, t_prompt=t_prompt, head_dim=head_dim, tile=tile)
    return pl.pallas_call(
        functools.partial(_fox_attn_kernel, tile=tile, head_dim=head_dim),
        grid=(width // pair, n_blk),
        in_specs=[pl.BlockSpec(memory_space=pltpu.SMEM),
                  pl.BlockSpec((tile, pair), lambda p, i: (i, p)),
                  pl.BlockSpec((t_prompt, pair), lambda p, i: (0, p)),
                  pl.BlockSpec((t_prompt, pair), lambda p, i: (0, p)),
                  pl.BlockSpec((2, n_blk, tile), lambda p, i: (p, 0, 0)),
                  pl.BlockSpec((tile, pair), lambda p, i: (i, p))],
        out_specs=pl.BlockSpec((tile, pair), lambda p, i: (i, p)),
        out_shape=jax.ShapeDtypeStruct((t_all, width), BF16),
        scratch_shapes=[pltpu.VMEM((2, tile, pair), BF16), pltpu.VMEM((2, tile, LANES), F32),
                        pltpu.VMEM((2, 2, tile, LANES), F32), pltpu.VMEM((2, 2, tile, tile), BF16),
                        pltpu.VMEM((2, tile, pair), F32)],
        compiler_params=_params(("arbitrary", "arbitrary")),
        name="fox_attn_prompt",
    )(first, q_b, k_b, v_b, c3, og)


def _fox_sample_kernel(prev_ref, q_ref, kn_ref, vn_ref, og_ref, lfn_ref, kc_ref, vc_ref, lfc_ref, su_ref, bt_ref,
                       o_ref, *, n_new, head_dim, n_heads):
    del prev_ref
    b = pl.program_id(0)
    pair = 2 * head_dim
    n_rows = kn_ref.shape[0]

    @pl.when(b == 0)
    def _():
        o_ref[...] = jnp.zeros_like(o_ref)

    su = su_ref[...]
    suf = sum(_dot(t, su) for t in _split3(lfc_ref[0])) * LOG2E
    bt = bt_ref[...]
    pre = sum(_dot(t, bt) for t in _split3(lfn_ref[...])) * LOG2E

    r0 = pl.multiple_of(b * n_new, n_new)
    qrow = lax.broadcasted_iota(jnp.int32, (n_new, n_rows), 0)
    col = lax.broadcasted_iota(jnp.int32, (n_new, n_rows), 1)
    visible = jnp.logical_and(col >= b * n_new, col <= b * n_new + qrow)
    lane = lax.broadcasted_iota(jnp.int32, (1, pair), 1)
    lo = lane < head_dim
    head_lanes = (lo, jnp.logical_not(lo))

    for p in range(n_heads // 2):
        cols = slice(p * pair, (p + 1) * pair)
        q = q_ref[pl.ds(r0, n_new), cols]
        k_c = kc_ref[0, :, cols].astype(BF16)
        v_c = vc_ref[0, :, cols].astype(BF16)
        k_n = kn_ref[:, cols]
        v_n = vn_ref[:, cols]
        outs = []
        for hh in range(2):
            head = 2 * p + hh
            qh = jnp.where(head_lanes[hh], q, jnp.zeros_like(q))
            u_c = _dot_nt(qh, k_c) + suf[head:head + 1, :]
            u_n = jnp.where(visible, _dot_nt(qh, k_n) - pre[head:head + 1, :], -jnp.inf)
            m = jnp.maximum(jnp.max(u_c, axis=1, keepdims=True), jnp.max(u_n, axis=1, keepdims=True))
            p_c = jnp.exp2(u_c - m)
            p_n = jnp.exp2(u_n - m)
            den = jnp.sum(p_c, axis=1, keepdims=True) + jnp.sum(p_n, axis=1, keepdims=True)
            outs.append((_dot(p_c.astype(BF16), v_c) + _dot(p_n.astype(BF16), v_n)) / den)
        o = jnp.where(lo, outs[0], outs[1])
        gate = jax.nn.sigmoid(og_ref[pl.ds(r0, n_new), cols])
        o_ref[pl.ds(r0, n_new), cols] = (o * gate).astype(BF16)


def _fox_attn_sample(o_prev, q_b, k_b, v_b, og, lf_new_t, cache_k, cache_v, cache_lf_t, *, t_prompt, n_new, head_dim):
    t_all, width = q_b.shape
    n_streams, past, _ = cache_k.shape
    n_heads = width // head_dim
    n_rows = n_streams * n_new
    assert t_prompt % n_rows == 0 and n_rows % LANES == 0
    su = (lax.broadcasted_iota(jnp.int32, (past, past), 0) > lax.broadcasted_iota(jnp.int32, (past, past), 1)).astype(BF16)
    rr = lax.broadcasted_iota(jnp.int32, (n_rows, n_rows), 0)
    cc = lax.broadcasted_iota(jnp.int32, (n_rows, n_rows), 1)
    bt = jnp.logical_and(rr // n_new == cc // n_new, rr <= cc).astype(BF16)
    rows = pl.BlockSpec((n_rows, width), lambda b: (t_prompt // n_rows, 0), pipeline_mode=pl.Buffered(1))
    tail = t_all - t_prompt
    return pl.pallas_call(
        functools.partial(_fox_sample_kernel, n_new=n_new, head_dim=head_dim, n_heads=n_heads),
        grid=(n_streams,),
        in_specs=[pl.BlockSpec(memory_space=pl.ANY), rows, rows, rows, rows,
                  _const_spec(lf_new_t.shape),
                  pl.BlockSpec((1, past, width), lambda b: (b, 0, 0)),
                  pl.BlockSpec((1, past, width), lambda b: (b, 0, 0)),
                  pl.BlockSpec((1, n_heads, past), lambda b: (b, 0, 0)),
                  _const_spec(su.shape), _const_spec(bt.shape)],
        out_specs=pl.BlockSpec((tail, width), lambda b: (t_prompt // tail, 0)),
        out_shape=jax.ShapeDtypeStruct((t_all, width), BF16),
        input_output_aliases={0: 0},
        compiler_params=_params(("arbitrary",)),
        name="fox_attn_sample",
    )(o_prev, q_b, k_b, v_b, og, lf_new_t, cache_k, cache_v, cache_lf_t, su, bt)


def _out_proj_kernel(y_ref, x_ref, w_ref, o_ref):
    o_ref[...] = x_ref[...] + _dot(y_ref[...], w_ref[...])


def _out_proj(y_b, x, w):
    t_all, d = x.shape
    k = y_b.shape[1]
    tm = ROW_TILE
    return pl.pallas_call(
        _out_proj_kernel,
        grid=(t_all // tm,),
        in_specs=[pl.BlockSpec((tm, k), lambda i: (i, 0)), pl.BlockSpec((tm, d), lambda i: (i, 0)),
                  _const_spec(w.shape)],
        out_specs=pl.BlockSpec((tm, d), lambda i: (i, 0)),
        out_shape=jax.ShapeDtypeStruct((t_all, d), F32),
        compiler_params=_params(("arbitrary",)),
        name="out_proj",
    )(y_b, x, w)


def _ret_proj_kernel(x_ref, gn_ref, w_ref, inv_ref, q_ref, kt_ref, v_ref, g_ref,
                     *, tm, t_prompt, pos0, past, n_new, n_heads, qk_dim, v_width):
    i = pl.program_id(0)
    h = _rmsnorm_rows(x_ref[...], gn_ref[...]).astype(BF16)
    row = i * tm + lax.broadcasted_iota(jnp.int32, (tm, 1), 0)
    pos = jnp.where(row < t_prompt, row - pos0, past + lax.rem(row - t_prompt, n_new))
    ang = pos.astype(F32) * inv_ref[...]
    cos = jnp.cos(ang)
    sin = jnp.sin(ang)
    half = qk_dim // 2
    qk_width = n_heads * qk_dim

    def rotary(z):
        parts = []
        for hh in range(n_heads):
            z1 = z[:, hh * qk_dim:hh * qk_dim + half]
            z2 = z[:, hh * qk_dim + half:(hh + 1) * qk_dim]
            parts += [z1 * cos - z2 * sin, z1 * sin + z2 * cos]
        return jnp.concatenate(parts, axis=1)

    q = rotary(_dot(h, w_ref[:, 0:qk_width])) * (qk_dim ** -0.5)
    q_ref[...] = q.astype(BF16)
    kt_ref[...] = rotary(_dot(h, w_ref[:, qk_width:2 * qk_width])).T
    v_ref[...] = _dot(h, w_ref[:, 2 * qk_width:2 * qk_width + v_width]).astype(BF16)
    g_ref[...] = _dot(h, w_ref[:, 2 * qk_width + v_width:])


def _ret_proj(x, gn, w_in, inv_freq, *, t_prompt, pos0, past, n_new, n_heads, qk_dim, v_width):
    t_all, d = x.shape
    tm = PROJ_TILE
    qk_width = n_heads * qk_dim
    row = lambda n: pl.BlockSpec((tm, n), lambda i: (i, 0))
    sds = jax.ShapeDtypeStruct
    return pl.pallas_call(
        functools.partial(_ret_proj_kernel, tm=tm, t_prompt=t_prompt, pos0=pos0, past=past, n_new=n_new,
                          n_heads=n_heads, qk_dim=qk_dim, v_width=v_width),
        grid=(t_all // tm,),
        in_specs=[row(d), _const_spec((1, d)), _const_spec(w_in.shape), _const_spec(inv_freq.shape)],
        out_specs=[row(qk_width), pl.BlockSpec((qk_width, tm), lambda i: (0, i)), row(v_width), row(v_width)],
        out_shape=[sds((t_all, qk_width), BF16), sds((qk_width, t_all), F32),
                   sds((t_all, v_width), BF16), sds((t_all, v_width), F32)],
        compiler_params=_params(("arbitrary",)),
        name="ret_proj",
    )(x, gn, w_in, inv_freq)


def _group_norm_gate(o, g, gn):
    mu = jnp.mean(o, axis=-1, keepdims=True)
    var = jnp.mean(jnp.square(o - mu), axis=-1, keepdims=True)
    y = (o - mu) * lax.rsqrt(var + EPS) * gn
    return (g * jax.nn.sigmoid(g) * y).astype(BF16)


def _ret_prompt_kernel(logg_ref, q_ref, kt_ref, v_ref, g_ref, gn_ref, y_ref, st_ref, s_ref, *, tb):
    hh = pl.program_id(0)
    blk = pl.program_id(1)
    lg = jnp.full((1, 1), logg_ref[hh], F32)

    @pl.when(blk == 0)
    def _():
        s_ref[...] = jnp.zeros_like(s_ref)

    r = lax.broadcasted_iota(jnp.int32, (tb, tb), 0)
    c = lax.broadcasted_iota(jnp.int32, (tb, tb), 1)
    decay = jnp.exp(lg * jnp.abs(r - c).astype(F32))
    dmat = jnp.where(c // CHUNK <= r // CHUNK, decay, 0.0)

    q = q_ref[...]
    kt = kt_ref[...]
    v = v_ref[...]
    s = _dot(q, kt.astype(BF16)) * dmat
    state = s_ref[...]
    q_dec = jnp.exp(lg * (lax.broadcasted_iota(jnp.int32, (tb, 1), 0) + 1).astype(F32))
    o = _dot(s.astype(BF16), v) + _dot(q, state.astype(BF16)) * q_dec
    k_dec = jnp.exp(lg * (tb - 1 - lax.broadcasted_iota(jnp.int32, (1, tb), 1)).astype(F32))
    new_state = state * jnp.exp(lg * tb) + _dot((kt * k_dec).astype(BF16), v)
    s_ref[...] = new_state
    y_ref[...] = _group_norm_gate(o, g_ref[...], gn_ref[...])

    @pl.when(blk == pl.num_programs(1) - 1)
    def _():
        st_ref[0] = new_state


def _ret_prompt(logg, q_b, k_t, v_b, g, gn, *, t_prompt, n_heads):
    t_all, qk_width = q_b.shape
    v_width = v_b.shape[1]
    qk_dim, v_dim = qk_width // n_heads, v_width // n_heads
    tb = RET_BLOCK
    return pl.pallas_call(
        functools.partial(_ret_prompt_kernel, tb=tb),
        grid=(n_heads, t_prompt // tb),
        in_specs=[pl.BlockSpec(memory_space=pltpu.SMEM),
                  pl.BlockSpec((tb, qk_dim), lambda h, i: (i, h)),
                  pl.BlockSpec((qk_dim, tb), lambda h, i: (h, i)),
                  pl.BlockSpec((tb, v_dim), lambda h, i: (i, h)),
                  pl.BlockSpec((tb, v_dim), lambda h, i: (i, h)),
                  pl.BlockSpec((1, v_dim), lambda h, i: (0, h))],
        out_specs=[pl.BlockSpec((tb, v_dim), lambda h, i: (i, h)),
                   pl.BlockSpec((1, qk_dim, v_dim), lambda h, i: (h, 0, 0))],
        out_shape=[jax.ShapeDtypeStruct((t_all, v_width), BF16),
                   jax.ShapeDtypeStruct((n_heads, qk_dim, v_dim), F32)],
        scratch_shapes=[pltpu.VMEM((qk_dim, v_dim), F32)],
        compiler_params=_params(("arbitrary", "arbitrary")),
        name="ret_prompt",
    )(logg, q_b, k_t, v_b, g, gn)


def _ret_sample_kernel(logg_ref, prev_ref, q_ref, kt_ref, v_ref, g_ref, gn_ref, st_ref, y_ref, nst_ref, *, n_new):
    del prev_ref
    hh = pl.program_id(0)
    b = pl.program_id(1)
    lg = jnp.full((1, 1), logg_ref[hh], F32)
    n_rows = kt_ref.shape[1]

    @pl.when(b == 0)
    def _():
        y_ref[...] = jnp.zeros_like(y_ref)

    r0 = pl.multiple_of(b * n_new, n_new)
    q = q_ref[pl.ds(r0, n_new), :]
    kt = kt_ref[...]
    v = v_ref[...]
    qrow = lax.broadcasted_iota(jnp.int32, (n_new, n_rows), 0)
    col = lax.broadcasted_iota(jnp.int32, (n_new, n_rows), 1) - b * n_new
    own = jnp.logical_and(col >= 0, col < n_new)
    dmat = jnp.where(own, jnp.exp(lg * jnp.abs(qrow - col).astype(F32)), 0.0)
    s = _dot(q, kt.astype(BF16)) * dmat
    state = st_ref[0, 0]
    q_dec = jnp.exp(lg * (lax.broadcasted_iota(jnp.int32, (n_new, 1), 0) + 1).astype(F32))
    o = _dot(s.astype(BF16), v) + _dot(q, state.astype(BF16)) * q_dec
    col1 = lax.broadcasted_iota(jnp.int32, (1, n_rows), 1) - b * n_new
    own1 = jnp.logical_and(col1 >= 0, col1 < n_new)
    k_dec = jnp.where(own1, jnp.exp(lg * (n_new - 1 - col1).astype(F32)), 0.0)
    nst_ref[0, 0] = state * jnp.exp(lg * n_new) + _dot((kt * k_dec).astype(BF16), v)
    y_ref[pl.ds(r0, n_new), :] = _group_norm_gate(o, g_ref[pl.ds(r0, n_new), :], gn_ref[...])


def _ret_sample(logg, y_prev, q_b, k_t, v_b, g, gn, state, *, t_prompt, n_new):
    t_all, qk_width = q_b.shape
    v_width = v_b.shape[1]
    n_streams, n_heads, qk_dim, v_dim = state.shape
    n_rows = n_streams * n_new
    tail = t_all - t_prompt
    blk = t_prompt // n_rows
    return pl.pallas_call(
        functools.partial(_ret_sample_kernel, n_new=n_new),
        grid=(n_heads, n_streams),
        in_specs=[pl.BlockSpec(memory_space=pltpu.SMEM),
                  pl.BlockSpec(memory_space=pl.ANY),
                  pl.BlockSpec((n_rows, qk_dim), lambda h, b: (blk, h)),
                  pl.BlockSpec((qk_dim, n_rows), lambda h, b: (h, blk)),
                  pl.BlockSpec((n_rows, v_dim), lambda h, b: (blk, h)),
                  pl.BlockSpec((n_rows, v_dim), lambda h, b: (blk, h)),
                  pl.BlockSpec((1, v_dim), lambda h, b: (0, h)),
                  pl.BlockSpec((1, 1, qk_dim, v_dim), lambda h, b: (b, h, 0, 0))],
        out_specs=[pl.BlockSpec((tail, v_dim), lambda h, b: (t_prompt // tail, h)),
                   pl.BlockSpec((1, 1, qk_dim, v_dim), lambda h, b: (b, h, 0, 0))],
        out_shape=[jax.ShapeDtypeStruct((t_all, v_width), BF16),
                   jax.ShapeDtypeStruct(state.shape, F32)],
        input_output_aliases={1: 0},
        compiler_params=_params(("arbitrary", "arbitrary")),
        name="ret_sample",
    )(logg, y_prev, q_b, k_t, v_b, g, gn, state)


def _top_values(s, k, with_rank):
    work = s
    vals = []
    rank = jnp.full(s.shape, NO_RANK, F32) if with_rank else None
    for r in range(k):
        m = jnp.max(work, axis=0, keepdims=True)
        vals.append(m)
        hit = work == m
        if with_rank:
            rank = jnp.where(hit, float(r), rank)
        work = jnp.where(hit, -jnp.inf, work)
    return vals, rank


def _peer_kernel(x_ref, gn_ref, wq_ref, sk_ref, u_ref, vt_ref, o_ref,
                 hb_ref, e0_ref, cnt_ref, e1_ref, rank_ref, w_ref, y_ref, *, n_heads, n_keys, topk):
    c = pl.program_id(1)
    ce, tm = u_ref.shape[0], x_ref.shape[0]
    rows_per_chunk = ce // n_keys
    sub = rank_ref.shape[2]

    @pl.when(c == 0)
    def _():
        hb = _rmsnorm_rows(x_ref[...], gn_ref[...]).astype(BF16)
        hb_ref[...] = hb
        y_ref[...] = jnp.zeros_like(y_ref)
        for h in range(n_heads):
            qh = _dot_nt(wq_ref[h * 2 * n_keys:(h + 1) * 2 * n_keys, :], hb)
            s0 = _dot(sk_ref[2 * h], qh[0:n_keys].astype(BF16))
            s1 = _dot(sk_ref[2 * h + 1], qh[n_keys:2 * n_keys].astype(BF16))
            a, rank0 = _top_values(s0, topk, True)
            b, rank1 = _top_values(s1, topk, True)
            half = topk // 2
            b_all = jnp.concatenate(b, axis=0)
            b_half = jnp.concatenate(b[:half], axis=0)
            cand = jnp.concatenate(
                [a[0] + b_all] + [a[i] + b_half for i in range(1, half)] + [jnp.concatenate(a[half:], axis=0) + b[0]],
                axis=0)
            f, _ = _top_values(cand, topk, False)
            tau = f[topk - 1]
            z = sum(jnp.exp(fk - f[0]) for fk in f)
            a_all = jnp.concatenate(a, axis=0)
            cnt_sorted = sum(jnp.where(a_all + bj >= tau, 1.0, 0.0) for bj in b)
            cnt = jnp.zeros_like(s0)
            for r in range(topk):
                cnt = jnp.where(rank0 == float(r), cnt_sorted[r:r + 1, :], cnt)
            cnt_ref[h] = cnt
            e0_ref[h] = jnp.exp(s0 - a[0])
            e1_ref[h] = (jnp.exp(s1 - b[0]) * (0.5 / z)).astype(BF16).reshape(n_keys // sub, sub, tm)
            rank_ref[h] = rank1.astype(BF16).reshape(n_keys // sub, sub, tm)

    hb = hb_ref[...]
    half = ce // 2
    acts = (_dot_nt(u_ref[0:half, :], hb), _dot_nt(u_ref[half:ce, :], hb))
    rows_per_slab = MXU_DIM // n_keys
    y_add = None
    for slab in range(ce // MXU_DIM):
        for il in range(slab * rows_per_slab, (slab + 1) * rows_per_slab):
            i = c * rows_per_chunk + il
            gate = None
            for h in range(n_heads):
                n_sel = jnp.broadcast_to(cnt_ref[h, pl.ds(i, 1), :], (sub, tm)).astype(BF16)
                e0 = jnp.broadcast_to(e0_ref[h, pl.ds(i, 1), :], (sub, tm)).astype(BF16)
                term = jnp.where(rank_ref[h] < n_sel[None], e1_ref[h], jnp.zeros((), BF16)) * e0[None]
                gate = term if gate is None else gate + term
            r0 = il * n_keys
            x_act = acts[r0 // half][r0 % half:r0 % half + n_keys, :]
            gelu2 = x_act * (1.0 + lax.erf(x_act * (2.0 ** -0.5)))
            w_ref[r0:r0 + n_keys, :] = gate.reshape(n_keys, tm) * gelu2.astype(BF16)
        rows = slice(slab * MXU_DIM, (slab + 1) * MXU_DIM)
        part = _dot(vt_ref[:, rows], w_ref[rows, :])
        y_add = part if y_add is None else y_add + part
    y_ref[...] += y_add

    @pl.when(c == pl.num_programs(1) - 1)
    def _():
        o_ref[...] = x_ref[...] + y_ref[...].T


def _peer(x, gn, wq_t, subkeys, u, v_t, *, n_heads, n_keys):
    t_all, d = x.shape
    n_experts = u.shape[0]
    tm = ROW_TILE
    ce = PEER_EXPERT_CHUNK
    head_scratch = pltpu.VMEM((n_heads, n_keys, tm), F32)
    head_scratch_b = pltpu.VMEM((n_heads, n_keys // BF16_TILE_ROWS, BF16_TILE_ROWS, tm), BF16)
    return pl.pallas_call(
        functools.partial(_peer_kernel, n_heads=n_heads, n_keys=n_keys, topk=PEER_TOPK),
        grid=(t_all // tm, n_experts // ce),
        in_specs=[pl.BlockSpec((tm, d), lambda i, c: (i, 0)), _const_spec((1, d)), _const_spec(wq_t.shape),
                  _const_spec(subkeys.shape),
                  pl.BlockSpec((ce, d), lambda i, c: (c, 0)),
                  pl.BlockSpec((d, ce), lambda i, c: (0, c))],
        out_specs=pl.BlockSpec((tm, d), lambda i, c: (i, 0)),
        out_shape=jax.ShapeDtypeStruct((t_all, d), F32),
        scratch_shapes=[pltpu.VMEM((tm, d), BF16), head_scratch, head_scratch, head_scratch_b, head_scratch_b,
                        pltpu.VMEM((ce, tm), BF16), pltpu.VMEM((d, tm), F32)],
        compiler_params=_params(("arbitrary", "arbitrary")),
        name="peer",
    )(x, gn, wq_t, subkeys, u, v_t)


def kernel(x_prompt, x_sample, cache_fox_k, cache_fox_v, cache_fox_lf, state_ret, meta_tokens, norm_mix, norm_ffn,
           fox_w_in, fox_b_f, fox_q_norm, fox_k_norm, fox_w_out, ret_w_in, ret_gn, ret_w_out,
           peer_w_q, peer_subkeys, peer_u, peer_v):
    batch, seq, d = x_prompt.shape
    n_streams, n_new, _ = x_sample.shape
    n_meta = meta_tokens.shape[0]
    assert batch == 1 and n_meta == N_META
    depth = norm_mix.shape[0]
    n_fox, _, past, fox_heads, fox_hd = cache_fox_k.shape
    fox_width = fox_heads * fox_hd
    n_ret, _, ret_heads, ret_qk, ret_v = state_ret.shape
    ret_v_width = ret_heads * ret_v
    peer_heads, _, n_keys, peer_half = peer_subkeys.shape[1:]
    assert n_keys == LANES and peer_half == LANES and fox_width == d

    length = n_meta + seq
    front = (-length) % ROW_TILE
    t_prompt = front + length
    n_rows = n_streams * n_new
    tail = ROW_TILE
    assert n_rows <= tail
    t_all = t_prompt + tail

    x = jnp.concatenate([jnp.zeros((front, d), F32), meta_tokens.astype(F32), x_prompt[0],
                         x_sample.reshape(n_rows, d), jnp.zeros((tail - n_rows, d), F32)], axis=0)

    half = ret_qk // 2
    inv_freq = (ROPE_BASE ** (-jnp.arange(half, dtype=F32) / half)).reshape(1, half)
    logg = jnp.log(1.0 - 2.0 ** (-5.0 - jnp.arange(ret_heads, dtype=F32)))

    kp, vp, lfp, ks, vs, lfs, srp, srs = [], [], [], [], [], [], [], []
    for layer in range(depth):
        j = layer // 2
        gn_mix = norm_mix[layer].reshape(1, d)
        if layer % 2 == 0:
            w_in = fox_w_in[j]
            w_main = w_in[:, :4 * fox_width].astype(BF16)
            w_f = jnp.pad(w_in[:, 4 * fox_width:], ((0, 0), (0, LANES - fox_heads))).astype(BF16)
            b_f = jnp.pad(fox_b_f[j], (0, LANES - fox_heads)).reshape(1, LANES)
            gq = jnp.tile(fox_q_norm[j], fox_heads).reshape(1, fox_width)
            gk = jnp.tile(fox_k_norm[j], fox_heads).reshape(1, fox_width)
            q_b, k_f, k_b, v_f, v_b, og, lf, csum = _fox_proj(x, gn_mix, w_main, w_f, b_f, gq, gk,
                                                              front=front, head_dim=fox_hd)
            o_b = _fox_attn_prompt(q_b, k_b, v_b, csum, og, t_prompt=t_prompt, head_dim=fox_hd)
            lf_new_t = lf[t_prompt:t_prompt + n_rows, :fox_heads].T
            o_b = _fox_attn_sample(o_b, q_b, k_b, v_b, og, lf_new_t,
                                   cache_fox_k[j].reshape(n_streams, past, fox_width),
                                   cache_fox_v[j].reshape(n_streams, past, fox_width),
                                   cache_fox_lf[j].transpose(0, 2, 1),
                                   t_prompt=t_prompt, n_new=n_new, head_dim=fox_hd)
            x = _out_proj(o_b, x, fox_w_out[j].astype(BF16))
            kp.append(k_f[front:t_prompt].reshape(1, length, fox_heads, fox_hd))
            vp.append(v_f[front:t_prompt].reshape(1, length, fox_heads, fox_hd))
            lfp.append(lf[front:t_prompt, :fox_heads].reshape(1, length, fox_heads))
            ks.append(k_f[t_prompt:t_prompt + n_rows].reshape(n_streams, n_new, fox_heads, fox_hd))
            vs.append(v_f[t_prompt:t_prompt + n_rows].reshape(n_streams, n_new, fox_heads, fox_hd))
            lfs.append(lf[t_prompt:t_prompt + n_rows, :fox_heads].reshape(n_streams, n_new, fox_heads))
        else:
            q_b, k_t, v_b, g = _ret_proj(x, gn_mix, ret_w_in[j].astype(BF16), inv_freq,
                                         t_prompt=t_prompt, pos0=front + n_meta, past=past, n_new=n_new,
                                         n_heads=ret_heads, qk_dim=ret_qk, v_width=ret_v_width)
            gn_ret = ret_gn[j].reshape(1, ret_v_width)
            y_b, st_p = _ret_prompt(logg, q_b, k_t, v_b, g, gn_ret, t_prompt=t_prompt, n_heads=ret_heads)
            y_b, st_s = _ret_sample(logg, y_b, q_b, k_t, v_b, g, gn_ret, state_ret[j],
                                    t_prompt=t_prompt, n_new=n_new)
            x = _out_proj(y_b, x, ret_w_out[j].astype(BF16))
            srp.append(st_p[None])
            srs.append(st_s)
        x = _peer(x, norm_ffn[layer].reshape(1, d), peer_w_q[layer].T.astype(BF16),
                  peer_subkeys[layer].reshape(2 * peer_heads, n_keys, peer_half).astype(BF16),
                  peer_u[layer].astype(BF16), peer_v[layer].T.astype(BF16),
                  n_heads=peer_heads, n_keys=n_keys)

    y_prompt = x[front + n_meta:t_prompt][None]
    y_sample = x[t_prompt:t_prompt + n_rows].reshape(n_streams, n_new, d)
    return (y_prompt, y_sample, jnp.stack(kp), jnp.stack(vp), jnp.stack(lfp), jnp.stack(srp),
            jnp.stack(ks), jnp.stack(vs), jnp.stack(lfs), jnp.stack(srs))
```

```python
import functools
import math

import jax
import jax.numpy as jnp
from jax import lax
from jax.experimental import pallas as pl
from jax.experimental.pallas import tpu as pltpu

F32 = jnp.float32
BF16 = jnp.bfloat16

EPS = 1e-6
N_META = 16
CHUNK = 64
ROPE_BASE = 10000.0
PEER_TOPK = 16

LANES = 128
MXU_DIM = 256
BF16_TILE_ROWS = 16
VMEM_LIMIT = 56 * 1024 * 1024

ROW_TILE = 512
PROJ_TILE = 256
RET_BLOCK = 512
PEER_EXPERT_CHUNK = 2048
MASKED_KEY = 1e30
NO_RANK = 99.0
LOG2E = math.log2(math.e)
ZERO_WEIGHT_LOG2 = 160.0

_NT = (((1,), (1,)), ((), ()))


def _const_spec(shape):
    return pl.BlockSpec(shape, lambda *_: (0,) * len(shape), pipeline_mode=pl.Buffered(1))


def _params(semantics):
    return pltpu.CompilerParams(dimension_semantics=semantics, vmem_limit_bytes=VMEM_LIMIT)


def _split3(x):
    a = x.astype(BF16)
    r = x - a.astype(F32)
    b = r.astype(BF16)
    c = (r - b.astype(F32)).astype(BF16)
    return a, b, c


def _dot(a, b):
    return jnp.dot(a, b, preferred_element_type=F32)


def _dot_nt(a, b):
    return lax.dot_general(a, b, _NT, preferred_element_type=F32)


def _rmsnorm_rows(x, g):
    ms = jnp.mean(x * x, axis=-1, keepdims=True)
    return x * lax.rsqrt(ms + EPS) * g


def _log_sigmoid(x):
    return jnp.minimum(x, 0.0) - jnp.log1p(jnp.exp(-jnp.abs(x)))


def _fox_proj_kernel(x_ref, gn_ref, w_ref, wf_ref, bf_ref, gq_ref, gk_ref, bd_ref, tril_ref,
                     q_ref, kf_ref, kb_ref, vf_ref, vb_ref, og_ref, lf_ref, c_ref, carry_ref,
                     *, tm, front, head_dim, width):
    i = pl.program_id(0)

    @pl.when(i == 0)
    def _():
        carry_ref[...] = jnp.zeros_like(carry_ref)

    h = _rmsnorm_rows(x_ref[...], gn_ref[...]).astype(BF16)

    def head_norm(z, g):
        zz = (z * z).astype(BF16)
        ss = jnp.concatenate(
            [_dot(zz[:, c * MXU_DIM:(c + 1) * MXU_DIM], bd_ref[...]) for c in range(width // MXU_DIM)], axis=1)
        return z * lax.rsqrt(ss * (1.0 / head_dim) + EPS) * g

    zq = _dot(h, w_ref[:, 0:width])
    q_ref[...] = (head_norm(zq, gq_ref[...]) * (head_dim ** -0.5 * LOG2E)).astype(BF16)
    zk = _dot(h, w_ref[:, width:2 * width])
    kn = head_norm(zk, gk_ref[...])
    kf_ref[...] = kn
    kb_ref[...] = kn.astype(BF16)
    zv = _dot(h, w_ref[:, 2 * width:3 * width])
    vf_ref[...] = zv
    vb_ref[...] = zv.astype(BF16)
    og_ref[...] = _dot(h, w_ref[:, 3 * width:4 * width])

    lf = _log_sigmoid(_dot(h, wf_ref[...]) + bf_ref[...])
    lf_ref[...] = lf
    row = i * tm + lax.broadcasted_iota(jnp.int32, (tm, 1), 0)
    real = row >= front
    l1, l2, l3 = _split3(jnp.where(real, lf, 0.0))
    tril = tril_ref[...]
    c = _dot(tril, l1) + _dot(tril, l2) + _dot(tril, l3) + carry_ref[...]
    carry_ref[...] = c[tm - 1:tm, :]
    c_ref[...] = jnp.where(real, c, MASKED_KEY)


def _fox_proj(x, gn, w_main, w_f, b_f, gq, gk, *, front, head_dim):
    t_all, d = x.shape
    width = w_main.shape[1] // 4
    tm = PROJ_TILE
    blk = MXU_DIM // head_dim
    bd = jnp.kron(jnp.eye(blk, dtype=F32), jnp.ones((head_dim, head_dim), F32)).astype(BF16)
    tril = (lax.broadcasted_iota(jnp.int32, (tm, tm), 0) >= lax.broadcasted_iota(jnp.int32, (tm, tm), 1)).astype(BF16)
    row = lambda n: pl.BlockSpec((tm, n), lambda i: (i, 0))
    sds = jax.ShapeDtypeStruct
    return pl.pallas_call(
        functools.partial(_fox_proj_kernel, tm=tm, front=front, head_dim=head_dim, width=width),
        grid=(t_all // tm,),
        in_specs=[row(d), _const_spec((1, d)), _const_spec(w_main.shape), _const_spec(w_f.shape),
                  _const_spec((1, LANES)), _const_spec((1, width)), _const_spec((1, width)),
                  _const_spec(bd.shape), _const_spec(tril.shape)],
        out_specs=[row(width), row(width), row(width), row(width), row(width), row(width), row(LANES), row(LANES)],
        out_shape=[sds((t_all, width), BF16), sds((t_all, width), F32), sds((t_all, width), BF16),
                   sds((t_all, width), F32), sds((t_all, width), BF16), sds((t_all, width), F32),
                   sds((t_all, LANES), F32), sds((t_all, LANES), F32)],
        scratch_shapes=[pltpu.VMEM((1, LANES), F32)],
        compiler_params=_params(("arbitrary",)),
        name="fox_proj",
    )(x, gn, w_main, w_f, b_f, gq, gk, bd, tril)


def _fox_attn_kernel(first_ref, q_ref, k_ref, v_ref, c_ref, og_ref, o_ref, qm_ref, m_ref, alpha_ref, p_ref, acc_ref,
                     *, tile, head_dim):
    qi = pl.program_id(1)
    first = first_ref[pl.program_id(0), qi]
    lane = lax.broadcasted_iota(jnp.int32, (1, 2 * head_dim), 1)
    lo = lane < head_dim
    head_lanes = (lo, jnp.logical_not(lo))
    q = q_ref[...]
    for hh in range(2):
        qm_ref[hh] = jnp.where(head_lanes[hh], q, jnp.zeros_like(q))
    m_ref[...] = jnp.full(m_ref.shape, -jnp.inf, F32)
    acc_ref[...] = jnp.zeros(acc_ref.shape, F32)
    alpha_ref[1] = jnp.ones(alpha_ref.shape[1:], F32)
    p_ref[1] = jnp.zeros(p_ref.shape[1:], BF16)
    one = jnp.ones((), BF16)

    def scores(j, slot, diagonal):
        k = k_ref[pl.ds(pl.multiple_of(j * tile, tile), tile), :]
        for hh in range(2):
            c_last = c_ref[hh, pl.ds(qi, 1), :][:, tile - 1:tile]
            bias = (c_ref[hh, pl.ds(j, 1), :] - c_last) * LOG2E
            u = _dot_nt(qm_ref[hh], k) - bias
            if diagonal:
                r = lax.broadcasted_iota(jnp.int32, (tile, tile), 0)
                cidx = lax.broadcasted_iota(jnp.int32, (tile, tile), 1)
                u = jnp.where(cidx <= r, u, -jnp.inf)
            m_prev = m_ref[hh]
            m_new = jnp.maximum(m_prev, jnp.max(u, axis=1, keepdims=True))
            m_wide = jnp.concatenate([m_new] * (tile // LANES), axis=1)
            p_ref[slot, hh] = jnp.exp2(u - m_wide).astype(BF16)
            alpha_ref[slot, hh] = jnp.exp2(m_prev - m_new)
            m_ref[hh] = m_new

    def accumulate(j, slot):
        v = v_ref[pl.ds(pl.multiple_of(j * tile, tile), tile), :]
        for hh in range(2):
            v_aug = jnp.where(head_lanes[hh], v, one)
            acc_ref[hh] = alpha_ref[slot, hh] * acc_ref[hh] + _dot(p_ref[slot, hh], v_aug)

    n_off = qi - first
    odd = lax.rem(n_off, 2)

    @pl.when(odd == 1)
    def _():
        scores(first, 0, False)
        accumulate(first, 0)

    def body(jj, pending):
        j0 = first + odd + 2 * jj
        scores(j0, 0, False)
        accumulate(pending, 1)
        scores(j0 + 1, 1, False)
        accumulate(j0, 0)
        return j0 + 1

    pending = lax.fori_loop(0, n_off // 2, body, first)
    scores(qi, 0, True)
    accumulate(pending, 1)
    accumulate(qi, 0)

    outs = []
    for hh in range(2):
        a = acc_ref[hh]
        outs.append(a / pltpu.roll(a, head_dim, 1))
    o = jnp.where(lo, outs[0], outs[1])
    o_ref[...] = (o * jax.nn.sigmoid(og_ref[...])).astype(BF16)


def _first_needed_block(gq, gk, csum, *, n_heads, t_prompt, head_dim, tile):
    bf16_slack = 1.0 + 2.0 ** -7
    q_norm = math.sqrt(head_dim) * jnp.max(jnp.abs(gq)) * (head_dim ** -0.5 * LOG2E) * bf16_slack
    k_norm = math.sqrt(head_dim) * jnp.max(jnp.abs(gk)) * bf16_slack
    c = csum[:t_prompt, :n_heads]
    c_first, c_last = c[0::tile], c[tile - 1::tile]
    bound = 2.0 * q_norm * k_norm + LOG2E * (c_first[:, None, :] - c_last[None, :, :])
    blk = jnp.arange(c_first.shape[0])
    earlier = blk[None, :, None] < blk[:, None, None]
    skipped = jnp.sum(jnp.logical_and(bound < -ZERO_WEIGHT_LOG2, earlier), axis=1)
    return skipped.reshape(-1, n_heads // 2, 2).min(-1).T.astype(jnp.int32)


def _fox_attn_prompt(q_b, k_b, v_b, csum, og, gq, gk, *, t_prompt, head_dim):
    t_all, width = q_b.shape
    tile = ROW_TILE
    pair = 2 * head_dim
    n_blk = t_prompt // tile
    n_heads = width // head_dim
    c3 = csum[:t_prompt, :n_heads].T.reshape(n_heads, n_blk, tile)
    first = _first_needed_block(gq, gk, csum, n_heads=n_heads, t_prompt=t_prompt, head_dim=head_dim, tile=tile)
    return pl.pallas_call(
        functools.partial(_fox_attn_kernel, tile=tile, head_dim=head_dim),
        grid=(width // pair, n_blk),
        in_specs=[pl.BlockSpec(memory_space=pltpu.SMEM),
                  pl.BlockSpec((tile, pair), lambda p, i: (i, p)),
                  pl.BlockSpec((t_prompt, pair), lambda p, i: (0, p)),
                  pl.BlockSpec((t_prompt, pair), lambda p, i: (0, p)),
                  pl.BlockSpec((2, n_blk, tile), lambda p, i: (p, 0, 0)),
                  pl.BlockSpec((tile, pair), lambda p, i: (i, p))],
        out_specs=pl.BlockSpec((tile, pair), lambda p, i: (i, p)),
        out_shape=jax.ShapeDtypeStruct((t_all, width), BF16),
        scratch_shapes=[pltpu.VMEM((2, tile, pair), BF16), pltpu.VMEM((2, tile, LANES), F32),
                        pltpu.VMEM((2, 2, tile, LANES), F32), pltpu.VMEM((2, 2, tile, tile), BF16),
                        pltpu.VMEM((2, tile, pair), F32)],
        compiler_params=_params(("arbitrary", "arbitrary")),
        name="fox_attn_prompt",
    )(first, q_b, k_b, v_b, c3, og)


def _fox_sample_kernel(prev_ref, q_ref, kn_ref, vn_ref, og_ref, lfn_ref, kc_ref, vc_ref, lfc_ref, su_ref, bt_ref,
                       o_ref, *, n_new, head_dim, n_heads):
    del prev_ref
    b = pl.program_id(0)
    pair = 2 * head_dim
    n_rows = kn_ref.shape[0]

    @pl.when(b == 0)
    def _():
        o_ref[...] = jnp.zeros_like(o_ref)

    su = su_ref[...]
    suf = sum(_dot(t, su) for t in _split3(lfc_ref[0])) * LOG2E
    bt = bt_ref[...]
    pre = sum(_dot(t, bt) for t in _split3(lfn_ref[...])) * LOG2E

    r0 = pl.multiple_of(b * n_new, n_new)
    qrow = lax.broadcasted_iota(jnp.int32, (n_new, n_rows), 0)
    col = lax.broadcasted_iota(jnp.int32, (n_new, n_rows), 1)
    visible = jnp.logical_and(col >= b * n_new, col <= b * n_new + qrow)
    lane = lax.broadcasted_iota(jnp.int32, (1, pair), 1)
    lo = lane < head_dim
    head_lanes = (lo, jnp.logical_not(lo))

    for p in range(n_heads // 2):
        cols = slice(p * pair, (p + 1) * pair)
        q = q_ref[pl.ds(r0, n_new), cols]
        k_c = kc_ref[0, :, cols].astype(BF16)
        v_c = vc_ref[0, :, cols].astype(BF16)
        k_n = kn_ref[:, cols]
        v_n = vn_ref[:, cols]
        outs = []
        for hh in range(2):
            head = 2 * p + hh
            qh = jnp.where(head_lanes[hh], q, jnp.zeros_like(q))
            u_c = _dot_nt(qh, k_c) + suf[head:head + 1, :]
            u_n = jnp.where(visible, _dot_nt(qh, k_n) - pre[head:head + 1, :], -jnp.inf)
            m = jnp.maximum(jnp.max(u_c, axis=1, keepdims=True), jnp.max(u_n, axis=1, keepdims=True))
            p_c = jnp.exp2(u_c - m)
            p_n = jnp.exp2(u_n - m)
            den = jnp.sum(p_c, axis=1, keepdims=True) + jnp.sum(p_n, axis=1, keepdims=True)
            outs.append((_dot(p_c.astype(BF16), v_c) + _dot(p_n.astype(BF16), v_n)) / den)
        o = jnp.where(lo, outs[0], outs[1])
        gate = jax.nn.sigmoid(og_ref[pl.ds(r0, n_new), cols])
        o_ref[pl.ds(r0, n_new), cols] = (o * gate).astype(BF16)


def _fox_attn_sample(o_prev, q_b, k_b, v_b, og, lf_new_t, cache_k, cache_v, cache_lf_t, *, t_prompt, n_new, head_dim):
    t_all, width = q_b.shape
    n_streams, past, _ = cache_k.shape
    n_heads = width // head_dim
    n_rows = n_streams * n_new
    assert t_prompt % n_rows == 0 and n_rows % LANES == 0
    su = (lax.broadcasted_iota(jnp.int32, (past, past), 0) > lax.broadcasted_iota(jnp.int32, (past, past), 1)).astype(BF16)
    rr = lax.broadcasted_iota(jnp.int32, (n_rows, n_rows), 0)
    cc = lax.broadcasted_iota(jnp.int32, (n_rows, n_rows), 1)
    bt = jnp.logical_and(rr // n_new == cc // n_new, rr <= cc).astype(BF16)
    rows = pl.BlockSpec((n_rows, width), lambda b: (t_prompt // n_rows, 0), pipeline_mode=pl.Buffered(1))
    tail = t_all - t_prompt
    return pl.pallas_call(
        functools.partial(_fox_sample_kernel, n_new=n_new, head_dim=head_dim, n_heads=n_heads),
        grid=(n_streams,),
        in_specs=[pl.BlockSpec(memory_space=pl.ANY), rows, rows, rows, rows,
                  _const_spec(lf_new_t.shape),
                  pl.BlockSpec((1, past, width), lambda b: (b, 0, 0)),
                  pl.BlockSpec((1, past, width), lambda b: (b, 0, 0)),
                  pl.BlockSpec((1, n_heads, past), lambda b: (b, 0, 0)),
                  _const_spec(su.shape), _const_spec(bt.shape)],
        out_specs=pl.BlockSpec((tail, width), lambda b: (t_prompt // tail, 0)),
        out_shape=jax.ShapeDtypeStruct((t_all, width), BF16),
        input_output_aliases={0: 0},
        compiler_params=_params(("arbitrary",)),
        name="fox_attn_sample",
    )(o_prev, q_b, k_b, v_b, og, lf_new_t, cache_k, cache_v, cache_lf_t, su, bt)


def _out_proj_kernel(y_ref, x_ref, w_ref, o_ref):
    o_ref[...] = x_ref[...] + _dot(y_ref[...], w_ref[...])


def _out_proj(y_b, x, w):
    t_all, d = x.shape
    k = y_b.shape[1]
    tm = ROW_TILE
    return pl.pallas_call(
        _out_proj_kernel,
        grid=(t_all // tm,),
        in_specs=[pl.BlockSpec((tm, k), lambda i: (i, 0)), pl.BlockSpec((tm, d), lambda i: (i, 0)),
                  _const_spec(w.shape)],
        out_specs=pl.BlockSpec((tm, d), lambda i: (i, 0)),
        out_shape=jax.ShapeDtypeStruct((t_all, d), F32),
        compiler_params=_params(("arbitrary",)),
        name="out_proj",
    )(y_b, x, w)


def _ret_proj_kernel(x_ref, gn_ref, w_ref, inv_ref, q_ref, kt_ref, v_ref, g_ref,
                     *, tm, t_prompt, pos0, past, n_new, n_heads, qk_dim, v_width):
    i = pl.program_id(0)
    h = _rmsnorm_rows(x_ref[...], gn_ref[...]).astype(BF16)
    row = i * tm + lax.broadcasted_iota(jnp.int32, (tm, 1), 0)
    pos = jnp.where(row < t_prompt, row - pos0, past + lax.rem(row - t_prompt, n_new))
    ang = pos.astype(F32) * inv_ref[...]
    cos = jnp.cos(ang)
    sin = jnp.sin(ang)
    half = qk_dim // 2
    qk_width = n_heads * qk_dim

    def rotary(z):
        parts = []
        for hh in range(n_heads):
            z1 = z[:, hh * qk_dim:hh * qk_dim + half]
            z2 = z[:, hh * qk_dim + half:(hh + 1) * qk_dim]
            parts += [z1 * cos - z2 * sin, z1 * sin + z2 * cos]
        return jnp.concatenate(parts, axis=1)

    q = rotary(_dot(h, w_ref[:, 0:qk_width])) * (qk_dim ** -0.5)
    q_ref[...] = q.astype(BF16)
    kt_ref[...] = rotary(_dot(h, w_ref[:, qk_width:2 * qk_width])).T
    v_ref[...] = _dot(h, w_ref[:, 2 * qk_width:2 * qk_width + v_width]).astype(BF16)
    g_ref[...] = _dot(h, w_ref[:, 2 * qk_width + v_width:])


def _ret_proj(x, gn, w_in, inv_freq, *, t_prompt, pos0, past, n_new, n_heads, qk_dim, v_width):
    t_all, d = x.shape
    tm = PROJ_TILE
    qk_width = n_heads * qk_dim
    row = lambda n: pl.BlockSpec((tm, n), lambda i: (i, 0))
    sds = jax.ShapeDtypeStruct
    return pl.pallas_call(
        functools.partial(_ret_proj_kernel, tm=tm, t_prompt=t_prompt, pos0=pos0, past=past, n_new=n_new,
                          n_heads=n_heads, qk_dim=qk_dim, v_width=v_width),
        grid=(t_all // tm,),
        in_specs=[row(d), _const_spec((1, d)), _const_spec(w_in.shape), _const_spec(inv_freq.shape)],
        out_specs=[row(qk_width), pl.BlockSpec((qk_width, tm), lambda i: (0, i)), row(v_width), row(v_width)],
        out_shape=[sds((t_all, qk_width), BF16), sds((qk_width, t_all), F32),
                   sds((t_all, v_width), BF16), sds((t_all, v_width), F32)],
        compiler_params=_params(("arbitrary",)),
        name="ret_proj",
    )(x, gn, w_in, inv_freq)


def _group_norm_gate(o, g, gn):
    mu = jnp.mean(o, axis=-1, keepdims=True)
    var = jnp.mean(jnp.square(o - mu), axis=-1, keepdims=True)
    y = (o - mu) * lax.rsqrt(var + EPS) * gn
    return (g * jax.nn.sigmoid(g) * y).astype(BF16)


def _ret_prompt_kernel(logg_ref, q_ref, kt_ref, v_ref, g_ref, gn_ref, y_ref, st_ref, s_ref, *, tb):
    hh = pl.program_id(0)
    blk = pl.program_id(1)
    lg = jnp.full((1, 1), logg_ref[hh], F32)

    @pl.when(blk == 0)
    def _():
        s_ref[...] = jnp.zeros_like(s_ref)

    r = lax.broadcasted_iota(jnp.int32, (tb, tb), 0)
    c = lax.broadcasted_iota(jnp.int32, (tb, tb), 1)
    decay = jnp.exp(lg * jnp.abs(r - c).astype(F32))
    dmat = jnp.where(c // CHUNK <= r // CHUNK, decay, 0.0)

    q = q_ref[...]
    kt = kt_ref[...]
    v = v_ref[...]
    s = _dot(q, kt.astype(BF16)) * dmat
    state = s_ref[...]
    q_dec = jnp.exp(lg * (lax.broadcasted_iota(jnp.int32, (tb, 1), 0) + 1).astype(F32))
    o = _dot(s.astype(BF16), v) + _dot(q, state.astype(BF16)) * q_dec
    k_dec = jnp.exp(lg * (tb - 1 - lax.broadcasted_iota(jnp.int32, (1, tb), 1)).astype(F32))
    new_state = state * jnp.exp(lg * tb) + _dot((kt * k_dec).astype(BF16), v)
    s_ref[...] = new_state
    y_ref[...] = _group_norm_gate(o, g_ref[...], gn_ref[...])

    @pl.when(blk == pl.num_programs(1) - 1)
    def _():
        st_ref[0] = new_state


def _ret_prompt(logg, q_b, k_t, v_b, g, gn, *, t_prompt, n_heads):
    t_all, qk_width = q_b.shape
    v_width = v_b.shape[1]
    qk_dim, v_dim = qk_width // n_heads, v_width // n_heads
    tb = RET_BLOCK
    return pl.pallas_call(
        functools.partial(_ret_prompt_kernel, tb=tb),
        grid=(n_heads, t_prompt // tb),
        in_specs=[pl.BlockSpec(memory_space=pltpu.SMEM),
                  pl.BlockSpec((tb, qk_dim), lambda h, i: (i, h)),
                  pl.BlockSpec((qk_dim, tb), lambda h, i: (h, i)),
                  pl.BlockSpec((tb, v_dim), lambda h, i: (i, h)),
                  pl.BlockSpec((tb, v_dim), lambda h, i: (i, h)),
                  pl.BlockSpec((1, v_dim), lambda h, i: (0, h))],
        out_specs=[pl.BlockSpec((tb, v_dim), lambda h, i: (i, h)),
                   pl.BlockSpec((1, qk_dim, v_dim), lambda h, i: (h, 0, 0))],
        out_shape=[jax.ShapeDtypeStruct((t_all, v_width), BF16),
                   jax.ShapeDtypeStruct((n_heads, qk_dim, v_dim), F32)],
        scratch_shapes=[pltpu.VMEM((qk_dim, v_dim), F32)],
        compiler_params=_params(("arbitrary", "arbitrary")),
        name="ret_prompt",
    )(logg, q_b, k_t, v_b, g, gn)


def _ret_sample_kernel(logg_ref, prev_ref, q_ref, kt_ref, v_ref, g_ref, gn_ref, st_ref, y_ref, nst_ref, *, n_new):
    del prev_ref
    hh = pl.program_id(0)
    b = pl.program_id(1)
    lg = jnp.full((1, 1), logg_ref[hh], F32)
    n_rows = kt_ref.shape[1]

    @pl.when(b == 0)
    def _():
        y_ref[...] = jnp.zeros_like(y_ref)

    r0 = pl.multiple_of(b * n_new, n_new)
    q = q_ref[pl.ds(r0, n_new), :]
    kt = kt_ref[...]
    v = v_ref[...]
    qrow = lax.broadcasted_iota(jnp.int32, (n_new, n_rows), 0)
    col = lax.broadcasted_iota(jnp.int32, (n_new, n_rows), 1) - b * n_new
    own = jnp.logical_and(col >= 0, col < n_new)
    dmat = jnp.where(own, jnp.exp(lg * jnp.abs(qrow - col).astype(F32)), 0.0)
    s = _dot(q, kt.astype(BF16)) * dmat
    state = st_ref[0, 0]
    q_dec = jnp.exp(lg * (lax.broadcasted_iota(jnp.int32, (n_new, 1), 0) + 1).astype(F32))
    o = _dot(s.astype(BF16), v) + _dot(q, state.astype(BF16)) * q_dec
    col1 = lax.broadcasted_iota(jnp.int32, (1, n_rows), 1) - b * n_new
    own1 = jnp.logical_and(col1 >= 0, col1 < n_new)
    k_dec = jnp.where(own1, jnp.exp(lg * (n_new - 1 - col1).astype(F32)), 0.0)
    nst_ref[0, 0] = state * jnp.exp(lg * n_new) + _dot((kt * k_dec).astype(BF16), v)
    y_ref[pl.ds(r0, n_new), :] = _group_norm_gate(o, g_ref[pl.ds(r0, n_new), :], gn_ref[...])


def _ret_sample(logg, y_prev, q_b, k_t, v_b, g, gn, state, *, t_prompt, n_new):
    t_all, qk_width = q_b.shape
    v_width = v_b.shape[1]
    n_streams, n_heads, qk_dim, v_dim = state.shape
    n_rows = n_streams * n_new
    tail = t_all - t_prompt
    blk = t_prompt // n_rows
    return pl.pallas_call(
        functools.partial(_ret_sample_kernel, n_new=n_new),
        grid=(n_heads, n_streams),
        in_specs=[pl.BlockSpec(memory_space=pltpu.SMEM),
                  pl.BlockSpec(memory_space=pl.ANY),
                  pl.BlockSpec((n_rows, qk_dim), lambda h, b: (blk, h)),
                  pl.BlockSpec((qk_dim, n_rows), lambda h, b: (h, blk)),
                  pl.BlockSpec((n_rows, v_dim), lambda h, b: (blk, h)),
                  pl.BlockSpec((n_rows, v_dim), lambda h, b: (blk, h)),
                  pl.BlockSpec((1, v_dim), lambda h, b: (0, h)),
                  pl.BlockSpec((1, 1, qk_dim, v_dim), lambda h, b: (b, h, 0, 0))],
        out_specs=[pl.BlockSpec((tail, v_dim), lambda h, b: (t_prompt // tail, h)),
                   pl.BlockSpec((1, 1, qk_dim, v_dim), lambda h, b: (b, h, 0, 0))],
        out_shape=[jax.ShapeDtypeStruct((t_all, v_width), BF16),
                   jax.ShapeDtypeStruct(state.shape, F32)],
        input_output_aliases={1: 0},
        compiler_params=_params(("arbitrary", "arbitrary")),
        name="ret_sample",
    )(logg, y_prev, q_b, k_t, v_b, g, gn, state)


def _top_values(s, k, with_rank):
    work = s
    vals = []
    rank = jnp.full(s.shape, NO_RANK, F32) if with_rank else None
    for r in range(k):
        m = jnp.max(work, axis=0, keepdims=True)
        vals.append(m)
        hit = work == m
        if with_rank:
            rank = jnp.where(hit, float(r), rank)
        work = jnp.where(hit, -jnp.inf, work)
    return vals, rank


def _peer_kernel(x_ref, gn_ref, wq_ref, sk_ref, u_ref, vt_ref, o_ref,
                 hb_ref, e0_ref, cnt_ref, e1_ref, rank_ref, w_ref, y_ref, *, n_heads, n_keys, topk):
    c = pl.program_id(1)
    ce, tm = u_ref.shape[0], x_ref.shape[0]
    rows_per_chunk = ce // n_keys
    sub = rank_ref.shape[2]

    @pl.when(c == 0)
    def _():
        hb = _rmsnorm_rows(x_ref[...], gn_ref[...]).astype(BF16)
        hb_ref[...] = hb
        y_ref[...] = jnp.zeros_like(y_ref)
        for h in range(n_heads):
            qh = _dot_nt(wq_ref[h * 2 * n_keys:(h + 1) * 2 * n_keys, :], hb)
            s0 = _dot(sk_ref[2 * h], qh[0:n_keys].astype(BF16))
            s1 = _dot(sk_ref[2 * h + 1], qh[n_keys:2 * n_keys].astype(BF16))
            a, rank0 = _top_values(s0, topk, True)
            b, rank1 = _top_values(s1, topk, True)
            half = topk // 2
            b_all = jnp.concatenate(b, axis=0)
            b_half = jnp.concatenate(b[:half], axis=0)
            cand = jnp.concatenate(
                [a[0] + b_all] + [a[i] + b_half for i in range(1, half)] + [jnp.concatenate(a[half:], axis=0) + b[0]],
                axis=0)
            f, _ = _top_values(cand, topk, False)
            tau = f[topk - 1]
            z = sum(jnp.exp(fk - f[0]) for fk in f)
            a_all = jnp.concatenate(a, axis=0)
            cnt_sorted = sum(jnp.where(a_all + bj >= tau, 1.0, 0.0) for bj in b)
            cnt = jnp.zeros_like(s0)
            for r in range(topk):
                cnt = jnp.where(rank0 == float(r), cnt_sorted[r:r + 1, :], cnt)
            cnt_ref[h] = cnt
            e0_ref[h] = jnp.exp(s0 - a[0])
            e1_ref[h] = (jnp.exp(s1 - b[0]) * (0.5 / z)).astype(BF16).reshape(n_keys // sub, sub, tm)
            rank_ref[h] = rank1.astype(BF16).reshape(n_keys // sub, sub, tm)

    hb = hb_ref[...]
    half = ce // 2
    acts = (_dot_nt(u_ref[0:half, :], hb), _dot_nt(u_ref[half:ce, :], hb))
    rows_per_slab = MXU_DIM // n_keys
    y_add = None
    for slab in range(ce // MXU_DIM):
        for il in range(slab * rows_per_slab, (slab + 1) * rows_per_slab):
            i = c * rows_per_chunk + il
            gate = None
            for h in range(n_heads):
                n_sel = jnp.broadcast_to(cnt_ref[h, pl.ds(i, 1), :], (sub, tm)).astype(BF16)
                e0 = jnp.broadcast_to(e0_ref[h, pl.ds(i, 1), :], (sub, tm)).astype(BF16)
                term = jnp.where(rank_ref[h] < n_sel[None], e1_ref[h], jnp.zeros((), BF16)) * e0[None]
                gate = term if gate is None else gate + term
            r0 = il * n_keys
            x_act = acts[r0 // half][r0 % half:r0 % half + n_keys, :]
            gelu2 = x_act * (1.0 + lax.erf(x_act * (2.0 ** -0.5)))
            w_ref[r0:r0 + n_keys, :] = gate.reshape(n_keys, tm) * gelu2.astype(BF16)
        rows = slice(slab * MXU_DIM, (slab + 1) * MXU_DIM)
        part = _dot(vt_ref[:, rows], w_ref[rows, :])
        y_add = part if y_add is None else y_add + part
    y_ref[...] += y_add

    @pl.when(c == pl.num_programs(1) - 1)
    def _():
        o_ref[...] = x_ref[...] + y_ref[...].T


def _peer(x, gn, wq_t, subkeys, u, v_t, *, layer, n_heads, n_keys):
    t_all, d = x.shape
    n_experts = u.shape[1]
    tm = ROW_TILE
    ce = PEER_EXPERT_CHUNK
    head_scratch = pltpu.VMEM((n_heads, n_keys, tm), F32)
    head_scratch_b = pltpu.VMEM((n_heads, n_keys // BF16_TILE_ROWS, BF16_TILE_ROWS, tm), BF16)
    return pl.pallas_call(
        functools.partial(_peer_kernel, n_heads=n_heads, n_keys=n_keys, topk=PEER_TOPK),
        grid=(t_all // tm, n_experts // ce),
        in_specs=[pl.BlockSpec((tm, d), lambda i, c: (i, 0)), _const_spec((1, d)), _const_spec(wq_t.shape),
                  _const_spec(subkeys.shape),
                  pl.BlockSpec((None, ce, d), lambda i, c: (layer, c, 0)),
                  pl.BlockSpec((None, d, ce), lambda i, c: (layer, 0, c))],
        out_specs=pl.BlockSpec((tm, d), lambda i, c: (i, 0)),
        out_shape=jax.ShapeDtypeStruct((t_all, d), F32),
        scratch_shapes=[pltpu.VMEM((tm, d), BF16), head_scratch, head_scratch, head_scratch_b, head_scratch_b,
                        pltpu.VMEM((ce, tm), BF16), pltpu.VMEM((d, tm), F32)],
        compiler_params=_params(("arbitrary", "arbitrary")),
        name="peer",
    )(x, gn, wq_t, subkeys, u, v_t)


def kernel(x_prompt, x_sample, cache_fox_k, cache_fox_v, cache_fox_lf, state_ret, meta_tokens, norm_mix, norm_ffn,
           fox_w_in, fox_b_f, fox_q_norm, fox_k_norm, fox_w_out, ret_w_in, ret_gn, ret_w_out,
           peer_w_q, peer_subkeys, peer_u, peer_v):
    batch, seq, d = x_prompt.shape
    n_streams, n_new, _ = x_sample.shape
    n_meta = meta_tokens.shape[0]
    assert batch == 1 and n_meta == N_META
    depth = norm_mix.shape[0]
    n_fox, _, past, fox_heads, fox_hd = cache_fox_k.shape
    fox_width = fox_heads * fox_hd
    n_ret, _, ret_heads, ret_qk, ret_v = state_ret.shape
    ret_v_width = ret_heads * ret_v
    peer_heads, _, n_keys, peer_half = peer_subkeys.shape[1:]
    assert n_keys == LANES and peer_half == LANES and fox_width == d

    length = n_meta + seq
    front = (-length) % ROW_TILE
    t_prompt = front + length
    n_rows = n_streams * n_new
    tail = ROW_TILE
    assert n_rows <= tail
    t_all = t_prompt + tail

    x = jnp.concatenate([jnp.zeros((front, d), F32), meta_tokens.astype(F32), x_prompt[0],
                         x_sample.reshape(n_rows, d), jnp.zeros((tail - n_rows, d), F32)], axis=0)

    half = ret_qk // 2
    inv_freq = (ROPE_BASE ** (-jnp.arange(half, dtype=F32) / half)).reshape(1, half)
    logg = jnp.log(1.0 - 2.0 ** (-5.0 - jnp.arange(ret_heads, dtype=F32)))

    peer_u_b = peer_u.astype(BF16)
    peer_vt_b = peer_v.astype(BF16).transpose(0, 2, 1)

    kp, vp, lfp, ks, vs, lfs, srp, srs = [], [], [], [], [], [], [], []
    for layer in range(depth):
        j = layer // 2
        gn_mix = norm_mix[layer].reshape(1, d)
        if layer % 2 == 0:
            w_in = fox_w_in[j]
            w_main = w_in[:, :4 * fox_width].astype(BF16)
            w_f = jnp.pad(w_in[:, 4 * fox_width:], ((0, 0), (0, LANES - fox_heads))).astype(BF16)
            b_f = jnp.pad(fox_b_f[j], (0, LANES - fox_heads)).reshape(1, LANES)
            gq = jnp.tile(fox_q_norm[j], fox_heads).reshape(1, fox_width)
            gk = jnp.tile(fox_k_norm[j], fox_heads).reshape(1, fox_width)
            q_b, k_f, k_b, v_f, v_b, og, lf, csum = _fox_proj(x, gn_mix, w_main, w_f, b_f, gq, gk,
                                                              front=front, head_dim=fox_hd)
            o_b = _fox_attn_prompt(q_b, k_b, v_b, csum, og, fox_q_norm[j], fox_k_norm[j],
                                   t_prompt=t_prompt, head_dim=fox_hd)
            lf_new_t = lf[t_prompt:t_prompt + n_rows, :fox_heads].T
            o_b = _fox_attn_sample(o_b, q_b, k_b, v_b, og, lf_new_t,
                                   cache_fox_k[j].reshape(n_streams, past, fox_width),
                                   cache_fox_v[j].reshape(n_streams, past, fox_width),
                                   cache_fox_lf[j].transpose(0, 2, 1),
                                   t_prompt=t_prompt, n_new=n_new, head_dim=fox_hd)
            x = _out_proj(o_b, x, fox_w_out[j].astype(BF16))
            kp.append(k_f[front:t_prompt].reshape(1, length, fox_heads, fox_hd))
            vp.append(v_f[front:t_prompt].reshape(1, length, fox_heads, fox_hd))
            lfp.append(lf[front:t_prompt, :fox_heads].reshape(1, length, fox_heads))
            ks.append(k_f[t_prompt:t_prompt + n_rows].reshape(n_streams, n_new, fox_heads, fox_hd))
            vs.append(v_f[t_prompt:t_prompt + n_rows].reshape(n_streams, n_new, fox_heads, fox_hd))
            lfs.append(lf[t_prompt:t_prompt + n_rows, :fox_heads].reshape(n_streams, n_new, fox_heads))
        else:
            q_b, k_t, v_b, g = _ret_proj(x, gn_mix, ret_w_in[j].astype(BF16), inv_freq,
                                         t_prompt=t_prompt, pos0=front + n_meta, past=past, n_new=n_new,
                                         n_heads=ret_heads, qk_dim=ret_qk, v_width=ret_v_width)
            gn_ret = ret_gn[j].reshape(1, ret_v_width)
            y_b, st_p = _ret_prompt(logg, q_b, k_t, v_b, g, gn_ret, t_prompt=t_prompt, n_heads=ret_heads)
            y_b, st_s = _ret_sample(logg, y_b, q_b, k_t, v_b, g, gn_ret, state_ret[j],
                                    t_prompt=t_prompt, n_new=n_new)
            x = _out_proj(y_b, x, ret_w_out[j].astype(BF16))
            srp.append(st_p[None])
            srs.append(st_s)
        x = _peer(x, norm_ffn[layer].reshape(1, d), peer_w_q[layer].T.astype(BF16),
                  peer_subkeys[layer].reshape(2 * peer_heads, n_keys, peer_half).astype(BF16),
                  peer_u_b, peer_vt_b, layer=layer, n_heads=peer_heads, n_keys=n_keys)

    y_prompt = x[front + n_meta:t_prompt][None]
    y_sample = x[t_prompt:t_prompt + n_rows].reshape(n_streams, n_new, d)
    return (y_prompt, y_sample, jnp.stack(kp), jnp.stack(vp), jnp.stack(lfp), jnp.stack(srp),
            jnp.stack(ks), jnp.stack(vs), jnp.stack(lfs), jnp.stack(srs))
```

```python
import functools
import math

import jax
import jax.numpy as jnp
from jax import lax
from jax.experimental import pallas as pl
from jax.experimental.pallas import tpu as pltpu

F32 = jnp.float32
BF16 = jnp.bfloat16

EPS = 1e-6
N_META = 16
CHUNK = 64
ROPE_BASE = 10000.0
PEER_TOPK = 16

LANES = 128
MXU_DIM = 256
BF16_TILE_ROWS = 16
VMEM_LIMIT = 56 * 1024 * 1024

ROW_TILE = 512
PROJ_TILE = 256
RET_BLOCK = 512
PEER_EXPERT_CHUNK = 2048
PEER_ACT_SLAB = 1024
MASKED_KEY = 1e30
NO_RANK = 99.0
RANK_MARK = -(2.0 ** 100)
LOG2E = math.log2(math.e)
ZERO_WEIGHT_LOG2 = 160.0

_NT = (((1,), (1,)), ((), ()))


def _const_spec(shape):
    return pl.BlockSpec(shape, lambda *_: (0,) * len(shape), pipeline_mode=pl.Buffered(1))


def _params(semantics):
    return pltpu.CompilerParams(dimension_semantics=semantics, vmem_limit_bytes=VMEM_LIMIT)


def _split3(x):
    a = x.astype(BF16)
    r = x - a.astype(F32)
    b = r.astype(BF16)
    c = (r - b.astype(F32)).astype(BF16)
    return a, b, c


def _dot(a, b):
    return jnp.dot(a, b, preferred_element_type=F32)


def _dot_nt(a, b):
    return lax.dot_general(a, b, _NT, preferred_element_type=F32)


def _rmsnorm_rows(x, g):
    ms = jnp.mean(x * x, axis=-1, keepdims=True)
    return x * lax.rsqrt(ms + EPS) * g


def _log_sigmoid(x):
    return jnp.minimum(x, 0.0) - jnp.log1p(jnp.exp(-jnp.abs(x)))


def _fox_proj_kernel(x_ref, gn_ref, w_ref, wf_ref, bf_ref, gq_ref, gk_ref, bd_ref, tril_ref,
                     q_ref, kf_ref, kb_ref, vf_ref, vb_ref, og_ref, lf_ref, c_ref, carry_ref,
                     *, tm, front, head_dim, width):
    i = pl.program_id(0)

    @pl.when(i == 0)
    def _():
        carry_ref[...] = jnp.zeros_like(carry_ref)

    h = _rmsnorm_rows(x_ref[...], gn_ref[...]).astype(BF16)

    def head_norm(z, g):
        zz = (z * z).astype(BF16)
        ss = jnp.concatenate(
            [_dot(zz[:, c * MXU_DIM:(c + 1) * MXU_DIM], bd_ref[...]) for c in range(width // MXU_DIM)], axis=1)
        return z * lax.rsqrt(ss * (1.0 / head_dim) + EPS) * g

    zq = _dot(h, w_ref[:, 0:width])
    q_ref[...] = (head_norm(zq, gq_ref[...]) * (head_dim ** -0.5 * LOG2E)).astype(BF16)
    zk = _dot(h, w_ref[:, width:2 * width])
    kn = head_norm(zk, gk_ref[...])
    kf_ref[...] = kn
    kb_ref[...] = kn.astype(BF16)
    zv = _dot(h, w_ref[:, 2 * width:3 * width])
    vf_ref[...] = zv
    vb_ref[...] = zv.astype(BF16)
    og_ref[...] = _dot(h, w_ref[:, 3 * width:4 * width])

    lf = _log_sigmoid(_dot(h, wf_ref[...]) + bf_ref[...])
    lf_ref[...] = lf
    row = i * tm + lax.broadcasted_iota(jnp.int32, (tm, 1), 0)
    real = row >= front
    l1, l2, l3 = _split3(jnp.where(real, lf, 0.0))
    tril = tril_ref[...]
    c = _dot(tril, l1) + _dot(tril, l2) + _dot(tril, l3) + carry_ref[...]
    carry_ref[...] = c[tm - 1:tm, :]
    c_ref[...] = jnp.where(real, c, MASKED_KEY)


def _fox_proj(x, gn, w_main, w_f, b_f, gq, gk, *, front, head_dim):
    t_all, d = x.shape
    width = w_main.shape[1] // 4
    tm = PROJ_TILE
    blk = MXU_DIM // head_dim
    bd = jnp.kron(jnp.eye(blk, dtype=F32), jnp.ones((head_dim, head_dim), F32)).astype(BF16)
    tril = (lax.broadcasted_iota(jnp.int32, (tm, tm), 0) >= lax.broadcasted_iota(jnp.int32, (tm, tm), 1)).astype(BF16)
    row = lambda n: pl.BlockSpec((tm, n), lambda i: (i, 0))
    sds = jax.ShapeDtypeStruct
    return pl.pallas_call(
        functools.partial(_fox_proj_kernel, tm=tm, front=front, head_dim=head_dim, width=width),
        grid=(t_all // tm,),
        in_specs=[row(d), _const_spec((1, d)), _const_spec(w_main.shape), _const_spec(w_f.shape),
                  _const_spec((1, LANES)), _const_spec((1, width)), _const_spec((1, width)),
                  _const_spec(bd.shape), _const_spec(tril.shape)],
        out_specs=[row(width), row(width), row(width), row(width), row(width), row(width), row(LANES), row(LANES)],
        out_shape=[sds((t_all, width), BF16), sds((t_all, width), F32), sds((t_all, width), BF16),
                   sds((t_all, width), F32), sds((t_all, width), BF16), sds((t_all, width), F32),
                   sds((t_all, LANES), F32), sds((t_all, LANES), F32)],
        scratch_shapes=[pltpu.VMEM((1, LANES), F32)],
        compiler_params=_params(("arbitrary",)),
        name="fox_proj",
    )(x, gn, w_main, w_f, b_f, gq, gk, bd, tril)


def _fox_attn_kernel(first_ref, q_ref, k_ref, v_ref, c_ref, og_ref, o_ref, qm_ref, m_ref, alpha_ref, p_ref, acc_ref,
                     *, tile, head_dim):
    qi = pl.program_id(1)
    first = first_ref[pl.program_id(0), qi]
    lane = lax.broadcasted_iota(jnp.int32, (1, 2 * head_dim), 1)
    lo = lane < head_dim
    head_lanes = (lo, jnp.logical_not(lo))
    q = q_ref[...]
    for hh in range(2):
        qm_ref[hh] = jnp.where(head_lanes[hh], q, jnp.zeros_like(q))
    m_ref[...] = jnp.full(m_ref.shape, -jnp.inf, F32)
    acc_ref[...] = jnp.zeros(acc_ref.shape, F32)
    alpha_ref[1] = jnp.ones(alpha_ref.shape[1:], F32)
    p_ref[1] = jnp.zeros(p_ref.shape[1:], BF16)
    one = jnp.ones((), BF16)

    def scores(j, slot, diagonal):
        k = k_ref[pl.ds(pl.multiple_of(j * tile, tile), tile), :]
        for hh in range(2):
            c_last = c_ref[hh, pl.ds(qi, 1), :][:, tile - 1:tile]
            bias = (c_ref[hh, pl.ds(j, 1), :] - c_last) * LOG2E
            u = _dot_nt(qm_ref[hh], k) - bias
            if diagonal:
                r = lax.broadcasted_iota(jnp.int32, (tile, tile), 0)
                cidx = lax.broadcasted_iota(jnp.int32, (tile, tile), 1)
                u = jnp.where(cidx <= r, u, -jnp.inf)
            m_prev = m_ref[hh]
            m_new = jnp.maximum(m_prev, jnp.max(u, axis=1, keepdims=True))
            m_wide = jnp.concatenate([m_new] * (tile // LANES), axis=1)
            p_ref[slot, hh] = jnp.exp2(u - m_wide).astype(BF16)
            alpha_ref[slot, hh] = jnp.exp2(m_prev - m_new)
            m_ref[hh] = m_new

    def accumulate(j, slot):
        v = v_ref[pl.ds(pl.multiple_of(j * tile, tile), tile), :]
        for hh in range(2):
            v_aug = jnp.where(head_lanes[hh], v, one)
            acc_ref[hh] = alpha_ref[slot, hh] * acc_ref[hh] + _dot(p_ref[slot, hh], v_aug)

    n_off = qi - first
    odd = lax.rem(n_off, 2)

    @pl.when(odd == 1)
    def _():
        scores(first, 0, False)
        accumulate(first, 0)

    def body(jj, pending):
        j0 = first + odd + 2 * jj
        scores(j0, 0, False)
        accumulate(pending, 1)
        scores(j0 + 1, 1, False)
        accumulate(j0, 0)
        return j0 + 1

    pending = lax.fori_loop(0, n_off // 2, body, first)
    scores(qi, 0, True)
    accumulate(pending, 1)
    accumulate(qi, 0)

    outs = []
    for hh in range(2):
        a = acc_ref[hh]
        outs.append(a / pltpu.roll(a, head_dim, 1))
    o = jnp.where(lo, outs[0], outs[1])
    o_ref[...] = (o * jax.nn.sigmoid(og_ref[...])).astype(BF16)


def _first_needed_block(gq, gk, csum, *, n_heads, t_prompt, head_dim, tile):
    bf16_slack = 1.0 + 2.0 ** -7
    q_norm = math.sqrt(head_dim) * jnp.max(jnp.abs(gq)) * (head_dim ** -0.5 * LOG2E) * bf16_slack
    k_norm = math.sqrt(head_dim) * jnp.max(jnp.abs(gk)) * bf16_slack
    c = csum[:t_prompt, :n_heads]
    c_first, c_last = c[0::tile], c[tile - 1::tile]
    bound = 2.0 * q_norm * k_norm + LOG2E * (c_first[:, None, :] - c_last[None, :, :])
    blk = jnp.arange(c_first.shape[0])
    earlier = blk[None, :, None] < blk[:, None, None]
    skipped = jnp.sum(jnp.logical_and(bound < -ZERO_WEIGHT_LOG2, earlier), axis=1)
    return skipped.reshape(-1, n_heads // 2, 2).min(-1).T.astype(jnp.int32)


def _fox_attn_prompt(q_b, k_b, v_b, csum, og, gq, gk, *, t_prompt, head_dim):
    t_all, width = q_b.shape
    tile = ROW_TILE
    pair = 2 * head_dim
    n_blk = t_prompt // tile
    n_heads = width // head_dim
    c3 = csum[:t_prompt, :n_heads].T.reshape(n_heads, n_blk, tile)
    first = _first_needed_block(gq, gk, csum, n_heads=n_heads, t_prompt=t_prompt, head_dim=head_dim, tile=tile)
    return pl.pallas_call(
        functools.partial(_fox_attn_kernel, tile=tile, head_dim=head_dim),
        grid=(width // pair, n_blk),
        in_specs=[pl.BlockSpec(memory_space=pltpu.SMEM),
                  pl.BlockSpec((tile, pair), lambda p, i: (i, p)),
                  pl.BlockSpec((t_prompt, pair), lambda p, i: (0, p)),
                  pl.BlockSpec((t_prompt, pair), lambda p, i: (0, p)),
                  pl.BlockSpec((2, n_blk, tile), lambda p, i: (p, 0, 0)),
                  pl.BlockSpec((tile, pair), lambda p, i: (i, p))],
        out_specs=pl.BlockSpec((tile, pair), lambda p, i: (i, p)),
        out_shape=jax.ShapeDtypeStruct((t_prompt, width), BF16),
        scratch_shapes=[pltpu.VMEM((2, tile, pair), BF16), pltpu.VMEM((2, tile, LANES), F32),
                        pltpu.VMEM((2, 2, tile, LANES), F32), pltpu.VMEM((2, 2, tile, tile), BF16),
                        pltpu.VMEM((2, tile, pair), F32)],
        compiler_params=_params(("arbitrary", "arbitrary")),
        name="fox_attn_prompt",
    )(first, q_b, k_b, v_b, c3, og)


def _fox_sample_kernel(q_ref, kn_ref, vn_ref, og_ref, lfn_ref, kc_ref, vc_ref, lfc_ref, su_ref, bt_ref,
                       o_ref, *, n_new, head_dim, n_heads):
    b = pl.program_id(0)
    pair = 2 * head_dim
    n_rows = kn_ref.shape[0]

    @pl.when(b == 0)
    def _():
        o_ref[...] = jnp.zeros_like(o_ref)

    su = su_ref[...]
    suf = sum(_dot(t, su) for t in _split3(lfc_ref[0])) * LOG2E
    bt = bt_ref[...]
    pre = sum(_dot(t, bt) for t in _split3(lfn_ref[...])) * LOG2E

    r0 = pl.multiple_of(b * n_new, n_new)
    qrow = lax.broadcasted_iota(jnp.int32, (n_new, n_rows), 0)
    col = lax.broadcasted_iota(jnp.int32, (n_new, n_rows), 1)
    visible = jnp.logical_and(col >= b * n_new, col <= b * n_new + qrow)
    lane = lax.broadcasted_iota(jnp.int32, (1, pair), 1)
    lo = lane < head_dim
    head_lanes = (lo, jnp.logical_not(lo))

    for p in range(n_heads // 2):
        cols = slice(p * pair, (p + 1) * pair)
        q = q_ref[pl.ds(r0, n_new), cols]
        k_c = kc_ref[0, :, cols].astype(BF16)
        v_c = vc_ref[0, :, cols].astype(BF16)
        k_n = kn_ref[:, cols]
        v_n = vn_ref[:, cols]
        outs = []
        for hh in range(2):
            head = 2 * p + hh
            qh = jnp.where(head_lanes[hh], q, jnp.zeros_like(q))
            u_c = _dot_nt(qh, k_c) + suf[head:head + 1, :]
            u_n = jnp.where(visible, _dot_nt(qh, k_n) - pre[head:head + 1, :], -jnp.inf)
            m = jnp.maximum(jnp.max(u_c, axis=1, keepdims=True), jnp.max(u_n, axis=1, keepdims=True))
            p_c = jnp.exp2(u_c - m)
            p_n = jnp.exp2(u_n - m)
            den = jnp.sum(p_c, axis=1, keepdims=True) + jnp.sum(p_n, axis=1, keepdims=True)
            outs.append((_dot(p_c.astype(BF16), v_c) + _dot(p_n.astype(BF16), v_n)) / den)
        o = jnp.where(lo, outs[0], outs[1])
        gate = jax.nn.sigmoid(og_ref[pl.ds(r0, n_new), cols])
        o_ref[pl.ds(r0, n_new), cols] = (o * gate).astype(BF16)


def _fox_attn_sample(q_b, k_b, v_b, og, lf_new_t, cache_k, cache_v, cache_lf_t, *, t_prompt, n_new, head_dim):
    t_all, width = q_b.shape
    n_streams, past, _ = cache_k.shape
    n_heads = width // head_dim
    n_rows = n_streams * n_new
    assert t_prompt % n_rows == 0 and n_rows % LANES == 0
    su = (lax.broadcasted_iota(jnp.int32, (past, past), 0) > lax.broadcasted_iota(jnp.int32, (past, past), 1)).astype(BF16)
    rr = lax.broadcasted_iota(jnp.int32, (n_rows, n_rows), 0)
    cc = lax.broadcasted_iota(jnp.int32, (n_rows, n_rows), 1)
    bt = jnp.logical_and(rr // n_new == cc // n_new, rr <= cc).astype(BF16)
    rows = pl.BlockSpec((n_rows, width), lambda b: (t_prompt // n_rows, 0), pipeline_mode=pl.Buffered(1))
    tail = t_all - t_prompt
    return pl.pallas_call(
        functools.partial(_fox_sample_kernel, n_new=n_new, head_dim=head_dim, n_heads=n_heads),
        grid=(n_streams,),
        in_specs=[rows, rows, rows, rows,
                  _const_spec(lf_new_t.shape),
                  pl.BlockSpec((1, past, width), lambda b: (b, 0, 0)),
                  pl.BlockSpec((1, past, width), lambda b: (b, 0, 0)),
                  pl.BlockSpec((1, n_heads, past), lambda b: (b, 0, 0)),
                  _const_spec(su.shape), _const_spec(bt.shape)],
        out_specs=pl.BlockSpec((tail, width), lambda b: (0, 0)),
        out_shape=jax.ShapeDtypeStruct((tail, width), BF16),
        compiler_params=_params(("arbitrary",)),
        name="fox_attn_sample",
    )(q_b, k_b, v_b, og, lf_new_t, cache_k, cache_v, cache_lf_t, su, bt)


def _out_proj_kernel(yp_ref, yt_ref, x_ref, w_ref, o_ref, *, n_prompt_tiles):
    y = jnp.where(pl.program_id(0) < n_prompt_tiles, yp_ref[...], yt_ref[...])
    o_ref[...] = x_ref[...] + _dot(y, w_ref[...])


def _out_proj(y_prompt, y_tail, x, w):
    t_all, d = x.shape
    t_prompt, k = y_prompt.shape
    tm = ROW_TILE
    assert y_tail.shape == (tm, k) and t_all == t_prompt + tm
    n_p = t_prompt // tm
    return pl.pallas_call(
        functools.partial(_out_proj_kernel, n_prompt_tiles=n_p),
        grid=(t_all // tm,),
        in_specs=[pl.BlockSpec((tm, k), lambda i: (jnp.minimum(i, n_p - 1), 0)),
                  pl.BlockSpec((tm, k), lambda i: (0, 0)),
                  pl.BlockSpec((tm, d), lambda i: (i, 0)),
                  _const_spec(w.shape)],
        out_specs=pl.BlockSpec((tm, d), lambda i: (i, 0)),
        out_shape=jax.ShapeDtypeStruct((t_all, d), F32),
        compiler_params=_params(("arbitrary",)),
        name="out_proj",
    )(y_prompt, y_tail, x, w)


def _ret_proj_kernel(x_ref, gn_ref, w_ref, inv_ref, q_ref, kt_ref, v_ref, g_ref,
                     *, tm, t_prompt, pos0, past, n_new, n_heads, qk_dim, v_width):
    i = pl.program_id(0)
    h = _rmsnorm_rows(x_ref[...], gn_ref[...]).astype(BF16)
    row = i * tm + lax.broadcasted_iota(jnp.int32, (tm, 1), 0)
    pos = jnp.where(row < t_prompt, row - pos0, past + lax.rem(row - t_prompt, n_new))
    ang = pos.astype(F32) * inv_ref[...]
    cos = jnp.cos(ang)
    sin = jnp.sin(ang)
    half = qk_dim // 2
    qk_width = n_heads * qk_dim

    def rotary(z):
        parts = []
        for hh in range(n_heads):
            z1 = z[:, hh * qk_dim:hh * qk_dim + half]
            z2 = z[:, hh * qk_dim + half:(hh + 1) * qk_dim]
            parts += [z1 * cos - z2 * sin, z1 * sin + z2 * cos]
        return jnp.concatenate(parts, axis=1)

    q = rotary(_dot(h, w_ref[:, 0:qk_width])) * (qk_dim ** -0.5)
    q_ref[...] = q.astype(BF16)
    kt_ref[...] = rotary(_dot(h, w_ref[:, qk_width:2 * qk_width])).T
    v_ref[...] = _dot(h, w_ref[:, 2 * qk_width:2 * qk_width + v_width]).astype(BF16)
    g_ref[...] = _dot(h, w_ref[:, 2 * qk_width + v_width:])


def _ret_proj(x, gn, w_in, inv_freq, *, t_prompt, pos0, past, n_new, n_heads, qk_dim, v_width):
    t_all, d = x.shape
    tm = PROJ_TILE
    qk_width = n_heads * qk_dim
    row = lambda n: pl.BlockSpec((tm, n), lambda i: (i, 0))
    sds = jax.ShapeDtypeStruct
    return pl.pallas_call(
        functools.partial(_ret_proj_kernel, tm=tm, t_prompt=t_prompt, pos0=pos0, past=past, n_new=n_new,
                          n_heads=n_heads, qk_dim=qk_dim, v_width=v_width),
        grid=(t_all // tm,),
        in_specs=[row(d), _const_spec((1, d)), _const_spec(w_in.shape), _const_spec(inv_freq.shape)],
        out_specs=[row(qk_width), pl.BlockSpec((qk_width, tm), lambda i: (0, i)), row(v_width), row(v_width)],
        out_shape=[sds((t_all, qk_width), BF16), sds((qk_width, t_all), F32),
                   sds((t_all, v_width), BF16), sds((t_all, v_width), F32)],
        compiler_params=_params(("arbitrary",)),
        name="ret_proj",
    )(x, gn, w_in, inv_freq)


def _group_norm_gate(o, g, gn):
    mu = jnp.mean(o, axis=-1, keepdims=True)
    var = jnp.mean(jnp.square(o - mu), axis=-1, keepdims=True)
    y = (o - mu) * lax.rsqrt(var + EPS) * gn
    return (g * jax.nn.sigmoid(g) * y).astype(BF16)


def _ret_prompt_kernel(logg_ref, q_ref, kt_ref, v_ref, g_ref, gn_ref, y_ref, st_ref, s_ref, *, tb):
    hh = pl.program_id(0)
    blk = pl.program_id(1)
    lg = jnp.full((1, 1), logg_ref[hh], F32)

    @pl.when(blk == 0)
    def _():
        s_ref[...] = jnp.zeros_like(s_ref)

    r = lax.broadcasted_iota(jnp.int32, (tb, tb), 0)
    c = lax.broadcasted_iota(jnp.int32, (tb, tb), 1)
    decay = jnp.exp(lg * jnp.abs(r - c).astype(F32))
    dmat = jnp.where(c // CHUNK <= r // CHUNK, decay, 0.0)

    q = q_ref[...]
    kt = kt_ref[...]
    v = v_ref[...]
    s = _dot(q, kt.astype(BF16)) * dmat
    state = s_ref[...]
    q_dec = jnp.exp(lg * (lax.broadcasted_iota(jnp.int32, (tb, 1), 0) + 1).astype(F32))
    o = _dot(s.astype(BF16), v) + _dot(q, state.astype(BF16)) * q_dec
    k_dec = jnp.exp(lg * (tb - 1 - lax.broadcasted_iota(jnp.int32, (1, tb), 1)).astype(F32))
    new_state = state * jnp.exp(lg * tb) + _dot((kt * k_dec).astype(BF16), v)
    s_ref[...] = new_state
    y_ref[...] = _group_norm_gate(o, g_ref[...], gn_ref[...])

    @pl.when(blk == pl.num_programs(1) - 1)
    def _():
        st_ref[0] = new_state


def _ret_prompt(logg, q_b, k_t, v_b, g, gn, *, t_prompt, n_heads):
    t_all, qk_width = q_b.shape
    v_width = v_b.shape[1]
    qk_dim, v_dim = qk_width // n_heads, v_width // n_heads
    tb = RET_BLOCK
    return pl.pallas_call(
        functools.partial(_ret_prompt_kernel, tb=tb),
        grid=(n_heads, t_prompt // tb),
        in_specs=[pl.BlockSpec(memory_space=pltpu.SMEM),
                  pl.BlockSpec((tb, qk_dim), lambda h, i: (i, h)),
                  pl.BlockSpec((qk_dim, tb), lambda h, i: (h, i)),
                  pl.BlockSpec((tb, v_dim), lambda h, i: (i, h)),
                  pl.BlockSpec((tb, v_dim), lambda h, i: (i, h)),
                  pl.BlockSpec((1, v_dim), lambda h, i: (0, h))],
        out_specs=[pl.BlockSpec((tb, v_dim), lambda h, i: (i, h)),
                   pl.BlockSpec((1, qk_dim, v_dim), lambda h, i: (h, 0, 0))],
        out_shape=[jax.ShapeDtypeStruct((t_prompt, v_width), BF16),
                   jax.ShapeDtypeStruct((n_heads, qk_dim, v_dim), F32)],
        scratch_shapes=[pltpu.VMEM((qk_dim, v_dim), F32)],
        compiler_params=_params(("arbitrary", "arbitrary")),
        name="ret_prompt",
    )(logg, q_b, k_t, v_b, g, gn)


def _ret_sample_kernel(logg_ref, q_ref, kt_ref, v_ref, g_ref, gn_ref, st_ref, y_ref, nst_ref, *, n_new):
    hh = pl.program_id(0)
    b = pl.program_id(1)
    lg = jnp.full((1, 1), logg_ref[hh], F32)
    n_rows = kt_ref.shape[1]

    @pl.when(b == 0)
    def _():
        y_ref[...] = jnp.zeros_like(y_ref)

    r0 = pl.multiple_of(b * n_new, n_new)
    q = q_ref[pl.ds(r0, n_new), :]
    kt = kt_ref[...]
    v = v_ref[...]
    qrow = lax.broadcasted_iota(jnp.int32, (n_new, n_rows), 0)
    col = lax.broadcasted_iota(jnp.int32, (n_new, n_rows), 1) - b * n_new
    own = jnp.logical_and(col >= 0, col < n_new)
    dmat = jnp.where(own, jnp.exp(lg * jnp.abs(qrow - col).astype(F32)), 0.0)
    s = _dot(q, kt.astype(BF16)) * dmat
    state = st_ref[0, 0]
    q_dec = jnp.exp(lg * (lax.broadcasted_iota(jnp.int32, (n_new, 1), 0) + 1).astype(F32))
    o = _dot(s.astype(BF16), v) + _dot(q, state.astype(BF16)) * q_dec
    col1 = lax.broadcasted_iota(jnp.int32, (1, n_rows), 1) - b * n_new
    own1 = jnp.logical_and(col1 >= 0, col1 < n_new)
    k_dec = jnp.where(own1, jnp.exp(lg * (n_new - 1 - col1).astype(F32)), 0.0)
    nst_ref[0, 0] = state * jnp.exp(lg * n_new) + _dot((kt * k_dec).astype(BF16), v)
    y_ref[pl.ds(r0, n_new), :] = _group_norm_gate(o, g_ref[pl.ds(r0, n_new), :], gn_ref[...])


def _ret_sample(logg, q_b, k_t, v_b, g, gn, state, *, t_prompt, n_new):
    t_all, qk_width = q_b.shape
    v_width = v_b.shape[1]
    n_streams, n_heads, qk_dim, v_dim = state.shape
    n_rows = n_streams * n_new
    tail = t_all - t_prompt
    blk = t_prompt // n_rows
    return pl.pallas_call(
        functools.partial(_ret_sample_kernel, n_new=n_new),
        grid=(n_heads, n_streams),
        in_specs=[pl.BlockSpec(memory_space=pltpu.SMEM),
                  pl.BlockSpec((n_rows, qk_dim), lambda h, b: (blk, h)),
                  pl.BlockSpec((qk_dim, n_rows), lambda h, b: (h, blk)),
                  pl.BlockSpec((n_rows, v_dim), lambda h, b: (blk, h)),
                  pl.BlockSpec((n_rows, v_dim), lambda h, b: (blk, h)),
                  pl.BlockSpec((1, v_dim), lambda h, b: (0, h)),
                  pl.BlockSpec((1, 1, qk_dim, v_dim), lambda h, b: (b, h, 0, 0))],
        out_specs=[pl.BlockSpec((tail, v_dim), lambda h, b: (0, h)),
                   pl.BlockSpec((1, 1, qk_dim, v_dim), lambda h, b: (b, h, 0, 0))],
        out_shape=[jax.ShapeDtypeStruct((tail, v_width), BF16),
                   jax.ShapeDtypeStruct(state.shape, F32)],
        compiler_params=_params(("arbitrary", "arbitrary")),
        name="ret_sample",
    )(logg, q_b, k_t, v_b, g, gn, state)


def _top_values(s, k, with_rank):
    work = s
    vals = []
    for r in range(k):
        m = jnp.max(work, axis=0, keepdims=True)
        vals.append(m)
        work = jnp.where(work == m, RANK_MARK * (1.0 + r / 1024.0), work)
    if not with_rank:
        return vals, None
    rank = jnp.where(work <= RANK_MARK, (work * (1.0 / RANK_MARK) - 1.0) * 1024.0, NO_RANK)
    return vals, rank


def _peer_kernel(x_ref, gn_ref, wq_ref, sk_ref, u_ref, vt_ref, o_ref,
                 hb_ref, e0_ref, cnt_ref, e1_ref, rank_ref, w_ref, y_ref, *, n_heads, n_keys, topk):
    c = pl.program_id(1)
    ce, tm = u_ref.shape[0], x_ref.shape[0]
    rows_per_chunk = ce // n_keys
    sub = rank_ref.shape[2]

    @pl.when(c == 0)
    def _():
        hb = _rmsnorm_rows(x_ref[...], gn_ref[...]).astype(BF16)
        hb_ref[...] = hb
        y_ref[...] = jnp.zeros_like(y_ref)
        for h in range(n_heads):
            qh = _dot_nt(wq_ref[h * 2 * n_keys:(h + 1) * 2 * n_keys, :], hb)
            s0 = _dot(sk_ref[2 * h], qh[0:n_keys].astype(BF16))
            s1 = _dot(sk_ref[2 * h + 1], qh[n_keys:2 * n_keys].astype(BF16))
            a, rank0 = _top_values(s0, topk, True)
            b, rank1 = _top_values(s1, topk, True)
            half = topk // 2
            b_all = jnp.concatenate(b, axis=0)
            b_half = jnp.concatenate(b[:half], axis=0)
            cand = jnp.concatenate(
                [a[0] + b_all] + [a[i] + b_half for i in range(1, half)] + [jnp.concatenate(a[half:], axis=0) + b[0]],
                axis=0)
            f, _ = _top_values(cand, topk, False)
            tau = f[topk - 1]
            z = sum(jnp.exp(fk - f[0]) for fk in f)
            a_all = jnp.concatenate(a, axis=0)
            cnt_sorted = sum(jnp.where(a_all + bj >= tau, 1.0, 0.0) for bj in b)
            cnt = jnp.zeros_like(s0)
            for r in range(topk):
                cnt = jnp.where(rank0 == float(r), cnt_sorted[r:r + 1, :], cnt)
            cnt_ref[h] = cnt
            e0_ref[h] = jnp.exp(s0 - a[0])
            e1_ref[h] = (jnp.exp(s1 - b[0]) * (0.5 / z)).astype(BF16).reshape(n_keys // sub, sub, tm)
            rank_ref[h] = rank1.astype(BF16).reshape(n_keys // sub, sub, tm)

    hb = hb_ref[...]
    half = PEER_ACT_SLAB
    acts = [_dot_nt(u_ref[a * half:(a + 1) * half, :], hb) for a in range(ce // half)]
    rows_per_slab = MXU_DIM // n_keys
    y_add = None
    for slab in range(ce // MXU_DIM):
        for il in range(slab * rows_per_slab, (slab + 1) * rows_per_slab):
            i = c * rows_per_chunk + il
            gate = None
            for h in range(n_heads):
                n_sel = jnp.broadcast_to(cnt_ref[h, pl.ds(i, 1), :], (sub, tm)).astype(BF16)
                e0 = jnp.broadcast_to(e0_ref[h, pl.ds(i, 1), :], (sub, tm)).astype(BF16)
                term = jnp.where(rank_ref[h] < n_sel[None], e1_ref[h], jnp.zeros((), BF16)) * e0[None]
                gate = term if gate is None else gate + term
            r0 = il * n_keys
            x_act = acts[r0 // half][r0 % half:r0 % half + n_keys, :]
            gelu2 = x_act * (1.0 + lax.erf(x_act * (2.0 ** -0.5)))
            w_ref[r0:r0 + n_keys, :] = gate.reshape(n_keys, tm) * gelu2.astype(BF16)
        rows = slice(slab * MXU_DIM, (slab + 1) * MXU_DIM)
        part = _dot(vt_ref[:, rows], w_ref[rows, :])
        y_add = part if y_add is None else y_add + part
    y_ref[...] += y_add

    @pl.when(c == pl.num_programs(1) - 1)
    def _():
        o_ref[...] = x_ref[...] + y_ref[...].T


def _peer(x, gn, wq_t, subkeys, u, v_t, *, layer, n_heads, n_keys):
    t_all, d = x.shape
    n_experts = u.shape[1]
    tm = ROW_TILE
    ce = PEER_EXPERT_CHUNK
    head_scratch = pltpu.VMEM((n_heads, n_keys, tm), F32)
    head_scratch_b = pltpu.VMEM((n_heads, n_keys // BF16_TILE_ROWS, BF16_TILE_ROWS, tm), BF16)
    return pl.pallas_call(
        functools.partial(_peer_kernel, n_heads=n_heads, n_keys=n_keys, topk=PEER_TOPK),
        grid=(t_all // tm, n_experts // ce),
        in_specs=[pl.BlockSpec((tm, d), lambda i, c: (i, 0)), _const_spec((1, d)), _const_spec(wq_t.shape),
                  _const_spec(subkeys.shape),
                  pl.BlockSpec((None, ce, d), lambda i, c: (layer, c, 0)),
                  pl.BlockSpec((None, d, ce), lambda i, c: (layer, 0, c))],
        out_specs=pl.BlockSpec((tm, d), lambda i, c: (i, 0)),
        out_shape=jax.ShapeDtypeStruct((t_all, d), F32),
        scratch_shapes=[pltpu.VMEM((tm, d), BF16), head_scratch, head_scratch, head_scratch_b, head_scratch_b,
                        pltpu.VMEM((ce, tm), BF16), pltpu.VMEM((d, tm), F32)],
        compiler_params=_params(("arbitrary", "arbitrary")),
        name="peer",
    )(x, gn, wq_t, subkeys, u, v_t)


def kernel(x_prompt, x_sample, cache_fox_k, cache_fox_v, cache_fox_lf, state_ret, meta_tokens, norm_mix, norm_ffn,
           fox_w_in, fox_b_f, fox_q_norm, fox_k_norm, fox_w_out, ret_w_in, ret_gn, ret_w_out,
           peer_w_q, peer_subkeys, peer_u, peer_v):
    batch, seq, d = x_prompt.shape
    n_streams, n_new, _ = x_sample.shape
    n_meta = meta_tokens.shape[0]
    assert batch == 1 and n_meta == N_META
    depth = norm_mix.shape[0]
    n_fox, _, past, fox_heads, fox_hd = cache_fox_k.shape
    fox_width = fox_heads * fox_hd
    n_ret, _, ret_heads, ret_qk, ret_v = state_ret.shape
    ret_v_width = ret_heads * ret_v
    peer_heads, _, n_keys, peer_half = peer_subkeys.shape[1:]
    assert n_keys == LANES and peer_half == LANES and fox_width == d

    length = n_meta + seq
    front = (-length) % ROW_TILE
    t_prompt = front + length
    n_rows = n_streams * n_new
    tail = ROW_TILE
    assert n_rows <= tail
    t_all = t_prompt + tail

    x = jnp.concatenate([jnp.zeros((front, d), F32), meta_tokens.astype(F32), x_prompt[0],
                         x_sample.reshape(n_rows, d), jnp.zeros((tail - n_rows, d), F32)], axis=0)

    half = ret_qk // 2
    inv_freq = (ROPE_BASE ** (-jnp.arange(half, dtype=F32) / half)).reshape(1, half)
    logg = jnp.log(1.0 - 2.0 ** (-5.0 - jnp.arange(ret_heads, dtype=F32)))

    peer_u_b = peer_u.astype(BF16)
    peer_vt_b = peer_v.astype(BF16).transpose(0, 2, 1)

    kp, vp, lfp, ks, vs, lfs, srp, srs = [], [], [], [], [], [], [], []
    for layer in range(depth):
        j = layer // 2
        gn_mix = norm_mix[layer].reshape(1, d)
        if layer % 2 == 0:
            w_in = fox_w_in[j]
            w_main = w_in[:, :4 * fox_width].astype(BF16)
            w_f = jnp.pad(w_in[:, 4 * fox_width:], ((0, 0), (0, LANES - fox_heads))).astype(BF16)
            b_f = jnp.pad(fox_b_f[j], (0, LANES - fox_heads)).reshape(1, LANES)
            gq = jnp.tile(fox_q_norm[j], fox_heads).reshape(1, fox_width)
            gk = jnp.tile(fox_k_norm[j], fox_heads).reshape(1, fox_width)
            q_b, k_f, k_b, v_f, v_b, og, lf, csum = _fox_proj(x, gn_mix, w_main, w_f, b_f, gq, gk,
                                                              front=front, head_dim=fox_hd)
            o_b = _fox_attn_prompt(q_b, k_b, v_b, csum, og, fox_q_norm[j], fox_k_norm[j],
                                   t_prompt=t_prompt, head_dim=fox_hd)
            lf_new_t = lf[t_prompt:t_prompt + n_rows, :fox_heads].T
            o_tail = _fox_attn_sample(q_b, k_b, v_b, og, lf_new_t,
                                      cache_fox_k[j].reshape(n_streams, past, fox_width),
                                      cache_fox_v[j].reshape(n_streams, past, fox_width),
                                      cache_fox_lf[j].transpose(0, 2, 1),
                                      t_prompt=t_prompt, n_new=n_new, head_dim=fox_hd)
            x = _out_proj(o_b, o_tail, x, fox_w_out[j].astype(BF16))
            kp.append(k_f)
            vp.append(v_f)
            lfp.append(lf[front:t_prompt, :fox_heads].reshape(1, length, fox_heads))
            ks.append(k_f[t_prompt:t_prompt + n_rows].reshape(n_streams, n_new, fox_heads, fox_hd))
            vs.append(v_f[t_prompt:t_prompt + n_rows].reshape(n_streams, n_new, fox_heads, fox_hd))
            lfs.append(lf[t_prompt:t_prompt + n_rows, :fox_heads].reshape(n_streams, n_new, fox_heads))
        else:
            q_b, k_t, v_b, g = _ret_proj(x, gn_mix, ret_w_in[j].astype(BF16), inv_freq,
                                         t_prompt=t_prompt, pos0=front + n_meta, past=past, n_new=n_new,
                                         n_heads=ret_heads, qk_dim=ret_qk, v_width=ret_v_width)
            gn_ret = ret_gn[j].reshape(1, ret_v_width)
            y_b, st_p = _ret_prompt(logg, q_b, k_t, v_b, g, gn_ret, t_prompt=t_prompt, n_heads=ret_heads)
            y_tail, st_s = _ret_sample(logg, q_b, k_t, v_b, g, gn_ret, state_ret[j],
                                       t_prompt=t_prompt, n_new=n_new)
            x = _out_proj(y_b, y_tail, x, ret_w_out[j].astype(BF16))
            srp.append(st_p[None])
            srs.append(st_s)
        x = _peer(x, norm_ffn[layer].reshape(1, d), peer_w_q[layer].T.astype(BF16),
                  peer_subkeys[layer].reshape(2 * peer_heads, n_keys, peer_half).astype(BF16),
                  peer_u_b, peer_vt_b, layer=layer, n_heads=peer_heads, n_keys=n_keys)

    y_prompt = x[front + n_meta:t_prompt][None]
    y_sample = x[t_prompt:t_prompt + n_rows].reshape(n_streams, n_new, d)
    prompt_cache = lambda a: jnp.stack(a)[:, front:t_prompt].reshape(len(a), 1, length, fox_heads, fox_hd)
    return (y_prompt, y_sample, prompt_cache(kp), prompt_cache(vp), jnp.stack(lfp), jnp.stack(srp),
            jnp.stack(ks), jnp.stack(vs), jnp.stack(lfs), jnp.stack(srs))
```

```python
import functools
import math

import jax
import jax.numpy as jnp
from jax import lax
from jax.experimental import pallas as pl
from jax.experimental.pallas import tpu as pltpu

F32 = jnp.float32
BF16 = jnp.bfloat16

EPS = 1e-6
N_META = 16
CHUNK = 64
ROPE_BASE = 10000.0
PEER_TOPK = 16

LANES = 128
MXU_DIM = 256
BF16_TILE_ROWS = 16
VMEM_LIMIT = 56 * 1024 * 1024

ROW_TILE = 512
PROJ_TILE = 256
RET_BLOCK = 512
PEER_EXPERT_CHUNK = 2048
PEER_ACT_SLAB = 1024
MASKED_KEY = 1e30
NO_RANK = 99.0
RANK_MARK = -(2.0 ** 100)
LOG2E = math.log2(math.e)
ZERO_WEIGHT_LOG2 = 160.0

_NT = (((1,), (1,)), ((), ()))


def _const_spec(shape):
    return pl.BlockSpec(shape, lambda *_: (0,) * len(shape), pipeline_mode=pl.Buffered(1))


def _params(semantics):
    return pltpu.CompilerParams(dimension_semantics=semantics, vmem_limit_bytes=VMEM_LIMIT)


def _split3(x):
    a = x.astype(BF16)
    r = x - a.astype(F32)
    b = r.astype(BF16)
    c = (r - b.astype(F32)).astype(BF16)
    return a, b, c


def _dot(a, b):
    return jnp.dot(a, b, preferred_element_type=F32)


def _dot_nt(a, b):
    return lax.dot_general(a, b, _NT, preferred_element_type=F32)


def _rmsnorm_rows(x, g):
    ms = jnp.mean(x * x, axis=-1, keepdims=True)
    return x * lax.rsqrt(ms + EPS) * g


def _log_sigmoid(x):
    return jnp.minimum(x, 0.0) - jnp.log1p(jnp.exp(-jnp.abs(x)))


def _fox_proj_kernel(x_ref, gn_ref, w_ref, wf_ref, bf_ref, gq_ref, gk_ref, bd_ref, tril_ref,
                     q_ref, kf_ref, kb_ref, vf_ref, vb_ref, og_ref, lf_ref, c_ref, carry_ref,
                     *, tm, front, head_dim, width):
    i = pl.program_id(0)

    @pl.when(i == 0)
    def _():
        carry_ref[...] = jnp.zeros_like(carry_ref)

    h = _rmsnorm_rows(x_ref[...], gn_ref[...]).astype(BF16)

    def head_norm(z, g):
        zz = (z * z).astype(BF16)
        ss = jnp.concatenate(
            [_dot(zz[:, c * MXU_DIM:(c + 1) * MXU_DIM], bd_ref[...]) for c in range(width // MXU_DIM)], axis=1)
        return z * lax.rsqrt(ss * (1.0 / head_dim) + EPS) * g

    zq = _dot(h, w_ref[:, 0:width])
    q_ref[...] = (head_norm(zq, gq_ref[...]) * (head_dim ** -0.5 * LOG2E)).astype(BF16)
    zk = _dot(h, w_ref[:, width:2 * width])
    kn = head_norm(zk, gk_ref[...])
    kf_ref[...] = kn
    kb_ref[...] = kn.astype(BF16)
    zv = _dot(h, w_ref[:, 2 * width:3 * width])
    vf_ref[...] = zv
    vb_ref[...] = zv.astype(BF16)
    og_ref[...] = _dot(h, w_ref[:, 3 * width:4 * width])

    lf = _log_sigmoid(_dot(h, wf_ref[...]) + bf_ref[...])
    lf_ref[...] = lf
    row = i * tm + lax.broadcasted_iota(jnp.int32, (tm, 1), 0)
    real = row >= front
    l1, l2, l3 = _split3(jnp.where(real, lf, 0.0))
    tril = tril_ref[...]
    c = _dot(tril, l1) + _dot(tril, l2) + _dot(tril, l3) + carry_ref[...]
    carry_ref[...] = c[tm - 1:tm, :]
    c_ref[...] = jnp.where(real, c, MASKED_KEY)


def _fox_proj(x, gn, w_main, w_f, b_f, gq, gk, *, front, head_dim):
    t_all, d = x.shape
    width = w_main.shape[1] // 4
    tm = PROJ_TILE
    blk = MXU_DIM // head_dim
    bd = jnp.kron(jnp.eye(blk, dtype=F32), jnp.ones((head_dim, head_dim), F32)).astype(BF16)
    tril = (lax.broadcasted_iota(jnp.int32, (tm, tm), 0) >= lax.broadcasted_iota(jnp.int32, (tm, tm), 1)).astype(BF16)
    row = lambda n: pl.BlockSpec((tm, n), lambda i: (i, 0))
    sds = jax.ShapeDtypeStruct
    return pl.pallas_call(
        functools.partial(_fox_proj_kernel, tm=tm, front=front, head_dim=head_dim, width=width),
        grid=(t_all // tm,),
        in_specs=[row(d), _const_spec((1, d)), _const_spec(w_main.shape), _const_spec(w_f.shape),
                  _const_spec((1, LANES)), _const_spec((1, width)), _const_spec((1, width)),
                  _const_spec(bd.shape), _const_spec(tril.shape)],
        out_specs=[row(width), row(width), row(width), row(width), row(width), row(width), row(LANES), row(LANES)],
        out_shape=[sds((t_all, width), BF16), sds((t_all, width), F32), sds((t_all, width), BF16),
                   sds((t_all, width), F32), sds((t_all, width), BF16), sds((t_all, width), F32),
                   sds((t_all, LANES), F32), sds((t_all, LANES), F32)],
        scratch_shapes=[pltpu.VMEM((1, LANES), F32)],
        compiler_params=_params(("arbitrary",)),
        name="fox_proj",
    )(x, gn, w_main, w_f, b_f, gq, gk, bd, tril)


def _fox_attn_kernel(first_ref, q_ref, k_ref, v_ref, c_ref, og_ref, o_ref,
                     qm_ref, m_ref, alpha_ref, p_ref, acc_ref, pend_ref, *, tile, head_dim):
    qi = pl.program_id(1)
    first = tuple(first_ref[2 * pl.program_id(0) + hh, qi] for hh in range(2))
    lane = lax.broadcasted_iota(jnp.int32, (1, 2 * head_dim), 1)
    lo = lane < head_dim
    head_lanes = (lo, jnp.logical_not(lo))
    q = q_ref[...]
    for hh in range(2):
        qm_ref[hh] = jnp.where(head_lanes[hh], q, jnp.zeros_like(q))
    m_ref[...] = jnp.full(m_ref.shape, -jnp.inf, F32)
    acc_ref[...] = jnp.zeros(acc_ref.shape, F32)
    alpha_ref[1] = jnp.ones(alpha_ref.shape[1:], F32)
    p_ref[1] = jnp.zeros(p_ref.shape[1:], BF16)
    one = jnp.ones((), BF16)
    both = (0, 1)

    def scores(j, slot, heads, diagonal=False):
        k = k_ref[pl.ds(pl.multiple_of(j * tile, tile), tile), :]
        for hh in heads:
            c_last = c_ref[hh, pl.ds(qi, 1), :][:, tile - 1:tile]
            bias = (c_ref[hh, pl.ds(j, 1), :] - c_last) * LOG2E
            u = _dot_nt(qm_ref[hh], k) - bias
            if diagonal:
                r = lax.broadcasted_iota(jnp.int32, (tile, tile), 0)
                cidx = lax.broadcasted_iota(jnp.int32, (tile, tile), 1)
                u = jnp.where(cidx <= r, u, -jnp.inf)
            m_prev = m_ref[hh]
            m_new = jnp.maximum(m_prev, jnp.max(u, axis=1, keepdims=True))
            m_wide = jnp.concatenate([m_new] * (tile // LANES), axis=1)
            p_ref[slot, hh] = jnp.exp2(u - m_wide).astype(BF16)
            alpha_ref[slot, hh] = jnp.exp2(m_prev - m_new)
            m_ref[hh] = m_new

    def accumulate(j, slot, heads):
        v = v_ref[pl.ds(pl.multiple_of(j * tile, tile), tile), :]
        for hh in heads:
            v_aug = jnp.where(head_lanes[hh], v, one)
            acc_ref[hh] = alpha_ref[slot, hh] * acc_ref[hh] + _dot(p_ref[slot, hh], v_aug)

    def blocks(start, stop, pending, heads):
        n = stop - start
        odd = lax.rem(n, 2)

        @pl.when(odd == 1)
        def _():
            scores(start, 0, heads)
            accumulate(pending, 1, heads)
            accumulate(start, 0, heads)
            for hh in heads:
                alpha_ref[1, hh] = jnp.ones(alpha_ref.shape[2:], F32)
                p_ref[1, hh] = jnp.zeros(p_ref.shape[2:], BF16)

        def body(jj, pend):
            j0 = start + odd + 2 * jj
            scores(j0, 0, heads)
            accumulate(pend, 1, heads)
            scores(j0 + 1, 1, heads)
            accumulate(j0, 0, heads)
            return j0 + 1

        return lax.fori_loop(0, n // 2, body, pending)

    joint = jnp.maximum(first[0], first[1])
    pend_ref[0] = joint
    for hh in both:
        @pl.when(first[hh] < first[1 - hh])
        def _():
            pend_ref[0] = blocks(first[hh], joint, joint, (hh,))

    pending = blocks(joint, qi, pend_ref[0], both)
    scores(qi, 0, both, diagonal=True)
    accumulate(pending, 1, both)
    accumulate(qi, 0, both)

    outs = []
    for hh in range(2):
        a = acc_ref[hh]
        outs.append(a / pltpu.roll(a, head_dim, 1))
    o = jnp.where(lo, outs[0], outs[1])
    o_ref[...] = (o * jax.nn.sigmoid(og_ref[...])).astype(BF16)


def _first_needed_block(gq, gk, csum, *, n_heads, t_prompt, head_dim, tile):
    bf16_slack = 1.0 + 2.0 ** -7
    q_norm = math.sqrt(head_dim) * jnp.max(jnp.abs(gq)) * (head_dim ** -0.5 * LOG2E) * bf16_slack
    k_norm = math.sqrt(head_dim) * jnp.max(jnp.abs(gk)) * bf16_slack
    c = csum[:t_prompt, :n_heads]
    c_first, c_last = c[0::tile], c[tile - 1::tile]
    bound = 2.0 * q_norm * k_norm + LOG2E * (c_first[:, None, :] - c_last[None, :, :])
    blk = jnp.arange(c_first.shape[0])
    earlier = blk[None, :, None] < blk[:, None, None]
    skipped = jnp.sum(jnp.logical_and(bound < -ZERO_WEIGHT_LOG2, earlier), axis=1)
    return skipped.T.astype(jnp.int32)


def _fox_attn_prompt(q_b, k_b, v_b, csum, og, gq, gk, *, t_prompt, head_dim):
    t_all, width = q_b.shape
    tile = ROW_TILE
    pair = 2 * head_dim
    n_blk = t_prompt // tile
    n_heads = width // head_dim
    c3 = csum[:t_prompt, :n_heads].T.reshape(n_heads, n_blk, tile)
    first = _first_needed_block(gq, gk, csum, n_heads=n_heads, t_prompt=t_prompt, head_dim=head_dim, tile=tile)
    return pl.pallas_call(
        functools.partial(_fox_attn_kernel, tile=tile, head_dim=head_dim),
        grid=(width // pair, n_blk),
        in_specs=[pl.BlockSpec(memory_space=pltpu.SMEM),
                  pl.BlockSpec((tile, pair), lambda p, i: (i, p)),
                  pl.BlockSpec((t_prompt, pair), lambda p, i: (0, p)),
                  pl.BlockSpec((t_prompt, pair), lambda p, i: (0, p)),
                  pl.BlockSpec((2, n_blk, tile), lambda p, i: (p, 0, 0)),
                  pl.BlockSpec((tile, pair), lambda p, i: (i, p))],
        out_specs=pl.BlockSpec((tile, pair), lambda p, i: (i, p)),
        out_shape=jax.ShapeDtypeStruct((t_prompt, width), BF16),
        scratch_shapes=[pltpu.VMEM((2, tile, pair), BF16), pltpu.VMEM((2, tile, LANES), F32),
                        pltpu.VMEM((2, 2, tile, LANES), F32), pltpu.VMEM((2, 2, tile, tile), BF16),
                        pltpu.VMEM((2, tile, pair), F32), pltpu.SMEM((1,), jnp.int32)],
        compiler_params=_params(("arbitrary", "arbitrary")),
        name="fox_attn_prompt",
    )(first, q_b, k_b, v_b, c3, og)


def _fox_sample_kernel(q_ref, kn_ref, vn_ref, og_ref, lfn_ref, kc_ref, vc_ref, lfc_ref, su_ref, bt_ref,
                       o_ref, *, n_new, head_dim, n_heads):
    b = pl.program_id(0)
    pair = 2 * head_dim
    n_rows = kn_ref.shape[0]

    @pl.when(b == 0)
    def _():
        o_ref[...] = jnp.zeros_like(o_ref)

    su = su_ref[...]
    suf = sum(_dot(t, su) for t in _split3(lfc_ref[0])) * LOG2E
    bt = bt_ref[...]
    pre = sum(_dot(t, bt) for t in _split3(lfn_ref[...])) * LOG2E

    r0 = pl.multiple_of(b * n_new, n_new)
    qrow = lax.broadcasted_iota(jnp.int32, (n_new, n_rows), 0)
    col = lax.broadcasted_iota(jnp.int32, (n_new, n_rows), 1)
    visible = jnp.logical_and(col >= b * n_new, col <= b * n_new + qrow)
    lane = lax.broadcasted_iota(jnp.int32, (1, pair), 1)
    lo = lane < head_dim
    head_lanes = (lo, jnp.logical_not(lo))

    for p in range(n_heads // 2):
        cols = slice(p * pair, (p + 1) * pair)
        q = q_ref[pl.ds(r0, n_new), cols]
        k_c = kc_ref[0, :, cols].astype(BF16)
        v_c = vc_ref[0, :, cols].astype(BF16)
        k_n = kn_ref[:, cols]
        v_n = vn_ref[:, cols]
        outs = []
        for hh in range(2):
            head = 2 * p + hh
            qh = jnp.where(head_lanes[hh], q, jnp.zeros_like(q))
            u_c = _dot_nt(qh, k_c) + suf[head:head + 1, :]
            u_n = jnp.where(visible, _dot_nt(qh, k_n) - pre[head:head + 1, :], -jnp.inf)
            m = jnp.maximum(jnp.max(u_c, axis=1, keepdims=True), jnp.max(u_n, axis=1, keepdims=True))
            p_c = jnp.exp2(u_c - m)
            p_n = jnp.exp2(u_n - m)
            den = jnp.sum(p_c, axis=1, keepdims=True) + jnp.sum(p_n, axis=1, keepdims=True)
            outs.append((_dot(p_c.astype(BF16), v_c) + _dot(p_n.astype(BF16), v_n)) / den)
        o = jnp.where(lo, outs[0], outs[1])
        gate = jax.nn.sigmoid(og_ref[pl.ds(r0, n_new), cols])
        o_ref[pl.ds(r0, n_new), cols] = (o * gate).astype(BF16)


def _fox_attn_sample(q_b, k_b, v_b, og, lf_new_t, cache_k, cache_v, cache_lf_t, *, t_prompt, n_new, head_dim):
    t_all, width = q_b.shape
    n_streams, past, _ = cache_k.shape
    n_heads = width // head_dim
    n_rows = n_streams * n_new
    assert t_prompt % n_rows == 0 and n_rows % LANES == 0
    su = (lax.broadcasted_iota(jnp.int32, (past, past), 0) > lax.broadcasted_iota(jnp.int32, (past, past), 1)).astype(BF16)
    rr = lax.broadcasted_iota(jnp.int32, (n_rows, n_rows), 0)
    cc = lax.broadcasted_iota(jnp.int32, (n_rows, n_rows), 1)
    bt = jnp.logical_and(rr // n_new == cc // n_new, rr <= cc).astype(BF16)
    rows = pl.BlockSpec((n_rows, width), lambda b: (t_prompt // n_rows, 0), pipeline_mode=pl.Buffered(1))
    tail = t_all - t_prompt
    return pl.pallas_call(
        functools.partial(_fox_sample_kernel, n_new=n_new, head_dim=head_dim, n_heads=n_heads),
        grid=(n_streams,),
        in_specs=[rows, rows, rows, rows,
                  _const_spec(lf_new_t.shape),
                  pl.BlockSpec((1, past, width), lambda b: (b, 0, 0)),
                  pl.BlockSpec((1, past, width), lambda b: (b, 0, 0)),
                  pl.BlockSpec((1, n_heads, past), lambda b: (b, 0, 0)),
                  _const_spec(su.shape), _const_spec(bt.shape)],
        out_specs=pl.BlockSpec((tail, width), lambda b: (0, 0)),
        out_shape=jax.ShapeDtypeStruct((tail, width), BF16),
        compiler_params=_params(("arbitrary",)),
        name="fox_attn_sample",
    )(q_b, k_b, v_b, og, lf_new_t, cache_k, cache_v, cache_lf_t, su, bt)


def _out_proj_kernel(yp_ref, yt_ref, x_ref, w_ref, o_ref, *, n_prompt_tiles):
    y = jnp.where(pl.program_id(0) < n_prompt_tiles, yp_ref[...], yt_ref[...])
    o_ref[...] = x_ref[...] + _dot(y, w_ref[...])


def _out_proj(y_prompt, y_tail, x, w):
    t_all, d = x.shape
    t_prompt, k = y_prompt.shape
    tm = ROW_TILE
    assert y_tail.shape == (tm, k) and t_all == t_prompt + tm
    n_p = t_prompt // tm
    return pl.pallas_call(
        functools.partial(_out_proj_kernel, n_prompt_tiles=n_p),
        grid=(t_all // tm,),
        in_specs=[pl.BlockSpec((tm, k), lambda i: (jnp.minimum(i, n_p - 1), 0)),
                  pl.BlockSpec((tm, k), lambda i: (0, 0)),
                  pl.BlockSpec((tm, d), lambda i: (i, 0)),
                  _const_spec(w.shape)],
        out_specs=pl.BlockSpec((tm, d), lambda i: (i, 0)),
        out_shape=jax.ShapeDtypeStruct((t_all, d), F32),
        compiler_params=_params(("arbitrary",)),
        name="out_proj",
    )(y_prompt, y_tail, x, w)


def _ret_proj_kernel(x_ref, gn_ref, w_ref, inv_ref, q_ref, kt_ref, v_ref, g_ref,
                     *, tm, t_prompt, pos0, past, n_new, n_heads, qk_dim, v_width):
    i = pl.program_id(0)
    h = _rmsnorm_rows(x_ref[...], gn_ref[...]).astype(BF16)
    row = i * tm + lax.broadcasted_iota(jnp.int32, (tm, 1), 0)
    pos = jnp.where(row < t_prompt, row - pos0, past + lax.rem(row - t_prompt, n_new))
    ang = pos.astype(F32) * inv_ref[...]
    cos = jnp.cos(ang)
    sin = jnp.sin(ang)
    half = qk_dim // 2
    qk_width = n_heads * qk_dim

    def rotary(z):
        parts = []
        for hh in range(n_heads):
            z1 = z[:, hh * qk_dim:hh * qk_dim + half]
            z2 = z[:, hh * qk_dim + half:(hh + 1) * qk_dim]
            parts += [z1 * cos - z2 * sin, z1 * sin + z2 * cos]
        return jnp.concatenate(parts, axis=1)

    q = rotary(_dot(h, w_ref[:, 0:qk_width])) * (qk_dim ** -0.5)
    q_ref[...] = q.astype(BF16)
    kt_ref[...] = rotary(_dot(h, w_ref[:, qk_width:2 * qk_width])).T
    v_ref[...] = _dot(h, w_ref[:, 2 * qk_width:2 * qk_width + v_width]).astype(BF16)
    g_ref[...] = _dot(h, w_ref[:, 2 * qk_width + v_width:])


def _ret_proj(x, gn, w_in, inv_freq, *, t_prompt, pos0, past, n_new, n_heads, qk_dim, v_width):
    t_all, d = x.shape
    tm = PROJ_TILE
    qk_width = n_heads * qk_dim
    row = lambda n: pl.BlockSpec((tm, n), lambda i: (i, 0))
    sds = jax.ShapeDtypeStruct
    return pl.pallas_call(
        functools.partial(_ret_proj_kernel, tm=tm, t_prompt=t_prompt, pos0=pos0, past=past, n_new=n_new,
                          n_heads=n_heads, qk_dim=qk_dim, v_width=v_width),
        grid=(t_all // tm,),
        in_specs=[row(d), _const_spec((1, d)), _const_spec(w_in.shape), _const_spec(inv_freq.shape)],
        out_specs=[row(qk_width), pl.BlockSpec((qk_width, tm), lambda i: (0, i)), row(v_width), row(v_width)],
        out_shape=[sds((t_all, qk_width), BF16), sds((qk_width, t_all), F32),
                   sds((t_all, v_width), BF16), sds((t_all, v_width), F32)],
        compiler_params=_params(("arbitrary",)),
        name="ret_proj",
    )(x, gn, w_in, inv_freq)


def _group_norm_gate(o, g, gn):
    mu = jnp.mean(o, axis=-1, keepdims=True)
    var = jnp.mean(jnp.square(o - mu), axis=-1, keepdims=True)
    y = (o - mu) * lax.rsqrt(var + EPS) * gn
    return (g * jax.nn.sigmoid(g) * y).astype(BF16)


def _ret_prompt_kernel(logg_ref, q_ref, kt_ref, v_ref, g_ref, gn_ref, y_ref, st_ref, s_ref, *, tb):
    hh = pl.program_id(0)
    blk = pl.program_id(1)
    lg = jnp.full((1, 1), logg_ref[hh], F32)

    @pl.when(blk == 0)
    def _():
        s_ref[...] = jnp.zeros_like(s_ref)

    r = lax.broadcasted_iota(jnp.int32, (tb, tb), 0)
    c = lax.broadcasted_iota(jnp.int32, (tb, tb), 1)
    decay = jnp.exp(lg * jnp.abs(r - c).astype(F32))
    dmat = jnp.where(c // CHUNK <= r // CHUNK, decay, 0.0)

    q = q_ref[...]
    kt = kt_ref[...]
    v = v_ref[...]
    s = _dot(q, kt.astype(BF16)) * dmat
    state = s_ref[...]
    q_dec = jnp.exp(lg * (lax.broadcasted_iota(jnp.int32, (tb, 1), 0) + 1).astype(F32))
    o = _dot(s.astype(BF16), v) + _dot(q, state.astype(BF16)) * q_dec
    k_dec = jnp.exp(lg * (tb - 1 - lax.broadcasted_iota(jnp.int32, (1, tb), 1)).astype(F32))
    new_state = state * jnp.exp(lg * tb) + _dot((kt * k_dec).astype(BF16), v)
    s_ref[...] = new_state
    y_ref[...] = _group_norm_gate(o, g_ref[...], gn_ref[...])

    @pl.when(blk == pl.num_programs(1) - 1)
    def _():
        st_ref[0] = new_state


def _ret_prompt(logg, q_b, k_t, v_b, g, gn, *, t_prompt, n_heads):
    t_all, qk_width = q_b.shape
    v_width = v_b.shape[1]
    qk_dim, v_dim = qk_width // n_heads, v_width // n_heads
    tb = RET_BLOCK
    return pl.pallas_call(
        functools.partial(_ret_prompt_kernel, tb=tb),
        grid=(n_heads, t_prompt // tb),
        in_specs=[pl.BlockSpec(memory_space=pltpu.SMEM),
                  pl.BlockSpec((tb, qk_dim), lambda h, i: (i, h)),
                  pl.BlockSpec((qk_dim, tb), lambda h, i: (h, i)),
                  pl.BlockSpec((tb, v_dim), lambda h, i: (i, h)),
                  pl.BlockSpec((tb, v_dim), lambda h, i: (i, h)),
                  pl.BlockSpec((1, v_dim), lambda h, i: (0, h))],
        out_specs=[pl.BlockSpec((tb, v_dim), lambda h, i: (i, h)),
                   pl.BlockSpec((1, qk_dim, v_dim), lambda h, i: (h, 0, 0))],
        out_shape=[jax.ShapeDtypeStruct((t_prompt, v_width), BF16),
                   jax.ShapeDtypeStruct((n_heads, qk_dim, v_dim), F32)],
        scratch_shapes=[pltpu.VMEM((qk_dim, v_dim), F32)],
        compiler_params=_params(("arbitrary", "arbitrary")),
        name="ret_prompt",
    )(logg, q_b, k_t, v_b, g, gn)


def _ret_sample_kernel(logg_ref, q_ref, kt_ref, v_ref, g_ref, gn_ref, st_ref, y_ref, nst_ref, *, n_new):
    hh = pl.program_id(0)
    b = pl.program_id(1)
    lg = jnp.full((1, 1), logg_ref[hh], F32)
    n_rows = kt_ref.shape[1]

    @pl.when(b == 0)
    def _():
        y_ref[...] = jnp.zeros_like(y_ref)

    r0 = pl.multiple_of(b * n_new, n_new)
    q = q_ref[pl.ds(r0, n_new), :]
    kt = kt_ref[...]
    v = v_ref[...]
    qrow = lax.broadcasted_iota(jnp.int32, (n_new, n_rows), 0)
    col = lax.broadcasted_iota(jnp.int32, (n_new, n_rows), 1) - b * n_new
    own = jnp.logical_and(col >= 0, col < n_new)
    dmat = jnp.where(own, jnp.exp(lg * jnp.abs(qrow - col).astype(F32)), 0.0)
    s = _dot(q, kt.astype(BF16)) * dmat
    state = st_ref[0, 0]
    q_dec = jnp.exp(lg * (lax.broadcasted_iota(jnp.int32, (n_new, 1), 0) + 1).astype(F32))
    o = _dot(s.astype(BF16), v) + _dot(q, state.astype(BF16)) * q_dec
    col1 = lax.broadcasted_iota(jnp.int32, (1, n_rows), 1) - b * n_new
    own1 = jnp.logical_and(col1 >= 0, col1 < n_new)
    k_dec = jnp.where(own1, jnp.exp(lg * (n_new - 1 - col1).astype(F32)), 0.0)
    nst_ref[0, 0] = state * jnp.exp(lg * n_new) + _dot((kt * k_dec).astype(BF16), v)
    y_ref[pl.ds(r0, n_new), :] = _group_norm_gate(o, g_ref[pl.ds(r0, n_new), :], gn_ref[...])


def _ret_sample(logg, q_b, k_t, v_b, g, gn, state, *, t_prompt, n_new):
    t_all, qk_width = q_b.shape
    v_width = v_b.shape[1]
    n_streams, n_heads, qk_dim, v_dim = state.shape
    n_rows = n_streams * n_new
    tail = t_all - t_prompt
    blk = t_prompt // n_rows
    return pl.pallas_call(
        functools.partial(_ret_sample_kernel, n_new=n_new),
        grid=(n_heads, n_streams),
        in_specs=[pl.BlockSpec(memory_space=pltpu.SMEM),
                  pl.BlockSpec((n_rows, qk_dim), lambda h, b: (blk, h)),
                  pl.BlockSpec((qk_dim, n_rows), lambda h, b: (h, blk)),
                  pl.BlockSpec((n_rows, v_dim), lambda h, b: (blk, h)),
                  pl.BlockSpec((n_rows, v_dim), lambda h, b: (blk, h)),
                  pl.BlockSpec((1, v_dim), lambda h, b: (0, h)),
                  pl.BlockSpec((1, 1, qk_dim, v_dim), lambda h, b: (b, h, 0, 0))],
        out_specs=[pl.BlockSpec((tail, v_dim), lambda h, b: (0, h)),
                   pl.BlockSpec((1, 1, qk_dim, v_dim), lambda h, b: (b, h, 0, 0))],
        out_shape=[jax.ShapeDtypeStruct((tail, v_width), BF16),
                   jax.ShapeDtypeStruct(state.shape, F32)],
        compiler_params=_params(("arbitrary", "arbitrary")),
        name="ret_sample",
    )(logg, q_b, k_t, v_b, g, gn, state)


def _top_values(s, k, with_rank):
    work = s
    vals = []
    for r in range(k):
        m = jnp.max(work, axis=0, keepdims=True)
        vals.append(m)
        work = jnp.where(work == m, RANK_MARK * (1.0 + r / 1024.0), work)
    if not with_rank:
        return vals, None
    rank = jnp.where(work <= RANK_MARK, (work * (1.0 / RANK_MARK) - 1.0) * 1024.0, NO_RANK)
    return vals, rank


def _peer_kernel(x_ref, gn_ref, wq_ref, sk_ref, u_ref, vt_ref, o_ref,
                 hb_ref, e0_ref, cnt_ref, e1_ref, rank_ref, w_ref, y_ref, *, n_heads, n_keys, topk):
    c = pl.program_id(1)
    ce, tm = u_ref.shape[0], x_ref.shape[0]
    rows_per_chunk = ce // n_keys
    sub = rank_ref.shape[2]

    @pl.when(c == 0)
    def _():
        hb = _rmsnorm_rows(x_ref[...], gn_ref[...]).astype(BF16)
        hb_ref[...] = hb
        y_ref[...] = jnp.zeros_like(y_ref)
        for h in range(n_heads):
            qh = _dot_nt(wq_ref[h * 2 * n_keys:(h + 1) * 2 * n_keys, :], hb)
            s0 = _dot(sk_ref[2 * h], qh[0:n_keys].astype(BF16))
            s1 = _dot(sk_ref[2 * h + 1], qh[n_keys:2 * n_keys].astype(BF16))
            a, rank0 = _top_values(s0, topk, True)
            b, rank1 = _top_values(s1, topk, True)
            half = topk // 2
            b_all = jnp.concatenate(b, axis=0)
            b_half = jnp.concatenate(b[:half], axis=0)
            cand = jnp.concatenate(
                [a[0] + b_all] + [a[i] + b_half for i in range(1, half)] + [jnp.concatenate(a[half:], axis=0) + b[0]],
                axis=0)
            f, _ = _top_values(cand, topk, False)
            tau = f[topk - 1]
            z = sum(jnp.exp(fk - f[0]) for fk in f)
            a_all = jnp.concatenate(a, axis=0)
            cnt_sorted = sum(jnp.where(a_all + bj >= tau, 1.0, 0.0) for bj in b)
            cnt = jnp.zeros_like(s0)
            for r in range(topk):
                cnt = jnp.where(rank0 == float(r), cnt_sorted[r:r + 1, :], cnt)
            cnt_ref[h] = cnt
            e0_ref[h] = jnp.exp(s0 - a[0])
            e1_ref[h] = (jnp.exp(s1 - b[0]) * (0.5 / z)).astype(BF16).reshape(n_keys // sub, sub, tm)
            rank_ref[h] = rank1.astype(BF16).reshape(n_keys // sub, sub, tm)

    hb = hb_ref[...]
    half = PEER_ACT_SLAB
    acts = [_dot_nt(u_ref[a * half:(a + 1) * half, :], hb) for a in range(ce // half)]
    rows_per_slab = MXU_DIM // n_keys
    y_add = None
    for slab in range(ce // MXU_DIM):
        for il in range(slab * rows_per_slab, (slab + 1) * rows_per_slab):
            i = c * rows_per_chunk + il
            gate = None
            for h in range(n_heads):
                n_sel = jnp.broadcast_to(cnt_ref[h, pl.ds(i, 1), :], (sub, tm)).astype(BF16)
                e0 = jnp.broadcast_to(e0_ref[h, pl.ds(i, 1), :], (sub, tm)).astype(BF16)
                term = jnp.where(rank_ref[h] < n_sel[None], e1_ref[h], jnp.zeros((), BF16)) * e0[None]
                gate = term if gate is None else gate + term
            r0 = il * n_keys
            x_act = acts[r0 // half][r0 % half:r0 % half + n_keys, :]
            gelu2 = x_act * (1.0 + lax.erf(x_act * (2.0 ** -0.5)))
            w_ref[r0:r0 + n_keys, :] = gate.reshape(n_keys, tm) * gelu2.astype(BF16)
        rows = slice(slab * MXU_DIM, (slab + 1) * MXU_DIM)
        part = _dot(vt_ref[:, rows], w_ref[rows, :])
        y_add = part if y_add is None else y_add + part
    y_ref[...] += y_add

    @pl.when(c == pl.num_programs(1) - 1)
    def _():
        o_ref[...] = x_ref[...] + y_ref[...].T


def _peer(x, gn, wq_t, subkeys, u, v_t, *, layer, n_heads, n_keys):
    t_all, d = x.shape
    n_experts = u.shape[1]
    tm = ROW_TILE
    ce = PEER_EXPERT_CHUNK
    head_scratch = pltpu.VMEM((n_heads, n_keys, tm), F32)
    head_scratch_b = pltpu.VMEM((n_heads, n_keys // BF16_TILE_ROWS, BF16_TILE_ROWS, tm), BF16)
    return pl.pallas_call(
        functools.partial(_peer_kernel, n_heads=n_heads, n_keys=n_keys, topk=PEER_TOPK),
        grid=(t_all // tm, n_experts // ce),
        in_specs=[pl.BlockSpec((tm, d), lambda i, c: (i, 0)), _const_spec((1, d)), _const_spec(wq_t.shape),
                  _const_spec(subkeys.shape),
                  pl.BlockSpec((None, ce, d), lambda i, c: (layer, c, 0)),
                  pl.BlockSpec((None, d, ce), lambda i, c: (layer, 0, c))],
        out_specs=pl.BlockSpec((tm, d), lambda i, c: (i, 0)),
        out_shape=jax.ShapeDtypeStruct((t_all, d), F32),
        scratch_shapes=[pltpu.VMEM((tm, d), BF16), head_scratch, head_scratch, head_scratch_b, head_scratch_b,
                        pltpu.VMEM((ce, tm), BF16), pltpu.VMEM((d, tm), F32)],
        compiler_params=_params(("arbitrary", "arbitrary")),
        name="peer",
    )(x, gn, wq_t, subkeys, u, v_t)


def kernel(x_prompt, x_sample, cache_fox_k, cache_fox_v, cache_fox_lf, state_ret, meta_tokens, norm_mix, norm_ffn,
           fox_w_in, fox_b_f, fox_q_norm, fox_k_norm, fox_w_out, ret_w_in, ret_gn, ret_w_out,
           peer_w_q, peer_subkeys, peer_u, peer_v):
    batch, seq, d = x_prompt.shape
    n_streams, n_new, _ = x_sample.shape
    n_meta = meta_tokens.shape[0]
    assert batch == 1 and n_meta == N_META
    depth = norm_mix.shape[0]
    n_fox, _, past, fox_heads, fox_hd = cache_fox_k.shape
    fox_width = fox_heads * fox_hd
    n_ret, _, ret_heads, ret_qk, ret_v = state_ret.shape
    ret_v_width = ret_heads * ret_v
    peer_heads, _, n_keys, peer_half = peer_subkeys.shape[1:]
    assert n_keys == LANES and peer_half == LANES and fox_width == d

    length = n_meta + seq
    front = (-length) % ROW_TILE
    t_prompt = front + length
    n_rows = n_streams * n_new
    tail = ROW_TILE
    assert n_rows <= tail
    t_all = t_prompt + tail

    x = jnp.concatenate([jnp.zeros((front, d), F32), meta_tokens.astype(F32), x_prompt[0],
                         x_sample.reshape(n_rows, d), jnp.zeros((tail - n_rows, d), F32)], axis=0)

    half = ret_qk // 2
    inv_freq = (ROPE_BASE ** (-jnp.arange(half, dtype=F32) / half)).reshape(1, half)
    logg = jnp.log(1.0 - 2.0 ** (-5.0 - jnp.arange(ret_heads, dtype=F32)))

    peer_u_b = peer_u.astype(BF16)
    peer_vt_b = peer_v.astype(BF16).transpose(0, 2, 1)

    kp, vp, lfp, ks, vs, lfs, srp, srs = [], [], [], [], [], [], [], []
    for layer in range(depth):
        j = layer // 2
        gn_mix = norm_mix[layer].reshape(1, d)
        if layer % 2 == 0:
            w_in = fox_w_in[j]
            w_main = w_in[:, :4 * fox_width].astype(BF16)
            w_f = jnp.pad(w_in[:, 4 * fox_width:], ((0, 0), (0, LANES - fox_heads))).astype(BF16)
            b_f = jnp.pad(fox_b_f[j], (0, LANES - fox_heads)).reshape(1, LANES)
            gq = jnp.tile(fox_q_norm[j], fox_heads).reshape(1, fox_width)
            gk = jnp.tile(fox_k_norm[j], fox_heads).reshape(1, fox_width)
            q_b, k_f, k_b, v_f, v_b, og, lf, csum = _fox_proj(x, gn_mix, w_main, w_f, b_f, gq, gk,
                                                              front=front, head_dim=fox_hd)
            o_b = _fox_attn_prompt(q_b, k_b, v_b, csum, og, fox_q_norm[j], fox_k_norm[j],
                                   t_prompt=t_prompt, head_dim=fox_hd)
            lf_new_t = lf[t_prompt:t_prompt + n_rows, :fox_heads].T
            o_tail = _fox_attn_sample(q_b, k_b, v_b, og, lf_new_t,
                                      cache_fox_k[j].reshape(n_streams, past, fox_width),
                                      cache_fox_v[j].reshape(n_streams, past, fox_width),
                                      cache_fox_lf[j].transpose(0, 2, 1),
                                      t_prompt=t_prompt, n_new=n_new, head_dim=fox_hd)
            x = _out_proj(o_b, o_tail, x, fox_w_out[j].astype(BF16))
            kp.append(k_f)
            vp.append(v_f)
            lfp.append(lf[front:t_prompt, :fox_heads].reshape(1, length, fox_heads))
            ks.append(k_f[t_prompt:t_prompt + n_rows].reshape(n_streams, n_new, fox_heads, fox_hd))
            vs.append(v_f[t_prompt:t_prompt + n_rows].reshape(n_streams, n_new, fox_heads, fox_hd))
            lfs.append(lf[t_prompt:t_prompt + n_rows, :fox_heads].reshape(n_streams, n_new, fox_heads))
        else:
            q_b, k_t, v_b, g = _ret_proj(x, gn_mix, ret_w_in[j].astype(BF16), inv_freq,
                                         t_prompt=t_prompt, pos0=front + n_meta, past=past, n_new=n_new,
                                         n_heads=ret_heads, qk_dim=ret_qk, v_width=ret_v_width)
            gn_ret = ret_gn[j].reshape(1, ret_v_width)
            y_b, st_p = _ret_prompt(logg, q_b, k_t, v_b, g, gn_ret, t_prompt=t_prompt, n_heads=ret_heads)
            y_tail, st_s = _ret_sample(logg, q_b, k_t, v_b, g, gn_ret, state_ret[j],
                                       t_prompt=t_prompt, n_new=n_new)
            x = _out_proj(y_b, y_tail, x, ret_w_out[j].astype(BF16))
            srp.append(st_p[None])
            srs.append(st_s)
        x = _peer(x, norm_ffn[layer].reshape(1, d), peer_w_q[layer].T.astype(BF16),
                  peer_subkeys[layer].reshape(2 * peer_heads, n_keys, peer_half).astype(BF16),
                  peer_u_b, peer_vt_b, layer=layer, n_heads=peer_heads, n_keys=n_keys)

    y_prompt = x[front + n_meta:t_prompt][None]
    y_sample = x[t_prompt:t_prompt + n_rows].reshape(n_streams, n_new, d)
    prompt_cache = lambda a: jnp.stack(a)[:, front:t_prompt].reshape(len(a), 1, length, fox_heads, fox_hd)
    return (y_prompt, y_sample, prompt_cache(kp), prompt_cache(vp), jnp.stack(lfp), jnp.stack(srp),
            jnp.stack(ks), jnp.stack(vs), jnp.stack(lfs), jnp.stack(srs))
```

```python
import functools
import math

import jax
import jax.numpy as jnp
from jax import lax
from jax.experimental import pallas as pl
from jax.experimental.pallas import tpu as pltpu

F32 = jnp.float32
BF16 = jnp.bfloat16

EPS = 1e-6
N_META = 16
CHUNK = 64
ROPE_BASE = 10000.0
PEER_TOPK = 16

LANES = 128
MXU_DIM = 256
BF16_TILE_ROWS = 16
VMEM_LIMIT = 56 * 1024 * 1024

ROW_TILE = 512
PROJ_TILE = 256
RET_BLOCK = 512
PEER_EXPERT_CHUNK = 2048
PEER_ACT_EDGES = (0, 1024)
MASKED_KEY = 1e30
NO_RANK = 99.0
RANK_MARK = -(2.0 ** 100)
LOG2E = math.log2(math.e)
ZERO_WEIGHT_LOG2 = 160.0

_NT = (((1,), (1,)), ((), ()))


def _const_spec(shape):
    return pl.BlockSpec(shape, lambda *_: (0,) * len(shape), pipeline_mode=pl.Buffered(1))


def _params(semantics):
    return pltpu.CompilerParams(dimension_semantics=semantics, vmem_limit_bytes=VMEM_LIMIT)


def _split3(x):
    a = x.astype(BF16)
    r = x - a.astype(F32)
    b = r.astype(BF16)
    c = (r - b.astype(F32)).astype(BF16)
    return a, b, c


def _dot(a, b):
    return jnp.dot(a, b, preferred_element_type=F32)


def _dot_nt(a, b):
    return lax.dot_general(a, b, _NT, preferred_element_type=F32)


def _rmsnorm_rows(x, g):
    ms = jnp.mean(x * x, axis=-1, keepdims=True)
    return x * lax.rsqrt(ms + EPS) * g


def _log_sigmoid(x):
    return jnp.minimum(x, 0.0) - jnp.log1p(jnp.exp(-jnp.abs(x)))


def _fox_proj_kernel(x_ref, gn_ref, w_ref, wf_ref, bf_ref, gq_ref, gk_ref, bd_ref, tril_ref,
                     q_ref, kf_ref, kb_ref, vf_ref, vb_ref, og_ref, lf_ref, c_ref, carry_ref,
                     *, tm, front, head_dim, width):
    i = pl.program_id(0)

    @pl.when(i == 0)
    def _():
        carry_ref[...] = jnp.zeros_like(carry_ref)

    h = _rmsnorm_rows(x_ref[...], gn_ref[...]).astype(BF16)

    def head_norm(z, g):
        zz = (z * z).astype(BF16)
        ss = jnp.concatenate(
            [_dot(zz[:, c * MXU_DIM:(c + 1) * MXU_DIM], bd_ref[...]) for c in range(width // MXU_DIM)], axis=1)
        return z * lax.rsqrt(ss * (1.0 / head_dim) + EPS) * g

    zq = _dot(h, w_ref[:, 0:width])
    q_ref[...] = (head_norm(zq, gq_ref[...]) * (head_dim ** -0.5 * LOG2E)).astype(BF16)
    zk = _dot(h, w_ref[:, width:2 * width])
    kn = head_norm(zk, gk_ref[...])
    kf_ref[...] = kn
    kb_ref[...] = kn.astype(BF16)
    zv = _dot(h, w_ref[:, 2 * width:3 * width])
    vf_ref[...] = zv
    vb_ref[...] = zv.astype(BF16)
    og_ref[...] = _dot(h, w_ref[:, 3 * width:4 * width])

    lf = _log_sigmoid(_dot(h, wf_ref[...]) + bf_ref[...])
    lf_ref[...] = lf
    row = i * tm + lax.broadcasted_iota(jnp.int32, (tm, 1), 0)
    real = row >= front
    l1, l2, l3 = _split3(jnp.where(real, lf, 0.0))
    tril = tril_ref[...]
    c = _dot(tril, l1) + _dot(tril, l2) + _dot(tril, l3) + carry_ref[...]
    carry_ref[...] = c[tm - 1:tm, :]
    c_ref[...] = jnp.where(real, c, MASKED_KEY)


def _fox_proj(x, gn, w_main, w_f, b_f, gq, gk, *, front, head_dim):
    t_all, d = x.shape
    width = w_main.shape[1] // 4
    tm = PROJ_TILE
    blk = MXU_DIM // head_dim
    bd = jnp.kron(jnp.eye(blk, dtype=F32), jnp.ones((head_dim, head_dim), F32)).astype(BF16)
    tril = (lax.broadcasted_iota(jnp.int32, (tm, tm), 0) >= lax.broadcasted_iota(jnp.int32, (tm, tm), 1)).astype(BF16)
    row = lambda n: pl.BlockSpec((tm, n), lambda i: (i, 0))
    sds = jax.ShapeDtypeStruct
    return pl.pallas_call(
        functools.partial(_fox_proj_kernel, tm=tm, front=front, head_dim=head_dim, width=width),
        grid=(t_all // tm,),
        in_specs=[row(d), _const_spec((1, d)), _const_spec(w_main.shape), _const_spec(w_f.shape),
                  _const_spec((1, LANES)), _const_spec((1, width)), _const_spec((1, width)),
                  _const_spec(bd.shape), _const_spec(tril.shape)],
        out_specs=[row(width), row(width), row(width), row(width), row(width), row(width), row(LANES), row(LANES)],
        out_shape=[sds((t_all, width), BF16), sds((t_all, width), F32), sds((t_all, width), BF16),
                   sds((t_all, width), F32), sds((t_all, width), BF16), sds((t_all, width), F32),
                   sds((t_all, LANES), F32), sds((t_all, LANES), F32)],
        scratch_shapes=[pltpu.VMEM((1, LANES), F32)],
        compiler_params=_params(("arbitrary",)),
        name="fox_proj",
    )(x, gn, w_main, w_f, b_f, gq, gk, bd, tril)


def _fox_attn_kernel(first_ref, q_ref, k_ref, v_ref, c_ref, og_ref, o_ref,
                     qm_ref, m_ref, alpha_ref, p_ref, acc_ref, pend_ref, *, tile, head_dim):
    qi = pl.program_id(1)
    first = tuple(first_ref[2 * pl.program_id(0) + hh, qi] for hh in range(2))
    lane = lax.broadcasted_iota(jnp.int32, (1, 2 * head_dim), 1)
    lo = lane < head_dim
    head_lanes = (lo, jnp.logical_not(lo))
    q = q_ref[...]
    for hh in range(2):
        qm_ref[hh] = jnp.where(head_lanes[hh], q, jnp.zeros_like(q))
    m_ref[...] = jnp.full(m_ref.shape, -jnp.inf, F32)
    acc_ref[...] = jnp.zeros(acc_ref.shape, F32)
    alpha_ref[1] = jnp.ones(alpha_ref.shape[1:], F32)
    p_ref[1] = jnp.zeros(p_ref.shape[1:], BF16)
    one = jnp.ones((), BF16)
    both = (0, 1)

    def scores(j, slot, heads, diagonal=False):
        k = k_ref[pl.ds(pl.multiple_of(j * tile, tile), tile), :]
        for hh in heads:
            c_last = c_ref[hh, pl.ds(qi, 1), :][:, tile - 1:tile]
            bias = (c_ref[hh, pl.ds(j, 1), :] - c_last) * LOG2E
            u = _dot_nt(qm_ref[hh], k) - bias
            if diagonal:
                r = lax.broadcasted_iota(jnp.int32, (tile, tile), 0)
                cidx = lax.broadcasted_iota(jnp.int32, (tile, tile), 1)
                u = jnp.where(cidx <= r, u, -jnp.inf)
            m_prev = m_ref[hh]
            m_new = jnp.maximum(m_prev, jnp.max(u, axis=1, keepdims=True))
            m_wide = jnp.concatenate([m_new] * (tile // LANES), axis=1)
            p_ref[slot, hh] = jnp.exp2(u - m_wide).astype(BF16)
            alpha_ref[slot, hh] = jnp.exp2(m_prev - m_new)
            m_ref[hh] = m_new

    def accumulate(j, slot, heads):
        v = v_ref[pl.ds(pl.multiple_of(j * tile, tile), tile), :]
        for hh in heads:
            v_aug = jnp.where(head_lanes[hh], v, one)
            acc_ref[hh] = alpha_ref[slot, hh] * acc_ref[hh] + _dot(p_ref[slot, hh], v_aug)

    def blocks(start, stop, pending, heads):
        n = stop - start
        odd = lax.rem(n, 2)

        @pl.when(odd == 1)
        def _():
            scores(start, 0, heads)
            accumulate(pending, 1, heads)
            accumulate(start, 0, heads)
            for hh in heads:
                alpha_ref[1, hh] = jnp.ones(alpha_ref.shape[2:], F32)
                p_ref[1, hh] = jnp.zeros(p_ref.shape[2:], BF16)

        def body(jj, pend):
            j0 = start + odd + 2 * jj
            scores(j0, 0, heads)
            accumulate(pend, 1, heads)
            scores(j0 + 1, 1, heads)
            accumulate(j0, 0, heads)
            return j0 + 1

        return lax.fori_loop(0, n // 2, body, pending)

    joint = jnp.maximum(first[0], first[1])
    pend_ref[0] = joint
    for hh in both:
        @pl.when(first[hh] < first[1 - hh])
        def _():
            pend_ref[0] = blocks(first[hh], joint, joint, (hh,))

    pending = blocks(joint, qi, pend_ref[0], both)
    scores(qi, 0, both, diagonal=True)
    accumulate(pending, 1, both)
    accumulate(qi, 0, both)

    outs = []
    for hh in range(2):
        a = acc_ref[hh]
        outs.append(a / pltpu.roll(a, head_dim, 1))
    o = jnp.where(lo, outs[0], outs[1])
    o_ref[...] = (o * jax.nn.sigmoid(og_ref[...])).astype(BF16)


def _first_needed_block(gq, gk, csum, *, n_heads, t_prompt, head_dim, tile):
    bf16_slack = 1.0 + 2.0 ** -7
    q_norm = math.sqrt(head_dim) * jnp.max(jnp.abs(gq)) * (head_dim ** -0.5 * LOG2E) * bf16_slack
    k_norm = math.sqrt(head_dim) * jnp.max(jnp.abs(gk)) * bf16_slack
    c = csum[:t_prompt, :n_heads]
    c_first, c_last = c[0::tile], c[tile - 1::tile]
    bound = 2.0 * q_norm * k_norm + LOG2E * (c_first[:, None, :] - c_last[None, :, :])
    blk = jnp.arange(c_first.shape[0])
    earlier = blk[None, :, None] < blk[:, None, None]
    skipped = jnp.sum(jnp.logical_and(bound < -ZERO_WEIGHT_LOG2, earlier), axis=1)
    return skipped.T.astype(jnp.int32)


def _fox_attn_prompt(q_b, k_b, v_b, csum, og, gq, gk, *, t_prompt, head_dim):
    t_all, width = q_b.shape
    tile = ROW_TILE
    pair = 2 * head_dim
    n_blk = t_prompt // tile
    n_heads = width // head_dim
    c3 = csum[:t_prompt, :n_heads].T.reshape(n_heads, n_blk, tile)
    first = _first_needed_block(gq, gk, csum, n_heads=n_heads, t_prompt=t_prompt, head_dim=head_dim, tile=tile)
    return pl.pallas_call(
        functools.partial(_fox_attn_kernel, tile=tile, head_dim=head_dim),
        grid=(width // pair, n_blk),
        in_specs=[pl.BlockSpec(memory_space=pltpu.SMEM),
                  pl.BlockSpec((tile, pair), lambda p, i: (i, p)),
                  pl.BlockSpec((t_prompt, pair), lambda p, i: (0, p)),
                  pl.BlockSpec((t_prompt, pair), lambda p, i: (0, p)),
                  pl.BlockSpec((2, n_blk, tile), lambda p, i: (p, 0, 0)),
                  pl.BlockSpec((tile, pair), lambda p, i: (i, p))],
        out_specs=pl.BlockSpec((tile, pair), lambda p, i: (i, p)),
        out_shape=jax.ShapeDtypeStruct((t_prompt, width), BF16),
        scratch_shapes=[pltpu.VMEM((2, tile, pair), BF16), pltpu.VMEM((2, tile, LANES), F32),
                        pltpu.VMEM((2, 2, tile, LANES), F32), pltpu.VMEM((2, 2, tile, tile), BF16),
                        pltpu.VMEM((2, tile, pair), F32), pltpu.SMEM((1,), jnp.int32)],
        compiler_params=_params(("arbitrary", "arbitrary")),
        name="fox_attn_prompt",
    )(first, q_b, k_b, v_b, c3, og)


def _fox_sample_kernel(q_ref, kn_ref, vn_ref, og_ref, lfn_ref, kc_ref, vc_ref, lfc_ref, su_ref, bt_ref,
                       o_ref, *, n_new, head_dim, n_heads):
    b = pl.program_id(0)
    pair = 2 * head_dim
    n_rows = kn_ref.shape[0]

    @pl.when(b == 0)
    def _():
        o_ref[...] = jnp.zeros_like(o_ref)

    su = su_ref[...]
    suf = sum(_dot(t, su) for t in _split3(lfc_ref[0])) * LOG2E
    bt = bt_ref[...]
    pre = sum(_dot(t, bt) for t in _split3(lfn_ref[...])) * LOG2E

    r0 = pl.multiple_of(b * n_new, n_new)
    qrow = lax.broadcasted_iota(jnp.int32, (n_new, n_rows), 0)
    col = lax.broadcasted_iota(jnp.int32, (n_new, n_rows), 1)
    visible = jnp.logical_and(col >= b * n_new, col <= b * n_new + qrow)
    lane = lax.broadcasted_iota(jnp.int32, (1, pair), 1)
    lo = lane < head_dim
    head_lanes = (lo, jnp.logical_not(lo))

    for p in range(n_heads // 2):
        cols = slice(p * pair, (p + 1) * pair)
        q = q_ref[pl.ds(r0, n_new), cols]
        k_c = kc_ref[0, :, cols].astype(BF16)
        v_c = vc_ref[0, :, cols].astype(BF16)
        k_n = kn_ref[:, cols]
        v_n = vn_ref[:, cols]
        outs = []
        for hh in range(2):
            head = 2 * p + hh
            qh = jnp.where(head_lanes[hh], q, jnp.zeros_like(q))
            u_c = _dot_nt(qh, k_c) + suf[head:head + 1, :]
            u_n = jnp.where(visible, _dot_nt(qh, k_n) - pre[head:head + 1, :], -jnp.inf)
            m = jnp.maximum(jnp.max(u_c, axis=1, keepdims=True), jnp.max(u_n, axis=1, keepdims=True))
            p_c = jnp.exp2(u_c - m)
            p_n = jnp.exp2(u_n - m)
            den = jnp.sum(p_c, axis=1, keepdims=True) + jnp.sum(p_n, axis=1, keepdims=True)
            outs.append((_dot(p_c.astype(BF16), v_c) + _dot(p_n.astype(BF16), v_n)) / den)
        o = jnp.where(lo, outs[0], outs[1])
        gate = jax.nn.sigmoid(og_ref[pl.ds(r0, n_new), cols])
        o_ref[pl.ds(r0, n_new), cols] = (o * gate).astype(BF16)


def _fox_attn_sample(q_b, k_b, v_b, og, lf_new_t, cache_k, cache_v, cache_lf_t, *, t_prompt, n_new, head_dim):
    t_all, width = q_b.shape
    n_streams, past, _ = cache_k.shape
    n_heads = width // head_dim
    n_rows = n_streams * n_new
    assert t_prompt % n_rows == 0 and n_rows % LANES == 0
    su = (lax.broadcasted_iota(jnp.int32, (past, past), 0) > lax.broadcasted_iota(jnp.int32, (past, past), 1)).astype(BF16)
    rr = lax.broadcasted_iota(jnp.int32, (n_rows, n_rows), 0)
    cc = lax.broadcasted_iota(jnp.int32, (n_rows, n_rows), 1)
    bt = jnp.logical_and(rr // n_new == cc // n_new, rr <= cc).astype(BF16)
    rows = pl.BlockSpec((n_rows, width), lambda b: (t_prompt // n_rows, 0), pipeline_mode=pl.Buffered(1))
    tail = t_all - t_prompt
    return pl.pallas_call(
        functools.partial(_fox_sample_kernel, n_new=n_new, head_dim=head_dim, n_heads=n_heads),
        grid=(n_streams,),
        in_specs=[rows, rows, rows, rows,
                  _const_spec(lf_new_t.shape),
                  pl.BlockSpec((1, past, width), lambda b: (b, 0, 0)),
                  pl.BlockSpec((1, past, width), lambda b: (b, 0, 0)),
                  pl.BlockSpec((1, n_heads, past), lambda b: (b, 0, 0)),
                  _const_spec(su.shape), _const_spec(bt.shape)],
        out_specs=pl.BlockSpec((tail, width), lambda b: (0, 0)),
        out_shape=jax.ShapeDtypeStruct((tail, width), BF16),
        compiler_params=_params(("arbitrary",)),
        name="fox_attn_sample",
    )(q_b, k_b, v_b, og, lf_new_t, cache_k, cache_v, cache_lf_t, su, bt)


def _ret_proj_kernel(x_ref, gn_ref, w_ref, inv_ref, q_ref, kt_ref, v_ref, g_ref,
                     *, tm, t_prompt, pos0, past, n_new, n_heads, qk_dim, v_width):
    i = pl.program_id(0)
    h = _rmsnorm_rows(x_ref[...], gn_ref[...]).astype(BF16)
    row = i * tm + lax.broadcasted_iota(jnp.int32, (tm, 1), 0)
    pos = jnp.where(row < t_prompt, row - pos0, past + lax.rem(row - t_prompt, n_new))
    ang = pos.astype(F32) * inv_ref[...]
    cos = jnp.cos(ang)
    sin = jnp.sin(ang)
    half = qk_dim // 2
    qk_width = n_heads * qk_dim

    def rotary(z):
        parts = []
        for hh in range(n_heads):
            z1 = z[:, hh * qk_dim:hh * qk_dim + half]
            z2 = z[:, hh * qk_dim + half:(hh + 1) * qk_dim]
            parts += [z1 * cos - z2 * sin, z1 * sin + z2 * cos]
        return jnp.concatenate(parts, axis=1)

    q = rotary(_dot(h, w_ref[:, 0:qk_width])) * (qk_dim ** -0.5)
    q_ref[...] = q.astype(BF16)
    kt_ref[...] = rotary(_dot(h, w_ref[:, qk_width:2 * qk_width])).T
    v_ref[...] = _dot(h, w_ref[:, 2 * qk_width:2 * qk_width + v_width]).astype(BF16)
    g_ref[...] = _dot(h, w_ref[:, 2 * qk_width + v_width:])


def _ret_proj(x, gn, w_in, inv_freq, *, t_prompt, pos0, past, n_new, n_heads, qk_dim, v_width):
    t_all, d = x.shape
    tm = PROJ_TILE
    qk_width = n_heads * qk_dim
    row = lambda n: pl.BlockSpec((tm, n), lambda i: (i, 0))
    sds = jax.ShapeDtypeStruct
    return pl.pallas_call(
        functools.partial(_ret_proj_kernel, tm=tm, t_prompt=t_prompt, pos0=pos0, past=past, n_new=n_new,
                          n_heads=n_heads, qk_dim=qk_dim, v_width=v_width),
        grid=(t_all // tm,),
        in_specs=[row(d), _const_spec((1, d)), _const_spec(w_in.shape), _const_spec(inv_freq.shape)],
        out_specs=[row(qk_width), pl.BlockSpec((qk_width, tm), lambda i: (0, i)), row(v_width), row(v_width)],
        out_shape=[sds((t_all, qk_width), BF16), sds((qk_width, t_all), F32),
                   sds((t_all, v_width), BF16), sds((t_all, v_width), F32)],
        compiler_params=_params(("arbitrary",)),
        name="ret_proj",
    )(x, gn, w_in, inv_freq)


def _group_norm_gate(o, g, gn):
    mu = jnp.mean(o, axis=-1, keepdims=True)
    var = jnp.mean(jnp.square(o - mu), axis=-1, keepdims=True)
    y = (o - mu) * lax.rsqrt(var + EPS) * gn
    return (g * jax.nn.sigmoid(g) * y).astype(BF16)


def _ret_prompt_kernel(logg_ref, q_ref, kt_ref, v_ref, g_ref, gn_ref, y_ref, st_ref, s_ref, dmat_ref, *, tb):
    hh = pl.program_id(0)
    blk = pl.program_id(1)
    lg = jnp.full((1, 1), logg_ref[hh], F32)

    @pl.when(blk == 0)
    def _():
        s_ref[...] = jnp.zeros_like(s_ref)
        r = lax.broadcasted_iota(jnp.int32, (tb, tb), 0)
        c = lax.broadcasted_iota(jnp.int32, (tb, tb), 1)
        decay = jnp.exp(lg * jnp.abs(r - c).astype(F32))
        dmat_ref[...] = jnp.where(c // CHUNK <= r // CHUNK, decay, 0.0)

    q = q_ref[...]
    kt = kt_ref[...]
    v = v_ref[...]
    s = _dot(q, kt.astype(BF16)) * dmat_ref[...]
    state = s_ref[...]
    q_dec = jnp.exp(lg * (lax.broadcasted_iota(jnp.int32, (tb, 1), 0) + 1).astype(F32))
    o = _dot(s.astype(BF16), v) + _dot(q, state.astype(BF16)) * q_dec
    k_dec = jnp.exp(lg * (tb - 1 - lax.broadcasted_iota(jnp.int32, (1, tb), 1)).astype(F32))
    new_state = state * jnp.exp(lg * tb) + _dot((kt * k_dec).astype(BF16), v)
    s_ref[...] = new_state
    y_ref[...] = _group_norm_gate(o, g_ref[...], gn_ref[...])

    @pl.when(blk == pl.num_programs(1) - 1)
    def _():
        st_ref[0] = new_state


def _ret_prompt(logg, q_b, k_t, v_b, g, gn, *, t_prompt, n_heads):
    t_all, qk_width = q_b.shape
    v_width = v_b.shape[1]
    qk_dim, v_dim = qk_width // n_heads, v_width // n_heads
    tb = RET_BLOCK
    return pl.pallas_call(
        functools.partial(_ret_prompt_kernel, tb=tb),
        grid=(n_heads, t_prompt // tb),
        in_specs=[pl.BlockSpec(memory_space=pltpu.SMEM),
                  pl.BlockSpec((tb, qk_dim), lambda h, i: (i, h)),
                  pl.BlockSpec((qk_dim, tb), lambda h, i: (h, i)),
                  pl.BlockSpec((tb, v_dim), lambda h, i: (i, h)),
                  pl.BlockSpec((tb, v_dim), lambda h, i: (i, h)),
                  pl.BlockSpec((1, v_dim), lambda h, i: (0, h))],
        out_specs=[pl.BlockSpec((tb, v_dim), lambda h, i: (i, h)),
                   pl.BlockSpec((1, qk_dim, v_dim), lambda h, i: (h, 0, 0))],
        out_shape=[jax.ShapeDtypeStruct((t_prompt, v_width), BF16),
                   jax.ShapeDtypeStruct((n_heads, qk_dim, v_dim), F32)],
        scratch_shapes=[pltpu.VMEM((qk_dim, v_dim), F32), pltpu.VMEM((tb, tb), F32)],
        compiler_params=_params(("arbitrary", "arbitrary")),
        name="ret_prompt",
    )(logg, q_b, k_t, v_b, g, gn)


def _ret_sample_kernel(logg_ref, q_ref, kt_ref, v_ref, g_ref, gn_ref, st_ref, y_ref, nst_ref, *, n_new):
    hh = pl.program_id(0)
    b = pl.program_id(1)
    lg = jnp.full((1, 1), logg_ref[hh], F32)
    n_rows = kt_ref.shape[1]

    @pl.when(b == 0)
    def _():
        y_ref[...] = jnp.zeros_like(y_ref)

    r0 = pl.multiple_of(b * n_new, n_new)
    q = q_ref[pl.ds(r0, n_new), :]
    kt = kt_ref[...]
    v = v_ref[...]
    qrow = lax.broadcasted_iota(jnp.int32, (n_new, n_rows), 0)
    col = lax.broadcasted_iota(jnp.int32, (n_new, n_rows), 1) - b * n_new
    own = jnp.logical_and(col >= 0, col < n_new)
    dmat = jnp.where(own, jnp.exp(lg * jnp.abs(qrow - col).astype(F32)), 0.0)
    s = _dot(q, kt.astype(BF16)) * dmat
    state = st_ref[0, 0]
    q_dec = jnp.exp(lg * (lax.broadcasted_iota(jnp.int32, (n_new, 1), 0) + 1).astype(F32))
    o = _dot(s.astype(BF16), v) + _dot(q, state.astype(BF16)) * q_dec
    col1 = lax.broadcasted_iota(jnp.int32, (1, n_rows), 1) - b * n_new
    own1 = jnp.logical_and(col1 >= 0, col1 < n_new)
    k_dec = jnp.where(own1, jnp.exp(lg * (n_new - 1 - col1).astype(F32)), 0.0)
    nst_ref[0, 0] = state * jnp.exp(lg * n_new) + _dot((kt * k_dec).astype(BF16), v)
    y_ref[pl.ds(r0, n_new), :] = _group_norm_gate(o, g_ref[pl.ds(r0, n_new), :], gn_ref[...])


def _ret_sample(logg, q_b, k_t, v_b, g, gn, state, *, t_prompt, n_new):
    t_all, qk_width = q_b.shape
    v_width = v_b.shape[1]
    n_streams, n_heads, qk_dim, v_dim = state.shape
    n_rows = n_streams * n_new
    tail = t_all - t_prompt
    blk = t_prompt // n_rows
    return pl.pallas_call(
        functools.partial(_ret_sample_kernel, n_new=n_new),
        grid=(n_heads, n_streams),
        in_specs=[pl.BlockSpec(memory_space=pltpu.SMEM),
                  pl.BlockSpec((n_rows, qk_dim), lambda h, b: (blk, h)),
                  pl.BlockSpec((qk_dim, n_rows), lambda h, b: (h, blk)),
                  pl.BlockSpec((n_rows, v_dim), lambda h, b: (blk, h)),
                  pl.BlockSpec((n_rows, v_dim), lambda h, b: (blk, h)),
                  pl.BlockSpec((1, v_dim), lambda h, b: (0, h)),
                  pl.BlockSpec((1, 1, qk_dim, v_dim), lambda h, b: (b, h, 0, 0))],
        out_specs=[pl.BlockSpec((tail, v_dim), lambda h, b: (0, h)),
                   pl.BlockSpec((1, 1, qk_dim, v_dim), lambda h, b: (b, h, 0, 0))],
        out_shape=[jax.ShapeDtypeStruct((tail, v_width), BF16),
                   jax.ShapeDtypeStruct(state.shape, F32)],
        compiler_params=_params(("arbitrary", "arbitrary")),
        name="ret_sample",
    )(logg, q_b, k_t, v_b, g, gn, state)


def _top_values(s, k, with_rank):
    work = s
    vals = []
    for r in range(k):
        m = jnp.max(work, axis=0, keepdims=True)
        vals.append(m)
        work = jnp.where(work == m, RANK_MARK * (1.0 + r / 1024.0), work)
    if not with_rank:
        return vals, None
    rank = jnp.where(work <= RANK_MARK, (work * (1.0 / RANK_MARK) - 1.0) * 1024.0, NO_RANK)
    return vals, rank


def _peer_kernel(x_ref, mp_ref, mt_ref, wo_ref, gn_ref, wq_ref, sk_ref, u_ref, vt_ref, o_ref,
                 xn_ref, hb_ref, e0_ref, cnt_ref, e1_ref, rank_ref, w_ref, y_ref,
                 *, n_heads, n_keys, topk, n_prompt_tiles):
    c = pl.program_id(1)
    ce, tm = u_ref.shape[0], x_ref.shape[0]
    rows_per_chunk = ce // n_keys
    sub = rank_ref.shape[2]

    @pl.when(c == 0)
    def _():
        mixed = jnp.where(pl.program_id(0) < n_prompt_tiles, mp_ref[...], mt_ref[...])
        xn = x_ref[...] + _dot(mixed, wo_ref[...])
        xn_ref[...] = xn
        hb = _rmsnorm_rows(xn, gn_ref[...]).astype(BF16)
        hb_ref[...] = hb
        y_ref[...] = jnp.zeros_like(y_ref)
        for h in range(n_heads):
            qh = _dot_nt(wq_ref[h * 2 * n_keys:(h + 1) * 2 * n_keys, :], hb)
            s0 = _dot(sk_ref[2 * h], qh[0:n_keys].astype(BF16))
            s1 = _dot(sk_ref[2 * h + 1], qh[n_keys:2 * n_keys].astype(BF16))
            a, rank0 = _top_values(s0, topk, True)
            b, rank1 = _top_values(s1, topk, True)
            half = topk // 2
            b_all = jnp.concatenate(b, axis=0)
            b_half = jnp.concatenate(b[:half], axis=0)
            cand = jnp.concatenate(
                [a[0] + b_all] + [a[i] + b_half for i in range(1, half)] + [jnp.concatenate(a[half:], axis=0) + b[0]],
                axis=0)
            f, _ = _top_values(cand, topk, False)
            tau = f[topk - 1]
            z = sum(jnp.exp(fk - f[0]) for fk in f)
            a_all = jnp.concatenate(a, axis=0)
            cnt_sorted = sum(jnp.where(a_all + bj >= tau, 1.0, 0.0) for bj in b)
            cnt = jnp.zeros_like(s0)
            for r in range(topk):
                cnt = jnp.where(rank0 == float(r), cnt_sorted[r:r + 1, :], cnt)
            cnt_ref[h] = cnt
            e0_ref[h] = jnp.exp(s0 - a[0])
            e1_ref[h] = (jnp.exp(s1 - b[0]) * (0.5 / z)).astype(BF16).reshape(n_keys // sub, sub, tm)
            rank_ref[h] = rank1.astype(BF16).reshape(n_keys // sub, sub, tm)

    hb = hb_ref[...]
    edges = PEER_ACT_EDGES + (ce,)
    act_of_row = {}
    for lo_row, hi_row in zip(edges[:-1], edges[1:]):
        piece = _dot_nt(u_ref[lo_row:hi_row, :], hb)
        for r in range(lo_row, hi_row, n_keys):
            act_of_row[r] = (piece, r - lo_row)
    rows_per_slab = MXU_DIM // n_keys
    y_add = None
    for slab in range(ce // MXU_DIM):
        for il in range(slab * rows_per_slab, (slab + 1) * rows_per_slab):
            i = c * rows_per_chunk + il
            gate = None
            for h in range(n_heads):
                n_sel = jnp.broadcast_to(cnt_ref[h, pl.ds(i, 1), :], (sub, tm)).astype(BF16)
                e0 = jnp.broadcast_to(e0_ref[h, pl.ds(i, 1), :], (sub, tm)).astype(BF16)
                term = jnp.where(rank_ref[h] < n_sel[None], e1_ref[h], jnp.zeros((), BF16)) * e0[None]
                gate = term if gate is None else gate + term
            r0 = il * n_keys
            piece, off = act_of_row[r0]
            x_act = piece[off:off + n_keys, :]
            gelu2 = x_act * (1.0 + lax.erf(x_act * (2.0 ** -0.5)))
            w_ref[r0:r0 + n_keys, :] = gate.reshape(n_keys, tm) * gelu2.astype(BF16)
        rows = slice(slab * MXU_DIM, (slab + 1) * MXU_DIM)
        part = _dot(vt_ref[:, rows], w_ref[rows, :])
        y_add = part if y_add is None else y_add + part
    y_ref[...] += y_add

    @pl.when(c == pl.num_programs(1) - 1)
    def _():
        o_ref[...] = xn_ref[...] + y_ref[...].T


def _peer(x, mix_prompt, mix_tail, w_out, gn, wq_t, subkeys, u, v_t, *, layer, n_heads, n_keys):
    t_all, d = x.shape
    t_prompt, k = mix_prompt.shape
    n_experts = u.shape[1]
    tm = ROW_TILE
    ce = PEER_EXPERT_CHUNK
    assert mix_tail.shape == (tm, k) and t_all == t_prompt + tm
    n_p = t_prompt // tm
    head_scratch = pltpu.VMEM((n_heads, n_keys, tm), F32)
    head_scratch_b = pltpu.VMEM((n_heads, n_keys // BF16_TILE_ROWS, BF16_TILE_ROWS, tm), BF16)
    return pl.pallas_call(
        functools.partial(_peer_kernel, n_heads=n_heads, n_keys=n_keys, topk=PEER_TOPK, n_prompt_tiles=n_p),
        grid=(t_all // tm, n_experts // ce),
        in_specs=[pl.BlockSpec((tm, d), lambda i, c: (i, 0)),
                  pl.BlockSpec((tm, k), lambda i, c: (jnp.minimum(i, n_p - 1), 0)),
                  _const_spec((tm, k)), _const_spec(w_out.shape),
                  _const_spec((1, d)), _const_spec(wq_t.shape),
                  _const_spec(subkeys.shape),
                  pl.BlockSpec((None, ce, d), lambda i, c: (layer, c, 0)),
                  pl.BlockSpec((None, d, ce), lambda i, c: (layer, 0, c))],
        out_specs=pl.BlockSpec((tm, d), lambda i, c: (i, 0)),
        out_shape=jax.ShapeDtypeStruct((t_all, d), F32),
        scratch_shapes=[pltpu.VMEM((tm, d), F32), pltpu.VMEM((tm, d), BF16),
                        head_scratch, head_scratch, head_scratch_b, head_scratch_b,
                        pltpu.VMEM((ce, tm), BF16), pltpu.VMEM((d, tm), F32)],
        compiler_params=_params(("arbitrary", "arbitrary")),
        name="peer",
    )(x, mix_prompt, mix_tail, w_out, gn, wq_t, subkeys, u, v_t)


def kernel(x_prompt, x_sample, cache_fox_k, cache_fox_v, cache_fox_lf, state_ret, meta_tokens, norm_mix, norm_ffn,
           fox_w_in, fox_b_f, fox_q_norm, fox_k_norm, fox_w_out, ret_w_in, ret_gn, ret_w_out,
           peer_w_q, peer_subkeys, peer_u, peer_v):
    batch, seq, d = x_prompt.shape
    n_streams, n_new, _ = x_sample.shape
    n_meta = meta_tokens.shape[0]
    assert batch == 1 and n_meta == N_META
    depth = norm_mix.shape[0]
    n_fox, _, past, fox_heads, fox_hd = cache_fox_k.shape
    fox_width = fox_heads * fox_hd
    n_ret, _, ret_heads, ret_qk, ret_v = state_ret.shape
    ret_v_width = ret_heads * ret_v
    peer_heads, _, n_keys, peer_half = peer_subkeys.shape[1:]
    assert n_keys == LANES and peer_half == LANES and fox_width == d

    length = n_meta + seq
    front = (-length) % ROW_TILE
    t_prompt = front + length
    n_rows = n_streams * n_new
    tail = ROW_TILE
    assert n_rows <= tail
    t_all = t_prompt + tail

    x = jnp.concatenate([jnp.zeros((front, d), F32), meta_tokens.astype(F32), x_prompt[0],
                         x_sample.reshape(n_rows, d), jnp.zeros((tail - n_rows, d), F32)], axis=0)

    half = ret_qk // 2
    inv_freq = (ROPE_BASE ** (-jnp.arange(half, dtype=F32) / half)).reshape(1, half)
    logg = jnp.log(1.0 - 2.0 ** (-5.0 - jnp.arange(ret_heads, dtype=F32)))

    peer_u_b = peer_u.astype(BF16)
    peer_vt_b = peer_v.astype(BF16).transpose(0, 2, 1)

    kp, vp, lfp, ks, vs, lfs, srp, srs = [], [], [], [], [], [], [], []
    for layer in range(depth):
        j = layer // 2
        gn_mix = norm_mix[layer].reshape(1, d)
        if layer % 2 == 0:
            w_in = fox_w_in[j]
            w_main = w_in[:, :4 * fox_width].astype(BF16)
            w_f = jnp.pad(w_in[:, 4 * fox_width:], ((0, 0), (0, LANES - fox_heads))).astype(BF16)
            b_f = jnp.pad(fox_b_f[j], (0, LANES - fox_heads)).reshape(1, LANES)
            gq = jnp.tile(fox_q_norm[j], fox_heads).reshape(1, fox_width)
            gk = jnp.tile(fox_k_norm[j], fox_heads).reshape(1, fox_width)
            q_b, k_f, k_b, v_f, v_b, og, lf, csum = _fox_proj(x, gn_mix, w_main, w_f, b_f, gq, gk,
                                                              front=front, head_dim=fox_hd)
            o_b = _fox_attn_prompt(q_b, k_b, v_b, csum, og, fox_q_norm[j], fox_k_norm[j],
                                   t_prompt=t_prompt, head_dim=fox_hd)
            lf_new_t = lf[t_prompt:t_prompt + n_rows, :fox_heads].T
            o_tail = _fox_attn_sample(q_b, k_b, v_b, og, lf_new_t,
                                      cache_fox_k[j].reshape(n_streams, past, fox_width),
                                      cache_fox_v[j].reshape(n_streams, past, fox_width),
                                      cache_fox_lf[j].transpose(0, 2, 1),
                                      t_prompt=t_prompt, n_new=n_new, head_dim=fox_hd)
            mix_prompt, mix_tail, w_out = o_b, o_tail, fox_w_out[j].astype(BF16)
            kp.append(k_f)
            vp.append(v_f)
            lfp.append(lf[front:t_prompt, :fox_heads].reshape(1, length, fox_heads))
            ks.append(k_f[t_prompt:t_prompt + n_rows].reshape(n_streams, n_new, fox_heads, fox_hd))
            vs.append(v_f[t_prompt:t_prompt + n_rows].reshape(n_streams, n_new, fox_heads, fox_hd))
            lfs.append(lf[t_prompt:t_prompt + n_rows, :fox_heads].reshape(n_streams, n_new, fox_heads))
        else:
            q_b, k_t, v_b, g = _ret_proj(x, gn_mix, ret_w_in[j].astype(BF16), inv_freq,
                                         t_prompt=t_prompt, pos0=front + n_meta, past=past, n_new=n_new,
                                         n_heads=ret_heads, qk_dim=ret_qk, v_width=ret_v_width)
            gn_ret = ret_gn[j].reshape(1, ret_v_width)
            y_b, st_p = _ret_prompt(logg, q_b, k_t, v_b, g, gn_ret, t_prompt=t_prompt, n_heads=ret_heads)
            y_tail, st_s = _ret_sample(logg, q_b, k_t, v_b, g, gn_ret, state_ret[j],
                                       t_prompt=t_prompt, n_new=n_new)
            mix_prompt, mix_tail, w_out = y_b, y_tail, ret_w_out[j].astype(BF16)
            srp.append(st_p[None])
            srs.append(st_s)
        x = _peer(x, mix_prompt, mix_tail, w_out, norm_ffn[layer].reshape(1, d), peer_w_q[layer].T.astype(BF16),
                  peer_subkeys[layer].reshape(2 * peer_heads, n_keys, peer_half).astype(BF16),
                  peer_u_b, peer_vt_b, layer=layer, n_heads=peer_heads, n_keys=n_keys)

    y_prompt = x[front + n_meta:t_prompt][None]
    y_sample = x[t_prompt:t_prompt + n_rows].reshape(n_streams, n_new, d)
    prompt_cache = lambda a: jnp.stack(a)[:, front:t_prompt].reshape(len(a), 1, length, fox_heads, fox_hd)
    return (y_prompt, y_sample, prompt_cache(kp), prompt_cache(vp), jnp.stack(lfp), jnp.stack(srp),
            jnp.stack(ks), jnp.stack(vs), jnp.stack(lfs), jnp.stack(srs))
```

```python
import functools
import math

import jax
import jax.numpy as jnp
from jax import lax
from jax.experimental import pallas as pl
from jax.experimental.pallas import tpu as pltpu

F32 = jnp.float32
BF16 = jnp.bfloat16

EPS = 1e-6
N_META = 16
CHUNK = 64
ROPE_BASE = 10000.0
PEER_TOPK = 16

LANES = 128
MXU_DIM = 256
BF16_TILE_ROWS = 16
VMEM_LIMIT = 56 * 1024 * 1024

ROW_TILE = 512
PROJ_TILE = 256
RET_BLOCK = 512
PEER_EXPERT_CHUNK = 2048
PEER_ACT_EDGES = (0, 1024)
MASKED_KEY = 1e30
NO_RANK = 99.0
RANK_MARK = -(2.0 ** 100)
LOG2E = math.log2(math.e)
ZERO_WEIGHT_LOG2 = 160.0

_NT = (((1,), (1,)), ((), ()))


def _const_spec(shape):
    return pl.BlockSpec(shape, lambda *_: (0,) * len(shape), pipeline_mode=pl.Buffered(1))


def _params(semantics):
    return pltpu.CompilerParams(dimension_semantics=semantics, vmem_limit_bytes=VMEM_LIMIT)


def _split3(x):
    a = x.astype(BF16)
    r = x - a.astype(F32)
    b = r.astype(BF16)
    c = (r - b.astype(F32)).astype(BF16)
    return a, b, c


def _dot(a, b):
    return jnp.dot(a, b, preferred_element_type=F32)


def _dot_nt(a, b):
    return lax.dot_general(a, b, _NT, preferred_element_type=F32)


def _rmsnorm_rows(x, g):
    ms = jnp.mean(x * x, axis=-1, keepdims=True)
    return x * lax.rsqrt(ms + EPS) * g


def _log_sigmoid(x):
    return jnp.minimum(x, 0.0) - jnp.log1p(jnp.exp(-jnp.abs(x)))


def _fox_proj_kernel(x_ref, gn_ref, w_ref, wf_ref, bf_ref, gq_ref, gk_ref, bd_ref, tril_ref,
                     q_ref, kf_ref, kb_ref, vf_ref, vb_ref, og_ref, lf_ref, c_ref, carry_ref,
                     *, tm, front, head_dim, width):
    i = pl.program_id(0)

    @pl.when(i == 0)
    def _():
        carry_ref[...] = jnp.zeros_like(carry_ref)

    h = _rmsnorm_rows(x_ref[...], gn_ref[...]).astype(BF16)

    def head_norm(z, g):
        zz = (z * z).astype(BF16)
        ss = jnp.concatenate(
            [_dot(zz[:, c * MXU_DIM:(c + 1) * MXU_DIM], bd_ref[...]) for c in range(width // MXU_DIM)], axis=1)
        return z * lax.rsqrt(ss * (1.0 / head_dim) + EPS) * g

    zq = _dot(h, w_ref[:, 0:width])
    q_ref[...] = (head_norm(zq, gq_ref[...]) * (head_dim ** -0.5 * LOG2E)).astype(BF16)
    zk = _dot(h, w_ref[:, width:2 * width])
    kn = head_norm(zk, gk_ref[...])
    kf_ref[...] = kn
    kb_ref[...] = kn.astype(BF16)
    zv = _dot(h, w_ref[:, 2 * width:3 * width])
    vf_ref[...] = zv
    vb_ref[...] = zv.astype(BF16)
    og_ref[...] = _dot(h, w_ref[:, 3 * width:4 * width])

    lf = _log_sigmoid(_dot(h, wf_ref[...]) + bf_ref[...])
    lf_ref[...] = lf
    row = i * tm + lax.broadcasted_iota(jnp.int32, (tm, 1), 0)
    real = row >= front
    l1, l2, l3 = _split3(jnp.where(real, lf, 0.0))
    tril = tril_ref[...]
    c = _dot(tril, l1) + _dot(tril, l2) + _dot(tril, l3) + carry_ref[...]
    carry_ref[...] = c[tm - 1:tm, :]
    c_ref[...] = jnp.where(real, c, MASKED_KEY)


def _fox_proj(x, gn, w_main, w_f, b_f, gq, gk, *, front, head_dim):
    t_all, d = x.shape
    width = w_main.shape[1] // 4
    tm = PROJ_TILE
    blk = MXU_DIM // head_dim
    bd = jnp.kron(jnp.eye(blk, dtype=F32), jnp.ones((head_dim, head_dim), F32)).astype(BF16)
    tril = (lax.broadcasted_iota(jnp.int32, (tm, tm), 0) >= lax.broadcasted_iota(jnp.int32, (tm, tm), 1)).astype(BF16)
    row = lambda n: pl.BlockSpec((tm, n), lambda i: (i, 0))
    sds = jax.ShapeDtypeStruct
    return pl.pallas_call(
        functools.partial(_fox_proj_kernel, tm=tm, front=front, head_dim=head_dim, width=width),
        grid=(t_all // tm,),
        in_specs=[row(d), _const_spec((1, d)), _const_spec(w_main.shape), _const_spec(w_f.shape),
                  _const_spec((1, LANES)), _const_spec((1, width)), _const_spec((1, width)),
                  _const_spec(bd.shape), _const_spec(tril.shape)],
        out_specs=[row(width), row(width), row(width), row(width), row(width), row(width), row(LANES), row(LANES)],
        out_shape=[sds((t_all, width), BF16), sds((t_all, width), F32), sds((t_all, width), BF16),
                   sds((t_all, width), F32), sds((t_all, width), BF16), sds((t_all, width), F32),
                   sds((t_all, LANES), F32), sds((t_all, LANES), F32)],
        scratch_shapes=[pltpu.VMEM((1, LANES), F32)],
        compiler_params=_params(("arbitrary",)),
        name="fox_proj",
    )(x, gn, w_main, w_f, b_f, gq, gk, bd, tril)


def _fox_attn_kernel(first_ref, q_ref, k_ref, v_ref, c_ref, og_ref, o_ref,
                     qm_ref, m_ref, alpha_ref, p_ref, acc_ref, pend_ref, *, tile, head_dim):
    qi = pl.program_id(1)
    first = tuple(first_ref[2 * pl.program_id(0) + hh, qi] for hh in range(2))
    lane = lax.broadcasted_iota(jnp.int32, (1, 2 * head_dim), 1)
    lo = lane < head_dim
    head_lanes = (lo, jnp.logical_not(lo))
    q = q_ref[...]
    for hh in range(2):
        qm_ref[hh] = jnp.where(head_lanes[hh], q, jnp.zeros_like(q))
    m_ref[...] = jnp.full(m_ref.shape, -jnp.inf, F32)
    acc_ref[...] = jnp.zeros(acc_ref.shape, F32)
    alpha_ref[1] = jnp.ones(alpha_ref.shape[1:], F32)
    p_ref[1] = jnp.zeros(p_ref.shape[1:], BF16)
    one = jnp.ones((), BF16)
    both = (0, 1)

    def scores(j, slot, heads, diagonal=False):
        k = k_ref[pl.ds(pl.multiple_of(j * tile, tile), tile), :]
        for hh in heads:
            c_last = c_ref[hh, pl.ds(qi, 1), :][:, tile - 1:tile]
            bias = (c_ref[hh, pl.ds(j, 1), :] - c_last) * LOG2E
            u = _dot_nt(qm_ref[hh], k) - bias
            if diagonal:
                r = lax.broadcasted_iota(jnp.int32, (tile, tile), 0)
                cidx = lax.broadcasted_iota(jnp.int32, (tile, tile), 1)
                u = jnp.where(cidx <= r, u, -jnp.inf)
            m_prev = m_ref[hh]
            m_new = jnp.maximum(m_prev, jnp.max(u, axis=1, keepdims=True))
            m_wide = jnp.concatenate([m_new] * (tile // LANES), axis=1)
            p_ref[slot, hh] = jnp.exp2(u - m_wide).astype(BF16)
            alpha_ref[slot, hh] = jnp.exp2(m_prev - m_new)
            m_ref[hh] = m_new

    def accumulate(j, slot, heads):
        v = v_ref[pl.ds(pl.multiple_of(j * tile, tile), tile), :]
        for hh in heads:
            v_aug = jnp.where(head_lanes[hh], v, one)
            acc_ref[hh] = alpha_ref[slot, hh] * acc_ref[hh] + _dot(p_ref[slot, hh], v_aug)

    def blocks(start, stop, pending, heads):
        n = stop - start
        odd = lax.rem(n, 2)

        @pl.when(odd == 1)
        def _():
            scores(start, 0, heads)
            accumulate(pending, 1, heads)
            accumulate(start, 0, heads)
            for hh in heads:
                alpha_ref[1, hh] = jnp.ones(alpha_ref.shape[2:], F32)
                p_ref[1, hh] = jnp.zeros(p_ref.shape[2:], BF16)

        def body(jj, pend):
            j0 = start + odd + 2 * jj
            scores(j0, 0, heads)
            accumulate(pend, 1, heads)
            scores(j0 + 1, 1, heads)
            accumulate(j0, 0, heads)
            return j0 + 1

        return lax.fori_loop(0, n // 2, body, pending)

    joint = jnp.maximum(first[0], first[1])
    pend_ref[0] = joint
    for hh in both:
        @pl.when(first[hh] < first[1 - hh])
        def _():
            pend_ref[0] = blocks(first[hh], joint, joint, (hh,))

    pending = blocks(joint, qi, pend_ref[0], both)
    scores(qi, 0, both, diagonal=True)
    accumulate(pending, 1, both)
    accumulate(qi, 0, both)

    outs = []
    for hh in range(2):
        a = acc_ref[hh]
        outs.append(a / pltpu.roll(a, head_dim, 1))
    o = jnp.where(lo, outs[0], outs[1])
    o_ref[...] = (o * jax.nn.sigmoid(og_ref[...])).astype(BF16)


def _first_needed_block(gq, gk, csum, *, n_heads, t_prompt, head_dim, tile):
    bf16_slack = 1.0 + 2.0 ** -7
    q_norm = math.sqrt(head_dim) * jnp.max(jnp.abs(gq)) * (head_dim ** -0.5 * LOG2E) * bf16_slack
    k_norm = math.sqrt(head_dim) * jnp.max(jnp.abs(gk)) * bf16_slack
    c = csum[:t_prompt, :n_heads]
    c_first, c_last = c[0::tile], c[tile - 1::tile]
    bound = 2.0 * q_norm * k_norm + LOG2E * (c_first[:, None, :] - c_last[None, :, :])
    blk = jnp.arange(c_first.shape[0])
    earlier = blk[None, :, None] < blk[:, None, None]
    skipped = jnp.sum(jnp.logical_and(bound < -ZERO_WEIGHT_LOG2, earlier), axis=1)
    return skipped.T.astype(jnp.int32)


def _fox_attn_prompt(q_b, k_b, v_b, csum, og, gq, gk, *, t_prompt, head_dim):
    t_all, width = q_b.shape
    tile = ROW_TILE
    pair = 2 * head_dim
    n_blk = t_prompt // tile
    n_heads = width // head_dim
    c3 = csum[:t_prompt, :n_heads].T.reshape(n_heads, n_blk, tile)
    first = _first_needed_block(gq, gk, csum, n_heads=n_heads, t_prompt=t_prompt, head_dim=head_dim, tile=tile)
    return pl.pallas_call(
        functools.partial(_fox_attn_kernel, tile=tile, head_dim=head_dim),
        grid=(width // pair, n_blk),
        in_specs=[pl.BlockSpec(memory_space=pltpu.SMEM),
                  pl.BlockSpec((tile, pair), lambda p, i: (i, p)),
                  pl.BlockSpec((t_prompt, pair), lambda p, i: (0, p)),
                  pl.BlockSpec((t_prompt, pair), lambda p, i: (0, p)),
                  pl.BlockSpec((2, n_blk, tile), lambda p, i: (p, 0, 0)),
                  pl.BlockSpec((tile, pair), lambda p, i: (i, p))],
        out_specs=pl.BlockSpec((tile, pair), lambda p, i: (i, p)),
        out_shape=jax.ShapeDtypeStruct((t_prompt, width), BF16),
        scratch_shapes=[pltpu.VMEM((2, tile, pair), BF16), pltpu.VMEM((2, tile, LANES), F32),
                        pltpu.VMEM((2, 2, tile, LANES), F32), pltpu.VMEM((2, 2, tile, tile), BF16),
                        pltpu.VMEM((2, tile, pair), F32), pltpu.SMEM((1,), jnp.int32)],
        compiler_params=_params(("arbitrary", "arbitrary")),
        name="fox_attn_prompt",
    )(first, q_b, k_b, v_b, c3, og)


def _fox_sample_kernel(q_ref, kn_ref, vn_ref, og_ref, lfn_ref, kc_ref, vc_ref, lfc_ref, su_ref, bt_ref,
                       o_ref, *, n_new, head_dim, n_heads):
    b = pl.program_id(0)
    pair = 2 * head_dim
    n_rows = kn_ref.shape[0]

    su = su_ref[...]
    suf = sum(_dot(t, su) for t in _split3(lfc_ref[0])) * LOG2E
    bt = bt_ref[...]
    pre = sum(_dot(t, bt) for t in _split3(lfn_ref[...])) * LOG2E

    r0 = pl.multiple_of(b * n_new, n_new)
    qrow = lax.broadcasted_iota(jnp.int32, (n_new, n_rows), 0)
    col = lax.broadcasted_iota(jnp.int32, (n_new, n_rows), 1)
    visible = jnp.logical_and(col >= b * n_new, col <= b * n_new + qrow)
    lane = lax.broadcasted_iota(jnp.int32, (1, pair), 1)
    lo = lane < head_dim
    head_lanes = (lo, jnp.logical_not(lo))

    for p in range(n_heads // 2):
        cols = slice(p * pair, (p + 1) * pair)
        q = q_ref[pl.ds(r0, n_new), cols]
        k_c = kc_ref[0, :, cols].astype(BF16)
        v_c = vc_ref[0, :, cols].astype(BF16)
        k_n = kn_ref[:, cols]
        v_n = vn_ref[:, cols]
        outs = []
        for hh in range(2):
            head = 2 * p + hh
            qh = jnp.where(head_lanes[hh], q, jnp.zeros_like(q))
            u_c = _dot_nt(qh, k_c) + suf[head:head + 1, :]
            u_n = jnp.where(visible, _dot_nt(qh, k_n) - pre[head:head + 1, :], -jnp.inf)
            m = jnp.maximum(jnp.max(u_c, axis=1, keepdims=True), jnp.max(u_n, axis=1, keepdims=True))
            p_c = jnp.exp2(u_c - m)
            p_n = jnp.exp2(u_n - m)
            den = jnp.sum(p_c, axis=1, keepdims=True) + jnp.sum(p_n, axis=1, keepdims=True)
            outs.append((_dot(p_c.astype(BF16), v_c) + _dot(p_n.astype(BF16), v_n)) / den)
        o = jnp.where(lo, outs[0], outs[1])
        gate = jax.nn.sigmoid(og_ref[pl.ds(r0, n_new), cols])
        o_ref[pl.ds(r0, n_new), cols] = (o * gate).astype(BF16)


def _fox_attn_sample(q_b, k_b, v_b, og, lf_new_t, cache_k, cache_v, cache_lf_t, *, n_new, head_dim):
    t_all, width = q_b.shape
    n_streams, past, _ = cache_k.shape
    n_heads = width // head_dim
    n_rows = n_streams * n_new
    assert n_rows % LANES == 0
    su = (lax.broadcasted_iota(jnp.int32, (past, past), 0) > lax.broadcasted_iota(jnp.int32, (past, past), 1)).astype(BF16)
    rr = lax.broadcasted_iota(jnp.int32, (n_rows, n_rows), 0)
    cc = lax.broadcasted_iota(jnp.int32, (n_rows, n_rows), 1)
    bt = jnp.logical_and(rr // n_new == cc // n_new, rr <= cc).astype(BF16)
    rows = pl.BlockSpec((n_rows, width), lambda b: (0, 0), pipeline_mode=pl.Buffered(1))
    return pl.pallas_call(
        functools.partial(_fox_sample_kernel, n_new=n_new, head_dim=head_dim, n_heads=n_heads),
        grid=(n_streams,),
        in_specs=[rows, rows, rows, rows,
                  _const_spec(lf_new_t.shape),
                  pl.BlockSpec((1, past, width), lambda b: (b, 0, 0)),
                  pl.BlockSpec((1, past, width), lambda b: (b, 0, 0)),
                  pl.BlockSpec((1, n_heads, past), lambda b: (b, 0, 0)),
                  _const_spec(su.shape), _const_spec(bt.shape)],
        out_specs=pl.BlockSpec((n_rows, width), lambda b: (0, 0)),
        out_shape=jax.ShapeDtypeStruct((n_rows, width), BF16),
        compiler_params=_params(("arbitrary",)),
        name="fox_attn_sample",
    )(q_b, k_b, v_b, og, lf_new_t, cache_k, cache_v, cache_lf_t, su, bt)


def _ret_proj_kernel(x_ref, gn_ref, w_ref, inv_ref, q_ref, kt_ref, v_ref, g_ref,
                     *, tm, n_sample, pos0, past, n_new, n_heads, qk_dim, v_width):
    i = pl.program_id(0)
    h = _rmsnorm_rows(x_ref[...], gn_ref[...]).astype(BF16)
    row = i * tm + lax.broadcasted_iota(jnp.int32, (tm, 1), 0)
    pos = jnp.where(row < n_sample, past + lax.rem(row, n_new), row - pos0)
    ang = pos.astype(F32) * inv_ref[...]
    cos = jnp.cos(ang)
    sin = jnp.sin(ang)
    half = qk_dim // 2
    qk_width = n_heads * qk_dim

    def rotary(z):
        parts = []
        for hh in range(n_heads):
            z1 = z[:, hh * qk_dim:hh * qk_dim + half]
            z2 = z[:, hh * qk_dim + half:(hh + 1) * qk_dim]
            parts += [z1 * cos - z2 * sin, z1 * sin + z2 * cos]
        return jnp.concatenate(parts, axis=1)

    q = rotary(_dot(h, w_ref[:, 0:qk_width])) * (qk_dim ** -0.5)
    q_ref[...] = q.astype(BF16)
    kt_ref[...] = rotary(_dot(h, w_ref[:, qk_width:2 * qk_width])).T
    v_ref[...] = _dot(h, w_ref[:, 2 * qk_width:2 * qk_width + v_width]).astype(BF16)
    g_ref[...] = _dot(h, w_ref[:, 2 * qk_width + v_width:])


def _ret_proj(x, gn, w_in, inv_freq, *, n_sample, pos0, past, n_new, n_heads, qk_dim, v_width):
    t_all, d = x.shape
    tm = PROJ_TILE
    qk_width = n_heads * qk_dim
    row = lambda n: pl.BlockSpec((tm, n), lambda i: (i, 0))
    sds = jax.ShapeDtypeStruct
    return pl.pallas_call(
        functools.partial(_ret_proj_kernel, tm=tm, n_sample=n_sample, pos0=pos0, past=past, n_new=n_new,
                          n_heads=n_heads, qk_dim=qk_dim, v_width=v_width),
        grid=(t_all // tm,),
        in_specs=[row(d), _const_spec((1, d)), _const_spec(w_in.shape), _const_spec(inv_freq.shape)],
        out_specs=[row(qk_width), pl.BlockSpec((qk_width, tm), lambda i: (0, i)), row(v_width), row(v_width)],
        out_shape=[sds((t_all, qk_width), BF16), sds((qk_width, t_all), F32),
                   sds((t_all, v_width), BF16), sds((t_all, v_width), F32)],
        compiler_params=_params(("arbitrary",)),
        name="ret_proj",
    )(x, gn, w_in, inv_freq)


def _group_norm_gate(o, g, gn):
    mu = jnp.mean(o, axis=-1, keepdims=True)
    var = jnp.mean(jnp.square(o - mu), axis=-1, keepdims=True)
    y = (o - mu) * lax.rsqrt(var + EPS) * gn
    return (g * jax.nn.sigmoid(g) * y).astype(BF16)


def _ret_prompt_kernel(logg_ref, q_ref, kt_ref, v_ref, g_ref, gn_ref, y_ref, st_ref, s_ref, dmat_ref,
                       *, tb, front):
    hh = pl.program_id(0)
    blk = pl.program_id(1)
    lg = jnp.full((1, 1), logg_ref[hh], F32)

    @pl.when(blk == 0)
    def _():
        s_ref[...] = jnp.zeros_like(s_ref)
        r = lax.broadcasted_iota(jnp.int32, (tb, tb), 0)
        c = lax.broadcasted_iota(jnp.int32, (tb, tb), 1)
        decay = jnp.exp(lg * jnp.abs(r - c).astype(F32))
        dmat_ref[...] = jnp.where(c // CHUNK <= r // CHUNK, decay, 0.0)

    q = q_ref[...]
    key_row = blk * tb + lax.broadcasted_iota(jnp.int32, (1, tb), 1)
    kt = jnp.where(key_row >= front, kt_ref[...], 0.0)
    v = v_ref[...]
    s = _dot(q, kt.astype(BF16)) * dmat_ref[...]
    state = s_ref[...]
    q_dec = jnp.exp(lg * (lax.broadcasted_iota(jnp.int32, (tb, 1), 0) + 1).astype(F32))
    o = _dot(s.astype(BF16), v) + _dot(q, state.astype(BF16)) * q_dec
    k_dec = jnp.exp(lg * (tb - 1 - lax.broadcasted_iota(jnp.int32, (1, tb), 1)).astype(F32))
    new_state = state * jnp.exp(lg * tb) + _dot((kt * k_dec).astype(BF16), v)
    s_ref[...] = new_state
    y_ref[...] = _group_norm_gate(o, g_ref[...], gn_ref[...])

    @pl.when(blk == pl.num_programs(1) - 1)
    def _():
        st_ref[0] = new_state


def _ret_prompt(logg, q_b, k_t, v_b, g, gn, *, front, t_prompt, n_heads):
    t_all, qk_width = q_b.shape
    v_width = v_b.shape[1]
    qk_dim, v_dim = qk_width // n_heads, v_width // n_heads
    tb = RET_BLOCK
    return pl.pallas_call(
        functools.partial(_ret_prompt_kernel, tb=tb, front=front),
        grid=(n_heads, t_prompt // tb),
        in_specs=[pl.BlockSpec(memory_space=pltpu.SMEM),
                  pl.BlockSpec((tb, qk_dim), lambda h, i: (i, h)),
                  pl.BlockSpec((qk_dim, tb), lambda h, i: (h, i)),
                  pl.BlockSpec((tb, v_dim), lambda h, i: (i, h)),
                  pl.BlockSpec((tb, v_dim), lambda h, i: (i, h)),
                  pl.BlockSpec((1, v_dim), lambda h, i: (0, h))],
        out_specs=[pl.BlockSpec((tb, v_dim), lambda h, i: (i, h)),
                   pl.BlockSpec((1, qk_dim, v_dim), lambda h, i: (h, 0, 0))],
        out_shape=[jax.ShapeDtypeStruct((t_prompt, v_width), BF16),
                   jax.ShapeDtypeStruct((n_heads, qk_dim, v_dim), F32)],
        scratch_shapes=[pltpu.VMEM((qk_dim, v_dim), F32), pltpu.VMEM((tb, tb), F32)],
        compiler_params=_params(("arbitrary", "arbitrary")),
        name="ret_prompt",
    )(logg, q_b, k_t, v_b, g, gn)


def _ret_sample_kernel(logg_ref, q_ref, kt_ref, v_ref, g_ref, gn_ref, st_ref, y_ref, nst_ref, *, n_new):
    hh = pl.program_id(0)
    b = pl.program_id(1)
    lg = jnp.full((1, 1), logg_ref[hh], F32)
    n_rows = kt_ref.shape[1]
    r0 = pl.multiple_of(b * n_new, n_new)
    q = q_ref[pl.ds(r0, n_new), :]
    kt = kt_ref[...]
    v = v_ref[...]
    qrow = lax.broadcasted_iota(jnp.int32, (n_new, n_rows), 0)
    col = lax.broadcasted_iota(jnp.int32, (n_new, n_rows), 1) - b * n_new
    own = jnp.logical_and(col >= 0, col < n_new)
    dmat = jnp.where(own, jnp.exp(lg * jnp.abs(qrow - col).astype(F32)), 0.0)
    s = _dot(q, kt.astype(BF16)) * dmat
    state = st_ref[0, 0]
    q_dec = jnp.exp(lg * (lax.broadcasted_iota(jnp.int32, (n_new, 1), 0) + 1).astype(F32))
    o = _dot(s.astype(BF16), v) + _dot(q, state.astype(BF16)) * q_dec
    col1 = lax.broadcasted_iota(jnp.int32, (1, n_rows), 1) - b * n_new
    own1 = jnp.logical_and(col1 >= 0, col1 < n_new)
    k_dec = jnp.where(own1, jnp.exp(lg * (n_new - 1 - col1).astype(F32)), 0.0)
    nst_ref[0, 0] = state * jnp.exp(lg * n_new) + _dot((kt * k_dec).astype(BF16), v)
    y_ref[pl.ds(r0, n_new), :] = _group_norm_gate(o, g_ref[pl.ds(r0, n_new), :], gn_ref[...])


def _ret_sample(logg, q_b, k_t, v_b, g, gn, state, *, n_new):
    t_all, qk_width = q_b.shape
    v_width = v_b.shape[1]
    n_streams, n_heads, qk_dim, v_dim = state.shape
    n_rows = n_streams * n_new
    blk = 0
    return pl.pallas_call(
        functools.partial(_ret_sample_kernel, n_new=n_new),
        grid=(n_heads, n_streams),
        in_specs=[pl.BlockSpec(memory_space=pltpu.SMEM),
                  pl.BlockSpec((n_rows, qk_dim), lambda h, b: (blk, h)),
                  pl.BlockSpec((qk_dim, n_rows), lambda h, b: (h, blk)),
                  pl.BlockSpec((n_rows, v_dim), lambda h, b: (blk, h)),
                  pl.BlockSpec((n_rows, v_dim), lambda h, b: (blk, h)),
                  pl.BlockSpec((1, v_dim), lambda h, b: (0, h)),
                  pl.BlockSpec((1, 1, qk_dim, v_dim), lambda h, b: (b, h, 0, 0))],
        out_specs=[pl.BlockSpec((n_rows, v_dim), lambda h, b: (0, h)),
                   pl.BlockSpec((1, 1, qk_dim, v_dim), lambda h, b: (b, h, 0, 0))],
        out_shape=[jax.ShapeDtypeStruct((n_rows, v_width), BF16),
                   jax.ShapeDtypeStruct(state.shape, F32)],
        compiler_params=_params(("arbitrary", "arbitrary")),
        name="ret_sample",
    )(logg, q_b, k_t, v_b, g, gn, state)


def _top_values(s, k, with_rank):
    work = s
    vals = []
    for r in range(k):
        m = jnp.max(work, axis=0, keepdims=True)
        vals.append(m)
        work = jnp.where(work == m, RANK_MARK * (1.0 + r / 1024.0), work)
    if not with_rank:
        return vals, None
    rank = jnp.where(work <= RANK_MARK, (work * (1.0 / RANK_MARK) - 1.0) * 1024.0, NO_RANK)
    return vals, rank


def _peer_kernel(x_ref, mp_ref, mt_ref, wo_ref, gn_ref, wq_ref, sk_ref, u_ref, vt_ref, o_ref,
                 xn_ref, hb_ref, e0_ref, cnt_ref, e1_ref, rank_ref, w_ref, y_ref,
                 *, n_heads, n_keys, topk):
    c = pl.program_id(1)
    ce, tm = u_ref.shape[0], x_ref.shape[0]
    rows_per_chunk = ce // n_keys
    sub = rank_ref.shape[2]

    @pl.when(c == 0)
    def _():
        n_sample = mt_ref.shape[0]
        mixed = mp_ref[...]
        head_rows = jnp.where(pl.program_id(0) == 0, mt_ref[...], mixed[:n_sample])
        mixed = jnp.concatenate([head_rows, mixed[n_sample:]], axis=0)
        xn = x_ref[...] + _dot(mixed, wo_ref[...])
        xn_ref[...] = xn
        hb = _rmsnorm_rows(xn, gn_ref[...]).astype(BF16)
        hb_ref[...] = hb
        y_ref[...] = jnp.zeros_like(y_ref)
        for h in range(n_heads):
            qh = _dot_nt(wq_ref[h * 2 * n_keys:(h + 1) * 2 * n_keys, :], hb)
            s0 = _dot(sk_ref[2 * h], qh[0:n_keys].astype(BF16))
            s1 = _dot(sk_ref[2 * h + 1], qh[n_keys:2 * n_keys].astype(BF16))
            a, rank0 = _top_values(s0, topk, True)
            b, rank1 = _top_values(s1, topk, True)
            half = topk // 2
            b_all = jnp.concatenate(b, axis=0)
            b_half = jnp.concatenate(b[:half], axis=0)
            cand = jnp.concatenate(
                [a[0] + b_all] + [a[i] + b_half for i in range(1, half)] + [jnp.concatenate(a[half:], axis=0) + b[0]],
                axis=0)
            f, _ = _top_values(cand, topk, False)
            tau = f[topk - 1]
            z = sum(jnp.exp(fk - f[0]) for fk in f)
            a_all = jnp.concatenate(a, axis=0)
            cnt_sorted = sum(jnp.where(a_all + bj >= tau, 1.0, 0.0) for bj in b)
            cnt = jnp.zeros_like(s0)
            for r in range(topk):
                cnt = jnp.where(rank0 == float(r), cnt_sorted[r:r + 1, :], cnt)
            cnt_ref[h] = cnt
            e0_ref[h] = jnp.exp(s0 - a[0])
            e1_ref[h] = (jnp.exp(s1 - b[0]) * (0.5 / z)).astype(BF16).reshape(n_keys // sub, sub, tm)
            rank_ref[h] = rank1.astype(BF16).reshape(n_keys // sub, sub, tm)

    hb = hb_ref[...]
    edges = PEER_ACT_EDGES + (ce,)
    act_of_row = {}
    for lo_row, hi_row in zip(edges[:-1], edges[1:]):
        piece = _dot_nt(u_ref[lo_row:hi_row, :], hb)
        for r in range(lo_row, hi_row, n_keys):
            act_of_row[r] = (piece, r - lo_row)
    rows_per_slab = MXU_DIM // n_keys
    y_add = None
    for slab in range(ce // MXU_DIM):
        for il in range(slab * rows_per_slab, (slab + 1) * rows_per_slab):
            i = c * rows_per_chunk + il
            gate = None
            for h in range(n_heads):
                n_sel = jnp.broadcast_to(cnt_ref[h, pl.ds(i, 1), :], (sub, tm)).astype(BF16)
                e0 = jnp.broadcast_to(e0_ref[h, pl.ds(i, 1), :], (sub, tm)).astype(BF16)
                term = jnp.where(rank_ref[h] < n_sel[None], e1_ref[h], jnp.zeros((), BF16)) * e0[None]
                gate = term if gate is None else gate + term
            r0 = il * n_keys
            piece, off = act_of_row[r0]
            x_act = piece[off:off + n_keys, :]
            gelu2 = x_act * (1.0 + lax.erf(x_act * (2.0 ** -0.5)))
            w_ref[r0:r0 + n_keys, :] = gate.reshape(n_keys, tm) * gelu2.astype(BF16)
        rows = slice(slab * MXU_DIM, (slab + 1) * MXU_DIM)
        part = _dot(vt_ref[:, rows], w_ref[rows, :])
        y_add = part if y_add is None else y_add + part
    y_ref[...] += y_add

    @pl.when(c == pl.num_programs(1) - 1)
    def _():
        o_ref[...] = xn_ref[...] + y_ref[...].T


def _peer(x, mix_prompt, mix_sample, w_out, gn, wq_t, subkeys, u, v_t, *, layer, n_heads, n_keys):
    t_all, d = x.shape
    k = mix_prompt.shape[1]
    n_sample = mix_sample.shape[0]
    n_experts = u.shape[1]
    tm = ROW_TILE
    ce = PEER_EXPERT_CHUNK
    assert mix_prompt.shape == (t_all, k) and mix_sample.shape == (n_sample, k) and n_sample <= tm
    head_scratch = pltpu.VMEM((n_heads, n_keys, tm), F32)
    head_scratch_b = pltpu.VMEM((n_heads, n_keys // BF16_TILE_ROWS, BF16_TILE_ROWS, tm), BF16)
    return pl.pallas_call(
        functools.partial(_peer_kernel, n_heads=n_heads, n_keys=n_keys, topk=PEER_TOPK),
        grid=(t_all // tm, n_experts // ce),
        in_specs=[pl.BlockSpec((tm, d), lambda i, c: (i, 0)),
                  pl.BlockSpec((tm, k), lambda i, c: (i, 0)),
                  _const_spec((n_sample, k)), _const_spec(w_out.shape),
                  _const_spec((1, d)), _const_spec(wq_t.shape),
                  _const_spec(subkeys.shape),
                  pl.BlockSpec((None, ce, d), lambda i, c: (layer, c, 0)),
                  pl.BlockSpec((None, d, ce), lambda i, c: (layer, 0, c))],
        out_specs=pl.BlockSpec((tm, d), lambda i, c: (i, 0)),
        out_shape=jax.ShapeDtypeStruct((t_all, d), F32),
        scratch_shapes=[pltpu.VMEM((tm, d), F32), pltpu.VMEM((tm, d), BF16),
                        head_scratch, head_scratch, head_scratch_b, head_scratch_b,
                        pltpu.VMEM((ce, tm), BF16), pltpu.VMEM((d, tm), F32)],
        compiler_params=_params(("arbitrary", "arbitrary")),
        name="peer",
    )(x, mix_prompt, mix_sample, w_out, gn, wq_t, subkeys, u, v_t)


def kernel(x_prompt, x_sample, cache_fox_k, cache_fox_v, cache_fox_lf, state_ret, meta_tokens, norm_mix, norm_ffn,
           fox_w_in, fox_b_f, fox_q_norm, fox_k_norm, fox_w_out, ret_w_in, ret_gn, ret_w_out,
           peer_w_q, peer_subkeys, peer_u, peer_v):
    batch, seq, d = x_prompt.shape
    n_streams, n_new, _ = x_sample.shape
    n_meta = meta_tokens.shape[0]
    assert batch == 1 and n_meta == N_META
    depth = norm_mix.shape[0]
    n_fox, _, past, fox_heads, fox_hd = cache_fox_k.shape
    fox_width = fox_heads * fox_hd
    n_ret, _, ret_heads, ret_qk, ret_v = state_ret.shape
    ret_v_width = ret_heads * ret_v
    peer_heads, _, n_keys, peer_half = peer_subkeys.shape[1:]
    assert n_keys == LANES and peer_half == LANES and fox_width == d

    length = n_meta + seq
    n_rows = n_streams * n_new
    front = (-length) % ROW_TILE
    if front < n_rows:
        front += ROW_TILE
    t_all = front + length

    x = jnp.concatenate([x_sample.reshape(n_rows, d), jnp.zeros((front - n_rows, d), F32),
                         meta_tokens.astype(F32), x_prompt[0]], axis=0)

    half = ret_qk // 2
    inv_freq = (ROPE_BASE ** (-jnp.arange(half, dtype=F32) / half)).reshape(1, half)
    logg = jnp.log(1.0 - 2.0 ** (-5.0 - jnp.arange(ret_heads, dtype=F32)))

    peer_u_b = peer_u.astype(BF16)
    peer_vt_b = peer_v.astype(BF16).transpose(0, 2, 1)

    kp, vp, lfp, ks, vs, lfs, srp, srs = [], [], [], [], [], [], [], []
    for layer in range(depth):
        j = layer // 2
        gn_mix = norm_mix[layer].reshape(1, d)
        if layer % 2 == 0:
            w_in = fox_w_in[j]
            w_main = w_in[:, :4 * fox_width].astype(BF16)
            w_f = jnp.pad(w_in[:, 4 * fox_width:], ((0, 0), (0, LANES - fox_heads))).astype(BF16)
            b_f = jnp.pad(fox_b_f[j], (0, LANES - fox_heads)).reshape(1, LANES)
            gq = jnp.tile(fox_q_norm[j], fox_heads).reshape(1, fox_width)
            gk = jnp.tile(fox_k_norm[j], fox_heads).reshape(1, fox_width)
            q_b, k_f, k_b, v_f, v_b, og, lf, csum = _fox_proj(x, gn_mix, w_main, w_f, b_f, gq, gk,
                                                              front=front, head_dim=fox_hd)
            o_b = _fox_attn_prompt(q_b, k_b, v_b, csum, og, fox_q_norm[j], fox_k_norm[j],
                                   t_prompt=t_all, head_dim=fox_hd)
            lf_new_t = lf[:n_rows, :fox_heads].T
            o_s = _fox_attn_sample(q_b, k_b, v_b, og, lf_new_t,
                                   cache_fox_k[j].reshape(n_streams, past, fox_width),
                                   cache_fox_v[j].reshape(n_streams, past, fox_width),
                                   cache_fox_lf[j].transpose(0, 2, 1),
                                   n_new=n_new, head_dim=fox_hd)
            mix_prompt, mix_sample, w_out = o_b, o_s, fox_w_out[j].astype(BF16)
            kp.append(k_f)
            vp.append(v_f)
            lfp.append(lf[front:, :fox_heads].reshape(1, length, fox_heads))
            ks.append(k_f[:n_rows].reshape(n_streams, n_new, fox_heads, fox_hd))
            vs.append(v_f[:n_rows].reshape(n_streams, n_new, fox_heads, fox_hd))
            lfs.append(lf[:n_rows, :fox_heads].reshape(n_streams, n_new, fox_heads))
        else:
            q_b, k_t, v_b, g = _ret_proj(x, gn_mix, ret_w_in[j].astype(BF16), inv_freq,
                                         n_sample=n_rows, pos0=front + n_meta, past=past, n_new=n_new,
                                         n_heads=ret_heads, qk_dim=ret_qk, v_width=ret_v_width)
            gn_ret = ret_gn[j].reshape(1, ret_v_width)
            y_b, st_p = _ret_prompt(logg, q_b, k_t, v_b, g, gn_ret, front=front, t_prompt=t_all, n_heads=ret_heads)
            y_s, st_s = _ret_sample(logg, q_b, k_t, v_b, g, gn_ret, state_ret[j], n_new=n_new)
            mix_prompt, mix_sample, w_out = y_b, y_s, ret_w_out[j].astype(BF16)
            srp.append(st_p[None])
            srs.append(st_s)
        x = _peer(x, mix_prompt, mix_sample, w_out, norm_ffn[layer].reshape(1, d), peer_w_q[layer].T.astype(BF16),
                  peer_subkeys[layer].reshape(2 * peer_heads, n_keys, peer_half).astype(BF16),
                  peer_u_b, peer_vt_b, layer=layer, n_heads=peer_heads, n_keys=n_keys)

    y_prompt = x[front + n_meta:][None]
    y_sample = x[:n_rows].reshape(n_streams, n_new, d)
    prompt_cache = lambda a: jnp.stack(a)[:, front:].reshape(len(a), 1, length, fox_heads, fox_hd)
    return (y_prompt, y_sample, prompt_cache(kp), prompt_cache(vp), jnp.stack(lfp), jnp.stack(srp),
            jnp.stack(ks), jnp.stack(vs), jnp.stack(lfs), jnp.stack(srs))
```

```python
import functools
import math

import jax
import jax.numpy as jnp
from jax import lax
from jax.experimental import pallas as pl
from jax.experimental.pallas import tpu as pltpu

F32 = jnp.float32
BF16 = jnp.bfloat16

EPS = 1e-6
N_META = 16
CHUNK = 64
ROPE_BASE = 10000.0
PEER_TOPK = 16

LANES = 128
MXU_DIM = 256
BF16_TILE_ROWS = 16
VMEM_LIMIT = 56 * 1024 * 1024

ROW_TILE = 512
PROJ_TILE = 256
RET_BLOCK = 512
PEER_EXPERT_CHUNK = 2048
PEER_ACT_EDGES = (0, 1024)
MASKED_KEY = 1e30
NO_RANK = 99.0
RANK_MARK = -(2.0 ** 100)
LOG2E = math.log2(math.e)
ZERO_WEIGHT_LOG2 = 160.0

_NT = (((1,), (1,)), ((), ()))


def _const_spec(shape):
    return pl.BlockSpec(shape, lambda *_: (0,) * len(shape), pipeline_mode=pl.Buffered(1))


def _params(semantics):
    return pltpu.CompilerParams(dimension_semantics=semantics, vmem_limit_bytes=VMEM_LIMIT)


def _split3(x):
    a = x.astype(BF16)
    r = x - a.astype(F32)
    b = r.astype(BF16)
    c = (r - b.astype(F32)).astype(BF16)
    return a, b, c


def _dot(a, b):
    return jnp.dot(a, b, preferred_element_type=F32)


def _dot_nt(a, b):
    return lax.dot_general(a, b, _NT, preferred_element_type=F32)


def _rmsnorm_rows(x, g):
    ms = jnp.mean(x * x, axis=-1, keepdims=True)
    return x * lax.rsqrt(ms + EPS) * g


def _log_sigmoid(x):
    return jnp.minimum(x, 0.0) - jnp.log1p(jnp.exp(-jnp.abs(x)))


def _fox_proj_kernel(x_ref, gn_ref, w_ref, wf_ref, bf_ref, gq_ref, gk_ref, bd_ref, tril_ref,
                     q_ref, kf_ref, kb_ref, vf_ref, vb_ref, og_ref, lf_ref, c_ref, carry_ref,
                     *, tm, front, head_dim, width):
    i = pl.program_id(0)

    @pl.when(i == 0)
    def _():
        carry_ref[...] = jnp.zeros_like(carry_ref)

    h = _rmsnorm_rows(x_ref[...], gn_ref[...]).astype(BF16)

    def head_norm(z, g):
        zz = (z * z).astype(BF16)
        ss = jnp.concatenate(
            [_dot(zz[:, c * MXU_DIM:(c + 1) * MXU_DIM], bd_ref[...]) for c in range(width // MXU_DIM)], axis=1)
        return z * lax.rsqrt(ss * (1.0 / head_dim) + EPS) * g

    zq = _dot(h, w_ref[:, 0:width])
    q_ref[...] = (head_norm(zq, gq_ref[...]) * (head_dim ** -0.5 * LOG2E)).astype(BF16)
    zk = _dot(h, w_ref[:, width:2 * width])
    kn = head_norm(zk, gk_ref[...])
    kf_ref[...] = kn
    kb_ref[...] = kn.astype(BF16)
    zv = _dot(h, w_ref[:, 2 * width:3 * width])
    vf_ref[...] = zv
    vb_ref[...] = zv.astype(BF16)
    og_ref[...] = _dot(h, w_ref[:, 3 * width:4 * width])

    lf = _log_sigmoid(_dot(h, wf_ref[...]) + bf_ref[...])
    lf_ref[...] = lf
    row = i * tm + lax.broadcasted_iota(jnp.int32, (tm, 1), 0)
    real = row >= front
    l1, l2, l3 = _split3(jnp.where(real, lf, 0.0))
    tril = tril_ref[...]
    c = _dot(tril, l1) + _dot(tril, l2) + _dot(tril, l3) + carry_ref[...]
    carry_ref[...] = c[tm - 1:tm, :]
    c_ref[...] = jnp.where(real, c, MASKED_KEY)


def _fox_proj(x, gn, w_main, w_f, b_f, gq, gk, *, front, head_dim):
    t_all, d = x.shape
    width = w_main.shape[1] // 4
    tm = PROJ_TILE
    blk = MXU_DIM // head_dim
    bd = jnp.kron(jnp.eye(blk, dtype=F32), jnp.ones((head_dim, head_dim), F32)).astype(BF16)
    tril = (lax.broadcasted_iota(jnp.int32, (tm, tm), 0) >= lax.broadcasted_iota(jnp.int32, (tm, tm), 1)).astype(BF16)
    row = lambda n: pl.BlockSpec((tm, n), lambda i: (i, 0))
    sds = jax.ShapeDtypeStruct
    return pl.pallas_call(
        functools.partial(_fox_proj_kernel, tm=tm, front=front, head_dim=head_dim, width=width),
        grid=(t_all // tm,),
        in_specs=[row(d), _const_spec((1, d)), _const_spec(w_main.shape), _const_spec(w_f.shape),
                  _const_spec((1, LANES)), _const_spec((1, width)), _const_spec((1, width)),
                  _const_spec(bd.shape), _const_spec(tril.shape)],
        out_specs=[row(width), row(width), row(width), row(width), row(width), row(width), row(LANES), row(LANES)],
        out_shape=[sds((t_all, width), BF16), sds((t_all, width), F32), sds((t_all, width), BF16),
                   sds((t_all, width), F32), sds((t_all, width), BF16), sds((t_all, width), F32),
                   sds((t_all, LANES), F32), sds((t_all, LANES), F32)],
        scratch_shapes=[pltpu.VMEM((1, LANES), F32)],
        compiler_params=_params(("arbitrary",)),
        name="fox_proj",
    )(x, gn, w_main, w_f, b_f, gq, gk, bd, tril)


def _fox_attn_kernel(first_ref, q_ref, k_ref, v_ref, c_ref, og_ref, o_ref,
                     qm_ref, m_ref, alpha_ref, p_ref, acc_ref, pend_ref, *, tile, head_dim):
    qi = pl.program_id(1)
    first = tuple(first_ref[2 * pl.program_id(0) + hh, qi] for hh in range(2))
    lane = lax.broadcasted_iota(jnp.int32, (1, 2 * head_dim), 1)
    lo = lane < head_dim
    head_lanes = (lo, jnp.logical_not(lo))
    q = q_ref[...]
    for hh in range(2):
        qm_ref[hh] = jnp.where(head_lanes[hh], q, jnp.zeros_like(q))
    m_ref[...] = jnp.full(m_ref.shape, -jnp.inf, F32)
    acc_ref[...] = jnp.zeros(acc_ref.shape, F32)
    alpha_ref[1] = jnp.ones(alpha_ref.shape[1:], F32)
    p_ref[1] = jnp.zeros(p_ref.shape[1:], BF16)
    one = jnp.ones((), BF16)
    both = (0, 1)

    def scores(j, slot, heads, diagonal=False):
        k = k_ref[pl.ds(pl.multiple_of(j * tile, tile), tile), :]
        for hh in heads:
            c_last = c_ref[hh, pl.ds(qi, 1), :][:, tile - 1:tile]
            bias = (c_ref[hh, pl.ds(j, 1), :] - c_last) * LOG2E
            u = _dot_nt(qm_ref[hh], k) - bias
            if diagonal:
                r = lax.broadcasted_iota(jnp.int32, (tile, tile), 0)
                cidx = lax.broadcasted_iota(jnp.int32, (tile, tile), 1)
                u = jnp.where(cidx <= r, u, -jnp.inf)
            m_prev = m_ref[hh]
            m_new = jnp.maximum(m_prev, jnp.max(u, axis=1, keepdims=True))
            m_wide = jnp.concatenate([m_new] * (tile // LANES), axis=1)
            p_ref[slot, hh] = jnp.exp2(u - m_wide).astype(BF16)
            alpha_ref[slot, hh] = jnp.exp2(m_prev - m_new)
            m_ref[hh] = m_new

    def accumulate(j, slot, heads):
        v = v_ref[pl.ds(pl.multiple_of(j * tile, tile), tile), :]
        for hh in heads:
            v_aug = jnp.where(head_lanes[hh], v, one)
            acc_ref[hh] = alpha_ref[slot, hh] * acc_ref[hh] + _dot(p_ref[slot, hh], v_aug)

    def blocks(start, stop, pending, heads):
        n = stop - start
        odd = lax.rem(n, 2)

        @pl.when(odd == 1)
        def _():
            scores(start, 0, heads)
            accumulate(pending, 1, heads)
            accumulate(start, 0, heads)
            for hh in heads:
                alpha_ref[1, hh] = jnp.ones(alpha_ref.shape[2:], F32)
                p_ref[1, hh] = jnp.zeros(p_ref.shape[2:], BF16)

        def body(jj, pend):
            j0 = start + odd + 2 * jj
            scores(j0, 0, heads)
            accumulate(pend, 1, heads)
            scores(j0 + 1, 1, heads)
            accumulate(j0, 0, heads)
            return j0 + 1

        return lax.fori_loop(0, n // 2, body, pending)

    joint = jnp.maximum(first[0], first[1])
    pend_ref[0] = joint
    for hh in both:
        @pl.when(first[hh] < first[1 - hh])
        def _():
            pend_ref[0] = blocks(first[hh], joint, joint, (hh,))

    pending = blocks(joint, qi, pend_ref[0], both)
    scores(qi, 0, both, diagonal=True)
    accumulate(pending, 1, both)
    accumulate(qi, 0, both)

    outs = []
    for hh in range(2):
        a = acc_ref[hh]
        outs.append(a / pltpu.roll(a, head_dim, 1))
    o = jnp.where(lo, outs[0], outs[1])
    o_ref[...] = (o * jax.nn.sigmoid(og_ref[...])).astype(BF16)


def _first_needed_block(gq, gk, csum, *, n_heads, t_prompt, head_dim, tile):
    bf16_slack = 1.0 + 2.0 ** -7
    q_norm = math.sqrt(head_dim) * jnp.max(jnp.abs(gq)) * (head_dim ** -0.5 * LOG2E) * bf16_slack
    k_norm = math.sqrt(head_dim) * jnp.max(jnp.abs(gk)) * bf16_slack
    c = csum[:t_prompt, :n_heads]
    c_first, c_last = c[0::tile], c[tile - 1::tile]
    bound = 2.0 * q_norm * k_norm + LOG2E * (c_first[:, None, :] - c_last[None, :, :])
    blk = jnp.arange(c_first.shape[0])
    earlier = blk[None, :, None] < blk[:, None, None]
    skipped = jnp.sum(jnp.logical_and(bound < -ZERO_WEIGHT_LOG2, earlier), axis=1)
    return skipped.T.astype(jnp.int32)


def _fox_attn_prompt(q_b, k_b, v_b, csum, og, gq, gk, *, t_prompt, head_dim):
    t_all, width = q_b.shape
    tile = ROW_TILE
    pair = 2 * head_dim
    n_blk = t_prompt // tile
    n_heads = width // head_dim
    c3 = csum[:t_prompt, :n_heads].T.reshape(n_heads, n_blk, tile)
    first = _first_needed_block(gq, gk, csum, n_heads=n_heads, t_prompt=t_prompt, head_dim=head_dim, tile=tile)
    return pl.pallas_call(
        functools.partial(_fox_attn_kernel, tile=tile, head_dim=head_dim),
        grid=(width // pair, n_blk),
        in_specs=[pl.BlockSpec(memory_space=pltpu.SMEM),
                  pl.BlockSpec((tile, pair), lambda p, i: (i, p)),
                  pl.BlockSpec((t_prompt, pair), lambda p, i: (0, p)),
                  pl.BlockSpec((t_prompt, pair), lambda p, i: (0, p)),
                  pl.BlockSpec((2, n_blk, tile), lambda p, i: (p, 0, 0)),
                  pl.BlockSpec((tile, pair), lambda p, i: (i, p))],
        out_specs=pl.BlockSpec((tile, pair), lambda p, i: (i, p)),
        out_shape=jax.ShapeDtypeStruct((t_prompt, width), BF16),
        scratch_shapes=[pltpu.VMEM((2, tile, pair), BF16), pltpu.VMEM((2, tile, LANES), F32),
                        pltpu.VMEM((2, 2, tile, LANES), F32), pltpu.VMEM((2, 2, tile, tile), BF16),
                        pltpu.VMEM((2, tile, pair), F32), pltpu.SMEM((1,), jnp.int32)],
        compiler_params=_params(("arbitrary", "arbitrary")),
        name="fox_attn_prompt",
    )(first, q_b, k_b, v_b, c3, og)


def _fox_sample_kernel(q_ref, kn_ref, vn_ref, og_ref, lfn_ref, kc_ref, vc_ref, lfc_ref, su_ref, bt_ref,
                       o_ref, *, n_new, head_dim, n_heads):
    b = pl.program_id(0)
    pair = 2 * head_dim
    n_rows = kn_ref.shape[0]

    su = su_ref[...]
    suf = sum(_dot(t, su) for t in _split3(lfc_ref[0])) * LOG2E
    bt = bt_ref[...]
    pre = sum(_dot(t, bt) for t in _split3(lfn_ref[...])) * LOG2E

    r0 = pl.multiple_of(b * n_new, n_new)
    qrow = lax.broadcasted_iota(jnp.int32, (n_new, n_rows), 0)
    col = lax.broadcasted_iota(jnp.int32, (n_new, n_rows), 1)
    visible = jnp.logical_and(col >= b * n_new, col <= b * n_new + qrow)
    lane = lax.broadcasted_iota(jnp.int32, (1, pair), 1)
    lo = lane < head_dim
    head_lanes = (lo, jnp.logical_not(lo))

    for p in range(n_heads // 2):
        cols = slice(p * pair, (p + 1) * pair)
        q = q_ref[pl.ds(r0, n_new), cols]
        k_c = kc_ref[0, :, cols].astype(BF16)
        v_c = vc_ref[0, :, cols].astype(BF16)
        k_n = kn_ref[:, cols]
        v_n = vn_ref[:, cols]
        outs = []
        for hh in range(2):
            head = 2 * p + hh
            qh = jnp.where(head_lanes[hh], q, jnp.zeros_like(q))
            u_c = _dot_nt(qh, k_c) + suf[head:head + 1, :]
            u_n = jnp.where(visible, _dot_nt(qh, k_n) - pre[head:head + 1, :], -jnp.inf)
            m = jnp.maximum(jnp.max(u_c, axis=1, keepdims=True), jnp.max(u_n, axis=1, keepdims=True))
            p_c = jnp.exp2(u_c - m)
            p_n = jnp.exp2(u_n - m)
            den = jnp.sum(p_c, axis=1, keepdims=True) + jnp.sum(p_n, axis=1, keepdims=True)
            outs.append((_dot(p_c.astype(BF16), v_c) + _dot(p_n.astype(BF16), v_n)) / den)
        o = jnp.where(lo, outs[0], outs[1])
        gate = jax.nn.sigmoid(og_ref[pl.ds(r0, n_new), cols])
        o_ref[pl.ds(r0, n_new), cols] = (o * gate).astype(BF16)


def _fox_attn_sample(q_b, k_b, v_b, og, lf_new_t, cache_k, cache_v, cache_lf_t, *, layer, n_new, head_dim):
    t_all, width = q_b.shape
    _, n_streams, past, _ = cache_k.shape
    n_heads = width // head_dim
    n_rows = n_streams * n_new
    assert n_rows % LANES == 0
    su = (lax.broadcasted_iota(jnp.int32, (past, past), 0) > lax.broadcasted_iota(jnp.int32, (past, past), 1)).astype(BF16)
    rr = lax.broadcasted_iota(jnp.int32, (n_rows, n_rows), 0)
    cc = lax.broadcasted_iota(jnp.int32, (n_rows, n_rows), 1)
    bt = jnp.logical_and(rr // n_new == cc // n_new, rr <= cc).astype(BF16)
    rows = pl.BlockSpec((n_rows, width), lambda b: (0, 0), pipeline_mode=pl.Buffered(1))
    return pl.pallas_call(
        functools.partial(_fox_sample_kernel, n_new=n_new, head_dim=head_dim, n_heads=n_heads),
        grid=(n_streams,),
        in_specs=[rows, rows, rows, rows,
                  _const_spec(lf_new_t.shape),
                  pl.BlockSpec((None, 1, past, width), lambda b: (layer, b, 0, 0)),
                  pl.BlockSpec((None, 1, past, width), lambda b: (layer, b, 0, 0)),
                  pl.BlockSpec((None, 1, n_heads, past), lambda b: (layer, b, 0, 0)),
                  _const_spec(su.shape), _const_spec(bt.shape)],
        out_specs=pl.BlockSpec((n_rows, width), lambda b: (0, 0)),
        out_shape=jax.ShapeDtypeStruct((n_rows, width), BF16),
        compiler_params=_params(("arbitrary",)),
        name="fox_attn_sample",
    )(q_b, k_b, v_b, og, lf_new_t, cache_k, cache_v, cache_lf_t, su, bt)


def _ret_proj_kernel(x_ref, gn_ref, w_ref, inv_ref, q_ref, kt_ref, v_ref, g_ref,
                     *, tm, n_sample, pos0, past, n_new, n_heads, qk_dim, v_width):
    i = pl.program_id(0)
    h = _rmsnorm_rows(x_ref[...], gn_ref[...]).astype(BF16)
    row = i * tm + lax.broadcasted_iota(jnp.int32, (tm, 1), 0)
    pos = jnp.where(row < n_sample, past + lax.rem(row, n_new), row - pos0)
    ang = pos.astype(F32) * inv_ref[...]
    cos = jnp.cos(ang)
    sin = jnp.sin(ang)
    half = qk_dim // 2
    qk_width = n_heads * qk_dim

    def rotary(z):
        parts = []
        for hh in range(n_heads):
            z1 = z[:, hh * qk_dim:hh * qk_dim + half]
            z2 = z[:, hh * qk_dim + half:(hh + 1) * qk_dim]
            parts += [z1 * cos - z2 * sin, z1 * sin + z2 * cos]
        return jnp.concatenate(parts, axis=1)

    q = rotary(_dot(h, w_ref[:, 0:qk_width])) * (qk_dim ** -0.5)
    q_ref[...] = q.astype(BF16)
    kt_ref[...] = rotary(_dot(h, w_ref[:, qk_width:2 * qk_width])).T
    v_ref[...] = _dot(h, w_ref[:, 2 * qk_width:2 * qk_width + v_width]).astype(BF16)
    g_ref[...] = _dot(h, w_ref[:, 2 * qk_width + v_width:])


def _ret_proj(x, gn, w_in, inv_freq, *, n_sample, pos0, past, n_new, n_heads, qk_dim, v_width):
    t_all, d = x.shape
    tm = PROJ_TILE
    qk_width = n_heads * qk_dim
    row = lambda n: pl.BlockSpec((tm, n), lambda i: (i, 0))
    sds = jax.ShapeDtypeStruct
    return pl.pallas_call(
        functools.partial(_ret_proj_kernel, tm=tm, n_sample=n_sample, pos0=pos0, past=past, n_new=n_new,
                          n_heads=n_heads, qk_dim=qk_dim, v_width=v_width),
        grid=(t_all // tm,),
        in_specs=[row(d), _const_spec((1, d)), _const_spec(w_in.shape), _const_spec(inv_freq.shape)],
        out_specs=[row(qk_width), pl.BlockSpec((qk_width, tm), lambda i: (0, i)), row(v_width), row(v_width)],
        out_shape=[sds((t_all, qk_width), BF16), sds((qk_width, t_all), F32),
                   sds((t_all, v_width), BF16), sds((t_all, v_width), F32)],
        compiler_params=_params(("arbitrary",)),
        name="ret_proj",
    )(x, gn, w_in, inv_freq)


def _group_norm_gate(o, g, gn):
    mu = jnp.mean(o, axis=-1, keepdims=True)
    var = jnp.mean(jnp.square(o - mu), axis=-1, keepdims=True)
    y = (o - mu) * lax.rsqrt(var + EPS) * gn
    return (g * jax.nn.sigmoid(g) * y).astype(BF16)


def _ret_prompt_kernel(logg_ref, q_ref, kt_ref, v_ref, g_ref, gn_ref, y_ref, st_ref, s_ref, dmat_ref,
                       *, tb, front):
    hh = pl.program_id(0)
    blk = pl.program_id(1)
    lg = jnp.full((1, 1), logg_ref[hh], F32)

    @pl.when(blk == 0)
    def _():
        s_ref[...] = jnp.zeros_like(s_ref)
        r = lax.broadcasted_iota(jnp.int32, (tb, tb), 0)
        c = lax.broadcasted_iota(jnp.int32, (tb, tb), 1)
        decay = jnp.exp(lg * jnp.abs(r - c).astype(F32))
        dmat_ref[...] = jnp.where(c // CHUNK <= r // CHUNK, decay, 0.0)

    q = q_ref[...]
    key_row = blk * tb + lax.broadcasted_iota(jnp.int32, (1, tb), 1)
    kt = jnp.where(key_row >= front, kt_ref[...], 0.0)
    v = v_ref[...]
    s = _dot(q, kt.astype(BF16)) * dmat_ref[...]
    state = s_ref[...]
    q_dec = jnp.exp(lg * (lax.broadcasted_iota(jnp.int32, (tb, 1), 0) + 1).astype(F32))
    o = _dot(s.astype(BF16), v) + _dot(q, state.astype(BF16)) * q_dec
    k_dec = jnp.exp(lg * (tb - 1 - lax.broadcasted_iota(jnp.int32, (1, tb), 1)).astype(F32))
    new_state = state * jnp.exp(lg * tb) + _dot((kt * k_dec).astype(BF16), v)
    s_ref[...] = new_state
    y_ref[...] = _group_norm_gate(o, g_ref[...], gn_ref[...])

    @pl.when(blk == pl.num_programs(1) - 1)
    def _():
        st_ref[0] = new_state


def _ret_prompt(logg, q_b, k_t, v_b, g, gn, *, front, t_prompt, n_heads):
    t_all, qk_width = q_b.shape
    v_width = v_b.shape[1]
    qk_dim, v_dim = qk_width // n_heads, v_width // n_heads
    tb = RET_BLOCK
    return pl.pallas_call(
        functools.partial(_ret_prompt_kernel, tb=tb, front=front),
        grid=(n_heads, t_prompt // tb),
        in_specs=[pl.BlockSpec(memory_space=pltpu.SMEM),
                  pl.BlockSpec((tb, qk_dim), lambda h, i: (i, h)),
                  pl.BlockSpec((qk_dim, tb), lambda h, i: (h, i)),
                  pl.BlockSpec((tb, v_dim), lambda h, i: (i, h)),
                  pl.BlockSpec((tb, v_dim), lambda h, i: (i, h)),
                  pl.BlockSpec((1, v_dim), lambda h, i: (0, h))],
        out_specs=[pl.BlockSpec((tb, v_dim), lambda h, i: (i, h)),
                   pl.BlockSpec((1, qk_dim, v_dim), lambda h, i: (h, 0, 0))],
        out_shape=[jax.ShapeDtypeStruct((t_prompt, v_width), BF16),
                   jax.ShapeDtypeStruct((n_heads, qk_dim, v_dim), F32)],
        scratch_shapes=[pltpu.VMEM((qk_dim, v_dim), F32), pltpu.VMEM((tb, tb), F32)],
        compiler_params=_params(("arbitrary", "arbitrary")),
        name="ret_prompt",
    )(logg, q_b, k_t, v_b, g, gn)


def _ret_sample_kernel(logg_ref, q_ref, kt_ref, v_ref, g_ref, gn_ref, st_ref, y_ref, nst_ref, *, n_new):
    hh = pl.program_id(0)
    b = pl.program_id(1)
    lg = jnp.full((1, 1), logg_ref[hh], F32)
    n_rows = kt_ref.shape[1]
    r0 = pl.multiple_of(b * n_new, n_new)
    q = q_ref[pl.ds(r0, n_new), :]
    kt = kt_ref[...]
    v = v_ref[...]
    qrow = lax.broadcasted_iota(jnp.int32, (n_new, n_rows), 0)
    col = lax.broadcasted_iota(jnp.int32, (n_new, n_rows), 1) - b * n_new
    own = jnp.logical_and(col >= 0, col < n_new)
    dmat = jnp.where(own, jnp.exp(lg * jnp.abs(qrow - col).astype(F32)), 0.0)
    s = _dot(q, kt.astype(BF16)) * dmat
    state = st_ref[0, 0]
    q_dec = jnp.exp(lg * (lax.broadcasted_iota(jnp.int32, (n_new, 1), 0) + 1).astype(F32))
    o = _dot(s.astype(BF16), v) + _dot(q, state.astype(BF16)) * q_dec
    col1 = lax.broadcasted_iota(jnp.int32, (1, n_rows), 1) - b * n_new
    own1 = jnp.logical_and(col1 >= 0, col1 < n_new)
    k_dec = jnp.where(own1, jnp.exp(lg * (n_new - 1 - col1).astype(F32)), 0.0)
    nst_ref[0, 0] = state * jnp.exp(lg * n_new) + _dot((kt * k_dec).astype(BF16), v)
    y_ref[pl.ds(r0, n_new), :] = _group_norm_gate(o, g_ref[pl.ds(r0, n_new), :], gn_ref[...])


def _ret_sample(logg, q_b, k_t, v_b, g, gn, state, *, n_new):
    t_all, qk_width = q_b.shape
    v_width = v_b.shape[1]
    n_streams, n_heads, qk_dim, v_dim = state.shape
    n_rows = n_streams * n_new
    blk = 0
    return pl.pallas_call(
        functools.partial(_ret_sample_kernel, n_new=n_new),
        grid=(n_heads, n_streams),
        in_specs=[pl.BlockSpec(memory_space=pltpu.SMEM),
                  pl.BlockSpec((n_rows, qk_dim), lambda h, b: (blk, h)),
                  pl.BlockSpec((qk_dim, n_rows), lambda h, b: (h, blk)),
                  pl.BlockSpec((n_rows, v_dim), lambda h, b: (blk, h)),
                  pl.BlockSpec((n_rows, v_dim), lambda h, b: (blk, h)),
                  pl.BlockSpec((1, v_dim), lambda h, b: (0, h)),
                  pl.BlockSpec((1, 1, qk_dim, v_dim), lambda h, b: (b, h, 0, 0))],
        out_specs=[pl.BlockSpec((n_rows, v_dim), lambda h, b: (0, h)),
                   pl.BlockSpec((1, 1, qk_dim, v_dim), lambda h, b: (b, h, 0, 0))],
        out_shape=[jax.ShapeDtypeStruct((n_rows, v_width), BF16),
                   jax.ShapeDtypeStruct(state.shape, F32)],
        compiler_params=_params(("arbitrary", "arbitrary")),
        name="ret_sample",
    )(logg, q_b, k_t, v_b, g, gn, state)


def _top_values(s, k, with_rank):
    work = s
    vals = []
    for r in range(k):
        m = jnp.max(work, axis=0, keepdims=True)
        vals.append(m)
        work = jnp.where(work == m, RANK_MARK * (1.0 + r / 1024.0), work)
    if not with_rank:
        return vals, None
    rank = jnp.where(work <= RANK_MARK, (work * (1.0 / RANK_MARK) - 1.0) * 1024.0, NO_RANK)
    return vals, rank


def _peer_kernel(x_ref, mp_ref, mt_ref, wo_ref, gn_ref, wq_ref, sk_ref, u_ref, vt_ref, o_ref,
                 xn_ref, hb_ref, e0_ref, cnt_ref, e1_ref, rank_ref, w_ref, y_ref,
                 *, n_heads, n_keys, topk):
    c = pl.program_id(1)
    ce, tm = u_ref.shape[0], x_ref.shape[0]
    rows_per_chunk = ce // n_keys
    sub = rank_ref.shape[2]

    @pl.when(c == 0)
    def _():
        n_sample = mt_ref.shape[0]
        mixed = mp_ref[...]
        head_rows = jnp.where(pl.program_id(0) == 0, mt_ref[...], mixed[:n_sample])
        mixed = jnp.concatenate([head_rows, mixed[n_sample:]], axis=0)
        xn = x_ref[...] + _dot(mixed, wo_ref[...])
        xn_ref[...] = xn
        hb = _rmsnorm_rows(xn, gn_ref[...]).astype(BF16)
        hb_ref[...] = hb
        y_ref[...] = jnp.zeros_like(y_ref)
        for h in range(n_heads):
            qh = _dot_nt(wq_ref[h * 2 * n_keys:(h + 1) * 2 * n_keys, :], hb)
            s0 = _dot(sk_ref[2 * h], qh[0:n_keys].astype(BF16))
            s1 = _dot(sk_ref[2 * h + 1], qh[n_keys:2 * n_keys].astype(BF16))
            a, rank0 = _top_values(s0, topk, True)
            b, rank1 = _top_values(s1, topk, True)
            half = topk // 2
            b_all = jnp.concatenate(b, axis=0)
            b_half = jnp.concatenate(b[:half], axis=0)
            cand = jnp.concatenate(
                [a[0] + b_all] + [a[i] + b_half for i in range(1, half)] + [jnp.concatenate(a[half:], axis=0) + b[0]],
                axis=0)
            f, _ = _top_values(cand, topk, False)
            tau = f[topk - 1]
            z = sum(jnp.exp(fk - f[0]) for fk in f)
            a_all = jnp.concatenate(a, axis=0)
            cnt_sorted = sum(jnp.where(a_all + bj >= tau, 1.0, 0.0) for bj in b)
            cnt = jnp.zeros_like(s0)
            for r in range(topk):
                cnt = jnp.where(rank0 == float(r), cnt_sorted[r:r + 1, :], cnt)
            cnt_ref[h] = cnt
            e0_ref[h] = jnp.exp(s0 - a[0])
            e1_ref[h] = (jnp.exp(s1 - b[0]) * (0.5 / z)).astype(BF16).reshape(n_keys // sub, sub, tm)
            rank_ref[h] = rank1.astype(BF16).reshape(n_keys // sub, sub, tm)

    hb = hb_ref[...]
    edges = PEER_ACT_EDGES + (ce,)
    act_of_row = {}
    for lo_row, hi_row in zip(edges[:-1], edges[1:]):
        piece = _dot_nt(u_ref[lo_row:hi_row, :], hb)
        for r in range(lo_row, hi_row, n_keys):
            act_of_row[r] = (piece, r - lo_row)
    rows_per_slab = MXU_DIM // n_keys
    y_add = None
    for slab in range(ce // MXU_DIM):
        for il in range(slab * rows_per_slab, (slab + 1) * rows_per_slab):
            i = c * rows_per_chunk + il
            gate = None
            for h in range(n_heads):
                n_sel = jnp.broadcast_to(cnt_ref[h, pl.ds(i, 1), :], (sub, tm)).astype(BF16)
                e0 = jnp.broadcast_to(e0_ref[h, pl.ds(i, 1), :], (sub, tm)).astype(BF16)
                term = jnp.where(rank_ref[h] < n_sel[None], e1_ref[h], jnp.zeros((), BF16)) * e0[None]
                gate = term if gate is None else gate + term
            r0 = il * n_keys
            piece, off = act_of_row[r0]
            x_act = piece[off:off + n_keys, :]
            gelu2 = x_act * (1.0 + lax.erf(x_act * (2.0 ** -0.5)))
            w_ref[r0:r0 + n_keys, :] = gate.reshape(n_keys, tm) * gelu2.astype(BF16)
        rows = slice(slab * MXU_DIM, (slab + 1) * MXU_DIM)
        part = _dot(vt_ref[:, rows], w_ref[rows, :])
        y_add = part if y_add is None else y_add + part
    y_ref[...] += y_add

    @pl.when(c == pl.num_programs(1) - 1)
    def _():
        o_ref[...] = xn_ref[...] + y_ref[...].T


def _peer(x, mix_prompt, mix_sample, w_out, gn, wq_t, subkeys, u, v_t, *, layer, n_heads, n_keys):
    t_all, d = x.shape
    k = mix_prompt.shape[1]
    n_sample = mix_sample.shape[0]
    n_experts = u.shape[1]
    tm = ROW_TILE
    ce = PEER_EXPERT_CHUNK
    assert mix_prompt.shape == (t_all, k) and mix_sample.shape == (n_sample, k) and n_sample <= tm
    head_scratch = pltpu.VMEM((n_heads, n_keys, tm), F32)
    head_scratch_b = pltpu.VMEM((n_heads, n_keys // BF16_TILE_ROWS, BF16_TILE_ROWS, tm), BF16)
    return pl.pallas_call(
        functools.partial(_peer_kernel, n_heads=n_heads, n_keys=n_keys, topk=PEER_TOPK),
        grid=(t_all // tm, n_experts // ce),
        in_specs=[pl.BlockSpec((tm, d), lambda i, c: (i, 0)),
                  pl.BlockSpec((tm, k), lambda i, c: (i, 0)),
                  _const_spec((n_sample, k)), _const_spec(w_out.shape),
                  _const_spec((1, d)), _const_spec(wq_t.shape),
                  _const_spec(subkeys.shape),
                  pl.BlockSpec((None, ce, d), lambda i, c: (layer, c, 0)),
                  pl.BlockSpec((None, d, ce), lambda i, c: (layer, 0, c))],
        out_specs=pl.BlockSpec((tm, d), lambda i, c: (i, 0)),
        out_shape=jax.ShapeDtypeStruct((t_all, d), F32),
        scratch_shapes=[pltpu.VMEM((tm, d), F32), pltpu.VMEM((tm, d), BF16),
                        head_scratch, head_scratch, head_scratch_b, head_scratch_b,
                        pltpu.VMEM((ce, tm), BF16), pltpu.VMEM((d, tm), F32)],
        compiler_params=_params(("arbitrary", "arbitrary")),
        name="peer",
    )(x, mix_prompt, mix_sample, w_out, gn, wq_t, subkeys, u, v_t)


def kernel(x_prompt, x_sample, cache_fox_k, cache_fox_v, cache_fox_lf, state_ret, meta_tokens, norm_mix, norm_ffn,
           fox_w_in, fox_b_f, fox_q_norm, fox_k_norm, fox_w_out, ret_w_in, ret_gn, ret_w_out,
           peer_w_q, peer_subkeys, peer_u, peer_v):
    batch, seq, d = x_prompt.shape
    n_streams, n_new, _ = x_sample.shape
    n_meta = meta_tokens.shape[0]
    assert batch == 1 and n_meta == N_META
    depth = norm_mix.shape[0]
    n_fox, _, past, fox_heads, fox_hd = cache_fox_k.shape
    fox_width = fox_heads * fox_hd
    n_ret, _, ret_heads, ret_qk, ret_v = state_ret.shape
    ret_v_width = ret_heads * ret_v
    peer_heads, _, n_keys, peer_half = peer_subkeys.shape[1:]
    assert n_keys == LANES and peer_half == LANES and fox_width == d

    length = n_meta + seq
    n_rows = n_streams * n_new
    front = (-length) % ROW_TILE
    if front < n_rows:
        front += ROW_TILE
    t_all = front + length

    x = jnp.concatenate([x_sample.reshape(n_rows, d), jnp.zeros((front - n_rows, d), F32),
                         meta_tokens.astype(F32), x_prompt[0]], axis=0)

    half = ret_qk // 2
    inv_freq = (ROPE_BASE ** (-jnp.arange(half, dtype=F32) / half)).reshape(1, half)
    logg = jnp.log(1.0 - 2.0 ** (-5.0 - jnp.arange(ret_heads, dtype=F32)))

    cache_k = cache_fox_k.reshape(n_fox, n_streams, past, fox_width)
    cache_v = cache_fox_v.reshape(n_fox, n_streams, past, fox_width)
    cache_lf_t = cache_fox_lf.transpose(0, 1, 3, 2)
    peer_u_b = peer_u.astype(BF16)
    peer_vt_b = peer_v.astype(BF16).transpose(0, 2, 1)

    kp, vp, lfp, ks, vs, lfs, srp, srs = [], [], [], [], [], [], [], []
    for layer in range(depth):
        j = layer // 2
        gn_mix = norm_mix[layer].reshape(1, d)
        if layer % 2 == 0:
            w_in = fox_w_in[j]
            w_main = w_in[:, :4 * fox_width].astype(BF16)
            w_f = jnp.pad(w_in[:, 4 * fox_width:], ((0, 0), (0, LANES - fox_heads))).astype(BF16)
            b_f = jnp.pad(fox_b_f[j], (0, LANES - fox_heads)).reshape(1, LANES)
            gq = jnp.tile(fox_q_norm[j], fox_heads).reshape(1, fox_width)
            gk = jnp.tile(fox_k_norm[j], fox_heads).reshape(1, fox_width)
            q_b, k_f, k_b, v_f, v_b, og, lf, csum = _fox_proj(x, gn_mix, w_main, w_f, b_f, gq, gk,
                                                              front=front, head_dim=fox_hd)
            o_b = _fox_attn_prompt(q_b, k_b, v_b, csum, og, fox_q_norm[j], fox_k_norm[j],
                                   t_prompt=t_all, head_dim=fox_hd)
            lf_new_t = lf[:n_rows, :fox_heads].T
            o_s = _fox_attn_sample(q_b, k_b, v_b, og, lf_new_t, cache_k, cache_v, cache_lf_t,
                                   layer=j, n_new=n_new, head_dim=fox_hd)
            mix_prompt, mix_sample, w_out = o_b, o_s, fox_w_out[j].astype(BF16)
            kp.append(k_f)
            vp.append(v_f)
            lfp.append(lf[front:, :fox_heads].reshape(1, length, fox_heads))
            ks.append(k_f[:n_rows].reshape(n_streams, n_new, fox_heads, fox_hd))
            vs.append(v_f[:n_rows].reshape(n_streams, n_new, fox_heads, fox_hd))
            lfs.append(lf[:n_rows, :fox_heads].reshape(n_streams, n_new, fox_heads))
        else:
            q_b, k_t, v_b, g = _ret_proj(x, gn_mix, ret_w_in[j].astype(BF16), inv_freq,
                                         n_sample=n_rows, pos0=front + n_meta, past=past, n_new=n_new,
                                         n_heads=ret_heads, qk_dim=ret_qk, v_width=ret_v_width)
            gn_ret = ret_gn[j].reshape(1, ret_v_width)
            y_b, st_p = _ret_prompt(logg, q_b, k_t, v_b, g, gn_ret, front=front, t_prompt=t_all, n_heads=ret_heads)
            y_s, st_s = _ret_sample(logg, q_b, k_t, v_b, g, gn_ret, state_ret[j], n_new=n_new)
            mix_prompt, mix_sample, w_out = y_b, y_s, ret_w_out[j].astype(BF16)
            srp.append(st_p[None])
            srs.append(st_s)
        x = _peer(x, mix_prompt, mix_sample, w_out, norm_ffn[layer].reshape(1, d), peer_w_q[layer].T.astype(BF16),
                  peer_subkeys[layer].reshape(2 * peer_heads, n_keys, peer_half).astype(BF16),
                  peer_u_b, peer_vt_b, layer=layer, n_heads=peer_heads, n_keys=n_keys)

    y_prompt = x[front + n_meta:][None]
    y_sample = x[:n_rows].reshape(n_streams, n_new, d)
    prompt_cache = lambda a: jnp.stack(a)[:, front:].reshape(len(a), 1, length, fox_heads, fox_hd)
    return (y_prompt, y_sample, prompt_cache(kp), prompt_cache(vp), jnp.stack(lfp), jnp.stack(srp),
            jnp.stack(ks), jnp.stack(vs), jnp.stack(lfs), jnp.stack(srs))
```

```python
import functools
import math

import jax
import jax.numpy as jnp
from jax import lax
from jax.experimental import pallas as pl
from jax.experimental.pallas import tpu as pltpu

F32 = jnp.float32
BF16 = jnp.bfloat16

EPS = 1e-6
N_META = 16
CHUNK = 64
ROPE_BASE = 10000.0
PEER_TOPK = 16

LANES = 128
MXU_DIM = 256
BF16_TILE_ROWS = 16
VMEM_LIMIT = 56 * 1024 * 1024

ROW_TILE = 512
PROJ_TILE = 256
RET_BLOCK = 512
PEER_EXPERT_CHUNK = 2048
PEER_ACT_EDGES = (0, 1024)
MASKED_KEY = 1e30
NO_RANK = 99.0
RANK_MARK = -(2.0 ** 100)
LOG2E = math.log2(math.e)
ZERO_WEIGHT_LOG2 = 160.0

_NT = (((1,), (1,)), ((), ()))


def _const_spec(shape):
    return pl.BlockSpec(shape, lambda *_: (0,) * len(shape), pipeline_mode=pl.Buffered(1))


def _params(semantics):
    return pltpu.CompilerParams(dimension_semantics=semantics, vmem_limit_bytes=VMEM_LIMIT)


def _split3(x):
    a = x.astype(BF16)
    r = x - a.astype(F32)
    b = r.astype(BF16)
    c = (r - b.astype(F32)).astype(BF16)
    return a, b, c


def _dot(a, b):
    return jnp.dot(a, b, preferred_element_type=F32)


def _dot_nt(a, b):
    return lax.dot_general(a, b, _NT, preferred_element_type=F32)


def _rmsnorm_rows(x, g):
    ms = jnp.mean(x * x, axis=-1, keepdims=True)
    return x * lax.rsqrt(ms + EPS) * g


def _log_sigmoid(x):
    return jnp.minimum(x, 0.0) - jnp.log1p(jnp.exp(-jnp.abs(x)))


def _fox_proj_kernel(x_ref, gn_ref, w_ref, wf_ref, bf_ref, gq_ref, gk_ref, bd_ref, tril_ref,
                     q_ref, kf_ref, kb_ref, vf_ref, vb_ref, og_ref, lf_ref, c_ref, carry_ref,
                     *, tm, front, head_dim, width):
    i = pl.program_id(0)

    @pl.when(i == 0)
    def _():
        carry_ref[...] = jnp.zeros_like(carry_ref)

    h = _rmsnorm_rows(x_ref[...], gn_ref[...]).astype(BF16)

    def head_norm(z, g):
        zz = (z * z).astype(BF16)
        ss = jnp.concatenate(
            [_dot(zz[:, c * MXU_DIM:(c + 1) * MXU_DIM], bd_ref[...]) for c in range(width // MXU_DIM)], axis=1)
        return z * lax.rsqrt(ss * (1.0 / head_dim) + EPS) * g

    zq = _dot(h, w_ref[:, 0:width])
    q_ref[...] = (head_norm(zq, gq_ref[...]) * (head_dim ** -0.5 * LOG2E)).astype(BF16)
    zk = _dot(h, w_ref[:, width:2 * width])
    kn = head_norm(zk, gk_ref[...])
    kf_ref[...] = kn
    kb_ref[...] = kn.astype(BF16)
    zv = _dot(h, w_ref[:, 2 * width:3 * width])
    vf_ref[...] = zv
    vb_ref[...] = zv.astype(BF16)
    og_ref[...] = _dot(h, w_ref[:, 3 * width:4 * width])

    lf = _log_sigmoid(_dot(h, wf_ref[...]) + bf_ref[...])
    lf_ref[...] = lf
    row = i * tm + lax.broadcasted_iota(jnp.int32, (tm, 1), 0)
    real = row >= front
    l1, l2, l3 = _split3(jnp.where(real, lf, 0.0))
    tril = tril_ref[...]
    c = _dot(tril, l1) + _dot(tril, l2) + _dot(tril, l3) + carry_ref[...]
    carry_ref[...] = c[tm - 1:tm, :]
    c_ref[...] = jnp.where(real, c, MASKED_KEY)


def _fox_proj(x, gn, w_main, w_f, b_f, gq, gk, *, front, head_dim):
    t_all, d = x.shape
    width = w_main.shape[1] // 4
    tm = PROJ_TILE
    blk = MXU_DIM // head_dim
    bd = jnp.kron(jnp.eye(blk, dtype=F32), jnp.ones((head_dim, head_dim), F32)).astype(BF16)
    tril = (lax.broadcasted_iota(jnp.int32, (tm, tm), 0) >= lax.broadcasted_iota(jnp.int32, (tm, tm), 1)).astype(BF16)
    row = lambda n: pl.BlockSpec((tm, n), lambda i: (i, 0))
    sds = jax.ShapeDtypeStruct
    return pl.pallas_call(
        functools.partial(_fox_proj_kernel, tm=tm, front=front, head_dim=head_dim, width=width),
        grid=(t_all // tm,),
        in_specs=[row(d), _const_spec((1, d)), _const_spec(w_main.shape), _const_spec(w_f.shape),
                  _const_spec((1, LANES)), _const_spec((1, width)), _const_spec((1, width)),
                  _const_spec(bd.shape), _const_spec(tril.shape)],
        out_specs=[row(width), row(width), row(width), row(width), row(width), row(width), row(LANES), row(LANES)],
        out_shape=[sds((t_all, width), BF16), sds((t_all, width), F32), sds((t_all, width), BF16),
                   sds((t_all, width), F32), sds((t_all, width), BF16), sds((t_all, width), F32),
                   sds((t_all, LANES), F32), sds((t_all, LANES), F32)],
        scratch_shapes=[pltpu.VMEM((1, LANES), F32)],
        compiler_params=_params(("arbitrary",)),
        name="fox_proj",
    )(x, gn, w_main, w_f, b_f, gq, gk, bd, tril)


def _fox_attn_kernel(first_ref, q_ref, k_ref, v_ref, c_ref, og_ref, o_ref,
                     qm_ref, m_ref, alpha_ref, p_ref, acc_ref, pend_ref, *, tile, head_dim):
    qi = pl.program_id(1)
    first = tuple(first_ref[2 * pl.program_id(0) + hh, qi] for hh in range(2))
    lane = lax.broadcasted_iota(jnp.int32, (1, 2 * head_dim), 1)
    lo = lane < head_dim
    head_lanes = (lo, jnp.logical_not(lo))
    q = q_ref[...]
    for hh in range(2):
        qm_ref[hh] = jnp.where(head_lanes[hh], q, jnp.zeros_like(q))
    m_ref[...] = jnp.full(m_ref.shape, -jnp.inf, F32)
    acc_ref[...] = jnp.zeros(acc_ref.shape, F32)
    alpha_ref[1] = jnp.ones(alpha_ref.shape[1:], F32)
    p_ref[1] = jnp.zeros(p_ref.shape[1:], BF16)
    one = jnp.ones((), BF16)
    both = (0, 1)

    def scores(j, slot, heads, diagonal=False):
        k = k_ref[pl.ds(pl.multiple_of(j * tile, tile), tile), :]
        for hh in heads:
            c_last = c_ref[hh, pl.ds(qi, 1), :][:, tile - 1:tile]
            bias = (c_ref[hh, pl.ds(j, 1), :] - c_last) * LOG2E
            u = _dot_nt(qm_ref[hh], k) - bias
            if diagonal:
                r = lax.broadcasted_iota(jnp.int32, (tile, tile), 0)
                cidx = lax.broadcasted_iota(jnp.int32, (tile, tile), 1)
                u = jnp.where(cidx <= r, u, -jnp.inf)
            m_prev = m_ref[hh]
            m_new = jnp.maximum(m_prev, jnp.max(u, axis=1, keepdims=True))
            m_wide = jnp.concatenate([m_new] * (tile // LANES), axis=1)
            p_ref[slot, hh] = jnp.exp2(u - m_wide).astype(BF16)
            alpha_ref[slot, hh] = jnp.exp2(m_prev - m_new)
            m_ref[hh] = m_new

    def accumulate(j, slot, heads):
        v = v_ref[pl.ds(pl.multiple_of(j * tile, tile), tile), :]
        for hh in heads:
            v_aug = jnp.where(head_lanes[hh], v, one)
            acc_ref[hh] = alpha_ref[slot, hh] * acc_ref[hh] + _dot(p_ref[slot, hh], v_aug)

    def block_pairs(start, n_pairs, pending, heads):
        def body(jj, pend):
            j0 = start + 2 * jj
            scores(j0, 0, heads)
            accumulate(pend, 1, heads)
            scores(j0 + 1, 1, heads)
            accumulate(j0, 0, heads)
            return j0 + 1

        return lax.fori_loop(0, n_pairs, body, pending)

    def blocks(start, stop, pending, heads):
        n = stop - start
        odd = lax.rem(n, 2)

        @pl.when(odd == 1)
        def _():
            scores(start, 0, heads)
            accumulate(pending, 1, heads)
            accumulate(start, 0, heads)
            for hh in heads:
                alpha_ref[1, hh] = jnp.ones(alpha_ref.shape[2:], F32)
                p_ref[1, hh] = jnp.zeros(p_ref.shape[2:], BF16)

        return block_pairs(start + odd, n // 2, pending, heads)

    joint = jnp.maximum(first[0], first[1])
    pend_ref[0] = joint
    for hh in both:
        @pl.when(first[hh] < first[1 - hh])
        def _():
            pend_ref[0] = blocks(first[hh], joint, joint, (hh,))

    n_joint = qi - joint
    pending = block_pairs(joint, n_joint // 2, pend_ref[0], both)

    @pl.when(lax.rem(n_joint, 2) == 0)
    def _():
        scores(qi, 0, both, diagonal=True)
        accumulate(pending, 1, both)
        accumulate(qi, 0, both)

    @pl.when(lax.rem(n_joint, 2) == 1)
    def _():
        scores(qi - 1, 0, both)
        accumulate(pending, 1, both)
        scores(qi, 1, both, diagonal=True)
        accumulate(qi - 1, 0, both)
        accumulate(qi, 1, both)

    outs = []
    for hh in range(2):
        a = acc_ref[hh]
        outs.append(a / pltpu.roll(a, head_dim, 1))
    o = jnp.where(lo, outs[0], outs[1])
    o_ref[...] = (o * jax.nn.sigmoid(og_ref[...])).astype(BF16)


def _first_needed_block(gq, gk, csum, *, n_heads, t_prompt, head_dim, tile):
    bf16_slack = 1.0 + 2.0 ** -7
    q_norm = math.sqrt(head_dim) * jnp.max(jnp.abs(gq)) * (head_dim ** -0.5 * LOG2E) * bf16_slack
    k_norm = math.sqrt(head_dim) * jnp.max(jnp.abs(gk)) * bf16_slack
    c = csum[:t_prompt, :n_heads]
    c_first, c_last = c[0::tile], c[tile - 1::tile]
    bound = 2.0 * q_norm * k_norm + LOG2E * (c_first[:, None, :] - c_last[None, :, :])
    blk = jnp.arange(c_first.shape[0])
    earlier = blk[None, :, None] < blk[:, None, None]
    skipped = jnp.sum(jnp.logical_and(bound < -ZERO_WEIGHT_LOG2, earlier), axis=1)
    return skipped.T.astype(jnp.int32)


def _fox_attn_prompt(q_b, k_b, v_b, csum, og, gq, gk, *, t_prompt, head_dim):
    t_all, width = q_b.shape
    tile = ROW_TILE
    pair = 2 * head_dim
    n_blk = t_prompt // tile
    n_heads = width // head_dim
    c3 = csum[:t_prompt, :n_heads].T.reshape(n_heads, n_blk, tile)
    first = _first_needed_block(gq, gk, csum, n_heads=n_heads, t_prompt=t_prompt, head_dim=head_dim, tile=tile)
    return pl.pallas_call(
        functools.partial(_fox_attn_kernel, tile=tile, head_dim=head_dim),
        grid=(width // pair, n_blk),
        in_specs=[pl.BlockSpec(memory_space=pltpu.SMEM),
                  pl.BlockSpec((tile, pair), lambda p, i: (i, p)),
                  pl.BlockSpec((t_prompt, pair), lambda p, i: (0, p)),
                  pl.BlockSpec((t_prompt, pair), lambda p, i: (0, p)),
                  pl.BlockSpec((2, n_blk, tile), lambda p, i: (p, 0, 0)),
                  pl.BlockSpec((tile, pair), lambda p, i: (i, p))],
        out_specs=pl.BlockSpec((tile, pair), lambda p, i: (i, p)),
        out_shape=jax.ShapeDtypeStruct((t_prompt, width), BF16),
        scratch_shapes=[pltpu.VMEM((2, tile, pair), BF16), pltpu.VMEM((2, tile, LANES), F32),
                        pltpu.VMEM((2, 2, tile, LANES), F32), pltpu.VMEM((2, 2, tile, tile), BF16),
                        pltpu.VMEM((2, tile, pair), F32), pltpu.SMEM((1,), jnp.int32)],
        compiler_params=_params(("arbitrary", "arbitrary")),
        name="fox_attn_prompt",
    )(first, q_b, k_b, v_b, c3, og)


def _fox_sample_kernel(q_ref, kn_ref, vn_ref, og_ref, lfn_ref, kc_ref, vc_ref, lfc_ref, su_ref, bt_ref,
                       o_ref, *, n_new, head_dim, n_heads):
    b = pl.program_id(0)
    pair = 2 * head_dim
    n_rows = kn_ref.shape[0]

    su = su_ref[...]
    suf = sum(_dot(t, su) for t in _split3(lfc_ref[0])) * LOG2E
    bt = bt_ref[...]
    pre = sum(_dot(t, bt) for t in _split3(lfn_ref[...])) * LOG2E

    r0 = pl.multiple_of(b * n_new, n_new)
    qrow = lax.broadcasted_iota(jnp.int32, (n_new, n_rows), 0)
    col = lax.broadcasted_iota(jnp.int32, (n_new, n_rows), 1)
    visible = jnp.logical_and(col >= b * n_new, col <= b * n_new + qrow)
    lane = lax.broadcasted_iota(jnp.int32, (1, pair), 1)
    lo = lane < head_dim
    head_lanes = (lo, jnp.logical_not(lo))

    for p in range(n_heads // 2):
        cols = slice(p * pair, (p + 1) * pair)
        q = q_ref[pl.ds(r0, n_new), cols]
        k_c = kc_ref[0, :, cols].astype(BF16)
        v_c = vc_ref[0, :, cols].astype(BF16)
        k_n = kn_ref[:, cols]
        v_n = vn_ref[:, cols]
        outs = []
        for hh in range(2):
            head = 2 * p + hh
            qh = jnp.where(head_lanes[hh], q, jnp.zeros_like(q))
            u_c = _dot_nt(qh, k_c) + suf[head:head + 1, :]
            u_n = jnp.where(visible, _dot_nt(qh, k_n) - pre[head:head + 1, :], -jnp.inf)
            m = jnp.maximum(jnp.max(u_c, axis=1, keepdims=True), jnp.max(u_n, axis=1, keepdims=True))
            p_c = jnp.exp2(u_c - m)
            p_n = jnp.exp2(u_n - m)
            den = jnp.sum(p_c, axis=1, keepdims=True) + jnp.sum(p_n, axis=1, keepdims=True)
            outs.append((_dot(p_c.astype(BF16), v_c) + _dot(p_n.astype(BF16), v_n)) / den)
        o = jnp.where(lo, outs[0], outs[1])
        gate = jax.nn.sigmoid(og_ref[pl.ds(r0, n_new), cols])
        o_ref[pl.ds(r0, n_new), cols] = (o * gate).astype(BF16)


def _fox_attn_sample(q_b, k_b, v_b, og, lf_new_t, cache_k, cache_v, cache_lf_t, *, layer, n_new, head_dim):
    t_all, width = q_b.shape
    _, n_streams, past, _ = cache_k.shape
    n_heads = width // head_dim
    n_rows = n_streams * n_new
    assert n_rows % LANES == 0
    su = (lax.broadcasted_iota(jnp.int32, (past, past), 0) > lax.broadcasted_iota(jnp.int32, (past, past), 1)).astype(BF16)
    rr = lax.broadcasted_iota(jnp.int32, (n_rows, n_rows), 0)
    cc = lax.broadcasted_iota(jnp.int32, (n_rows, n_rows), 1)
    bt = jnp.logical_and(rr // n_new == cc // n_new, rr <= cc).astype(BF16)
    rows = pl.BlockSpec((n_rows, width), lambda b: (0, 0), pipeline_mode=pl.Buffered(1))
    return pl.pallas_call(
        functools.partial(_fox_sample_kernel, n_new=n_new, head_dim=head_dim, n_heads=n_heads),
        grid=(n_streams,),
        in_specs=[rows, rows, rows, rows,
                  _const_spec(lf_new_t.shape),
                  pl.BlockSpec((None, 1, past, width), lambda b: (layer, b, 0, 0)),
                  pl.BlockSpec((None, 1, past, width), lambda b: (layer, b, 0, 0)),
                  pl.BlockSpec((None, 1, n_heads, past), lambda b: (layer, b, 0, 0)),
                  _const_spec(su.shape), _const_spec(bt.shape)],
        out_specs=pl.BlockSpec((n_rows, width), lambda b: (0, 0)),
        out_shape=jax.ShapeDtypeStruct((n_rows, width), BF16),
        compiler_params=_params(("arbitrary",)),
        name="fox_attn_sample",
    )(q_b, k_b, v_b, og, lf_new_t, cache_k, cache_v, cache_lf_t, su, bt)


def _ret_proj_kernel(x_ref, gn_ref, w_ref, inv_ref, q_ref, kt_ref, v_ref, g_ref,
                     *, tm, n_sample, pos0, past, n_new, n_heads, qk_dim, v_width):
    i = pl.program_id(0)
    h = _rmsnorm_rows(x_ref[...], gn_ref[...]).astype(BF16)
    row = i * tm + lax.broadcasted_iota(jnp.int32, (tm, 1), 0)
    pos = jnp.where(row < n_sample, past + lax.rem(row, n_new), row - pos0)
    ang = pos.astype(F32) * inv_ref[...]
    cos = jnp.cos(ang)
    sin = jnp.sin(ang)
    half = qk_dim // 2
    qk_width = n_heads * qk_dim

    def rotary(z):
        parts = []
        for hh in range(n_heads):
            z1 = z[:, hh * qk_dim:hh * qk_dim + half]
            z2 = z[:, hh * qk_dim + half:(hh + 1) * qk_dim]
            parts += [z1 * cos - z2 * sin, z1 * sin + z2 * cos]
        return jnp.concatenate(parts, axis=1)

    q = rotary(_dot(h, w_ref[:, 0:qk_width])) * (qk_dim ** -0.5)
    q_ref[...] = q.astype(BF16)
    kt_ref[...] = rotary(_dot(h, w_ref[:, qk_width:2 * qk_width])).T
    v_ref[...] = _dot(h, w_ref[:, 2 * qk_width:2 * qk_width + v_width]).astype(BF16)
    g_ref[...] = _dot(h, w_ref[:, 2 * qk_width + v_width:])


def _ret_proj(x, gn, w_in, inv_freq, *, n_sample, pos0, past, n_new, n_heads, qk_dim, v_width):
    t_all, d = x.shape
    tm = PROJ_TILE
    qk_width = n_heads * qk_dim
    row = lambda n: pl.BlockSpec((tm, n), lambda i: (i, 0))
    sds = jax.ShapeDtypeStruct
    return pl.pallas_call(
        functools.partial(_ret_proj_kernel, tm=tm, n_sample=n_sample, pos0=pos0, past=past, n_new=n_new,
                          n_heads=n_heads, qk_dim=qk_dim, v_width=v_width),
        grid=(t_all // tm,),
        in_specs=[row(d), _const_spec((1, d)), _const_spec(w_in.shape), _const_spec(inv_freq.shape)],
        out_specs=[row(qk_width), pl.BlockSpec((qk_width, tm), lambda i: (0, i)), row(v_width), row(v_width)],
        out_shape=[sds((t_all, qk_width), BF16), sds((qk_width, t_all), F32),
                   sds((t_all, v_width), BF16), sds((t_all, v_width), F32)],
        compiler_params=_params(("arbitrary",)),
        name="ret_proj",
    )(x, gn, w_in, inv_freq)


def _group_norm_gate(o, g, gn):
    mu = jnp.mean(o, axis=-1, keepdims=True)
    var = jnp.mean(jnp.square(o - mu), axis=-1, keepdims=True)
    y = (o - mu) * lax.rsqrt(var + EPS) * gn
    return (g * jax.nn.sigmoid(g) * y).astype(BF16)


def _ret_prompt_kernel(logg_ref, q_ref, kt_ref, v_ref, g_ref, gn_ref, y_ref, st_ref, s_ref, dmat_ref,
                       *, tb, front):
    hh = pl.program_id(0)
    blk = pl.program_id(1)
    lg = jnp.full((1, 1), logg_ref[hh], F32)

    @pl.when(blk == 0)
    def _():
        s_ref[...] = jnp.zeros_like(s_ref)
        r = lax.broadcasted_iota(jnp.int32, (tb, tb), 0)
        c = lax.broadcasted_iota(jnp.int32, (tb, tb), 1)
        decay = jnp.exp(lg * jnp.abs(r - c).astype(F32))
        dmat_ref[...] = jnp.where(c // CHUNK <= r // CHUNK, decay, 0.0)

    q = q_ref[...]
    key_row = blk * tb + lax.broadcasted_iota(jnp.int32, (1, tb), 1)
    kt = jnp.where(key_row >= front, kt_ref[...], 0.0)
    v = v_ref[...]
    s = _dot(q, kt.astype(BF16)) * dmat_ref[...]
    state = s_ref[...]
    q_dec = jnp.exp(lg * (lax.broadcasted_iota(jnp.int32, (tb, 1), 0) + 1).astype(F32))
    o = _dot(s.astype(BF16), v) + _dot(q, state.astype(BF16)) * q_dec
    k_dec = jnp.exp(lg * (tb - 1 - lax.broadcasted_iota(jnp.int32, (1, tb), 1)).astype(F32))
    new_state = state * jnp.exp(lg * tb) + _dot((kt * k_dec).astype(BF16), v)
    s_ref[...] = new_state
    y_ref[...] = _group_norm_gate(o, g_ref[...], gn_ref[...])

    @pl.when(blk == pl.num_programs(1) - 1)
    def _():
        st_ref[0] = new_state


def _ret_prompt(logg, q_b, k_t, v_b, g, gn, *, front, t_prompt, n_heads):
    t_all, qk_width = q_b.shape
    v_width = v_b.shape[1]
    qk_dim, v_dim = qk_width // n_heads, v_width // n_heads
    tb = RET_BLOCK
    return pl.pallas_call(
        functools.partial(_ret_prompt_kernel, tb=tb, front=front),
        grid=(n_heads, t_prompt // tb),
        in_specs=[pl.BlockSpec(memory_space=pltpu.SMEM),
                  pl.BlockSpec((tb, qk_dim), lambda h, i: (i, h)),
                  pl.BlockSpec((qk_dim, tb), lambda h, i: (h, i)),
                  pl.BlockSpec((tb, v_dim), lambda h, i: (i, h)),
                  pl.BlockSpec((tb, v_dim), lambda h, i: (i, h)),
                  pl.BlockSpec((1, v_dim), lambda h, i: (0, h))],
        out_specs=[pl.BlockSpec((tb, v_dim), lambda h, i: (i, h)),
                   pl.BlockSpec((1, qk_dim, v_dim), lambda h, i: (h, 0, 0))],
        out_shape=[jax.ShapeDtypeStruct((t_prompt, v_width), BF16),
                   jax.ShapeDtypeStruct((n_heads, qk_dim, v_dim), F32)],
        scratch_shapes=[pltpu.VMEM((qk_dim, v_dim), F32), pltpu.VMEM((tb, tb), F32)],
        compiler_params=_params(("arbitrary", "arbitrary")),
        name="ret_prompt",
    )(logg, q_b, k_t, v_b, g, gn)


def _ret_sample_kernel(logg_ref, q_ref, kt_ref, v_ref, g_ref, gn_ref, st_ref, y_ref, nst_ref, *, n_new):
    hh = pl.program_id(0)
    b = pl.program_id(1)
    lg = jnp.full((1, 1), logg_ref[hh], F32)
    n_rows = kt_ref.shape[1]
    r0 = pl.multiple_of(b * n_new, n_new)
    q = q_ref[pl.ds(r0, n_new), :]
    kt = kt_ref[...]
    v = v_ref[...]
    qrow = lax.broadcasted_iota(jnp.int32, (n_new, n_rows), 0)
    col = lax.broadcasted_iota(jnp.int32, (n_new, n_rows), 1) - b * n_new
    own = jnp.logical_and(col >= 0, col < n_new)
    dmat = jnp.where(own, jnp.exp(lg * jnp.abs(qrow - col).astype(F32)), 0.0)
    s = _dot(q, kt.astype(BF16)) * dmat
    state = st_ref[0, 0]
    q_dec = jnp.exp(lg * (lax.broadcasted_iota(jnp.int32, (n_new, 1), 0) + 1).astype(F32))
    o = _dot(s.astype(BF16), v) + _dot(q, state.astype(BF16)) * q_dec
    col1 = lax.broadcasted_iota(jnp.int32, (1, n_rows), 1) - b * n_new
    own1 = jnp.logical_and(col1 >= 0, col1 < n_new)
    k_dec = jnp.where(own1, jnp.exp(lg * (n_new - 1 - col1).astype(F32)), 0.0)
    nst_ref[0, 0] = state * jnp.exp(lg * n_new) + _dot((kt * k_dec).astype(BF16), v)
    y_ref[pl.ds(r0, n_new), :] = _group_norm_gate(o, g_ref[pl.ds(r0, n_new), :], gn_ref[...])


def _ret_sample(logg, q_b, k_t, v_b, g, gn, state, *, n_new):
    t_all, qk_width = q_b.shape
    v_width = v_b.shape[1]
    n_streams, n_heads, qk_dim, v_dim = state.shape
    n_rows = n_streams * n_new
    blk = 0
    return pl.pallas_call(
        functools.partial(_ret_sample_kernel, n_new=n_new),
        grid=(n_heads, n_streams),
        in_specs=[pl.BlockSpec(memory_space=pltpu.SMEM),
                  pl.BlockSpec((n_rows, qk_dim), lambda h, b: (blk, h)),
                  pl.BlockSpec((qk_dim, n_rows), lambda h, b: (h, blk)),
                  pl.BlockSpec((n_rows, v_dim), lambda h, b: (blk, h)),
                  pl.BlockSpec((n_rows, v_dim), lambda h, b: (blk, h)),
                  pl.BlockSpec((1, v_dim), lambda h, b: (0, h)),
                  pl.BlockSpec((1, 1, qk_dim, v_dim), lambda h, b: (b, h, 0, 0))],
        out_specs=[pl.BlockSpec((n_rows, v_dim), lambda h, b: (0, h)),
                   pl.BlockSpec((1, 1, qk_dim, v_dim), lambda h, b: (b, h, 0, 0))],
        out_shape=[jax.ShapeDtypeStruct((n_rows, v_width), BF16),
                   jax.ShapeDtypeStruct(state.shape, F32)],
        compiler_params=_params(("arbitrary", "arbitrary")),
        name="ret_sample",
    )(logg, q_b, k_t, v_b, g, gn, state)


def _top_values(s, k, with_rank):
    work = s
    vals = []
    for r in range(k):
        m = jnp.max(work, axis=0, keepdims=True)
        vals.append(m)
        work = jnp.where(work == m, RANK_MARK * (1.0 + r / 1024.0), work)
    if not with_rank:
        return vals, None
    rank = jnp.where(work <= RANK_MARK, (work * (1.0 / RANK_MARK) - 1.0) * 1024.0, NO_RANK)
    return vals, rank


def _peer_kernel(x_ref, mp_ref, mt_ref, wo_ref, gn_ref, wq_ref, sk_ref, u_ref, vt_ref, o_ref,
                 xn_ref, hb_ref, e0_ref, cnt_ref, e1_ref, rank_ref, w_ref, y_ref,
                 *, n_heads, n_keys, topk):
    c = pl.program_id(1)
    ce, tm = u_ref.shape[0], x_ref.shape[0]
    rows_per_chunk = ce // n_keys
    sub = rank_ref.shape[2]

    @pl.when(c == 0)
    def _():
        n_sample = mt_ref.shape[0]
        mixed = mp_ref[...]
        head_rows = jnp.where(pl.program_id(0) == 0, mt_ref[...], mixed[:n_sample])
        mixed = jnp.concatenate([head_rows, mixed[n_sample:]], axis=0)
        xn = x_ref[...] + _dot(mixed, wo_ref[...])
        xn_ref[...] = xn
        hb = _rmsnorm_rows(xn, gn_ref[...]).astype(BF16)
        hb_ref[...] = hb
        y_ref[...] = jnp.zeros_like(y_ref)
        for h in range(n_heads):
            qh = _dot_nt(wq_ref[h * 2 * n_keys:(h + 1) * 2 * n_keys, :], hb)
            s0 = _dot(sk_ref[2 * h], qh[0:n_keys].astype(BF16))
            s1 = _dot(sk_ref[2 * h + 1], qh[n_keys:2 * n_keys].astype(BF16))
            a, rank0 = _top_values(s0, topk, True)
            b, rank1 = _top_values(s1, topk, True)
            half = topk // 2
            b_all = jnp.concatenate(b, axis=0)
            b_half = jnp.concatenate(b[:half], axis=0)
            cand = jnp.concatenate(
                [a[0] + b_all] + [a[i] + b_half for i in range(1, half)] + [jnp.concatenate(a[half:], axis=0) + b[0]],
                axis=0)
            f, _ = _top_values(cand, topk, False)
            tau = f[topk - 1]
            z = sum(jnp.exp(fk - f[0]) for fk in f)
            a_all = jnp.concatenate(a, axis=0)
            cnt_sorted = sum(jnp.where(a_all + bj >= tau, 1.0, 0.0) for bj in b)
            cnt = jnp.zeros_like(s0)
            for r in range(topk):
                cnt = jnp.where(rank0 == float(r), cnt_sorted[r:r + 1, :], cnt)
            cnt_ref[h] = cnt
            e0_ref[h] = jnp.exp(s0 - a[0])
            e1_ref[h] = (jnp.exp(s1 - b[0]) * (0.5 / z)).astype(BF16).reshape(n_keys // sub, sub, tm)
            rank_ref[h] = rank1.astype(BF16).reshape(n_keys // sub, sub, tm)

    hb = hb_ref[...]
    edges = PEER_ACT_EDGES + (ce,)
    act_of_row = {}
    for lo_row, hi_row in zip(edges[:-1], edges[1:]):
        piece = _dot_nt(u_ref[lo_row:hi_row, :], hb)
        for r in range(lo_row, hi_row, n_keys):
            act_of_row[r] = (piece, r - lo_row)
    rows_per_slab = MXU_DIM // n_keys
    y_add = None
    for slab in range(ce // MXU_DIM):
        for il in range(slab * rows_per_slab, (slab + 1) * rows_per_slab):
            i = c * rows_per_chunk + il
            gate = None
            for h in range(n_heads):
                n_sel = jnp.broadcast_to(cnt_ref[h, pl.ds(i, 1), :], (sub, tm)).astype(BF16)
                e0 = jnp.broadcast_to(e0_ref[h, pl.ds(i, 1), :], (sub, tm)).astype(BF16)
                term = jnp.where(rank_ref[h] < n_sel[None], e1_ref[h], jnp.zeros((), BF16)) * e0[None]
                gate = term if gate is None else gate + term
            r0 = il * n_keys
            piece, off = act_of_row[r0]
            x_act = piece[off:off + n_keys, :]
            gelu2 = x_act * (1.0 + lax.erf(x_act * (2.0 ** -0.5)))
            w_ref[r0:r0 + n_keys, :] = gate.reshape(n_keys, tm) * gelu2.astype(BF16)
        rows = slice(slab * MXU_DIM, (slab + 1) * MXU_DIM)
        part = _dot(vt_ref[:, rows], w_ref[rows, :])
        y_add = part if y_add is None else y_add + part
    y_ref[...] += y_add

    @pl.when(c == pl.num_programs(1) - 1)
    def _():
        o_ref[...] = xn_ref[...] + y_ref[...].T


def _peer(x, mix_prompt, mix_sample, w_out, gn, wq_t, subkeys, u, v_t, *, layer, n_heads, n_keys):
    t_all, d = x.shape
    k = mix_prompt.shape[1]
    n_sample = mix_sample.shape[0]
    n_experts = u.shape[1]
    tm = ROW_TILE
    ce = PEER_EXPERT_CHUNK
    assert mix_prompt.shape == (t_all, k) and mix_sample.shape == (n_sample, k) and n_sample <= tm
    head_scratch = pltpu.VMEM((n_heads, n_keys, tm), F32)
    head_scratch_b = pltpu.VMEM((n_heads, n_keys // BF16_TILE_ROWS, BF16_TILE_ROWS, tm), BF16)
    return pl.pallas_call(
        functools.partial(_peer_kernel, n_heads=n_heads, n_keys=n_keys, topk=PEER_TOPK),
        grid=(t_all // tm, n_experts // ce),
        in_specs=[pl.BlockSpec((tm, d), lambda i, c: (i, 0)),
                  pl.BlockSpec((tm, k), lambda i, c: (i, 0)),
                  _const_spec((n_sample, k)), _const_spec(w_out.shape),
                  _const_spec((1, d)), _const_spec(wq_t.shape),
                  _const_spec(subkeys.shape),
                  pl.BlockSpec((None, ce, d), lambda i, c: (layer, c, 0)),
                  pl.BlockSpec((None, d, ce), lambda i, c: (layer, 0, c))],
        out_specs=pl.BlockSpec((tm, d), lambda i, c: (i, 0)),
        out_shape=jax.ShapeDtypeStruct((t_all, d), F32),
        scratch_shapes=[pltpu.VMEM((tm, d), F32), pltpu.VMEM((tm, d), BF16),
                        head_scratch, head_scratch, head_scratch_b, head_scratch_b,
                        pltpu.VMEM((ce, tm), BF16), pltpu.VMEM((d, tm), F32)],
        compiler_params=_params(("arbitrary", "arbitrary")),
        name="peer",
    )(x, mix_prompt, mix_sample, w_out, gn, wq_t, subkeys, u, v_t)


def kernel(x_prompt, x_sample, cache_fox_k, cache_fox_v, cache_fox_lf, state_ret, meta_tokens, norm_mix, norm_ffn,
           fox_w_in, fox_b_f, fox_q_norm, fox_k_norm, fox_w_out, ret_w_in, ret_gn, ret_w_out,
           peer_w_q, peer_subkeys, peer_u, peer_v):
    batch, seq, d = x_prompt.shape
    n_streams, n_new, _ = x_sample.shape
    n_meta = meta_tokens.shape[0]
    assert batch == 1 and n_meta == N_META
    depth = norm_mix.shape[0]
    n_fox, _, past, fox_heads, fox_hd = cache_fox_k.shape
    fox_width = fox_heads * fox_hd
    n_ret, _, ret_heads, ret_qk, ret_v = state_ret.shape
    ret_v_width = ret_heads * ret_v
    peer_heads, _, n_keys, peer_half = peer_subkeys.shape[1:]
    assert n_keys == LANES and peer_half == LANES and fox_width == d

    length = n_meta + seq
    n_rows = n_streams * n_new
    front = (-length) % ROW_TILE
    if front < n_rows:
        front += ROW_TILE
    t_all = front + length

    x = jnp.concatenate([x_sample.reshape(n_rows, d), jnp.zeros((front - n_rows, d), F32),
                         meta_tokens.astype(F32), x_prompt[0]], axis=0)

    half = ret_qk // 2
    inv_freq = (ROPE_BASE ** (-jnp.arange(half, dtype=F32) / half)).reshape(1, half)
    logg = jnp.log(1.0 - 2.0 ** (-5.0 - jnp.arange(ret_heads, dtype=F32)))

    cache_k = cache_fox_k.reshape(n_fox, n_streams, past, fox_width)
    cache_v = cache_fox_v.reshape(n_fox, n_streams, past, fox_width)
    cache_lf_t = cache_fox_lf.transpose(0, 1, 3, 2)
    peer_u_b = peer_u.astype(BF16)
    peer_vt_b = peer_v.astype(BF16).transpose(0, 2, 1)

    kp, vp, lfp, ks, vs, lfs, srp, srs = [], [], [], [], [], [], [], []
    for layer in range(depth):
        j = layer // 2
        gn_mix = norm_mix[layer].reshape(1, d)
        if layer % 2 == 0:
            w_in = fox_w_in[j]
            w_main = w_in[:, :4 * fox_width].astype(BF16)
            w_f = jnp.pad(w_in[:, 4 * fox_width:], ((0, 0), (0, LANES - fox_heads))).astype(BF16)
            b_f = jnp.pad(fox_b_f[j], (0, LANES - fox_heads)).reshape(1, LANES)
            gq = jnp.tile(fox_q_norm[j], fox_heads).reshape(1, fox_width)
            gk = jnp.tile(fox_k_norm[j], fox_heads).reshape(1, fox_width)
            q_b, k_f, k_b, v_f, v_b, og, lf, csum = _fox_proj(x, gn_mix, w_main, w_f, b_f, gq, gk,
                                                              front=front, head_dim=fox_hd)
            o_b = _fox_attn_prompt(q_b, k_b, v_b, csum, og, fox_q_norm[j], fox_k_norm[j],
                                   t_prompt=t_all, head_dim=fox_hd)
            lf_new_t = lf[:n_rows, :fox_heads].T
            o_s = _fox_attn_sample(q_b, k_b, v_b, og, lf_new_t, cache_k, cache_v, cache_lf_t,
                                   layer=j, n_new=n_new, head_dim=fox_hd)
            mix_prompt, mix_sample, w_out = o_b, o_s, fox_w_out[j].astype(BF16)
            kp.append(k_f)
            vp.append(v_f)
            lfp.append(lf[front:, :fox_heads].reshape(1, length, fox_heads))
            ks.append(k_f[:n_rows].reshape(n_streams, n_new, fox_heads, fox_hd))
            vs.append(v_f[:n_rows].reshape(n_streams, n_new, fox_heads, fox_hd))
            lfs.append(lf[:n_rows, :fox_heads].reshape(n_streams, n_new, fox_heads))
        else:
            q_b, k_t, v_b, g = _ret_proj(x, gn_mix, ret_w_in[j].astype(BF16), inv_freq,
                                         n_sample=n_rows, pos0=front + n_meta, past=past, n_new=n_new,
                                         n_heads=ret_heads, qk_dim=ret_qk, v_width=ret_v_width)
            gn_ret = ret_gn[j].reshape(1, ret_v_width)
            y_b, st_p = _ret_prompt(logg, q_b, k_t, v_b, g, gn_ret, front=front, t_prompt=t_all, n_heads=ret_heads)
            y_s, st_s = _ret_sample(logg, q_b, k_t, v_b, g, gn_ret, state_ret[j], n_new=n_new)
            mix_prompt, mix_sample, w_out = y_b, y_s, ret_w_out[j].astype(BF16)
            srp.append(st_p[None])
            srs.append(st_s)
        x = _peer(x, mix_prompt, mix_sample, w_out, norm_ffn[layer].reshape(1, d), peer_w_q[layer].T.astype(BF16),
                  peer_subkeys[layer].reshape(2 * peer_heads, n_keys, peer_half).astype(BF16),
                  peer_u_b, peer_vt_b, layer=layer, n_heads=peer_heads, n_keys=n_keys)

    y_prompt = x[front + n_meta:][None]
    y_sample = x[:n_rows].reshape(n_streams, n_new, d)
    prompt_cache = lambda a: jnp.stack(a)[:, front:].reshape(len(a), 1, length, fox_heads, fox_hd)
    return (y_prompt, y_sample, prompt_cache(kp), prompt_cache(vp), jnp.stack(lfp), jnp.stack(srp),
            jnp.stack(ks), jnp.stack(vs), jnp.stack(lfs), jnp.stack(srs))
```

```python
import functools
import math

import jax
import jax.numpy as jnp
from jax import lax
from jax.experimental import pallas as pl
from jax.experimental.pallas import tpu as pltpu

F32 = jnp.float32
BF16 = jnp.bfloat16

EPS = 1e-6
N_META = 16
CHUNK = 64
ROPE_BASE = 10000.0
PEER_TOPK = 16

LANES = 128
MXU_DIM = 256
BF16_TILE_ROWS = 16
VMEM_LIMIT = 56 * 1024 * 1024

ROW_TILE = 512
PROJ_TILE = 256
RET_BLOCK = 512
PEER_EXPERT_CHUNK = 2048
PEER_ACT_EDGES = (0, 1024)
MASKED_KEY = 1e30
NO_RANK = 99.0
RANK_MARK = -(2.0 ** 100)
LOG2E = math.log2(math.e)
ZERO_WEIGHT_LOG2 = 160.0

_NT = (((1,), (1,)), ((), ()))


def _const_spec(shape):
    return pl.BlockSpec(shape, lambda *_: (0,) * len(shape), pipeline_mode=pl.Buffered(1))


def _params(semantics):
    return pltpu.CompilerParams(dimension_semantics=semantics, vmem_limit_bytes=VMEM_LIMIT)


def _split3(x):
    a = x.astype(BF16)
    r = x - a.astype(F32)
    b = r.astype(BF16)
    c = (r - b.astype(F32)).astype(BF16)
    return a, b, c


def _dot(a, b):
    return jnp.dot(a, b, preferred_element_type=F32)


def _dot_nt(a, b):
    return lax.dot_general(a, b, _NT, preferred_element_type=F32)


def _rmsnorm_rows(x, g):
    ms = jnp.mean(x * x, axis=-1, keepdims=True)
    return x * lax.rsqrt(ms + EPS) * g


def _log_sigmoid(x):
    return jnp.minimum(x, 0.0) - jnp.log1p(jnp.exp(-jnp.abs(x)))


def _fox_proj_kernel(x_ref, gn_ref, w_ref, wf_ref, bf_ref, gq_ref, gk_ref, bd_ref, tril_ref,
                     q_ref, kf_ref, kb_ref, vf_ref, vb_ref, og_ref, lf_ref, c_ref, carry_ref,
                     *, tm, front, head_dim, width):
    i = pl.program_id(0)

    @pl.when(i == 0)
    def _():
        carry_ref[...] = jnp.zeros_like(carry_ref)

    h = _rmsnorm_rows(x_ref[...], gn_ref[...]).astype(BF16)

    def head_norm(z, g):
        zz = (z * z).astype(BF16)
        ss = jnp.concatenate(
            [_dot(zz[:, c * MXU_DIM:(c + 1) * MXU_DIM], bd_ref[...]) for c in range(width // MXU_DIM)], axis=1)
        return z * lax.rsqrt(ss * (1.0 / head_dim) + EPS) * g

    zq = _dot(h, w_ref[:, 0:width])
    q_ref[...] = (head_norm(zq, gq_ref[...]) * (head_dim ** -0.5 * LOG2E)).astype(BF16)
    zk = _dot(h, w_ref[:, width:2 * width])
    kn = head_norm(zk, gk_ref[...])
    kf_ref[...] = kn
    kb_ref[...] = kn.astype(BF16)
    zv = _dot(h, w_ref[:, 2 * width:3 * width])
    vf_ref[...] = zv
    vb_ref[...] = zv.astype(BF16)
    og_ref[...] = _dot(h, w_ref[:, 3 * width:4 * width])

    lf = _log_sigmoid(_dot(h, wf_ref[...]) + bf_ref[...])
    lf_ref[...] = lf
    row = i * tm + lax.broadcasted_iota(jnp.int32, (tm, 1), 0)
    real = row >= front
    l1, l2, l3 = _split3(jnp.where(real, lf, 0.0))
    tril = tril_ref[...]
    c = _dot(tril, l1) + _dot(tril, l2) + _dot(tril, l3) + carry_ref[...]
    carry_ref[...] = c[tm - 1:tm, :]
    c_ref[...] = jnp.where(real, c, MASKED_KEY)


def _fox_proj(x, gn, w_main, w_f, b_f, gq, gk, *, front, head_dim):
    t_all, d = x.shape
    width = w_main.shape[1] // 4
    tm = PROJ_TILE
    blk = MXU_DIM // head_dim
    bd = jnp.kron(jnp.eye(blk, dtype=F32), jnp.ones((head_dim, head_dim), F32)).astype(BF16)
    tril = (lax.broadcasted_iota(jnp.int32, (tm, tm), 0) >= lax.broadcasted_iota(jnp.int32, (tm, tm), 1)).astype(BF16)
    row = lambda n: pl.BlockSpec((tm, n), lambda i: (i, 0))
    sds = jax.ShapeDtypeStruct
    return pl.pallas_call(
        functools.partial(_fox_proj_kernel, tm=tm, front=front, head_dim=head_dim, width=width),
        grid=(t_all // tm,),
        in_specs=[row(d), _const_spec((1, d)), _const_spec(w_main.shape), _const_spec(w_f.shape),
                  _const_spec((1, LANES)), _const_spec((1, width)), _const_spec((1, width)),
                  _const_spec(bd.shape), _const_spec(tril.shape)],
        out_specs=[row(width), row(width), row(width), row(width), row(width), row(width), row(LANES), row(LANES)],
        out_shape=[sds((t_all, width), BF16), sds((t_all, width), F32), sds((t_all, width), BF16),
                   sds((t_all, width), F32), sds((t_all, width), BF16), sds((t_all, width), F32),
                   sds((t_all, LANES), F32), sds((t_all, LANES), F32)],
        scratch_shapes=[pltpu.VMEM((1, LANES), F32)],
        compiler_params=_params(("arbitrary",)),
        name="fox_proj",
    )(x, gn, w_main, w_f, b_f, gq, gk, bd, tril)


def _fox_attn_kernel(first_ref, q_ref, k_ref, v_ref, c_ref, og_ref, o_ref,
                     qm_ref, m_ref, alpha_ref, p_ref, acc_ref, pend_ref, *, tile, head_dim):
    qi = pl.program_id(1)
    first = tuple(first_ref[2 * pl.program_id(0) + hh, qi] for hh in range(2))
    lane = lax.broadcasted_iota(jnp.int32, (1, 2 * head_dim), 1)
    lo = lane < head_dim
    head_lanes = (lo, jnp.logical_not(lo))
    q = q_ref[...]
    for hh in range(2):
        qm_ref[hh] = jnp.where(head_lanes[hh], q, jnp.zeros_like(q))
    m_ref[...] = jnp.full(m_ref.shape, -jnp.inf, F32)
    acc_ref[...] = jnp.zeros(acc_ref.shape, F32)
    alpha_ref[1] = jnp.ones(alpha_ref.shape[1:], F32)
    p_ref[1] = jnp.zeros(p_ref.shape[1:], BF16)
    one = jnp.ones((), BF16)
    both = (0, 1)

    def scores(j, slot, heads, diagonal=False):
        k = k_ref[pl.ds(pl.multiple_of(j * tile, tile), tile), :]
        for hh in heads:
            c_last = c_ref[hh, pl.ds(qi, 1), :][:, tile - 1:tile]
            bias = (c_ref[hh, pl.ds(j, 1), :] - c_last) * LOG2E
            u = _dot_nt(qm_ref[hh], k) - bias
            if diagonal:
                r = lax.broadcasted_iota(jnp.int32, (tile, tile), 0)
                cidx = lax.broadcasted_iota(jnp.int32, (tile, tile), 1)
                u = jnp.where(cidx <= r, u, -jnp.inf)
            m_prev = m_ref[hh]
            m_new = jnp.maximum(m_prev, jnp.max(u, axis=1, keepdims=True))
            m_wide = jnp.concatenate([m_new] * (tile // LANES), axis=1)
            p_ref[slot, hh] = jnp.exp2(u - m_wide).astype(BF16)
            alpha_ref[slot, hh] = jnp.exp2(m_prev - m_new)
            m_ref[hh] = m_new

    def accumulate(j, slot, heads):
        v = v_ref[pl.ds(pl.multiple_of(j * tile, tile), tile), :]
        for hh in heads:
            v_aug = jnp.where(head_lanes[hh], v, one)
            acc_ref[hh] = alpha_ref[slot, hh] * acc_ref[hh] + _dot(p_ref[slot, hh], v_aug)

    def block_pairs(start, n_pairs, pending, heads):
        def body(jj, pend):
            j0 = start + 2 * jj
            scores(j0, 0, heads)
            accumulate(pend, 1, heads)
            scores(j0 + 1, 1, heads)
            accumulate(j0, 0, heads)
            return j0 + 1

        return lax.fori_loop(0, n_pairs, body, pending)

    def blocks(start, stop, pending, heads):
        n = stop - start
        odd = lax.rem(n, 2)

        @pl.when(odd == 1)
        def _():
            scores(start, 0, heads)
            accumulate(pending, 1, heads)
            accumulate(start, 0, heads)
            for hh in heads:
                alpha_ref[1, hh] = jnp.ones(alpha_ref.shape[2:], F32)
                p_ref[1, hh] = jnp.zeros(p_ref.shape[2:], BF16)

        return block_pairs(start + odd, n // 2, pending, heads)

    joint = jnp.maximum(first[0], first[1])
    pend_ref[0] = joint
    for hh in both:
        @pl.when(first[hh] < first[1 - hh])
        def _():
            pend_ref[0] = blocks(first[hh], joint, joint, (hh,))

    n_joint = qi - joint
    pending = block_pairs(joint, n_joint // 2, pend_ref[0], both)

    @pl.when(lax.rem(n_joint, 2) == 0)
    def _():
        scores(qi, 0, both, diagonal=True)
        accumulate(pending, 1, both)
        accumulate(qi, 0, both)

    @pl.when(lax.rem(n_joint, 2) == 1)
    def _():
        scores(qi - 1, 0, both)
        accumulate(pending, 1, both)
        scores(qi, 1, both, diagonal=True)
        accumulate(qi - 1, 0, both)
        accumulate(qi, 1, both)

    outs = []
    for hh in range(2):
        a = acc_ref[hh]
        outs.append(a / pltpu.roll(a, head_dim, 1))
    o = jnp.where(lo, outs[0], outs[1])
    o_ref[...] = (o * jax.nn.sigmoid(og_ref[...])).astype(BF16)


def _first_needed_block(gq, gk, csum, *, n_heads, t_prompt, head_dim, tile):
    bf16_slack = 1.0 + 2.0 ** -7
    q_norm = math.sqrt(head_dim) * jnp.max(jnp.abs(gq)) * (head_dim ** -0.5 * LOG2E) * bf16_slack
    k_norm = math.sqrt(head_dim) * jnp.max(jnp.abs(gk)) * bf16_slack
    c = csum[:t_prompt, :n_heads]
    c_first, c_last = c[0::tile], c[tile - 1::tile]
    bound = 2.0 * q_norm * k_norm + LOG2E * (c_first[:, None, :] - c_last[None, :, :])
    blk = jnp.arange(c_first.shape[0])
    earlier = blk[None, :, None] < blk[:, None, None]
    skipped = jnp.sum(jnp.logical_and(bound < -ZERO_WEIGHT_LOG2, earlier), axis=1)
    return skipped.T.astype(jnp.int32)


def _fox_attn_prompt(q_b, k_b, v_b, csum, og, gq, gk, *, t_prompt, head_dim):
    t_all, width = q_b.shape
    tile = ROW_TILE
    pair = 2 * head_dim
    n_blk = t_prompt // tile
    n_heads = width // head_dim
    c3 = csum[:t_prompt, :n_heads].T.reshape(n_heads, n_blk, tile)
    first = _first_needed_block(gq, gk, csum, n_heads=n_heads, t_prompt=t_prompt, head_dim=head_dim, tile=tile)
    return pl.pallas_call(
        functools.partial(_fox_attn_kernel, tile=tile, head_dim=head_dim),
        grid=(width // pair, n_blk),
        in_specs=[pl.BlockSpec(memory_space=pltpu.SMEM),
                  pl.BlockSpec((tile, pair), lambda p, i: (i, p)),
                  pl.BlockSpec((t_prompt, pair), lambda p, i: (0, p)),
                  pl.BlockSpec((t_prompt, pair), lambda p, i: (0, p)),
                  pl.BlockSpec((2, n_blk, tile), lambda p, i: (p, 0, 0)),
                  pl.BlockSpec((tile, pair), lambda p, i: (i, p))],
        out_specs=pl.BlockSpec((tile, pair), lambda p, i: (i, p)),
        out_shape=jax.ShapeDtypeStruct((t_prompt, width), BF16),
        scratch_shapes=[pltpu.VMEM((2, tile, pair), BF16), pltpu.VMEM((2, tile, LANES), F32),
                        pltpu.VMEM((2, 2, tile, LANES), F32), pltpu.VMEM((2, 2, tile, tile), BF16),
                        pltpu.VMEM((2, tile, pair), F32), pltpu.SMEM((1,), jnp.int32)],
        compiler_params=_params(("arbitrary", "arbitrary")),
        name="fox_attn_prompt",
    )(first, q_b, k_b, v_b, c3, og)


def _fox_sample_kernel(q_ref, kn_ref, vn_ref, og_ref, lfn_ref, kc_ref, vc_ref, lfc_ref, su_ref, bt_ref,
                       o_ref, *, n_new, head_dim, n_heads):
    b = pl.program_id(0)
    pair = 2 * head_dim
    n_rows = kn_ref.shape[0]

    su = su_ref[...]
    suf = sum(_dot(t, su) for t in _split3(lfc_ref[0])) * LOG2E
    bt = bt_ref[...]
    pre = sum(_dot(t, bt) for t in _split3(lfn_ref[...])) * LOG2E

    r0 = pl.multiple_of(b * n_new, n_new)
    qrow = lax.broadcasted_iota(jnp.int32, (n_new, n_rows), 0)
    col = lax.broadcasted_iota(jnp.int32, (n_new, n_rows), 1)
    visible = jnp.logical_and(col >= b * n_new, col <= b * n_new + qrow)
    lane = lax.broadcasted_iota(jnp.int32, (1, pair), 1)
    lo = lane < head_dim
    head_lanes = (lo, jnp.logical_not(lo))

    for p in range(n_heads // 2):
        cols = slice(p * pair, (p + 1) * pair)
        q = q_ref[pl.ds(r0, n_new), cols]
        k_c = kc_ref[0, :, cols].astype(BF16)
        v_c = vc_ref[0, :, cols].astype(BF16)
        k_n = kn_ref[:, cols]
        v_n = vn_ref[:, cols]
        outs = []
        for hh in range(2):
            head = 2 * p + hh
            qh = jnp.where(head_lanes[hh], q, jnp.zeros_like(q))
            u_c = _dot_nt(qh, k_c) + suf[head:head + 1, :]
            u_n = jnp.where(visible, _dot_nt(qh, k_n) - pre[head:head + 1, :], -jnp.inf)
            m = jnp.maximum(jnp.max(u_c, axis=1, keepdims=True), jnp.max(u_n, axis=1, keepdims=True))
            p_c = jnp.exp2(u_c - m)
            p_n = jnp.exp2(u_n - m)
            den = jnp.sum(p_c, axis=1, keepdims=True) + jnp.sum(p_n, axis=1, keepdims=True)
            outs.append((_dot(p_c.astype(BF16), v_c) + _dot(p_n.astype(BF16), v_n)) / den)
        o = jnp.where(lo, outs[0], outs[1])
        gate = jax.nn.sigmoid(og_ref[pl.ds(r0, n_new), cols])
        o_ref[pl.ds(r0, n_new), cols] = (o * gate).astype(BF16)


def _fox_attn_sample(q_b, k_b, v_b, og, lf_new_t, cache_k, cache_v, cache_lf_t, *, layer, n_new, head_dim):
    t_all, width = q_b.shape
    _, n_streams, past, _ = cache_k.shape
    n_heads = width // head_dim
    n_rows = n_streams * n_new
    assert n_rows % LANES == 0
    su = (lax.broadcasted_iota(jnp.int32, (past, past), 0) > lax.broadcasted_iota(jnp.int32, (past, past), 1)).astype(BF16)
    rr = lax.broadcasted_iota(jnp.int32, (n_rows, n_rows), 0)
    cc = lax.broadcasted_iota(jnp.int32, (n_rows, n_rows), 1)
    bt = jnp.logical_and(rr // n_new == cc // n_new, rr <= cc).astype(BF16)
    rows = pl.BlockSpec((n_rows, width), lambda b: (0, 0), pipeline_mode=pl.Buffered(1))
    return pl.pallas_call(
        functools.partial(_fox_sample_kernel, n_new=n_new, head_dim=head_dim, n_heads=n_heads),
        grid=(n_streams,),
        in_specs=[rows, rows, rows, rows,
                  _const_spec(lf_new_t.shape),
                  pl.BlockSpec((None, 1, past, width), lambda b: (layer, b, 0, 0)),
                  pl.BlockSpec((None, 1, past, width), lambda b: (layer, b, 0, 0)),
                  pl.BlockSpec((None, 1, n_heads, past), lambda b: (layer, b, 0, 0)),
                  _const_spec(su.shape), _const_spec(bt.shape)],
        out_specs=pl.BlockSpec((n_rows, width), lambda b: (0, 0)),
        out_shape=jax.ShapeDtypeStruct((n_rows, width), BF16),
        compiler_params=_params(("arbitrary",)),
        name="fox_attn_sample",
    )(q_b, k_b, v_b, og, lf_new_t, cache_k, cache_v, cache_lf_t, su, bt)


def _ret_proj_kernel(x_ref, gn_ref, w_ref, inv_ref, q_ref, kt_ref, v_ref, g_ref,
                     *, tm, n_sample, pos0, past, n_new, n_heads, qk_dim, v_width):
    i = pl.program_id(0)
    h = _rmsnorm_rows(x_ref[...], gn_ref[...]).astype(BF16)
    row = i * tm + lax.broadcasted_iota(jnp.int32, (tm, 1), 0)
    pos = jnp.where(row < n_sample, past + lax.rem(row, n_new), row - pos0)
    ang = pos.astype(F32) * inv_ref[...]
    cos = jnp.cos(ang)
    sin = jnp.sin(ang)
    half = qk_dim // 2
    qk_width = n_heads * qk_dim

    def rotary(z):
        parts = []
        for hh in range(n_heads):
            z1 = z[:, hh * qk_dim:hh * qk_dim + half]
            z2 = z[:, hh * qk_dim + half:(hh + 1) * qk_dim]
            parts += [z1 * cos - z2 * sin, z1 * sin + z2 * cos]
        return jnp.concatenate(parts, axis=1)

    q = rotary(_dot(h, w_ref[:, 0:qk_width])) * (qk_dim ** -0.5)
    q_ref[...] = q.astype(BF16)
    kt_ref[...] = rotary(_dot(h, w_ref[:, qk_width:2 * qk_width])).T
    v_ref[...] = _dot(h, w_ref[:, 2 * qk_width:2 * qk_width + v_width]).astype(BF16)
    g_ref[...] = _dot(h, w_ref[:, 2 * qk_width + v_width:])


def _ret_proj(x, gn, w_in, inv_freq, *, n_sample, pos0, past, n_new, n_heads, qk_dim, v_width):
    t_all, d = x.shape
    tm = PROJ_TILE
    qk_width = n_heads * qk_dim
    row = lambda n: pl.BlockSpec((tm, n), lambda i: (i, 0))
    sds = jax.ShapeDtypeStruct
    return pl.pallas_call(
        functools.partial(_ret_proj_kernel, tm=tm, n_sample=n_sample, pos0=pos0, past=past, n_new=n_new,
                          n_heads=n_heads, qk_dim=qk_dim, v_width=v_width),
        grid=(t_all // tm,),
        in_specs=[row(d), _const_spec((1, d)), _const_spec(w_in.shape), _const_spec(inv_freq.shape)],
        out_specs=[row(qk_width), pl.BlockSpec((qk_width, tm), lambda i: (0, i)), row(v_width), row(v_width)],
        out_shape=[sds((t_all, qk_width), BF16), sds((qk_width, t_all), F32),
                   sds((t_all, v_width), BF16), sds((t_all, v_width), F32)],
        compiler_params=_params(("arbitrary",)),
        name="ret_proj",
    )(x, gn, w_in, inv_freq)


def _group_norm_gate(o, g, gn):
    mu = jnp.mean(o, axis=-1, keepdims=True)
    var = jnp.mean(jnp.square(o - mu), axis=-1, keepdims=True)
    y = (o - mu) * lax.rsqrt(var + EPS) * gn
    return (g * jax.nn.sigmoid(g) * y).astype(BF16)


def _ret_prompt_kernel(logg_ref, q_ref, kt_ref, v_ref, g_ref, gn_ref, y_ref, st_ref, s_ref, dmat_ref,
                       *, tb, front):
    hh = pl.program_id(0)
    blk = pl.program_id(1)
    lg = jnp.full((1, 1), logg_ref[hh], F32)

    @pl.when(blk == 0)
    def _():
        s_ref[...] = jnp.zeros_like(s_ref)
        r = lax.broadcasted_iota(jnp.int32, (tb, tb), 0)
        c = lax.broadcasted_iota(jnp.int32, (tb, tb), 1)
        decay = jnp.exp(lg * jnp.abs(r - c).astype(F32))
        dmat_ref[...] = jnp.where(c // CHUNK <= r // CHUNK, decay, 0.0)

    q = q_ref[...]
    key_row = blk * tb + lax.broadcasted_iota(jnp.int32, (1, tb), 1)
    kt = jnp.where(key_row >= front, kt_ref[...], 0.0)
    v = v_ref[...]
    s = _dot(q, kt.astype(BF16)) * dmat_ref[...]
    state = s_ref[...]
    q_dec = jnp.exp(lg * (lax.broadcasted_iota(jnp.int32, (tb, 1), 0) + 1).astype(F32))
    o = _dot(s.astype(BF16), v) + _dot(q, state.astype(BF16)) * q_dec
    k_dec = jnp.exp(lg * (tb - 1 - lax.broadcasted_iota(jnp.int32, (1, tb), 1)).astype(F32))
    new_state = state * jnp.exp(lg * tb) + _dot((kt * k_dec).astype(BF16), v)
    s_ref[...] = new_state
    y_ref[...] = _group_norm_gate(o, g_ref[...], gn_ref[...])

    @pl.when(blk == pl.num_programs(1) - 1)
    def _():
        st_ref[0] = new_state


def _ret_prompt(logg, q_b, k_t, v_b, g, gn, *, front, t_prompt, n_heads):
    t_all, qk_width = q_b.shape
    v_width = v_b.shape[1]
    qk_dim, v_dim = qk_width // n_heads, v_width // n_heads
    tb = RET_BLOCK
    return pl.pallas_call(
        functools.partial(_ret_prompt_kernel, tb=tb, front=front),
        grid=(n_heads, t_prompt // tb),
        in_specs=[pl.BlockSpec(memory_space=pltpu.SMEM),
                  pl.BlockSpec((tb, qk_dim), lambda h, i: (i, h)),
                  pl.BlockSpec((qk_dim, tb), lambda h, i: (h, i)),
                  pl.BlockSpec((tb, v_dim), lambda h, i: (i, h)),
                  pl.BlockSpec((tb, v_dim), lambda h, i: (i, h)),
                  pl.BlockSpec((1, v_dim), lambda h, i: (0, h))],
        out_specs=[pl.BlockSpec((tb, v_dim), lambda h, i: (i, h)),
                   pl.BlockSpec((1, qk_dim, v_dim), lambda h, i: (h, 0, 0))],
        out_shape=[jax.ShapeDtypeStruct((t_prompt, v_width), BF16),
                   jax.ShapeDtypeStruct((n_heads, qk_dim, v_dim), F32)],
        scratch_shapes=[pltpu.VMEM((qk_dim, v_dim), F32), pltpu.VMEM((tb, tb), F32)],
        compiler_params=_params(("arbitrary", "arbitrary")),
        name="ret_prompt",
    )(logg, q_b, k_t, v_b, g, gn)


def _ret_sample_kernel(logg_ref, q_ref, kt_ref, v_ref, g_ref, gn_ref, st_ref, y_ref, nst_ref, *, n_new):
    hh = pl.program_id(0)
    b = pl.program_id(1)
    lg = jnp.full((1, 1), logg_ref[hh], F32)
    n_rows = kt_ref.shape[1]
    r0 = pl.multiple_of(b * n_new, n_new)
    q = q_ref[pl.ds(r0, n_new), :]
    kt = kt_ref[...]
    v = v_ref[...]
    qrow = lax.broadcasted_iota(jnp.int32, (n_new, n_rows), 0)
    col = lax.broadcasted_iota(jnp.int32, (n_new, n_rows), 1) - b * n_new
    own = jnp.logical_and(col >= 0, col < n_new)
    dmat = jnp.where(own, jnp.exp(lg * jnp.abs(qrow - col).astype(F32)), 0.0)
    s = _dot(q, kt.astype(BF16)) * dmat
    state = st_ref[0, 0]
    q_dec = jnp.exp(lg * (lax.broadcasted_iota(jnp.int32, (n_new, 1), 0) + 1).astype(F32))
    o = _dot(s.astype(BF16), v) + _dot(q, state.astype(BF16)) * q_dec
    col1 = lax.broadcasted_iota(jnp.int32, (1, n_rows), 1) - b * n_new
    own1 = jnp.logical_and(col1 >= 0, col1 < n_new)
    k_dec = jnp.where(own1, jnp.exp(lg * (n_new - 1 - col1).astype(F32)), 0.0)
    nst_ref[0, 0] = state * jnp.exp(lg * n_new) + _dot((kt * k_dec).astype(BF16), v)
    y_ref[pl.ds(r0, n_new), :] = _group_norm_gate(o, g_ref[pl.ds(r0, n_new), :], gn_ref[...])


def _ret_sample(logg, q_b, k_t, v_b, g, gn, state, *, n_new):
    t_all, qk_width = q_b.shape
    v_width = v_b.shape[1]
    n_streams, n_heads, qk_dim, v_dim = state.shape
    n_rows = n_streams * n_new
    blk = 0
    return pl.pallas_call(
        functools.partial(_ret_sample_kernel, n_new=n_new),
        grid=(n_heads, n_streams),
        in_specs=[pl.BlockSpec(memory_space=pltpu.SMEM),
                  pl.BlockSpec((n_rows, qk_dim), lambda h, b: (blk, h)),
                  pl.BlockSpec((qk_dim, n_rows), lambda h, b: (h, blk)),
                  pl.BlockSpec((n_rows, v_dim), lambda h, b: (blk, h)),
                  pl.BlockSpec((n_rows, v_dim), lambda h, b: (blk, h)),
                  pl.BlockSpec((1, v_dim), lambda h, b: (0, h)),
                  pl.BlockSpec((1, 1, qk_dim, v_dim), lambda h, b: (b, h, 0, 0))],
        out_specs=[pl.BlockSpec((n_rows, v_dim), lambda h, b: (0, h)),
                   pl.BlockSpec((1, 1, qk_dim, v_dim), lambda h, b: (b, h, 0, 0))],
        out_shape=[jax.ShapeDtypeStruct((n_rows, v_width), BF16),
                   jax.ShapeDtypeStruct(state.shape, F32)],
        compiler_params=_params(("arbitrary", "arbitrary")),
        name="ret_sample",
    )(logg, q_b, k_t, v_b, g, gn, state)


def _top_values(s, k, with_rank):
    work = s
    vals = []
    for r in range(k):
        m = jnp.max(work, axis=0, keepdims=True)
        vals.append(m)
        work = jnp.where(work == m, RANK_MARK * (1.0 + r / 1024.0), work)
    if not with_rank:
        return vals, None
    rank = jnp.where(work <= RANK_MARK, (work * (1.0 / RANK_MARK) - 1.0) * 1024.0, NO_RANK)
    return vals, rank


def _peer_kernel(x_ref, mp_ref, mt_ref, wo_ref, gn_ref, wq_ref, sk_ref, u_ref, vt_ref, o_ref,
                 xn_ref, hb_ref, e0_ref, cnt_ref, e1_ref, rank_ref, w_ref, y_ref,
                 *, n_heads, n_keys, topk):
    c = pl.program_id(1)
    ce, tm = u_ref.shape[0], x_ref.shape[0]
    rows_per_chunk = ce // n_keys
    sub = rank_ref.shape[2]

    @pl.when(c == 0)
    def _():
        n_sample = mt_ref.shape[0]
        mixed = mp_ref[...]
        head_rows = jnp.where(pl.program_id(0) == 0, mt_ref[...], mixed[:n_sample])
        mixed = jnp.concatenate([head_rows, mixed[n_sample:]], axis=0)
        xn = x_ref[...] + _dot(mixed, wo_ref[...])
        xn_ref[...] = xn
        hb = _rmsnorm_rows(xn, gn_ref[...]).astype(BF16)
        hb_ref[...] = hb
        y_ref[...] = jnp.zeros_like(y_ref)
        for h in range(n_heads):
            qh = _dot_nt(wq_ref[h * 2 * n_keys:(h + 1) * 2 * n_keys, :], hb)
            s0 = _dot(sk_ref[2 * h], qh[0:n_keys].astype(BF16))
            s1 = _dot(sk_ref[2 * h + 1], qh[n_keys:2 * n_keys].astype(BF16))
            a, rank0 = _top_values(s0, topk, True)
            b, rank1 = _top_values(s1, topk, True)
            half = topk // 2
            b_all = jnp.concatenate(b, axis=0)
            b_half = jnp.concatenate(b[:half], axis=0)
            cand = jnp.concatenate(
                [a[0] + b_all] + [a[i] + b_half for i in range(1, half)] + [jnp.concatenate(a[half:], axis=0) + b[0]],
                axis=0)
            f, _ = _top_values(cand, topk, False)
            tau = f[topk - 1]
            z = sum(jnp.exp(fk - f[0]) for fk in f)
            a_all = jnp.concatenate(a, axis=0)
            cnt_sorted = sum(jnp.where(a_all + bj >= tau, 1.0, 0.0) for bj in b)
            rank0_b = rank0.astype(BF16).reshape(n_keys // sub, sub, tm)
            cnt = jnp.zeros(rank0_b.shape, BF16)
            for r in range(topk):
                n_r = jnp.broadcast_to(cnt_sorted[r:r + 1, :], (sub, tm)).astype(BF16)
                cnt = jnp.where(rank0_b == float(r), n_r[None], cnt)
            cnt_ref[h] = cnt.reshape(n_keys, tm).astype(F32)
            e0_ref[h] = jnp.exp(s0 - a[0])
            e1_ref[h] = (jnp.exp(s1 - b[0]) * (0.5 / z)).astype(BF16).reshape(n_keys // sub, sub, tm)
            rank_ref[h] = rank1.astype(BF16).reshape(n_keys // sub, sub, tm)

    hb = hb_ref[...]
    edges = PEER_ACT_EDGES + (ce,)
    act_of_row = {}
    for lo_row, hi_row in zip(edges[:-1], edges[1:]):
        piece = _dot_nt(u_ref[lo_row:hi_row, :], hb)
        for r in range(lo_row, hi_row, n_keys):
            act_of_row[r] = (piece, r - lo_row)
    rows_per_slab = MXU_DIM // n_keys
    y_add = None
    for slab in range(ce // MXU_DIM):
        for il in range(slab * rows_per_slab, (slab + 1) * rows_per_slab):
            i = c * rows_per_chunk + il
            gate = None
            for h in range(n_heads):
                n_sel = jnp.broadcast_to(cnt_ref[h, pl.ds(i, 1), :], (sub, tm)).astype(BF16)
                e0 = jnp.broadcast_to(e0_ref[h, pl.ds(i, 1), :], (sub, tm)).astype(BF16)
                term = jnp.where(rank_ref[h] < n_sel[None], e1_ref[h], jnp.zeros((), BF16)) * e0[None]
                gate = term if gate is None else gate + term
            r0 = il * n_keys
            piece, off = act_of_row[r0]
            x_act = piece[off:off + n_keys, :]
            gelu2 = x_act * (1.0 + lax.erf(x_act * (2.0 ** -0.5)))
            w_ref[r0:r0 + n_keys, :] = gate.reshape(n_keys, tm) * gelu2.astype(BF16)
        rows = slice(slab * MXU_DIM, (slab + 1) * MXU_DIM)
        part = _dot(vt_ref[:, rows], w_ref[rows, :])
        y_add = part if y_add is None else y_add + part
    y_ref[...] += y_add

    @pl.when(c == pl.num_programs(1) - 1)
    def _():
        o_ref[...] = xn_ref[...] + y_ref[...].T


def _peer(x, mix_prompt, mix_sample, w_out, gn, wq_t, subkeys, u, v_t, *, layer, n_heads, n_keys):
    t_all, d = x.shape
    k = mix_prompt.shape[1]
    n_sample = mix_sample.shape[0]
    n_experts = u.shape[1]
    tm = ROW_TILE
    ce = PEER_EXPERT_CHUNK
    assert mix_prompt.shape == (t_all, k) and mix_sample.shape == (n_sample, k) and n_sample <= tm
    head_scratch = pltpu.VMEM((n_heads, n_keys, tm), F32)
    head_scratch_b = pltpu.VMEM((n_heads, n_keys // BF16_TILE_ROWS, BF16_TILE_ROWS, tm), BF16)
    return pl.pallas_call(
        functools.partial(_peer_kernel, n_heads=n_heads, n_keys=n_keys, topk=PEER_TOPK),
        grid=(t_all // tm, n_experts // ce),
        in_specs=[pl.BlockSpec((tm, d), lambda i, c: (i, 0)),
                  pl.BlockSpec((tm, k), lambda i, c: (i, 0)),
                  _const_spec((n_sample, k)), _const_spec(w_out.shape),
                  _const_spec((1, d)), _const_spec(wq_t.shape),
                  _const_spec(subkeys.shape),
                  pl.BlockSpec((None, ce, d), lambda i, c: (layer, c, 0)),
                  pl.BlockSpec((None, d, ce), lambda i, c: (layer, 0, c))],
        out_specs=pl.BlockSpec((tm, d), lambda i, c: (i, 0)),
        out_shape=jax.ShapeDtypeStruct((t_all, d), F32),
        scratch_shapes=[pltpu.VMEM((tm, d), F32), pltpu.VMEM((tm, d), BF16),
                        head_scratch, head_scratch, head_scratch_b, head_scratch_b,
                        pltpu.VMEM((ce, tm), BF16), pltpu.VMEM((d, tm), F32)],
        compiler_params=_params(("arbitrary", "arbitrary")),
        name="peer",
    )(x, mix_prompt, mix_sample, w_out, gn, wq_t, subkeys, u, v_t)


def kernel(x_prompt, x_sample, cache_fox_k, cache_fox_v, cache_fox_lf, state_ret, meta_tokens, norm_mix, norm_ffn,
           fox_w_in, fox_b_f, fox_q_norm, fox_k_norm, fox_w_out, ret_w_in, ret_gn, ret_w_out,
           peer_w_q, peer_subkeys, peer_u, peer_v):
    batch, seq, d = x_prompt.shape
    n_streams, n_new, _ = x_sample.shape
    n_meta = meta_tokens.shape[0]
    assert batch == 1 and n_meta == N_META
    depth = norm_mix.shape[0]
    n_fox, _, past, fox_heads, fox_hd = cache_fox_k.shape
    fox_width = fox_heads * fox_hd
    n_ret, _, ret_heads, ret_qk, ret_v = state_ret.shape
    ret_v_width = ret_heads * ret_v
    peer_heads, _, n_keys, peer_half = peer_subkeys.shape[1:]
    assert n_keys == LANES and peer_half == LANES and fox_width == d

    length = n_meta + seq
    n_rows = n_streams * n_new
    front = (-length) % ROW_TILE
    if front < n_rows:
        front += ROW_TILE
    t_all = front + length

    x = jnp.concatenate([x_sample.reshape(n_rows, d), jnp.zeros((front - n_rows, d), F32),
                         meta_tokens.astype(F32), x_prompt[0]], axis=0)

    half = ret_qk // 2
    inv_freq = (ROPE_BASE ** (-jnp.arange(half, dtype=F32) / half)).reshape(1, half)
    logg = jnp.log(1.0 - 2.0 ** (-5.0 - jnp.arange(ret_heads, dtype=F32)))

    cache_k = cache_fox_k.reshape(n_fox, n_streams, past, fox_width)
    cache_v = cache_fox_v.reshape(n_fox, n_streams, past, fox_width)
    cache_lf_t = cache_fox_lf.transpose(0, 1, 3, 2)
    peer_u_b = peer_u.astype(BF16)
    peer_vt_b = peer_v.astype(BF16).transpose(0, 2, 1)

    kp, vp, lfp, ks, vs, lfs, srp, srs = [], [], [], [], [], [], [], []
    for layer in range(depth):
        j = layer // 2
        gn_mix = norm_mix[layer].reshape(1, d)
        if layer % 2 == 0:
            w_in = fox_w_in[j]
            w_main = w_in[:, :4 * fox_width].astype(BF16)
            w_f = jnp.pad(w_in[:, 4 * fox_width:], ((0, 0), (0, LANES - fox_heads))).astype(BF16)
            b_f = jnp.pad(fox_b_f[j], (0, LANES - fox_heads)).reshape(1, LANES)
            gq = jnp.tile(fox_q_norm[j], fox_heads).reshape(1, fox_width)
            gk = jnp.tile(fox_k_norm[j], fox_heads).reshape(1, fox_width)
            q_b, k_f, k_b, v_f, v_b, og, lf, csum = _fox_proj(x, gn_mix, w_main, w_f, b_f, gq, gk,
                                                              front=front, head_dim=fox_hd)
            o_b = _fox_attn_prompt(q_b, k_b, v_b, csum, og, fox_q_norm[j], fox_k_norm[j],
                                   t_prompt=t_all, head_dim=fox_hd)
            lf_new_t = lf[:n_rows, :fox_heads].T
            o_s = _fox_attn_sample(q_b, k_b, v_b, og, lf_new_t, cache_k, cache_v, cache_lf_t,
                                   layer=j, n_new=n_new, head_dim=fox_hd)
            mix_prompt, mix_sample, w_out = o_b, o_s, fox_w_out[j].astype(BF16)
            kp.append(k_f)
            vp.append(v_f)
            lfp.append(lf[front:, :fox_heads].reshape(1, length, fox_heads))
            ks.append(k_f[:n_rows].reshape(n_streams, n_new, fox_heads, fox_hd))
            vs.append(v_f[:n_rows].reshape(n_streams, n_new, fox_heads, fox_hd))
            lfs.append(lf[:n_rows, :fox_heads].reshape(n_streams, n_new, fox_heads))
        else:
            q_b, k_t, v_b, g = _ret_proj(x, gn_mix, ret_w_in[j].astype(BF16), inv_freq,
                                         n_sample=n_rows, pos0=front + n_meta, past=past, n_new=n_new,
                                         n_heads=ret_heads, qk_dim=ret_qk, v_width=ret_v_width)
            gn_ret = ret_gn[j].reshape(1, ret_v_width)
            y_b, st_p = _ret_prompt(logg, q_b, k_t, v_b, g, gn_ret, front=front, t_prompt=t_all, n_heads=ret_heads)
            y_s, st_s = _ret_sample(logg, q_b, k_t, v_b, g, gn_ret, state_ret[j], n_new=n_new)
            mix_prompt, mix_sample, w_out = y_b, y_s, ret_w_out[j].astype(BF16)
            srp.append(st_p[None])
            srs.append(st_s)
        x = _peer(x, mix_prompt, mix_sample, w_out, norm_ffn[layer].reshape(1, d), peer_w_q[layer].T.astype(BF16),
                  peer_subkeys[layer].reshape(2 * peer_heads, n_keys, peer_half).astype(BF16),
                  peer_u_b, peer_vt_b, layer=layer, n_heads=peer_heads, n_keys=n_keys)

    y_prompt = x[front + n_meta:][None]
    y_sample = x[:n_rows].reshape(n_streams, n_new, d)
    prompt_cache = lambda a: jnp.stack(a)[:, front:].reshape(len(a), 1, length, fox_heads, fox_hd)
    return (y_prompt, y_sample, prompt_cache(kp), prompt_cache(vp), jnp.stack(lfp), jnp.stack(srp),
            jnp.stack(ks), jnp.stack(vs), jnp.stack(lfs), jnp.stack(srs))
```

```python
import functools
import math

import jax
import jax.numpy as jnp
from jax import lax
from jax.experimental import pallas as pl
from jax.experimental.pallas import tpu as pltpu

F32 = jnp.float32
BF16 = jnp.bfloat16

EPS = 1e-6
N_META = 16
CHUNK = 64
ROPE_BASE = 10000.0
PEER_TOPK = 16

LANES = 128
MXU_DIM = 256
BF16_TILE_ROWS = 16
VMEM_LIMIT = 56 * 1024 * 1024

ROW_TILE = 512
PROJ_TILE = 256
RET_BLOCK = 512
PEER_EXPERT_CHUNK = 2048
PEER_ACT_EDGES = (0, 1024)
MASKED_KEY = 1e30
NO_RANK = 99.0
RANK_MARK = -(2.0 ** 100)
LOG2E = math.log2(math.e)
ZERO_WEIGHT_LOG2 = 160.0

_NT = (((1,), (1,)), ((), ()))


def _const_spec(shape):
    return pl.BlockSpec(shape, lambda *_: (0,) * len(shape), pipeline_mode=pl.Buffered(1))


def _params(semantics):
    return pltpu.CompilerParams(dimension_semantics=semantics, vmem_limit_bytes=VMEM_LIMIT)


def _split3(x):
    a = x.astype(BF16)
    r = x - a.astype(F32)
    b = r.astype(BF16)
    c = (r - b.astype(F32)).astype(BF16)
    return a, b, c


def _dot(a, b):
    return jnp.dot(a, b, preferred_element_type=F32)


def _dot_nt(a, b):
    return lax.dot_general(a, b, _NT, preferred_element_type=F32)


def _rmsnorm_rows(x, g):
    ms = jnp.mean(x * x, axis=-1, keepdims=True)
    return x * lax.rsqrt(ms + EPS) * g


def _log_sigmoid(x):
    return jnp.minimum(x, 0.0) - jnp.log1p(jnp.exp(-jnp.abs(x)))


def _fox_proj_kernel(x_ref, gn_ref, w_ref, wf_ref, bf_ref, gq_ref, gk_ref, bd_ref, tril_ref,
                     q_ref, kf_ref, kb_ref, vf_ref, vb_ref, og_ref, lf_ref, c_ref, carry_ref,
                     *, tm, front, head_dim, width):
    i = pl.program_id(0)

    @pl.when(i == 0)
    def _():
        carry_ref[...] = jnp.zeros_like(carry_ref)

    h = _rmsnorm_rows(x_ref[...], gn_ref[...]).astype(BF16)

    def head_norm(z, g):
        zz = (z * z).astype(BF16)
        ss = jnp.concatenate(
            [_dot(zz[:, c * MXU_DIM:(c + 1) * MXU_DIM], bd_ref[...]) for c in range(width // MXU_DIM)], axis=1)
        return z * lax.rsqrt(ss * (1.0 / head_dim) + EPS) * g

    zq = _dot(h, w_ref[:, 0:width])
    q_ref[...] = (head_norm(zq, gq_ref[...]) * (head_dim ** -0.5 * LOG2E)).astype(BF16)
    zk = _dot(h, w_ref[:, width:2 * width])
    kn = head_norm(zk, gk_ref[...])
    kf_ref[...] = kn
    kb_ref[...] = kn.astype(BF16)
    zv = _dot(h, w_ref[:, 2 * width:3 * width])
    vf_ref[...] = zv
    vb_ref[...] = zv.astype(BF16)
    og_ref[...] = _dot(h, w_ref[:, 3 * width:4 * width])

    lf = _log_sigmoid(_dot(h, wf_ref[...]) + bf_ref[...])
    lf_ref[...] = lf
    row = i * tm + lax.broadcasted_iota(jnp.int32, (tm, 1), 0)
    real = row >= front
    l1, l2, l3 = _split3(jnp.where(real, lf, 0.0))
    tril = tril_ref[...]
    c = _dot(tril, l1) + _dot(tril, l2) + _dot(tril, l3) + carry_ref[...]
    carry_ref[...] = c[tm - 1:tm, :]
    c_ref[...] = jnp.where(real, c, MASKED_KEY)


def _fox_proj(x, gn, w_main, w_f, b_f, gq, gk, *, front, head_dim):
    t_all, d = x.shape
    width = w_main.shape[1] // 4
    tm = PROJ_TILE
    blk = MXU_DIM // head_dim
    bd = jnp.kron(jnp.eye(blk, dtype=F32), jnp.ones((head_dim, head_dim), F32)).astype(BF16)
    tril = (lax.broadcasted_iota(jnp.int32, (tm, tm), 0) >= lax.broadcasted_iota(jnp.int32, (tm, tm), 1)).astype(BF16)
    row = lambda n: pl.BlockSpec((tm, n), lambda i: (i, 0))
    sds = jax.ShapeDtypeStruct
    return pl.pallas_call(
        functools.partial(_fox_proj_kernel, tm=tm, front=front, head_dim=head_dim, width=width),
        grid=(t_all // tm,),
        in_specs=[row(d), _const_spec((1, d)), _const_spec(w_main.shape), _const_spec(w_f.shape),
                  _const_spec((1, LANES)), _const_spec((1, width)), _const_spec((1, width)),
                  _const_spec(bd.shape), _const_spec(tril.shape)],
        out_specs=[row(width), row(width), row(width), row(width), row(width), row(width), row(LANES), row(LANES)],
        out_shape=[sds((t_all, width), BF16), sds((t_all, width), F32), sds((t_all, width), BF16),
                   sds((t_all, width), F32), sds((t_all, width), BF16), sds((t_all, width), F32),
                   sds((t_all, LANES), F32), sds((t_all, LANES), F32)],
        scratch_shapes=[pltpu.VMEM((1, LANES), F32)],
        compiler_params=_params(("arbitrary",)),
        name="fox_proj",
    )(x, gn, w_main, w_f, b_f, gq, gk, bd, tril)


def _fox_attn_kernel(first_ref, q_ref, k_ref, v_ref, c_ref, og_ref, o_ref,
                     qm_ref, m_ref, alpha_ref, p_ref, acc_ref, pend_ref, *, tile, head_dim):
    qi = pl.program_id(1)
    first = tuple(first_ref[2 * pl.program_id(0) + hh, qi] for hh in range(2))
    lane = lax.broadcasted_iota(jnp.int32, (1, 2 * head_dim), 1)
    lo = lane < head_dim
    head_lanes = (lo, jnp.logical_not(lo))
    q = q_ref[...]
    for hh in range(2):
        qm_ref[hh] = jnp.where(head_lanes[hh], q, jnp.zeros_like(q))
    m_ref[...] = jnp.full(m_ref.shape, -jnp.inf, F32)
    acc_ref[...] = jnp.zeros(acc_ref.shape, F32)
    alpha_ref[1] = jnp.ones(alpha_ref.shape[1:], F32)
    p_ref[1] = jnp.zeros(p_ref.shape[1:], BF16)
    one = jnp.ones((), BF16)
    both = (0, 1)

    def scores(j, slot, heads, diagonal=False):
        k = k_ref[pl.ds(pl.multiple_of(j * tile, tile), tile), :]
        for hh in heads:
            c_last = c_ref[hh, pl.ds(qi, 1), :][:, tile - 1:tile]
            bias = (c_ref[hh, pl.ds(j, 1), :] - c_last) * LOG2E
            u = _dot_nt(qm_ref[hh], k) - bias
            if diagonal:
                r = lax.broadcasted_iota(jnp.int32, (tile, tile), 0)
                cidx = lax.broadcasted_iota(jnp.int32, (tile, tile), 1)
                u = jnp.where(cidx <= r, u, -jnp.inf)
            m_prev = m_ref[hh]
            m_new = jnp.maximum(m_prev, jnp.max(u, axis=1, keepdims=True))
            m_wide = jnp.concatenate([m_new] * (tile // LANES), axis=1)
            p_ref[slot, hh] = jnp.exp2(u - m_wide).astype(BF16)
            alpha_ref[slot, hh] = jnp.exp2(m_prev - m_new)
            m_ref[hh] = m_new

    def accumulate(j, slot, heads):
        v = v_ref[pl.ds(pl.multiple_of(j * tile, tile), tile), :]
        for hh in heads:
            v_aug = jnp.where(head_lanes[hh], v, one)
            acc_ref[hh] = alpha_ref[slot, hh] * acc_ref[hh] + _dot(p_ref[slot, hh], v_aug)

    def block_pairs(start, n_pairs, pending, heads):
        def body(jj, pend):
            j0 = start + 2 * jj
            scores(j0, 0, heads)
            accumulate(pend, 1, heads)
            scores(j0 + 1, 1, heads)
            accumulate(j0, 0, heads)
            return j0 + 1

        return lax.fori_loop(0, n_pairs, body, pending)

    def blocks(start, stop, pending, heads):
        n = stop - start
        odd = lax.rem(n, 2)

        @pl.when(odd == 1)
        def _():
            scores(start, 0, heads)
            accumulate(pending, 1, heads)
            accumulate(start, 0, heads)
            for hh in heads:
                alpha_ref[1, hh] = jnp.ones(alpha_ref.shape[2:], F32)
                p_ref[1, hh] = jnp.zeros(p_ref.shape[2:], BF16)

        return block_pairs(start + odd, n // 2, pending, heads)

    joint = jnp.maximum(first[0], first[1])
    pend_ref[0] = joint
    for hh in both:
        @pl.when(first[hh] < first[1 - hh])
        def _():
            pend_ref[0] = blocks(first[hh], joint, joint, (hh,))

    n_joint = qi - joint
    pending = block_pairs(joint, n_joint // 2, pend_ref[0], both)
    waiting = jnp.logical_or(n_joint >= 2, jnp.abs(first[0] - first[1]) >= 2)
    left_over = lax.rem(n_joint, 2) == 1

    def tail(flush, paired):
        if paired:
            scores(qi - 1, 0, both)
            if flush:
                accumulate(pending, 1, both)
            scores(qi, 1, both, diagonal=True)
            accumulate(qi - 1, 0, both)
            accumulate(qi, 1, both)
        else:
            scores(qi, 0, both, diagonal=True)
            if flush:
                accumulate(pending, 1, both)
            accumulate(qi, 0, both)

    for flush in (False, True):
        for paired in (False, True):
            @pl.when(jnp.logical_and(waiting == flush, left_over == paired))
            def _():
                tail(flush, paired)

    outs = []
    for hh in range(2):
        a = acc_ref[hh]
        outs.append(a / pltpu.roll(a, head_dim, 1))
    o = jnp.where(lo, outs[0], outs[1])
    o_ref[...] = (o * jax.nn.sigmoid(og_ref[...])).astype(BF16)


def _first_needed_block(gq, gk, csum, *, n_heads, t_prompt, head_dim, tile):
    bf16_slack = 1.0 + 2.0 ** -7
    q_norm = math.sqrt(head_dim) * jnp.max(jnp.abs(gq)) * (head_dim ** -0.5 * LOG2E) * bf16_slack
    k_norm = math.sqrt(head_dim) * jnp.max(jnp.abs(gk)) * bf16_slack
    c = csum[:t_prompt, :n_heads]
    c_first, c_last = c[0::tile], c[tile - 1::tile]
    bound = 2.0 * q_norm * k_norm + LOG2E * (c_first[:, None, :] - c_last[None, :, :])
    blk = jnp.arange(c_first.shape[0])
    earlier = blk[None, :, None] < blk[:, None, None]
    skipped = jnp.sum(jnp.logical_and(bound < -ZERO_WEIGHT_LOG2, earlier), axis=1)
    return skipped.T.astype(jnp.int32)


def _fox_attn_prompt(q_b, k_b, v_b, csum, og, gq, gk, *, t_prompt, head_dim):
    t_all, width = q_b.shape
    tile = ROW_TILE
    pair = 2 * head_dim
    n_blk = t_prompt // tile
    n_heads = width // head_dim
    c3 = csum[:t_prompt, :n_heads].T.reshape(n_heads, n_blk, tile)
    first = _first_needed_block(gq, gk, csum, n_heads=n_heads, t_prompt=t_prompt, head_dim=head_dim, tile=tile)
    return pl.pallas_call(
        functools.partial(_fox_attn_kernel, tile=tile, head_dim=head_dim),
        grid=(width // pair, n_blk),
        in_specs=[pl.BlockSpec(memory_space=pltpu.SMEM),
                  pl.BlockSpec((tile, pair), lambda p, i: (i, p)),
                  pl.BlockSpec((t_prompt, pair), lambda p, i: (0, p)),
                  pl.BlockSpec((t_prompt, pair), lambda p, i: (0, p)),
                  pl.BlockSpec((2, n_blk, tile), lambda p, i: (p, 0, 0)),
                  pl.BlockSpec((tile, pair), lambda p, i: (i, p))],
        out_specs=pl.BlockSpec((tile, pair), lambda p, i: (i, p)),
        out_shape=jax.ShapeDtypeStruct((t_prompt, width), BF16),
        scratch_shapes=[pltpu.VMEM((2, tile, pair), BF16), pltpu.VMEM((2, tile, LANES), F32),
                        pltpu.VMEM((2, 2, tile, LANES), F32), pltpu.VMEM((2, 2, tile, tile), BF16),
                        pltpu.VMEM((2, tile, pair), F32), pltpu.SMEM((1,), jnp.int32)],
        compiler_params=_params(("arbitrary", "arbitrary")),
        name="fox_attn_prompt",
    )(first, q_b, k_b, v_b, c3, og)


def _fox_sample_kernel(q_ref, kn_ref, vn_ref, og_ref, lfn_ref, kc_ref, vc_ref, lfc_ref, su_ref, bt_ref,
                       o_ref, *, n_new, head_dim, n_heads):
    b = pl.program_id(0)
    pair = 2 * head_dim
    n_rows = kn_ref.shape[0]

    su = su_ref[...]
    suf = sum(_dot(t, su) for t in _split3(lfc_ref[0])) * LOG2E
    bt = bt_ref[...]
    pre = sum(_dot(t, bt) for t in _split3(lfn_ref[...])) * LOG2E

    r0 = pl.multiple_of(b * n_new, n_new)
    qrow = lax.broadcasted_iota(jnp.int32, (n_new, n_rows), 0)
    col = lax.broadcasted_iota(jnp.int32, (n_new, n_rows), 1)
    visible = jnp.logical_and(col >= b * n_new, col <= b * n_new + qrow)
    lane = lax.broadcasted_iota(jnp.int32, (1, pair), 1)
    lo = lane < head_dim
    head_lanes = (lo, jnp.logical_not(lo))

    for p in range(n_heads // 2):
        cols = slice(p * pair, (p + 1) * pair)
        q = q_ref[pl.ds(r0, n_new), cols]
        k_c = kc_ref[0, :, cols].astype(BF16)
        v_c = vc_ref[0, :, cols].astype(BF16)
        k_n = kn_ref[:, cols]
        v_n = vn_ref[:, cols]
        outs = []
        for hh in range(2):
            head = 2 * p + hh
            qh = jnp.where(head_lanes[hh], q, jnp.zeros_like(q))
            u_c = _dot_nt(qh, k_c) + suf[head:head + 1, :]
            u_n = jnp.where(visible, _dot_nt(qh, k_n) - pre[head:head + 1, :], -jnp.inf)
            m = jnp.maximum(jnp.max(u_c, axis=1, keepdims=True), jnp.max(u_n, axis=1, keepdims=True))
            p_c = jnp.exp2(u_c - m)
            p_n = jnp.exp2(u_n - m)
            den = jnp.sum(p_c, axis=1, keepdims=True) + jnp.sum(p_n, axis=1, keepdims=True)
            outs.append((_dot(p_c.astype(BF16), v_c) + _dot(p_n.astype(BF16), v_n)) / den)
        o = jnp.where(lo, outs[0], outs[1])
        gate = jax.nn.sigmoid(og_ref[pl.ds(r0, n_new), cols])
        o_ref[pl.ds(r0, n_new), cols] = (o * gate).astype(BF16)


def _fox_attn_sample(q_b, k_b, v_b, og, lf_new_t, cache_k, cache_v, cache_lf_t, *, layer, n_new, head_dim):
    t_all, width = q_b.shape
    _, n_streams, past, _ = cache_k.shape
    n_heads = width // head_dim
    n_rows = n_streams * n_new
    assert n_rows % LANES == 0
    su = (lax.broadcasted_iota(jnp.int32, (past, past), 0) > lax.broadcasted_iota(jnp.int32, (past, past), 1)).astype(BF16)
    rr = lax.broadcasted_iota(jnp.int32, (n_rows, n_rows), 0)
    cc = lax.broadcasted_iota(jnp.int32, (n_rows, n_rows), 1)
    bt = jnp.logical_and(rr // n_new == cc // n_new, rr <= cc).astype(BF16)
    rows = pl.BlockSpec((n_rows, width), lambda b: (0, 0), pipeline_mode=pl.Buffered(1))
    return pl.pallas_call(
        functools.partial(_fox_sample_kernel, n_new=n_new, head_dim=head_dim, n_heads=n_heads),
        grid=(n_streams,),
        in_specs=[rows, rows, rows, rows,
                  _const_spec(lf_new_t.shape),
                  pl.BlockSpec((None, 1, past, width), lambda b: (layer, b, 0, 0)),
                  pl.BlockSpec((None, 1, past, width), lambda b: (layer, b, 0, 0)),
                  pl.BlockSpec((None, 1, n_heads, past), lambda b: (layer, b, 0, 0)),
                  _const_spec(su.shape), _const_spec(bt.shape)],
        out_specs=pl.BlockSpec((n_rows, width), lambda b: (0, 0)),
        out_shape=jax.ShapeDtypeStruct((n_rows, width), BF16),
        compiler_params=_params(("arbitrary",)),
        name="fox_attn_sample",
    )(q_b, k_b, v_b, og, lf_new_t, cache_k, cache_v, cache_lf_t, su, bt)


def _ret_proj_kernel(x_ref, gn_ref, w_ref, inv_ref, q_ref, kt_ref, v_ref, g_ref,
                     *, tm, n_sample, pos0, past, n_new, n_heads, qk_dim, v_width):
    i = pl.program_id(0)
    h = _rmsnorm_rows(x_ref[...], gn_ref[...]).astype(BF16)
    row = i * tm + lax.broadcasted_iota(jnp.int32, (tm, 1), 0)
    pos = jnp.where(row < n_sample, past + lax.rem(row, n_new), row - pos0)
    ang = pos.astype(F32) * inv_ref[...]
    cos = jnp.cos(ang)
    sin = jnp.sin(ang)
    half = qk_dim // 2
    qk_width = n_heads * qk_dim

    def rotary(z):
        parts = []
        for hh in range(n_heads):
            z1 = z[:, hh * qk_dim:hh * qk_dim + half]
            z2 = z[:, hh * qk_dim + half:(hh + 1) * qk_dim]
            parts += [z1 * cos - z2 * sin, z1 * sin + z2 * cos]
        return jnp.concatenate(parts, axis=1)

    q = rotary(_dot(h, w_ref[:, 0:qk_width])) * (qk_dim ** -0.5)
    q_ref[...] = q.astype(BF16)
    kt_ref[...] = rotary(_dot(h, w_ref[:, qk_width:2 * qk_width])).T
    v_ref[...] = _dot(h, w_ref[:, 2 * qk_width:2 * qk_width + v_width]).astype(BF16)
    g_ref[...] = _dot(h, w_ref[:, 2 * qk_width + v_width:])


def _ret_proj(x, gn, w_in, inv_freq, *, n_sample, pos0, past, n_new, n_heads, qk_dim, v_width):
    t_all, d = x.shape
    tm = PROJ_TILE
    qk_width = n_heads * qk_dim
    row = lambda n: pl.BlockSpec((tm, n), lambda i: (i, 0))
    sds = jax.ShapeDtypeStruct
    return pl.pallas_call(
        functools.partial(_ret_proj_kernel, tm=tm, n_sample=n_sample, pos0=pos0, past=past, n_new=n_new,
                          n_heads=n_heads, qk_dim=qk_dim, v_width=v_width),
        grid=(t_all // tm,),
        in_specs=[row(d), _const_spec((1, d)), _const_spec(w_in.shape), _const_spec(inv_freq.shape)],
        out_specs=[row(qk_width), pl.BlockSpec((qk_width, tm), lambda i: (0, i)), row(v_width), row(v_width)],
        out_shape=[sds((t_all, qk_width), BF16), sds((qk_width, t_all), F32),
                   sds((t_all, v_width), BF16), sds((t_all, v_width), F32)],
        compiler_params=_params(("arbitrary",)),
        name="ret_proj",
    )(x, gn, w_in, inv_freq)


def _group_norm_gate(o, g, gn):
    mu = jnp.mean(o, axis=-1, keepdims=True)
    var = jnp.mean(jnp.square(o - mu), axis=-1, keepdims=True)
    y = (o - mu) * lax.rsqrt(var + EPS) * gn
    return (g * jax.nn.sigmoid(g) * y).astype(BF16)


def _ret_prompt_kernel(logg_ref, q_ref, kt_ref, v_ref, g_ref, gn_ref, y_ref, st_ref, s_ref, dmat_ref,
                       *, tb, front):
    hh = pl.program_id(0)
    blk = pl.program_id(1)
    lg = jnp.full((1, 1), logg_ref[hh], F32)

    @pl.when(blk == 0)
    def _():
        s_ref[...] = jnp.zeros_like(s_ref)
        r = lax.broadcasted_iota(jnp.int32, (tb, tb), 0)
        c = lax.broadcasted_iota(jnp.int32, (tb, tb), 1)
        decay = jnp.exp(lg * jnp.abs(r - c).astype(F32))
        dmat_ref[...] = jnp.where(c // CHUNK <= r // CHUNK, decay, 0.0)

    q = q_ref[...]
    key_row = blk * tb + lax.broadcasted_iota(jnp.int32, (1, tb), 1)
    kt = jnp.where(key_row >= front, kt_ref[...], 0.0)
    v = v_ref[...]
    s = _dot(q, kt.astype(BF16)) * dmat_ref[...]
    state = s_ref[...]
    q_dec = jnp.exp(lg * (lax.broadcasted_iota(jnp.int32, (tb, 1), 0) + 1).astype(F32))
    o = _dot(s.astype(BF16), v) + _dot(q, state.astype(BF16)) * q_dec
    k_dec = jnp.exp(lg * (tb - 1 - lax.broadcasted_iota(jnp.int32, (1, tb), 1)).astype(F32))
    new_state = state * jnp.exp(lg * tb) + _dot((kt * k_dec).astype(BF16), v)
    s_ref[...] = new_state
    y_ref[...] = _group_norm_gate(o, g_ref[...], gn_ref[...])

    @pl.when(blk == pl.num_programs(1) - 1)
    def _():
        st_ref[0] = new_state


def _ret_prompt(logg, q_b, k_t, v_b, g, gn, *, front, t_prompt, n_heads):
    t_all, qk_width = q_b.shape
    v_width = v_b.shape[1]
    qk_dim, v_dim = qk_width // n_heads, v_width // n_heads
    tb = RET_BLOCK
    return pl.pallas_call(
        functools.partial(_ret_prompt_kernel, tb=tb, front=front),
        grid=(n_heads, t_prompt // tb),
        in_specs=[pl.BlockSpec(memory_space=pltpu.SMEM),
                  pl.BlockSpec((tb, qk_dim), lambda h, i: (i, h)),
                  pl.BlockSpec((qk_dim, tb), lambda h, i: (h, i)),
                  pl.BlockSpec((tb, v_dim), lambda h, i: (i, h)),
                  pl.BlockSpec((tb, v_dim), lambda h, i: (i, h)),
                  pl.BlockSpec((1, v_dim), lambda h, i: (0, h))],
        out_specs=[pl.BlockSpec((tb, v_dim), lambda h, i: (i, h)),
                   pl.BlockSpec((1, qk_dim, v_dim), lambda h, i: (h, 0, 0))],
        out_shape=[jax.ShapeDtypeStruct((t_prompt, v_width), BF16),
                   jax.ShapeDtypeStruct((n_heads, qk_dim, v_dim), F32)],
        scratch_shapes=[pltpu.VMEM((qk_dim, v_dim), F32), pltpu.VMEM((tb, tb), F32)],
        compiler_params=_params(("arbitrary", "arbitrary")),
        name="ret_prompt",
    )(logg, q_b, k_t, v_b, g, gn)


def _ret_sample_kernel(logg_ref, q_ref, kt_ref, v_ref, g_ref, gn_ref, st_ref, y_ref, nst_ref, *, n_new):
    hh = pl.program_id(0)
    b = pl.program_id(1)
    lg = jnp.full((1, 1), logg_ref[hh], F32)
    n_rows = kt_ref.shape[1]
    r0 = pl.multiple_of(b * n_new, n_new)
    q = q_ref[pl.ds(r0, n_new), :]
    kt = kt_ref[...]
    v = v_ref[...]
    qrow = lax.broadcasted_iota(jnp.int32, (n_new, n_rows), 0)
    col = lax.broadcasted_iota(jnp.int32, (n_new, n_rows), 1) - b * n_new
    own = jnp.logical_and(col >= 0, col < n_new)
    dmat = jnp.where(own, jnp.exp(lg * jnp.abs(qrow - col).astype(F32)), 0.0)
    s = _dot(q, kt.astype(BF16)) * dmat
    state = st_ref[0, 0]
    q_dec = jnp.exp(lg * (lax.broadcasted_iota(jnp.int32, (n_new, 1), 0) + 1).astype(F32))
    o = _dot(s.astype(BF16), v) + _dot(q, state.astype(BF16)) * q_dec
    col1 = lax.broadcasted_iota(jnp.int32, (1, n_rows), 1) - b * n_new
    own1 = jnp.logical_and(col1 >= 0, col1 < n_new)
    k_dec = jnp.where(own1, jnp.exp(lg * (n_new - 1 - col1).astype(F32)), 0.0)
    nst_ref[0, 0] = state * jnp.exp(lg * n_new) + _dot((kt * k_dec).astype(BF16), v)
    y_ref[pl.ds(r0, n_new), :] = _group_norm_gate(o, g_ref[pl.ds(r0, n_new), :], gn_ref[...])


def _ret_sample(logg, q_b, k_t, v_b, g, gn, state, *, n_new):
    t_all, qk_width = q_b.shape
    v_width = v_b.shape[1]
    n_streams, n_heads, qk_dim, v_dim = state.shape
    n_rows = n_streams * n_new
    blk = 0
    return pl.pallas_call(
        functools.partial(_ret_sample_kernel, n_new=n_new),
        grid=(n_heads, n_streams),
        in_specs=[pl.BlockSpec(memory_space=pltpu.SMEM),
                  pl.BlockSpec((n_rows, qk_dim), lambda h, b: (blk, h)),
                  pl.BlockSpec((qk_dim, n_rows), lambda h, b: (h, blk)),
                  pl.BlockSpec((n_rows, v_dim), lambda h, b: (blk, h)),
                  pl.BlockSpec((n_rows, v_dim), lambda h, b: (blk, h)),
                  pl.BlockSpec((1, v_dim), lambda h, b: (0, h)),
                  pl.BlockSpec((1, 1, qk_dim, v_dim), lambda h, b: (b, h, 0, 0))],
        out_specs=[pl.BlockSpec((n_rows, v_dim), lambda h, b: (0, h)),
                   pl.BlockSpec((1, 1, qk_dim, v_dim), lambda h, b: (b, h, 0, 0))],
        out_shape=[jax.ShapeDtypeStruct((n_rows, v_width), BF16),
                   jax.ShapeDtypeStruct(state.shape, F32)],
        compiler_params=_params(("arbitrary", "arbitrary")),
        name="ret_sample",
    )(logg, q_b, k_t, v_b, g, gn, state)


def _top_values(s, k, with_rank):
    work = s
    vals = []
    for r in range(k):
        m = jnp.max(work, axis=0, keepdims=True)
        vals.append(m)
        work = jnp.where(work == m, RANK_MARK * (1.0 + r / 1024.0), work)
    if not with_rank:
        return vals, None
    rank = jnp.where(work <= RANK_MARK, (work * (1.0 / RANK_MARK) - 1.0) * 1024.0, NO_RANK)
    return vals, rank


def _peer_kernel(x_ref, mp_ref, mt_ref, wo_ref, gn_ref, wq_ref, sk_ref, u_ref, vt_ref, o_ref,
                 xn_ref, hb_ref, e0_ref, cnt_ref, e1_ref, rank_ref, w_ref, y_ref,
                 *, n_heads, n_keys, topk):
    c = pl.program_id(1)
    ce, tm = u_ref.shape[0], x_ref.shape[0]
    rows_per_chunk = ce // n_keys
    sub = rank_ref.shape[2]

    @pl.when(c == 0)
    def _():
        n_sample = mt_ref.shape[0]
        mixed = mp_ref[...]
        head_rows = jnp.where(pl.program_id(0) == 0, mt_ref[...], mixed[:n_sample])
        mixed = jnp.concatenate([head_rows, mixed[n_sample:]], axis=0)
        xn = x_ref[...] + _dot(mixed, wo_ref[...])
        xn_ref[...] = xn
        hb = _rmsnorm_rows(xn, gn_ref[...]).astype(BF16)
        hb_ref[...] = hb
        y_ref[...] = jnp.zeros_like(y_ref)
        for h in range(n_heads):
            qh = _dot_nt(wq_ref[h * 2 * n_keys:(h + 1) * 2 * n_keys, :], hb)
            s0 = _dot(sk_ref[2 * h], qh[0:n_keys].astype(BF16))
            s1 = _dot(sk_ref[2 * h + 1], qh[n_keys:2 * n_keys].astype(BF16))
            a, rank0 = _top_values(s0, topk, True)
            b, rank1 = _top_values(s1, topk, True)
            half = topk // 2
            b_all = jnp.concatenate(b, axis=0)
            b_half = jnp.concatenate(b[:half], axis=0)
            cand = jnp.concatenate(
                [a[0] + b_all] + [a[i] + b_half for i in range(1, half)] + [jnp.concatenate(a[half:], axis=0) + b[0]],
                axis=0)
            f, _ = _top_values(cand, topk, False)
            tau = f[topk - 1]
            z = sum(jnp.exp(fk - f[0]) for fk in f)
            a_all = jnp.concatenate(a, axis=0)
            cnt_sorted = sum(jnp.where(a_all + bj >= tau, 1.0, 0.0) for bj in b)
            rank0_b = rank0.astype(BF16).reshape(n_keys // sub, sub, tm)
            cnt = jnp.zeros(rank0_b.shape, BF16)
            for r in range(topk):
                n_r = jnp.broadcast_to(cnt_sorted[r:r + 1, :], (sub, tm)).astype(BF16)
                cnt = jnp.where(rank0_b == float(r), n_r[None], cnt)
            cnt_ref[h] = cnt.reshape(n_keys, tm).astype(F32)
            e0_ref[h] = jnp.exp(s0 - a[0])
            e1_ref[h] = (jnp.exp(s1 - b[0]) * (0.5 / z)).astype(BF16).reshape(n_keys // sub, sub, tm)
            rank_ref[h] = rank1.astype(BF16).reshape(n_keys // sub, sub, tm)

    hb = hb_ref[...]
    edges = PEER_ACT_EDGES + (ce,)
    act_of_row = {}
    for lo_row, hi_row in zip(edges[:-1], edges[1:]):
        piece = _dot_nt(u_ref[lo_row:hi_row, :], hb)
        for r in range(lo_row, hi_row, n_keys):
            act_of_row[r] = (piece, r - lo_row)
    rows_per_slab = MXU_DIM // n_keys
    y_add = None
    for slab in range(ce // MXU_DIM):
        for il in range(slab * rows_per_slab, (slab + 1) * rows_per_slab):
            i = c * rows_per_chunk + il
            gate = None
            for h in range(n_heads):
                n_sel = jnp.broadcast_to(cnt_ref[h, pl.ds(i, 1), :], (sub, tm)).astype(BF16)
                e0 = jnp.broadcast_to(e0_ref[h, pl.ds(i, 1), :], (sub, tm)).astype(BF16)
                term = jnp.where(rank_ref[h] < n_sel[None], e1_ref[h], jnp.zeros((), BF16)) * e0[None]
                gate = term if gate is None else gate + term
            r0 = il * n_keys
            piece, off = act_of_row[r0]
            x_act = piece[off:off + n_keys, :]
            gelu2 = x_act * (1.0 + lax.erf(x_act * (2.0 ** -0.5)))
            w_ref[r0:r0 + n_keys, :] = gate.reshape(n_keys, tm) * gelu2.astype(BF16)
        rows = slice(slab * MXU_DIM, (slab + 1) * MXU_DIM)
        part = _dot(vt_ref[:, rows], w_ref[rows, :])
        y_add = part if y_add is None else y_add + part
    y_ref[...] += y_add

    @pl.when(c == pl.num_programs(1) - 1)
    def _():
        o_ref[...] = xn_ref[...] + y_ref[...].T


def _peer(x, mix_prompt, mix_sample, w_out, gn, wq_t, subkeys, u, v_t, *, layer, n_heads, n_keys):
    t_all, d = x.shape
    k = mix_prompt.shape[1]
    n_sample = mix_sample.shape[0]
    n_experts = u.shape[1]
    tm = ROW_TILE
    ce = PEER_EXPERT_CHUNK
    assert mix_prompt.shape == (t_all, k) and mix_sample.shape == (n_sample, k) and n_sample <= tm
    head_scratch = pltpu.VMEM((n_heads, n_keys, tm), F32)
    head_scratch_b = pltpu.VMEM((n_heads, n_keys // BF16_TILE_ROWS, BF16_TILE_ROWS, tm), BF16)
    return pl.pallas_call(
        functools.partial(_peer_kernel, n_heads=n_heads, n_keys=n_keys, topk=PEER_TOPK),
        grid=(t_all // tm, n_experts // ce),
        in_specs=[pl.BlockSpec((tm, d), lambda i, c: (i, 0)),
                  pl.BlockSpec((tm, k), lambda i, c: (i, 0)),
                  _const_spec((n_sample, k)), _const_spec(w_out.shape),
                  _const_spec((1, d)), _const_spec(wq_t.shape),
                  _const_spec(subkeys.shape),
                  pl.BlockSpec((None, ce, d), lambda i, c: (layer, c, 0)),
                  pl.BlockSpec((None, d, ce), lambda i, c: (layer, 0, c))],
        out_specs=pl.BlockSpec((tm, d), lambda i, c: (i, 0)),
        out_shape=jax.ShapeDtypeStruct((t_all, d), F32),
        scratch_shapes=[pltpu.VMEM((tm, d), F32), pltpu.VMEM((tm, d), BF16),
                        head_scratch, head_scratch, head_scratch_b, head_scratch_b,
                        pltpu.VMEM((ce, tm), BF16), pltpu.VMEM((d, tm), F32)],
        compiler_params=_params(("arbitrary", "arbitrary")),
        name="peer",
    )(x, mix_prompt, mix_sample, w_out, gn, wq_t, subkeys, u, v_t)


def kernel(x_prompt, x_sample, cache_fox_k, cache_fox_v, cache_fox_lf, state_ret, meta_tokens, norm_mix, norm_ffn,
           fox_w_in, fox_b_f, fox_q_norm, fox_k_norm, fox_w_out, ret_w_in, ret_gn, ret_w_out,
           peer_w_q, peer_subkeys, peer_u, peer_v):
    batch, seq, d = x_prompt.shape
    n_streams, n_new, _ = x_sample.shape
    n_meta = meta_tokens.shape[0]
    assert batch == 1 and n_meta == N_META
    depth = norm_mix.shape[0]
    n_fox, _, past, fox_heads, fox_hd = cache_fox_k.shape
    fox_width = fox_heads * fox_hd
    n_ret, _, ret_heads, ret_qk, ret_v = state_ret.shape
    ret_v_width = ret_heads * ret_v
    peer_heads, _, n_keys, peer_half = peer_subkeys.shape[1:]
    assert n_keys == LANES and peer_half == LANES and fox_width == d

    length = n_meta + seq
    n_rows = n_streams * n_new
    front = (-length) % ROW_TILE
    if front < n_rows:
        front += ROW_TILE
    t_all = front + length

    x = jnp.concatenate([x_sample.reshape(n_rows, d), jnp.zeros((front - n_rows, d), F32),
                         meta_tokens.astype(F32), x_prompt[0]], axis=0)

    half = ret_qk // 2
    inv_freq = (ROPE_BASE ** (-jnp.arange(half, dtype=F32) / half)).reshape(1, half)
    logg = jnp.log(1.0 - 2.0 ** (-5.0 - jnp.arange(ret_heads, dtype=F32)))

    cache_k = cache_fox_k.reshape(n_fox, n_streams, past, fox_width)
    cache_v = cache_fox_v.reshape(n_fox, n_streams, past, fox_width)
    cache_lf_t = cache_fox_lf.transpose(0, 1, 3, 2)
    peer_u_b = peer_u.astype(BF16)
    peer_vt_b = peer_v.astype(BF16).transpose(0, 2, 1)

    kp, vp, lfp, ks, vs, lfs, srp, srs = [], [], [], [], [], [], [], []
    for layer in range(depth):
        j = layer // 2
        gn_mix = norm_mix[layer].reshape(1, d)
        if layer % 2 == 0:
            w_in = fox_w_in[j]
            w_main = w_in[:, :4 * fox_width].astype(BF16)
            w_f = jnp.pad(w_in[:, 4 * fox_width:], ((0, 0), (0, LANES - fox_heads))).astype(BF16)
            b_f = jnp.pad(fox_b_f[j], (0, LANES - fox_heads)).reshape(1, LANES)
            gq = jnp.tile(fox_q_norm[j], fox_heads).reshape(1, fox_width)
            gk = jnp.tile(fox_k_norm[j], fox_heads).reshape(1, fox_width)
            q_b, k_f, k_b, v_f, v_b, og, lf, csum = _fox_proj(x, gn_mix, w_main, w_f, b_f, gq, gk,
                                                              front=front, head_dim=fox_hd)
            o_b = _fox_attn_prompt(q_b, k_b, v_b, csum, og, fox_q_norm[j], fox_k_norm[j],
                                   t_prompt=t_all, head_dim=fox_hd)
            lf_new_t = lf[:n_rows, :fox_heads].T
            o_s = _fox_attn_sample(q_b, k_b, v_b, og, lf_new_t, cache_k, cache_v, cache_lf_t,
                                   layer=j, n_new=n_new, head_dim=fox_hd)
            mix_prompt, mix_sample, w_out = o_b, o_s, fox_w_out[j].astype(BF16)
            kp.append(k_f)
            vp.append(v_f)
            lfp.append(lf[front:, :fox_heads].reshape(1, length, fox_heads))
            ks.append(k_f[:n_rows].reshape(n_streams, n_new, fox_heads, fox_hd))
            vs.append(v_f[:n_rows].reshape(n_streams, n_new, fox_heads, fox_hd))
            lfs.append(lf[:n_rows, :fox_heads].reshape(n_streams, n_new, fox_heads))
        else:
            q_b, k_t, v_b, g = _ret_proj(x, gn_mix, ret_w_in[j].astype(BF16), inv_freq,
                                         n_sample=n_rows, pos0=front + n_meta, past=past, n_new=n_new,
                                         n_heads=ret_heads, qk_dim=ret_qk, v_width=ret_v_width)
            gn_ret = ret_gn[j].reshape(1, ret_v_width)
            y_b, st_p = _ret_prompt(logg, q_b, k_t, v_b, g, gn_ret, front=front, t_prompt=t_all, n_heads=ret_heads)
            y_s, st_s = _ret_sample(logg, q_b, k_t, v_b, g, gn_ret, state_ret[j], n_new=n_new)
            mix_prompt, mix_sample, w_out = y_b, y_s, ret_w_out[j].astype(BF16)
            srp.append(st_p[None])
            srs.append(st_s)
        x = _peer(x, mix_prompt, mix_sample, w_out, norm_ffn[layer].reshape(1, d), peer_w_q[layer].T.astype(BF16),
                  peer_subkeys[layer].reshape(2 * peer_heads, n_keys, peer_half).astype(BF16),
                  peer_u_b, peer_vt_b, layer=layer, n_heads=peer_heads, n_keys=n_keys)

    y_prompt = x[front + n_meta:][None]
    y_sample = x[:n_rows].reshape(n_streams, n_new, d)
    prompt_cache = lambda a: jnp.stack(a)[:, front:].reshape(len(a), 1, length, fox_heads, fox_hd)
    return (y_prompt, y_sample, prompt_cache(kp), prompt_cache(vp), jnp.stack(lfp), jnp.stack(srp),
            jnp.stack(ks), jnp.stack(vs), jnp.stack(lfs), jnp.stack(srs))
```

```python
import functools
import math

import jax
import jax.numpy as jnp
from jax import lax
from jax.experimental import pallas as pl
from jax.experimental.pallas import tpu as pltpu

F32 = jnp.float32
BF16 = jnp.bfloat16

EPS = 1e-6
N_META = 16
CHUNK = 64
ROPE_BASE = 10000.0
PEER_TOPK = 16

LANES = 128
MXU_DIM = 256
BF16_TILE_ROWS = 16
VMEM_LIMIT = 56 * 1024 * 1024

ROW_TILE = 512
PROJ_TILE = 256
RET_BLOCK = 512
PEER_EXPERT_CHUNK = 2048
PEER_ACT_EDGES = (0, 1024)
TOPK_SLAB = 256
MASKED_KEY = 1e30
NO_RANK = 99.0
RANK_MARK = -(2.0 ** 100)
LOG2E = math.log2(math.e)
ZERO_WEIGHT_LOG2 = 160.0

_NT = (((1,), (1,)), ((), ()))


def _const_spec(shape):
    return pl.BlockSpec(shape, lambda *_: (0,) * len(shape), pipeline_mode=pl.Buffered(1))


def _params(semantics):
    return pltpu.CompilerParams(dimension_semantics=semantics, vmem_limit_bytes=VMEM_LIMIT)


def _split3(x):
    a = x.astype(BF16)
    r = x - a.astype(F32)
    b = r.astype(BF16)
    c = (r - b.astype(F32)).astype(BF16)
    return a, b, c


def _dot(a, b):
    return jnp.dot(a, b, preferred_element_type=F32)


def _dot_nt(a, b):
    return lax.dot_general(a, b, _NT, preferred_element_type=F32)


def _rmsnorm_rows(x, g):
    ms = jnp.mean(x * x, axis=-1, keepdims=True)
    return x * lax.rsqrt(ms + EPS) * g


def _log_sigmoid(x):
    return jnp.minimum(x, 0.0) - jnp.log1p(jnp.exp(-jnp.abs(x)))


def _fox_proj_kernel(x_ref, gn_ref, w_ref, wf_ref, bf_ref, gq_ref, gk_ref, bd_ref, tril_ref,
                     q_ref, kf_ref, kb_ref, vf_ref, vb_ref, og_ref, lf_ref, c_ref, carry_ref,
                     *, tm, front, head_dim, width):
    i = pl.program_id(0)

    @pl.when(i == 0)
    def _():
        carry_ref[...] = jnp.zeros_like(carry_ref)

    h = _rmsnorm_rows(x_ref[...], gn_ref[...]).astype(BF16)

    def head_norm(z, g):
        zz = (z * z).astype(BF16)
        ss = jnp.concatenate(
            [_dot(zz[:, c * MXU_DIM:(c + 1) * MXU_DIM], bd_ref[...]) for c in range(width // MXU_DIM)], axis=1)
        return z * lax.rsqrt(ss * (1.0 / head_dim) + EPS) * g

    zq = _dot(h, w_ref[:, 0:width])
    q_ref[...] = (head_norm(zq, gq_ref[...]) * (head_dim ** -0.5 * LOG2E)).astype(BF16)
    zk = _dot(h, w_ref[:, width:2 * width])
    kn = head_norm(zk, gk_ref[...])
    kf_ref[...] = kn
    kb_ref[...] = kn.astype(BF16)
    zv = _dot(h, w_ref[:, 2 * width:3 * width])
    vf_ref[...] = zv
    vb_ref[...] = zv.astype(BF16)
    og_ref[...] = _dot(h, w_ref[:, 3 * width:4 * width])

    lf = _log_sigmoid(_dot(h, wf_ref[...]) + bf_ref[...])
    lf_ref[...] = lf
    row = i * tm + lax.broadcasted_iota(jnp.int32, (tm, 1), 0)
    real = row >= front
    l1, l2, l3 = _split3(jnp.where(real, lf, 0.0))
    tril = tril_ref[...]
    c = _dot(tril, l1) + _dot(tril, l2) + _dot(tril, l3) + carry_ref[...]
    carry_ref[...] = c[tm - 1:tm, :]
    c_ref[...] = jnp.where(real, c, MASKED_KEY)


def _fox_proj(x, gn, w_main, w_f, b_f, gq, gk, *, front, head_dim):
    t_all, d = x.shape
    width = w_main.shape[1] // 4
    tm = PROJ_TILE
    blk = MXU_DIM // head_dim
    bd = jnp.kron(jnp.eye(blk, dtype=F32), jnp.ones((head_dim, head_dim), F32)).astype(BF16)
    tril = (lax.broadcasted_iota(jnp.int32, (tm, tm), 0) >= lax.broadcasted_iota(jnp.int32, (tm, tm), 1)).astype(BF16)
    row = lambda n: pl.BlockSpec((tm, n), lambda i: (i, 0))
    sds = jax.ShapeDtypeStruct
    return pl.pallas_call(
        functools.partial(_fox_proj_kernel, tm=tm, front=front, head_dim=head_dim, width=width),
        grid=(t_all // tm,),
        in_specs=[row(d), _const_spec((1, d)), _const_spec(w_main.shape), _const_spec(w_f.shape),
                  _const_spec((1, LANES)), _const_spec((1, width)), _const_spec((1, width)),
                  _const_spec(bd.shape), _const_spec(tril.shape)],
        out_specs=[row(width), row(width), row(width), row(width), row(width), row(width), row(LANES), row(LANES)],
        out_shape=[sds((t_all, width), BF16), sds((t_all, width), F32), sds((t_all, width), BF16),
                   sds((t_all, width), F32), sds((t_all, width), BF16), sds((t_all, width), F32),
                   sds((t_all, LANES), F32), sds((t_all, LANES), F32)],
        scratch_shapes=[pltpu.VMEM((1, LANES), F32)],
        compiler_params=_params(("arbitrary",)),
        name="fox_proj",
    )(x, gn, w_main, w_f, b_f, gq, gk, bd, tril)


def _fox_attn_kernel(first_ref, q_ref, k_ref, v_ref, c_ref, og_ref, o_ref,
                     qm_ref, m_ref, alpha_ref, p_ref, acc_ref, pend_ref, *, tile, head_dim):
    qi = pl.program_id(1)
    first = tuple(first_ref[2 * pl.program_id(0) + hh, qi] for hh in range(2))
    lane = lax.broadcasted_iota(jnp.int32, (1, 2 * head_dim), 1)
    lo = lane < head_dim
    head_lanes = (lo, jnp.logical_not(lo))
    q = q_ref[...]
    for hh in range(2):
        qm_ref[hh] = jnp.where(head_lanes[hh], q, jnp.zeros_like(q))
    m_ref[...] = jnp.full(m_ref.shape, -jnp.inf, F32)
    acc_ref[...] = jnp.zeros(acc_ref.shape, F32)
    alpha_ref[1] = jnp.ones(alpha_ref.shape[1:], F32)
    p_ref[1] = jnp.zeros(p_ref.shape[1:], BF16)
    one = jnp.ones((), BF16)
    both = (0, 1)

    def scores(j, slot, heads, diagonal=False):
        k = k_ref[pl.ds(pl.multiple_of(j * tile, tile), tile), :]
        for hh in heads:
            c_last = c_ref[hh, pl.ds(qi, 1), :][:, tile - 1:tile]
            bias = (c_ref[hh, pl.ds(j, 1), :] - c_last) * LOG2E
            u = _dot_nt(qm_ref[hh], k) - bias
            if diagonal:
                r = lax.broadcasted_iota(jnp.int32, (tile, tile), 0)
                cidx = lax.broadcasted_iota(jnp.int32, (tile, tile), 1)
                u = jnp.where(cidx <= r, u, -jnp.inf)
            m_prev = m_ref[hh]
            m_new = jnp.maximum(m_prev, jnp.max(u, axis=1, keepdims=True))
            m_wide = jnp.concatenate([m_new] * (tile // LANES), axis=1)
            p_ref[slot, hh] = jnp.exp2(u - m_wide).astype(BF16)
            alpha_ref[slot, hh] = jnp.exp2(m_prev - m_new)
            m_ref[hh] = m_new

    def accumulate(j, slot, heads):
        v = v_ref[pl.ds(pl.multiple_of(j * tile, tile), tile), :]
        for hh in heads:
            v_aug = jnp.where(head_lanes[hh], v, one)
            acc_ref[hh] = alpha_ref[slot, hh] * acc_ref[hh] + _dot(p_ref[slot, hh], v_aug)

    def block_pairs(start, n_pairs, pending, heads):
        def body(jj, pend):
            j0 = start + 2 * jj
            scores(j0, 0, heads)
            accumulate(pend, 1, heads)
            scores(j0 + 1, 1, heads)
            accumulate(j0, 0, heads)
            return j0 + 1

        return lax.fori_loop(0, n_pairs, body, pending)

    def blocks(start, stop, pending, heads):
        n = stop - start
        odd = lax.rem(n, 2)

        @pl.when(odd == 1)
        def _():
            scores(start, 0, heads)
            accumulate(pending, 1, heads)
            accumulate(start, 0, heads)
            for hh in heads:
                alpha_ref[1, hh] = jnp.ones(alpha_ref.shape[2:], F32)
                p_ref[1, hh] = jnp.zeros(p_ref.shape[2:], BF16)

        return block_pairs(start + odd, n // 2, pending, heads)

    joint = jnp.maximum(first[0], first[1])
    pend_ref[0] = joint
    for hh in both:
        @pl.when(first[hh] < first[1 - hh])
        def _():
            pend_ref[0] = blocks(first[hh], joint, joint, (hh,))

    n_joint = qi - joint
    pending = block_pairs(joint, n_joint // 2, pend_ref[0], both)

    @pl.when(lax.rem(n_joint, 2) == 0)
    def _():
        scores(qi, 0, both, diagonal=True)
        accumulate(pending, 1, both)
        accumulate(qi, 0, both)

    @pl.when(lax.rem(n_joint, 2) == 1)
    def _():
        scores(qi - 1, 0, both)
        accumulate(pending, 1, both)
        scores(qi, 1, both, diagonal=True)
        accumulate(qi - 1, 0, both)
        accumulate(qi, 1, both)

    outs = []
    for hh in range(2):
        a = acc_ref[hh]
        outs.append(a / pltpu.roll(a, head_dim, 1))
    o = jnp.where(lo, outs[0], outs[1])
    o_ref[...] = (o * jax.nn.sigmoid(og_ref[...])).astype(BF16)


def _first_needed_block(gq, gk, csum, *, n_heads, t_prompt, head_dim, tile):
    bf16_slack = 1.0 + 2.0 ** -7
    q_norm = math.sqrt(head_dim) * jnp.max(jnp.abs(gq)) * (head_dim ** -0.5 * LOG2E) * bf16_slack
    k_norm = math.sqrt(head_dim) * jnp.max(jnp.abs(gk)) * bf16_slack
    c = csum[:t_prompt, :n_heads]
    c_first, c_last = c[0::tile], c[tile - 1::tile]
    bound = 2.0 * q_norm * k_norm + LOG2E * (c_first[:, None, :] - c_last[None, :, :])
    blk = jnp.arange(c_first.shape[0])
    earlier = blk[None, :, None] < blk[:, None, None]
    skipped = jnp.sum(jnp.logical_and(bound < -ZERO_WEIGHT_LOG2, earlier), axis=1)
    return skipped.T.astype(jnp.int32)


def _fox_attn_prompt(q_b, k_b, v_b, csum, og, gq, gk, *, t_prompt, head_dim):
    t_all, width = q_b.shape
    tile = ROW_TILE
    pair = 2 * head_dim
    n_blk = t_prompt // tile
    n_heads = width // head_dim
    c3 = csum[:t_prompt, :n_heads].T.reshape(n_heads, n_blk, tile)
    first = _first_needed_block(gq, gk, csum, n_heads=n_heads, t_prompt=t_prompt, head_dim=head_dim, tile=tile)
    return pl.pallas_call(
        functools.partial(_fox_attn_kernel, tile=tile, head_dim=head_dim),
        grid=(width // pair, n_blk),
        in_specs=[pl.BlockSpec(memory_space=pltpu.SMEM),
                  pl.BlockSpec((tile, pair), lambda p, i: (i, p)),
                  pl.BlockSpec((t_prompt, pair), lambda p, i: (0, p)),
                  pl.BlockSpec((t_prompt, pair), lambda p, i: (0, p)),
                  pl.BlockSpec((2, n_blk, tile), lambda p, i: (p, 0, 0)),
                  pl.BlockSpec((tile, pair), lambda p, i: (i, p))],
        out_specs=pl.BlockSpec((tile, pair), lambda p, i: (i, p)),
        out_shape=jax.ShapeDtypeStruct((t_prompt, width), BF16),
        scratch_shapes=[pltpu.VMEM((2, tile, pair), BF16), pltpu.VMEM((2, tile, LANES), F32),
                        pltpu.VMEM((2, 2, tile, LANES), F32), pltpu.VMEM((2, 2, tile, tile), BF16),
                        pltpu.VMEM((2, tile, pair), F32), pltpu.SMEM((1,), jnp.int32)],
        compiler_params=_params(("arbitrary", "arbitrary")),
        name="fox_attn_prompt",
    )(first, q_b, k_b, v_b, c3, og)


def _fox_sample_kernel(q_ref, kn_ref, vn_ref, og_ref, lfn_ref, kc_ref, vc_ref, lfc_ref, su_ref, bt_ref,
                       o_ref, *, n_new, head_dim, n_heads):
    b = pl.program_id(0)
    pair = 2 * head_dim
    n_rows = kn_ref.shape[0]

    su = su_ref[...]
    suf = sum(_dot(t, su) for t in _split3(lfc_ref[0])) * LOG2E
    bt = bt_ref[...]
    pre = sum(_dot(t, bt) for t in _split3(lfn_ref[...])) * LOG2E

    r0 = pl.multiple_of(b * n_new, n_new)
    qrow = lax.broadcasted_iota(jnp.int32, (n_new, n_rows), 0)
    col = lax.broadcasted_iota(jnp.int32, (n_new, n_rows), 1)
    visible = jnp.logical_and(col >= b * n_new, col <= b * n_new + qrow)
    lane = lax.broadcasted_iota(jnp.int32, (1, pair), 1)
    lo = lane < head_dim
    head_lanes = (lo, jnp.logical_not(lo))

    for p in range(n_heads // 2):
        cols = slice(p * pair, (p + 1) * pair)
        q = q_ref[pl.ds(r0, n_new), cols]
        k_c = kc_ref[0, :, cols].astype(BF16)
        v_c = vc_ref[0, :, cols].astype(BF16)
        k_n = kn_ref[:, cols]
        v_n = vn_ref[:, cols]
        outs = []
        for hh in range(2):
            head = 2 * p + hh
            qh = jnp.where(head_lanes[hh], q, jnp.zeros_like(q))
            u_c = _dot_nt(qh, k_c) + suf[head:head + 1, :]
            u_n = jnp.where(visible, _dot_nt(qh, k_n) - pre[head:head + 1, :], -jnp.inf)
            m = jnp.maximum(jnp.max(u_c, axis=1, keepdims=True), jnp.max(u_n, axis=1, keepdims=True))
            p_c = jnp.exp2(u_c - m)
            p_n = jnp.exp2(u_n - m)
            den = jnp.sum(p_c, axis=1, keepdims=True) + jnp.sum(p_n, axis=1, keepdims=True)
            outs.append((_dot(p_c.astype(BF16), v_c) + _dot(p_n.astype(BF16), v_n)) / den)
        o = jnp.where(lo, outs[0], outs[1])
        gate = jax.nn.sigmoid(og_ref[pl.ds(r0, n_new), cols])
        o_ref[pl.ds(r0, n_new), cols] = (o * gate).astype(BF16)


def _fox_attn_sample(q_b, k_b, v_b, og, lf_new_t, cache_k, cache_v, cache_lf_t, *, layer, n_new, head_dim):
    t_all, width = q_b.shape
    _, n_streams, past, _ = cache_k.shape
    n_heads = width // head_dim
    n_rows = n_streams * n_new
    assert n_rows % LANES == 0
    su = (lax.broadcasted_iota(jnp.int32, (past, past), 0) > lax.broadcasted_iota(jnp.int32, (past, past), 1)).astype(BF16)
    rr = lax.broadcasted_iota(jnp.int32, (n_rows, n_rows), 0)
    cc = lax.broadcasted_iota(jnp.int32, (n_rows, n_rows), 1)
    bt = jnp.logical_and(rr // n_new == cc // n_new, rr <= cc).astype(BF16)
    rows = pl.BlockSpec((n_rows, width), lambda b: (0, 0), pipeline_mode=pl.Buffered(1))
    return pl.pallas_call(
        functools.partial(_fox_sample_kernel, n_new=n_new, head_dim=head_dim, n_heads=n_heads),
        grid=(n_streams,),
        in_specs=[rows, rows, rows, rows,
                  _const_spec(lf_new_t.shape),
                  pl.BlockSpec((None, 1, past, width), lambda b: (layer, b, 0, 0)),
                  pl.BlockSpec((None, 1, past, width), lambda b: (layer, b, 0, 0)),
                  pl.BlockSpec((None, 1, n_heads, past), lambda b: (layer, b, 0, 0)),
                  _const_spec(su.shape), _const_spec(bt.shape)],
        out_specs=pl.BlockSpec((n_rows, width), lambda b: (0, 0)),
        out_shape=jax.ShapeDtypeStruct((n_rows, width), BF16),
        compiler_params=_params(("arbitrary",)),
        name="fox_attn_sample",
    )(q_b, k_b, v_b, og, lf_new_t, cache_k, cache_v, cache_lf_t, su, bt)


def _ret_proj_kernel(x_ref, gn_ref, w_ref, inv_ref, q_ref, kt_ref, v_ref, g_ref,
                     *, tm, n_sample, pos0, past, n_new, n_heads, qk_dim, v_width):
    i = pl.program_id(0)
    h = _rmsnorm_rows(x_ref[...], gn_ref[...]).astype(BF16)
    row = i * tm + lax.broadcasted_iota(jnp.int32, (tm, 1), 0)
    pos = jnp.where(row < n_sample, past + lax.rem(row, n_new), row - pos0)
    ang = pos.astype(F32) * inv_ref[...]
    cos = jnp.cos(ang)
    sin = jnp.sin(ang)
    half = qk_dim // 2
    qk_width = n_heads * qk_dim

    def rotary(z):
        parts = []
        for hh in range(n_heads):
            z1 = z[:, hh * qk_dim:hh * qk_dim + half]
            z2 = z[:, hh * qk_dim + half:(hh + 1) * qk_dim]
            parts += [z1 * cos - z2 * sin, z1 * sin + z2 * cos]
        return jnp.concatenate(parts, axis=1)

    q = rotary(_dot(h, w_ref[:, 0:qk_width])) * (qk_dim ** -0.5)
    q_ref[...] = q.astype(BF16)
    kt_ref[...] = rotary(_dot(h, w_ref[:, qk_width:2 * qk_width])).T
    v_ref[...] = _dot(h, w_ref[:, 2 * qk_width:2 * qk_width + v_width]).astype(BF16)
    g_ref[...] = _dot(h, w_ref[:, 2 * qk_width + v_width:])


def _ret_proj(x, gn, w_in, inv_freq, *, n_sample, pos0, past, n_new, n_heads, qk_dim, v_width):
    t_all, d = x.shape
    tm = PROJ_TILE
    qk_width = n_heads * qk_dim
    row = lambda n: pl.BlockSpec((tm, n), lambda i: (i, 0))
    sds = jax.ShapeDtypeStruct
    return pl.pallas_call(
        functools.partial(_ret_proj_kernel, tm=tm, n_sample=n_sample, pos0=pos0, past=past, n_new=n_new,
                          n_heads=n_heads, qk_dim=qk_dim, v_width=v_width),
        grid=(t_all // tm,),
        in_specs=[row(d), _const_spec((1, d)), _const_spec(w_in.shape), _const_spec(inv_freq.shape)],
        out_specs=[row(qk_width), pl.BlockSpec((qk_width, tm), lambda i: (0, i)), row(v_width), row(v_width)],
        out_shape=[sds((t_all, qk_width), BF16), sds((qk_width, t_all), F32),
                   sds((t_all, v_width), BF16), sds((t_all, v_width), F32)],
        compiler_params=_params(("arbitrary",)),
        name="ret_proj",
    )(x, gn, w_in, inv_freq)


def _group_norm_gate(o, g, gn):
    mu = jnp.mean(o, axis=-1, keepdims=True)
    var = jnp.mean(jnp.square(o - mu), axis=-1, keepdims=True)
    y = (o - mu) * lax.rsqrt(var + EPS) * gn
    return (g * jax.nn.sigmoid(g) * y).astype(BF16)


def _ret_prompt_kernel(logg_ref, q_ref, kt_ref, v_ref, g_ref, gn_ref, y_ref, st_ref, s_ref, dmat_ref,
                       *, tb, front):
    hh = pl.program_id(0)
    blk = pl.program_id(1)
    lg = jnp.full((1, 1), logg_ref[hh], F32)

    @pl.when(blk == 0)
    def _():
        s_ref[...] = jnp.zeros_like(s_ref)
        r = lax.broadcasted_iota(jnp.int32, (tb, tb), 0)
        c = lax.broadcasted_iota(jnp.int32, (tb, tb), 1)
        decay = jnp.exp(lg * jnp.abs(r - c).astype(F32))
        dmat_ref[...] = jnp.where(c // CHUNK <= r // CHUNK, decay, 0.0)

    q = q_ref[...]
    key_row = blk * tb + lax.broadcasted_iota(jnp.int32, (1, tb), 1)
    kt = jnp.where(key_row >= front, kt_ref[...], 0.0)
    v = v_ref[...]
    s = _dot(q, kt.astype(BF16)) * dmat_ref[...]
    state = s_ref[...]
    q_dec = jnp.exp(lg * (lax.broadcasted_iota(jnp.int32, (tb, 1), 0) + 1).astype(F32))
    o = _dot(s.astype(BF16), v) + _dot(q, state.astype(BF16)) * q_dec
    k_dec = jnp.exp(lg * (tb - 1 - lax.broadcasted_iota(jnp.int32, (1, tb), 1)).astype(F32))
    new_state = state * jnp.exp(lg * tb) + _dot((kt * k_dec).astype(BF16), v)
    s_ref[...] = new_state
    y_ref[...] = _group_norm_gate(o, g_ref[...], gn_ref[...])

    @pl.when(blk == pl.num_programs(1) - 1)
    def _():
        st_ref[0] = new_state


def _ret_prompt(logg, q_b, k_t, v_b, g, gn, *, front, t_prompt, n_heads):
    t_all, qk_width = q_b.shape
    v_width = v_b.shape[1]
    qk_dim, v_dim = qk_width // n_heads, v_width // n_heads
    tb = RET_BLOCK
    return pl.pallas_call(
        functools.partial(_ret_prompt_kernel, tb=tb, front=front),
        grid=(n_heads, t_prompt // tb),
        in_specs=[pl.BlockSpec(memory_space=pltpu.SMEM),
                  pl.BlockSpec((tb, qk_dim), lambda h, i: (i, h)),
                  pl.BlockSpec((qk_dim, tb), lambda h, i: (h, i)),
                  pl.BlockSpec((tb, v_dim), lambda h, i: (i, h)),
                  pl.BlockSpec((tb, v_dim), lambda h, i: (i, h)),
                  pl.BlockSpec((1, v_dim), lambda h, i: (0, h))],
        out_specs=[pl.BlockSpec((tb, v_dim), lambda h, i: (i, h)),
                   pl.BlockSpec((1, qk_dim, v_dim), lambda h, i: (h, 0, 0))],
        out_shape=[jax.ShapeDtypeStruct((t_prompt, v_width), BF16),
                   jax.ShapeDtypeStruct((n_heads, qk_dim, v_dim), F32)],
        scratch_shapes=[pltpu.VMEM((qk_dim, v_dim), F32), pltpu.VMEM((tb, tb), F32)],
        compiler_params=_params(("arbitrary", "arbitrary")),
        name="ret_prompt",
    )(logg, q_b, k_t, v_b, g, gn)


def _ret_sample_kernel(logg_ref, q_ref, kt_ref, v_ref, g_ref, gn_ref, st_ref, y_ref, nst_ref, *, n_new):
    hh = pl.program_id(0)
    b = pl.program_id(1)
    lg = jnp.full((1, 1), logg_ref[hh], F32)
    n_rows = kt_ref.shape[1]
    r0 = pl.multiple_of(b * n_new, n_new)
    q = q_ref[pl.ds(r0, n_new), :]
    kt = kt_ref[...]
    v = v_ref[...]
    qrow = lax.broadcasted_iota(jnp.int32, (n_new, n_rows), 0)
    col = lax.broadcasted_iota(jnp.int32, (n_new, n_rows), 1) - b * n_new
    own = jnp.logical_and(col >= 0, col < n_new)
    dmat = jnp.where(own, jnp.exp(lg * jnp.abs(qrow - col).astype(F32)), 0.0)
    s = _dot(q, kt.astype(BF16)) * dmat
    state = st_ref[0, 0]
    q_dec = jnp.exp(lg * (lax.broadcasted_iota(jnp.int32, (n_new, 1), 0) + 1).astype(F32))
    o = _dot(s.astype(BF16), v) + _dot(q, state.astype(BF16)) * q_dec
    col1 = lax.broadcasted_iota(jnp.int32, (1, n_rows), 1) - b * n_new
    own1 = jnp.logical_and(col1 >= 0, col1 < n_new)
    k_dec = jnp.where(own1, jnp.exp(lg * (n_new - 1 - col1).astype(F32)), 0.0)
    nst_ref[0, 0] = state * jnp.exp(lg * n_new) + _dot((kt * k_dec).astype(BF16), v)
    y_ref[pl.ds(r0, n_new), :] = _group_norm_gate(o, g_ref[pl.ds(r0, n_new), :], gn_ref[...])


def _ret_sample(logg, q_b, k_t, v_b, g, gn, state, *, n_new):
    t_all, qk_width = q_b.shape
    v_width = v_b.shape[1]
    n_streams, n_heads, qk_dim, v_dim = state.shape
    n_rows = n_streams * n_new
    blk = 0
    return pl.pallas_call(
        functools.partial(_ret_sample_kernel, n_new=n_new),
        grid=(n_heads, n_streams),
        in_specs=[pl.BlockSpec(memory_space=pltpu.SMEM),
                  pl.BlockSpec((n_rows, qk_dim), lambda h, b: (blk, h)),
                  pl.BlockSpec((qk_dim, n_rows), lambda h, b: (h, blk)),
                  pl.BlockSpec((n_rows, v_dim), lambda h, b: (blk, h)),
                  pl.BlockSpec((n_rows, v_dim), lambda h, b: (blk, h)),
                  pl.BlockSpec((1, v_dim), lambda h, b: (0, h)),
                  pl.BlockSpec((1, 1, qk_dim, v_dim), lambda h, b: (b, h, 0, 0))],
        out_specs=[pl.BlockSpec((n_rows, v_dim), lambda h, b: (0, h)),
                   pl.BlockSpec((1, 1, qk_dim, v_dim), lambda h, b: (b, h, 0, 0))],
        out_shape=[jax.ShapeDtypeStruct((n_rows, v_width), BF16),
                   jax.ShapeDtypeStruct(state.shape, F32)],
        compiler_params=_params(("arbitrary", "arbitrary")),
        name="ret_sample",
    )(logg, q_b, k_t, v_b, g, gn, state)


def _top_values(s, k, with_rank):
    cols = s.shape[1]
    if cols > TOPK_SLAB:
        parts = [_top_values(s[:, j:j + TOPK_SLAB], k, with_rank) for j in range(0, cols, TOPK_SLAB)]
        vals = [jnp.concatenate([p[0][r] for p in parts], axis=1) for r in range(k)]
        rank = jnp.concatenate([p[1] for p in parts], axis=1) if with_rank else None
        return vals, rank
    work = s
    vals = []
    for r in range(k):
        m = jnp.max(work, axis=0, keepdims=True)
        vals.append(m)
        work = jnp.where(work == m, RANK_MARK * (1.0 + r / 1024.0), work)
    if not with_rank:
        return vals, None
    rank = jnp.where(work <= RANK_MARK, (work * (1.0 / RANK_MARK) - 1.0) * 1024.0, NO_RANK)
    return vals, rank


def _peer_kernel(x_ref, mp_ref, mt_ref, wo_ref, gn_ref, wq_ref, sk_ref, u_ref, vt_ref, o_ref,
                 xn_ref, hb_ref, e0_ref, cnt_ref, e1_ref, rank_ref, w_ref, y_ref,
                 *, n_heads, n_keys, topk):
    c = pl.program_id(1)
    ce, tm = u_ref.shape[0], x_ref.shape[0]
    rows_per_chunk = ce // n_keys
    sub = rank_ref.shape[2]

    @pl.when(c == 0)
    def _():
        n_sample = mt_ref.shape[0]
        mixed = mp_ref[...]
        head_rows = jnp.where(pl.program_id(0) == 0, mt_ref[...], mixed[:n_sample])
        mixed = jnp.concatenate([head_rows, mixed[n_sample:]], axis=0)
        xn = x_ref[...] + _dot(mixed, wo_ref[...])
        xn_ref[...] = xn
        hb = _rmsnorm_rows(xn, gn_ref[...]).astype(BF16)
        hb_ref[...] = hb
        y_ref[...] = jnp.zeros_like(y_ref)
        for h in range(n_heads):
            qh = _dot_nt(wq_ref[h * 2 * n_keys:(h + 1) * 2 * n_keys, :], hb)
            s0 = _dot(sk_ref[2 * h], qh[0:n_keys].astype(BF16))
            s1 = _dot(sk_ref[2 * h + 1], qh[n_keys:2 * n_keys].astype(BF16))
            a, rank0 = _top_values(s0, topk, True)
            b, rank1 = _top_values(s1, topk, True)
            half = topk // 2
            b_all = jnp.concatenate(b, axis=0)
            b_half = jnp.concatenate(b[:half], axis=0)
            cand = jnp.concatenate(
                [a[0] + b_all] + [a[i] + b_half for i in range(1, half)] + [jnp.concatenate(a[half:], axis=0) + b[0]],
                axis=0)
            f, _ = _top_values(cand, topk, False)
            tau = f[topk - 1]
            z = sum(jnp.exp(fk - f[0]) for fk in f)
            a_all = jnp.concatenate(a, axis=0)
            cnt_sorted = sum(jnp.where(a_all + bj >= tau, 1.0, 0.0) for bj in b)
            rank0_b = rank0.astype(BF16).reshape(n_keys // sub, sub, tm)
            cnt = jnp.zeros(rank0_b.shape, BF16)
            for r in range(topk):
                n_r = jnp.broadcast_to(cnt_sorted[r:r + 1, :], (sub, tm)).astype(BF16)
                cnt = jnp.where(rank0_b == float(r), n_r[None], cnt)
            cnt_ref[h] = cnt.reshape(n_keys, tm).astype(F32)
            e0_ref[h] = jnp.exp(s0 - a[0])
            e1_ref[h] = (jnp.exp(s1 - b[0]) * (0.5 / z)).astype(BF16).reshape(n_keys // sub, sub, tm)
            rank_ref[h] = rank1.astype(BF16).reshape(n_keys // sub, sub, tm)

    hb = hb_ref[...]
    edges = PEER_ACT_EDGES + (ce,)
    act_of_row = {}
    for lo_row, hi_row in zip(edges[:-1], edges[1:]):
        piece = _dot_nt(u_ref[lo_row:hi_row, :], hb)
        for r in range(lo_row, hi_row, n_keys):
            act_of_row[r] = (piece, r - lo_row)
    rows_per_slab = MXU_DIM // n_keys
    y_add = None
    for slab in range(ce // MXU_DIM):
        for il in range(slab * rows_per_slab, (slab + 1) * rows_per_slab):
            i = c * rows_per_chunk + il
            gate = None
            for h in range(n_heads):
                n_sel = jnp.broadcast_to(cnt_ref[h, pl.ds(i, 1), :], (sub, tm)).astype(BF16)
                e0 = jnp.broadcast_to(e0_ref[h, pl.ds(i, 1), :], (sub, tm)).astype(BF16)
                term = jnp.where(rank_ref[h] < n_sel[None], e1_ref[h], jnp.zeros((), BF16)) * e0[None]
                gate = term if gate is None else gate + term
            r0 = il * n_keys
            piece, off = act_of_row[r0]
            x_act = piece[off:off + n_keys, :]
            gelu2 = x_act * (1.0 + lax.erf(x_act * (2.0 ** -0.5)))
            w_ref[r0:r0 + n_keys, :] = gate.reshape(n_keys, tm) * gelu2.astype(BF16)
        rows = slice(slab * MXU_DIM, (slab + 1) * MXU_DIM)
        part = _dot(vt_ref[:, rows], w_ref[rows, :])
        y_add = part if y_add is None else y_add + part
    y_ref[...] += y_add

    @pl.when(c == pl.num_programs(1) - 1)
    def _():
        o_ref[...] = xn_ref[...] + y_ref[...].T


def _peer(x, mix_prompt, mix_sample, w_out, gn, wq_t, subkeys, u, v_t, *, layer, n_heads, n_keys):
    t_all, d = x.shape
    k = mix_prompt.shape[1]
    n_sample = mix_sample.shape[0]
    n_experts = u.shape[1]
    tm = ROW_TILE
    ce = PEER_EXPERT_CHUNK
    assert mix_prompt.shape == (t_all, k) and mix_sample.shape == (n_sample, k) and n_sample <= tm
    head_scratch = pltpu.VMEM((n_heads, n_keys, tm), F32)
    head_scratch_b = pltpu.VMEM((n_heads, n_keys // BF16_TILE_ROWS, BF16_TILE_ROWS, tm), BF16)
    return pl.pallas_call(
        functools.partial(_peer_kernel, n_heads=n_heads, n_keys=n_keys, topk=PEER_TOPK),
        grid=(t_all // tm, n_experts // ce),
        in_specs=[pl.BlockSpec((tm, d), lambda i, c: (i, 0)),
                  pl.BlockSpec((tm, k), lambda i, c: (i, 0)),
                  _const_spec((n_sample, k)), _const_spec(w_out.shape),
                  _const_spec((1, d)), _const_spec(wq_t.shape),
                  _const_spec(subkeys.shape),
                  pl.BlockSpec((None, ce, d), lambda i, c: (layer, c, 0)),
                  pl.BlockSpec((None, d, ce), lambda i, c: (layer, 0, c))],
        out_specs=pl.BlockSpec((tm, d), lambda i, c: (i, 0)),
        out_shape=jax.ShapeDtypeStruct((t_all, d), F32),
        scratch_shapes=[pltpu.VMEM((tm, d), F32), pltpu.VMEM((tm, d), BF16),
                        head_scratch, head_scratch, head_scratch_b, head_scratch_b,
                        pltpu.VMEM((ce, tm), BF16), pltpu.VMEM((d, tm), F32)],
        compiler_params=_params(("arbitrary", "arbitrary")),
        name="peer",
    )(x, mix_prompt, mix_sample, w_out, gn, wq_t, subkeys, u, v_t)


def kernel(x_prompt, x_sample, cache_fox_k, cache_fox_v, cache_fox_lf, state_ret, meta_tokens, norm_mix, norm_ffn,
           fox_w_in, fox_b_f, fox_q_norm, fox_k_norm, fox_w_out, ret_w_in, ret_gn, ret_w_out,
           peer_w_q, peer_subkeys, peer_u, peer_v):
    batch, seq, d = x_prompt.shape
    n_streams, n_new, _ = x_sample.shape
    n_meta = meta_tokens.shape[0]
    assert batch == 1 and n_meta == N_META
    depth = norm_mix.shape[0]
    n_fox, _, past, fox_heads, fox_hd = cache_fox_k.shape
    fox_width = fox_heads * fox_hd
    n_ret, _, ret_heads, ret_qk, ret_v = state_ret.shape
    ret_v_width = ret_heads * ret_v
    peer_heads, _, n_keys, peer_half = peer_subkeys.shape[1:]
    assert n_keys == LANES and peer_half == LANES and fox_width == d

    length = n_meta + seq
    n_rows = n_streams * n_new
    front = (-length) % ROW_TILE
    if front < n_rows:
        front += ROW_TILE
    t_all = front + length

    x = jnp.concatenate([x_sample.reshape(n_rows, d), jnp.zeros((front - n_rows, d), F32),
                         meta_tokens.astype(F32), x_prompt[0]], axis=0)

    half = ret_qk // 2
    inv_freq = (ROPE_BASE ** (-jnp.arange(half, dtype=F32) / half)).reshape(1, half)
    logg = jnp.log(1.0 - 2.0 ** (-5.0 - jnp.arange(ret_heads, dtype=F32)))

    cache_k = cache_fox_k.reshape(n_fox, n_streams, past, fox_width)
    cache_v = cache_fox_v.reshape(n_fox, n_streams, past, fox_width)
    cache_lf_t = cache_fox_lf.transpose(0, 1, 3, 2)
    peer_u_b = peer_u.astype(BF16)
    peer_vt_b = peer_v.astype(BF16).transpose(0, 2, 1)

    kp, vp, lfp, ks, vs, lfs, srp, srs = [], [], [], [], [], [], [], []
    for layer in range(depth):
        j = layer // 2
        gn_mix = norm_mix[layer].reshape(1, d)
        if layer % 2 == 0:
            w_in = fox_w_in[j]
            w_main = w_in[:, :4 * fox_width].astype(BF16)
            w_f = jnp.pad(w_in[:, 4 * fox_width:], ((0, 0), (0, LANES - fox_heads))).astype(BF16)
            b_f = jnp.pad(fox_b_f[j], (0, LANES - fox_heads)).reshape(1, LANES)
            gq = jnp.tile(fox_q_norm[j], fox_heads).reshape(1, fox_width)
            gk = jnp.tile(fox_k_norm[j], fox_heads).reshape(1, fox_width)
            q_b, k_f, k_b, v_f, v_b, og, lf, csum = _fox_proj(x, gn_mix, w_main, w_f, b_f, gq, gk,
                                                              front=front, head_dim=fox_hd)
            o_b = _fox_attn_prompt(q_b, k_b, v_b, csum, og, fox_q_norm[j], fox_k_norm[j],
                                   t_prompt=t_all, head_dim=fox_hd)
            lf_new_t = lf[:n_rows, :fox_heads].T
            o_s = _fox_attn_sample(q_b, k_b, v_b, og, lf_new_t, cache_k, cache_v, cache_lf_t,
                                   layer=j, n_new=n_new, head_dim=fox_hd)
            mix_prompt, mix_sample, w_out = o_b, o_s, fox_w_out[j].astype(BF16)
            kp.append(k_f)
            vp.append(v_f)
            lfp.append(lf[front:, :fox_heads].reshape(1, length, fox_heads))
            ks.append(k_f[:n_rows].reshape(n_streams, n_new, fox_heads, fox_hd))
            vs.append(v_f[:n_rows].reshape(n_streams, n_new, fox_heads, fox_hd))
            lfs.append(lf[:n_rows, :fox_heads].reshape(n_streams, n_new, fox_heads))
        else:
            q_b, k_t, v_b, g = _ret_proj(x, gn_mix, ret_w_in[j].astype(BF16), inv_freq,
                                         n_sample=n_rows, pos0=front + n_meta, past=past, n_new=n_new,
                                         n_heads=ret_heads, qk_dim=ret_qk, v_width=ret_v_width)
            gn_ret = ret_gn[j].reshape(1, ret_v_width)
            y_b, st_p = _ret_prompt(logg, q_b, k_t, v_b, g, gn_ret, front=front, t_prompt=t_all, n_heads=ret_heads)
            y_s, st_s = _ret_sample(logg, q_b, k_t, v_b, g, gn_ret, state_ret[j], n_new=n_new)
            mix_prompt, mix_sample, w_out = y_b, y_s, ret_w_out[j].astype(BF16)
            srp.append(st_p[None])
            srs.append(st_s)
        x = _peer(x, mix_prompt, mix_sample, w_out, norm_ffn[layer].reshape(1, d), peer_w_q[layer].T.astype(BF16),
                  peer_subkeys[layer].reshape(2 * peer_heads, n_keys, peer_half).astype(BF16),
                  peer_u_b, peer_vt_b, layer=layer, n_heads=peer_heads, n_keys=n_keys)

    y_prompt = x[front + n_meta:][None]
    y_sample = x[:n_rows].reshape(n_streams, n_new, d)
    prompt_cache = lambda a: jnp.stack(a)[:, front:].reshape(len(a), 1, length, fox_heads, fox_hd)
    return (y_prompt, y_sample, prompt_cache(kp), prompt_cache(vp), jnp.stack(lfp), jnp.stack(srp),
            jnp.stack(ks), jnp.stack(vs), jnp.stack(lfs), jnp.stack(srs))
```
